```python
import math
import jax, jax.numpy as jnp
from jax import lax
import numpy as np

D_MODEL = 1024
BATCH = 4
SEQ = 4096
DEPTH = 2

BRANCH_WIDTH = D_MODEL // 2
NORM_EPS = 1e-6

RET_HEADS = 4
RET_DK = BRANCH_WIDTH // (2 * RET_HEADS)
RET_DV = BRANCH_WIDTH // RET_HEADS
RET_CHUNK = 128
ROPE_BASE = 10000.0
MAX_POS_OFFSET = 1024

GLA_HEADS = 4
GLA_DK = BRANCH_WIDTH // (2 * GLA_HEADS)
GLA_DV = BRANCH_WIDTH // GLA_HEADS
GLA_GATE_RANK = 16
GLA_GATE_NORMALIZER = 16.0
GLA_CHUNK = 64

SSD_HEAD_DIM = 64
SSD_HEADS = BRANCH_WIDTH // SSD_HEAD_DIM
SSD_GROUPS = 2
SSD_STATE = 128
SSD_CONV = 4
SSD_CHUNK = 128
SSD_CONV_CH = BRANCH_WIDTH + 2 * SSD_GROUPS * SSD_STATE

RWKV_HEAD_DIM = 64
RWKV_HEADS = BRANCH_WIDTH // RWKV_HEAD_DIM
RWKV_DECAY_RANK = 64
RWKV_ICL_RANK = 64
RWKV_SHIFT_WIDTH = 3 * BRANCH_WIDTH + RWKV_DECAY_RANK + RWKV_ICL_RANK
RWKV_LN_EPS = 64e-5

MEM_TOKENS = 256
MEM_HEADS = 4
MEM_HEAD_DIM = 64
MEM_WIDTH = MEM_HEADS * MEM_HEAD_DIM

N_BRANCHES = 5
IN_SIZES = (
    RET_HEADS * RET_DK, RET_HEADS * RET_DK, BRANCH_WIDTH, BRANCH_WIDTH,
    GLA_HEADS * GLA_DK, GLA_HEADS * GLA_DK, BRANCH_WIDTH, GLA_GATE_RANK, BRANCH_WIDTH,
    SSD_CONV_CH, SSD_HEADS, BRANCH_WIDTH,
    RWKV_SHIFT_WIDTH, BRANCH_WIDTH,
    MEM_WIDTH,
    N_BRANCHES * D_MODEL,
)
IN_TOTAL = sum(IN_SIZES)

kernel_name = 'hybrid_ret_gla_ssd_rwkv7_mem'


def split_cols(p, sizes):
    out, start = [], 0
    for s in sizes:
        out.append(p[..., start:start + s])
        start += s
    return out


def to_heads(t, n_heads):
    b, s, _ = t.shape
    return t.reshape(b, s, n_heads, -1).transpose(0, 2, 1, 3)


def from_heads(t):
    b, h, s, d = t.shape
    return t.transpose(0, 2, 1, 3).reshape(b, s, h * d)


def rms_norm(x, g):
    xf = x.astype(jnp.float32)
    y = xf * lax.rsqrt(jnp.mean(xf * xf, axis=-1, keepdims=True) + NORM_EPS)
    return (y * g.astype(jnp.float32)).astype(x.dtype)


def rms_unit(xf):
    return xf * lax.rsqrt(jnp.mean(xf * xf, axis=-1, keepdims=True) + NORM_EPS)


def rotate_interleaved(x, positions):
    dk = x.shape[-1]
    inv = 1.0 / (ROPE_BASE ** jnp.linspace(0.0, 1.0, dk // 2, dtype=jnp.float32))
    ang = positions.astype(jnp.float32)[:, None, :, None] * inv
    sin, cos = jnp.sin(ang), jnp.cos(ang)
    xp = x.reshape(*x.shape[:-1], dk // 2, 2)
    x1, x2 = xp[..., 0], xp[..., 1]
    return jnp.stack([x1 * cos - x2 * sin, x2 * cos + x1 * sin], axis=-1).reshape(x.shape)


def causal_depthwise_conv(x, w, bias):
    k, c = w.shape
    y = lax.conv_general_dilated(x, w[:, None, :], window_strides=(1,), padding=[(k - 1, 0)],
                                 dimension_numbers=('NWC', 'WIO', 'NWC'), feature_group_count=c)
    return y + bias


def chunked_scalar_decay(q, k, v, log_a, chunk):
    f32 = jnp.float32
    b, h, t, dk = q.shape
    dv = v.shape[-1]
    n = t // chunk
    qc = q.astype(f32).reshape(b, h, n, chunk, dk)
    kc = k.astype(f32).reshape(b, h, n, chunk, dk)
    vc = v.astype(f32).reshape(b, h, n, chunk, dv)
    cum = jnp.cumsum(log_a.astype(f32).reshape(b, h, n, chunk), axis=-1)
    causal = jnp.tril(jnp.ones((chunk, chunk), dtype=bool))
    seg = jnp.exp(jnp.where(causal, cum[..., :, None] - cum[..., None, :], -jnp.inf))
    scores = jnp.einsum('bhntd,bhnsd->bhnts', qc, kc) * seg
    y_intra = jnp.einsum('bhnts,bhnsv->bhntv', scores, vc)
    last = cum[..., -1:]
    chunk_state = jnp.einsum('bhnsd,bhnsv->bhndv', kc * jnp.exp(last - cum)[..., None], vc)
    chunk_decay = jnp.exp(last[..., 0])

    def step(state, inp):
        dec, cs = inp
        return state * dec[..., None, None] + cs, state

    init = jnp.zeros((b, h, dk, dv), f32)
    _, prev = lax.scan(step, init, (jnp.moveaxis(chunk_decay, 2, 0), jnp.moveaxis(chunk_state, 2, 0)))
    prev = jnp.moveaxis(prev, 0, 2)
    y_inter = jnp.einsum('bhntd,bhndv->bhntv', qc * jnp.exp(cum)[..., None], prev)
    return (y_intra + y_inter).reshape(b, h, t, dv)


def chunked_vector_decay(q, k, v, log_g, chunk):
    f32 = jnp.float32
    b, h, t, dk = q.shape
    dv = v.shape[-1]
    n = t // chunk
    qc = q.astype(f32).reshape(b, h, n, chunk, dk)
    kc = k.astype(f32).reshape(b, h, n, chunk, dk)
    vc = v.astype(f32).reshape(b, h, n, chunk, dv)
    cum = jnp.cumsum(log_g.astype(f32).reshape(b, h, n, chunk, dk), axis=3)
    ref = cum[:, :, :, chunk // 2:chunk // 2 + 1, :]
    q_in = qc * jnp.exp(cum - ref)
    k_in = kc * jnp.exp(ref - cum)
    causal = jnp.tril(jnp.ones((chunk, chunk), dtype=bool))
    scores = jnp.where(causal, jnp.einsum('bhntd,bhnsd->bhnts', q_in, k_in), 0.0)
    y_intra = jnp.einsum('bhnts,bhnsv->bhntv', scores, vc)
    last = cum[:, :, :, -1:, :]
    chunk_state = jnp.einsum('bhnsd,bhnsv->bhndv', kc * jnp.exp(last - cum), vc)
    chunk_decay = jnp.exp(last[:, :, :, 0, :])

    def step(state, inp):
        dec, cs = inp
        return state * dec[..., None] + cs, state

    init = jnp.zeros((b, h, dk, dv), f32)
    _, prev = lax.scan(step, init, (jnp.moveaxis(chunk_decay, 2, 0), jnp.moveaxis(chunk_state, 2, 0)))
    prev = jnp.moveaxis(prev, 0, 2)
    y_inter = jnp.einsum('bhntd,bhndv->bhntv', qc * jnp.exp(cum), prev)
    return (y_intra + y_inter).reshape(b, h, t, dv)


def retention_branch(q, k, v, positions):
    f32 = jnp.float32
    b, t, _ = q.shape
    qh = rotate_interleaved(to_heads(q.astype(f32), RET_HEADS), positions)
    kh = rotate_interleaved(to_heads(k.astype(f32), RET_HEADS) * RET_DK ** -0.5, positions)
    vh = to_heads(v.astype(f32), RET_HEADS)
    log_gamma = jnp.log(1.0 - 2.0 ** (-5.0 - jnp.arange(RET_HEADS, dtype=f32)))
    log_a = jnp.broadcast_to(log_gamma[None, :, None], (b, RET_HEADS, t))
    o = chunked_scalar_decay(qh, kh, vh, log_a, RET_CHUNK)
    return from_heads(rms_unit(o))


def gla_branch(q, k, v, gk_low, gk_w2, gk_b, norm_g):
    f32 = jnp.float32
    log_g = jax.nn.log_sigmoid((gk_low @ gk_w2 + gk_b).astype(f32)) / GLA_GATE_NORMALIZER
    qh = to_heads(q.astype(f32), GLA_HEADS) * GLA_DK ** -0.5
    kh = to_heads(k.astype(f32), GLA_HEADS)
    vh = to_heads(v.astype(f32), GLA_HEADS)
    o = chunked_vector_decay(qh, kh, vh, to_heads(log_g, GLA_HEADS), GLA_CHUNK)
    return from_heads(rms_unit(o) * norm_g.astype(f32))


def ssd_branch(xbc, dt_raw, z, conv_w, conv_b, dt_bias, a_log, d_skip, norm_g):
    f32 = jnp.float32
    xbc = jax.nn.silu(causal_depthwise_conv(xbc.astype(f32), conv_w.astype(f32), conv_b.astype(f32)))
    xs, bm, cm = split_cols(xbc, (BRANCH_WIDTH, SSD_GROUPS * SSD_STATE, SSD_GROUPS * SSD_STATE))
    rep = SSD_HEADS // SSD_GROUPS
    xh = to_heads(xs, SSD_HEADS)
    bh = jnp.repeat(to_heads(bm, SSD_GROUPS), rep, axis=1)
    ch = jnp.repeat(to_heads(cm, SSD_GROUPS), rep, axis=1)
    dt = jax.nn.softplus(dt_raw.astype(f32) + dt_bias.astype(f32)).transpose(0, 2, 1)
    a = -jnp.exp(a_log.astype(f32))
    y = chunked_scalar_decay(ch, bh, xh * dt[..., None], dt * a[None, :, None], SSD_CHUNK)
    y = from_heads(y + d_skip.astype(f32)[None, :, None, None] * xh)
    y = y * jax.nn.silu(z.astype(f32))
    b, t, _ = y.shape
    y = rms_unit(y.reshape(b, t, SSD_GROUPS, -1)).reshape(b, t, BRANCH_WIDTH)
    return y * norm_g.astype(f32)


def rwkv7_branch(u, mu, w0, w2, a0, a2, k_k, k_a, r_k, ln_g, ln_b):
    f32 = jnp.float32
    b, t, _ = u.shape
    u = u.astype(f32)
    u_prev = jnp.pad(u, ((0, 0), (1, 0), (0, 0)))[:, :-1]
    xs = u + (u_prev - u) * mu.astype(f32)
    r, k, v, wl, al = split_cols(xs, (BRANCH_WIDTH, BRANCH_WIDTH, BRANCH_WIDTH, RWKV_DECAY_RANK, RWKV_ICL_RANK))
    log_w = -jax.nn.softplus(-(w0.astype(f32) + jnp.tanh(wl) @ w2.astype(f32))) - 0.5
    decay = jnp.exp(-jnp.exp(log_w))
    a = jax.nn.sigmoid(a0.astype(f32) + al @ a2.astype(f32))
    kk = k * k_k.astype(f32)
    k = k * (1.0 + (a - 1.0) * k_a.astype(f32))
    shp = (b, t, RWKV_HEADS, RWKV_HEAD_DIM)
    r, decay, k, v, a, kk = [z_.reshape(shp) for z_ in (r, decay, k, v, a, kk)]
    kk = kk / jnp.maximum(jnp.sqrt(jnp.sum(kk * kk, axis=-1, keepdims=True)), 1e-12)

    def step(state, inp):
        r_t, w_t, k_t, v_t, a_t, b_t = inp
        sa = jnp.einsum('bhvk,bhk->bhv', state, a_t)
        state = (state * w_t[:, :, None, :] + sa[..., None] * b_t[:, :, None, :]
                 + v_t[..., None] * k_t[:, :, None, :])
        return state, jnp.einsum('bhvk,bhk->bhv', state, r_t)

    init = jnp.zeros((b, RWKV_HEADS, RWKV_HEAD_DIM, RWKV_HEAD_DIM), f32)
    xs_seq = tuple(jnp.moveaxis(z_, 1, 0) for z_ in (r, decay, k, v, -kk, kk * a))
    _, y = lax.scan(step, init, xs_seq)
    y = jnp.moveaxis(y, 0, 1)
    mean = jnp.mean(y, axis=-1, keepdims=True)
    var = jnp.mean(jnp.square(y - mean), axis=-1, keepdims=True)
    y = ((y - mean) * lax.rsqrt(var + RWKV_LN_EPS)).reshape(b, t, BRANCH_WIDTH)
    y = y * ln_g.astype(f32) + ln_b.astype(f32)
    bonus = jnp.sum(r * k * r_k.astype(f32), axis=-1, keepdims=True) * v
    return y + bonus.reshape(b, t, BRANCH_WIDTH)


def memory_branch(q, mem_n, w_kv):
    f32 = jnp.float32
    km, vm = split_cols(mem_n @ w_kv, (MEM_WIDTH, MEM_WIDTH))
    qh = to_heads(q.astype(f32), MEM_HEADS) * MEM_HEAD_DIM ** -0.5
    kh = to_heads(km.astype(f32), MEM_HEADS)
    vh = to_heads(vm.astype(f32), MEM_HEADS)
    p = jax.nn.softmax(jnp.einsum('bhtd,bhmd->bhtm', qh, kh), axis=-1)
    return from_heads(jnp.einsum('bhtm,bhmd->bhtd', p, vh))


def setup_inputs(seed: int = 0) -> dict:
    key = jax.random.key(seed)
    f32 = jnp.float32
    counter = [0]

    def nk():
        counter[0] += 1
        return jax.random.fold_in(key, counter[0])

    def nrm(shape, scale):
        return jax.random.normal(nk(), shape, f32) * scale

    def gain(shape):
        return 1.0 + nrm(shape, 0.05)

    L, D, W = DEPTH, D_MODEL, BRANCH_WIDTH
    x = nrm((BATCH, SEQ, D), 1.0)
    mem = nrm((BATCH, MEM_TOKENS, D), 1.0)
    offset = jax.random.randint(nk(), (BATCH, 1), 0, MAX_POS_OFFSET, dtype=jnp.int32)
    positions = offset + jnp.arange(SEQ, dtype=jnp.int32)[None, :]
    dt0 = jnp.exp(jax.random.uniform(nk(), (L, SSD_HEADS), f32, math.log(1e-3), math.log(1e-1)))
    return {
        'x': x,
        'mem': mem,
        'positions': positions,
        'norm_g': gain((L, D)),
        'w_in': nrm((L, D, IN_TOTAL), D ** -0.5),
        'gla_gk_w2': nrm((L, GLA_GATE_RANK, GLA_HEADS * GLA_DK), GLA_GATE_RANK ** -0.5),
        'gla_gk_b': nrm((L, GLA_HEADS * GLA_DK), 0.1),
        'gla_norm_g': gain((L, GLA_DV)),
        'ssd_conv_w': nrm((L, SSD_CONV, SSD_CONV_CH), SSD_CONV ** -0.5),
        'ssd_conv_b': nrm((L, SSD_CONV_CH), 0.02),
        'ssd_dt_bias': dt0 + jnp.log(-jnp.expm1(-dt0)),
        'ssd_a_log': jnp.log(jax.random.uniform(nk(), (L, SSD_HEADS), f32, 1.0, 16.0)),
        'ssd_d': gain((L, SSD_HEADS)),
        'ssd_norm_g': gain((L, W)),
        'rwkv_mu': jax.random.uniform(nk(), (L, RWKV_SHIFT_WIDTH), f32),
        'rwkv_w0': jnp.linspace(-6.0, -1.0, W, dtype=f32)[None, :] + nrm((L, W), 0.1),
        'rwkv_w2': nrm((L, RWKV_DECAY_RANK, W), 0.5 * RWKV_DECAY_RANK ** -0.5),
        'rwkv_a0': nrm((L, W), 0.1),
        'rwkv_a2': nrm((L, RWKV_ICL_RANK, W), 0.5 * RWKV_ICL_RANK ** -0.5),
        'rwkv_k_k': 0.85 + nrm((L, W), 0.05),
        'rwkv_k_a': gain((L, W)),
        'rwkv_r_k': nrm((L, RWKV_HEADS, RWKV_HEAD_DIM), 0.1),
        'rwkv_ln_g': gain((L, W)),
        'rwkv_ln_b': nrm((L, W), 0.02),
        'mem_norm_g': gain((L, D)),
        'w_mem_kv': nrm((L, D, 2 * MEM_WIDTH), D ** -0.5),
        'w_up_ret': nrm((L, W, D), W ** -0.5),
        'w_up_gla': nrm((L, W, D), W ** -0.5),
        'w_up_ssd': nrm((L, W, D), W ** -0.5),
        'w_up_rwkv': nrm((L, W, D), W ** -0.5),
        'w_up_mem': nrm((L, MEM_WIDTH, D), MEM_WIDTH ** -0.5),
        'w_out': nrm((L, D, D), D ** -0.5),
        'final_norm_g': gain((D,)),
    }


def reference(x, mem, positions, norm_g, w_in, gla_gk_w2, gla_gk_b, gla_norm_g,
              ssd_conv_w, ssd_conv_b, ssd_dt_bias, ssd_a_log, ssd_d, ssd_norm_g,
              rwkv_mu, rwkv_w0, rwkv_w2, rwkv_a0, rwkv_a2, rwkv_k_k, rwkv_k_a, rwkv_r_k,
              rwkv_ln_g, rwkv_ln_b, mem_norm_g, w_mem_kv,
              w_up_ret, w_up_gla, w_up_ssd, w_up_rwkv, w_up_mem, w_out, final_norm_g):
    f32 = jnp.float32
    b, t, d = x.shape
    for l in range(DEPTH):
        h = rms_norm(x, norm_g[l])
        p = h @ w_in[l]
        (ret_q, ret_k, ret_v, ret_g, gla_q, gla_k, gla_v, gla_gk, gla_g,
         ssd_xbc, ssd_dt, ssd_z, rwkv_in, rwkv_g, mem_q, gates) = split_cols(p, IN_SIZES)

        o_ret = retention_branch(ret_q, ret_k, ret_v, positions) * jax.nn.silu(ret_g.astype(f32))
        o_gla = gla_branch(gla_q, gla_k, gla_v, gla_gk, gla_gk_w2[l], gla_gk_b[l],
                           gla_norm_g[l]) * jax.nn.silu(gla_g.astype(f32))
        o_ssd = ssd_branch(ssd_xbc, ssd_dt, ssd_z, ssd_conv_w[l], ssd_conv_b[l], ssd_dt_bias[l],
                           ssd_a_log[l], ssd_d[l], ssd_norm_g[l])
        o_rwkv = rwkv7_branch(rwkv_in, rwkv_mu[l], rwkv_w0[l], rwkv_w2[l], rwkv_a0[l], rwkv_a2[l],
                              rwkv_k_k[l], rwkv_k_a[l], rwkv_r_k[l], rwkv_ln_g[l],
                              rwkv_ln_b[l]) * jax.nn.silu(rwkv_g.astype(f32))
        o_mem = memory_branch(mem_q, rms_norm(mem, mem_norm_g[l]), w_mem_kv[l])

        g = jax.nn.sigmoid(gates.astype(f32)).reshape(b, t, N_BRANCHES, d)
        branches = (o_ret.astype(x.dtype) @ w_up_ret[l], o_gla.astype(x.dtype) @ w_up_gla[l],
                    o_ssd.astype(x.dtype) @ w_up_ssd[l], o_rwkv.astype(x.dtype) @ w_up_rwkv[l],
                    o_mem.astype(x.dtype) @ w_up_mem[l])
        merged = g[:, :, 0] * branches[0].astype(f32)
        for i in range(1, N_BRANCHES):
            merged = merged + g[:, :, i] * branches[i].astype(f32)
        x = x + merged.astype(x.dtype) @ w_out[l]
    return rms_norm(x, final_norm_g)
```

```python
import functools
import math

import jax
import jax.numpy as jnp
import numpy as np
from jax import lax
from jax.experimental import pallas as pl
from jax.experimental.pallas import tpu as pltpu

F32 = jnp.float32
BF16 = jnp.bfloat16
MXU_DTYPE = jnp.bfloat16

D_MODEL = 1024
WIDTH = 512
NORM_EPS = 1e-6
N_BRANCHES = 5

RET_HEADS, RET_DK, RET_DV, RET_CHUNK = 4, 64, 128, 128
ROPE_BASE = 10000.0
GLA_HEADS, GLA_DK, GLA_DV, GLA_RANK, GLA_NORMALIZER, GLA_CHUNK = 4, 64, 128, 16, 16.0, 64
SSD_HEADS, SSD_P, SSD_GROUPS, SSD_STATE, SSD_CONV, SSD_CHUNK = 8, 64, 2, 128, 4, 128
RWKV_HEADS, RWKV_N, RWKV_RANK, RWKV_CHUNK = 8, 64, 64, 64
RWKV_LN_EPS = 64e-5
MEM_HEADS, MEM_HEAD_DIM, MEM_WIDTH = 4, 64, 256

LANES = 128
CARRY_ROWS = 8
TOKEN_TILE = 512
VMEM_LIMIT = 56 * 1024 * 1024

_IN_SIZES = (256, 256, 512, 512, 256, 256, 512, 16, 512, 1024, 8, 512, 1664, 512, 256, 5120)
_IN_OFFS = tuple(int(v) for v in np.cumsum((0,) + _IN_SIZES))


def _dot(a, b):
    return jnp.dot(a.astype(MXU_DTYPE), b.astype(MXU_DTYPE), preferred_element_type=F32)


def _dot_nt(a, b):
    return lax.dot_general(a.astype(MXU_DTYPE), b.astype(MXU_DTYPE), (((1,), (1,)), ((), ())),
                           preferred_element_type=F32)


def _dot_tn(a, b):
    return lax.dot_general(a.astype(MXU_DTYPE), b.astype(MXU_DTYPE), (((0,), (0,)), ((), ())),
                           preferred_element_type=F32)


def _split3(x):
    hi = x.astype(BF16)
    r1 = x - hi.astype(F32)
    mid = r1.astype(BF16)
    lo = (r1 - mid.astype(F32)).astype(BF16)
    return hi, mid, lo


def _sel_dot(sel, x):
    s = sel.astype(BF16)
    return sum(jnp.dot(s, p, preferred_element_type=F32) for p in _split3(x))


def _dot_sel(x, sel, pieces=3):
    s = sel.astype(BF16)
    return sum(jnp.dot(p, s, preferred_element_type=F32) for p in _split3(x)[:pieces])


def _iota(shape, dim):
    return lax.broadcasted_iota(jnp.int32, shape, dim)


def _tril(n, strict=False):
    r, c = _iota((n, n), 0), _iota((n, n), 1)
    return (r > c) if strict else (r >= c)


def _silu(x):
    return x * jax.nn.sigmoid(x)


def _softplus(x):
    return jnp.maximum(x, 0.0) + jnp.log1p(jnp.exp(-jnp.abs(x)))


def _rows(c, n):
    return pl.ds(pl.multiple_of(c * n, n), n)


def _full_spec(shape):
    zeros = (0,) * len(shape)
    return pl.BlockSpec(shape, lambda *_: zeros)


def _params(semantics):
    return pltpu.CompilerParams(dimension_semantics=semantics, vmem_limit_bytes=VMEM_LIMIT)


def _rmsnorm_kernel(x_ref, g_ref, o_ref):
    x = x_ref[...]
    y = x * lax.rsqrt(jnp.mean(x * x, axis=-1, keepdims=True) + NORM_EPS)
    o_ref[...] = (y * g_ref[...]).astype(o_ref.dtype)


def _rmsnorm(x2d, g, out_dtype, interpret):
    m, d = x2d.shape
    tm = min(1024, m)
    return pl.pallas_call(
        _rmsnorm_kernel,
        grid=(m // tm,),
        in_specs=[pl.BlockSpec((tm, d), lambda i: (i, 0)), _full_spec((1, d))],
        out_specs=pl.BlockSpec((tm, d), lambda i: (i, 0)),
        out_shape=jax.ShapeDtypeStruct((m, d), out_dtype),
        compiler_params=_params(("parallel",)),
        interpret=interpret,
        name="rmsnorm",
    )(x2d, g.reshape(1, d))


def _ret_kernel(h_ref, pos_ref, inv_ref, w_ref, o_ref, p_scr, s_scr, *, tb):
    C = RET_CHUNK

    @pl.when(pl.program_id(1) == 0)
    def _():
        s_scr[...] = jnp.zeros_like(s_scr)

    p = jnp.dot(h_ref[...], w_ref[...], preferred_element_type=F32)
    ang = pos_ref[...].astype(F32) * inv_ref[...]
    cos, sin = jnp.cos(ang), jnp.sin(ang)
    q1, q2 = p[:, 0:128], p[:, 128:256]
    k1, k2 = p[:, 256:384] * RET_DK ** -0.5, p[:, 384:512] * RET_DK ** -0.5
    p_scr[:, 0:128] = q1 * cos - q2 * sin
    p_scr[:, 128:256] = q2 * cos + q1 * sin
    p_scr[:, 256:384] = k1 * cos - k2 * sin
    p_scr[:, 384:512] = k2 * cos + k1 * sin
    p_scr[:, 512:1536] = p[:, 512:1536]

    def log_gamma(head):
        return jnp.log(1.0 - jnp.exp2(-5.0 - head.astype(F32)))

    qk_head = (_iota((1, 256), 1) >> 5) & 3
    lg_lane = log_gamma(qk_head)
    tau = _iota((C, 1), 0).astype(F32)
    dq = jnp.exp(lg_lane * (tau + 1.0))
    dk = jnp.exp(lg_lane * (C - 1.0 - tau))
    ds = jnp.exp(lg_lane * float(C))
    diff = (_iota((C, C), 0) - _iota((C, C), 1)).astype(F32)
    causal = _tril(C)
    bd_mask = (_iota((WIDTH, 256), 0) >> 7) == ((_iota((WIDTH, 256), 1) >> 5) & 3)

    def chunk(c, carry):
        rows = _rows(c, C)
        q = p_scr[rows, 0:256]
        k = p_scr[rows, 256:512]
        v = p_scr[rows, 512:1024]
        st = s_scr[...]
        y_inter = _dot_nt(q * dq, st)
        for hd in range(RET_HEADS):
            lg = math.log(1.0 - 2.0 ** (-5.0 - hd))
            seg = jnp.where(causal, jnp.exp(lg * diff), 0.0)
            kh = jnp.where(qk_head == hd, k, 0.0)
            sc = _dot_nt(q, kh) * seg
            cols = slice(hd * RET_DV, (hd + 1) * RET_DV)
            y = _dot(sc, v[:, cols]) + y_inter[:, cols]
            y = y * lax.rsqrt(jnp.mean(y * y, axis=-1, keepdims=True) + NORM_EPS)
            g = p_scr[rows, 1024 + hd * RET_DV:1024 + (hd + 1) * RET_DV]
            o_ref[rows, cols] = (y * _silu(g)).astype(o_ref.dtype)
        s_scr[...] = st * ds + jnp.where(bd_mask, _dot_tn(v, k * dk), 0.0)
        return carry

    lax.fori_loop(0, tb // C, chunk, 0)


def _retention(h, pos_col, inv_row, w, batch, seq, interpret):
    tb = min(TOKEN_TILE, seq)
    nt = seq // tb
    kern = functools.partial(_ret_kernel, tb=tb)
    return pl.pallas_call(
        kern,
        grid=(batch, nt),
        in_specs=[pl.BlockSpec((tb, D_MODEL), lambda b, i: (b * nt + i, 0)),
                  pl.BlockSpec((tb, 1), lambda b, i: (b * nt + i, 0)),
                  _full_spec((1, LANES)),
                  _full_spec(w.shape)],
        out_specs=pl.BlockSpec((tb, WIDTH), lambda b, i: (b * nt + i, 0)),
        out_shape=jax.ShapeDtypeStruct((batch * seq, WIDTH), MXU_DTYPE),
        scratch_shapes=[pltpu.VMEM((tb, 1536), F32), pltpu.VMEM((WIDTH, 256), F32)],
        compiler_params=_params(("parallel", "arbitrary")),
        interpret=interpret,
        name="retention",
    )(h, pos_col, inv_row, w)


def _gla_kernel(h_ref, w_ref, w2_ref, gb_ref, ng_ref, o_ref, p_scr, lg_scr, s_scr, *, tb):
    C = GLA_CHUNK

    @pl.when(pl.program_id(1) == 0)
    def _():
        s_scr[...] = jnp.zeros_like(s_scr)

    p = jnp.dot(h_ref[...], w_ref[...], preferred_element_type=F32)
    p_scr[...] = p[:, 0:1536]
    pre = _dot(p[:, 1536:1664], w2_ref[...]) + gb_ref[...]
    lg_scr[...] = -_softplus(-pre) / GLA_NORMALIZER

    tri = _tril(C)
    k_head = _iota((1, 256), 1) >> 6
    bd_mask = (_iota((WIDTH, 256), 0) >> 7) == (_iota((WIDTH, 256), 1) >> 6)
    ng = ng_ref[...]

    def chunk(c, carry):
        rows = _rows(c, C)
        q = p_scr[rows, 0:256] * GLA_DK ** -0.5
        k = p_scr[rows, 256:512]
        v = p_scr[rows, 512:1024]
        cum = _sel_dot(tri, lg_scr[rows, :])
        ref = cum[C // 2:C // 2 + 1, :]
        last = cum[C - 1:C, :]
        q_in = q * jnp.exp(cum - ref)
        k_in = k * jnp.exp(ref - cum)
        st = s_scr[...]
        y_inter = _dot_nt(q * jnp.exp(cum), st)
        for hd in range(GLA_HEADS):
            kh = jnp.where(k_head == hd, k_in, 0.0)
            sc = jnp.where(tri, _dot_nt(q_in, kh), 0.0)
            cols = slice(hd * GLA_DV, (hd + 1) * GLA_DV)
            y = _dot(sc, v[:, cols]) + y_inter[:, cols]
            y = y * lax.rsqrt(jnp.mean(y * y, axis=-1, keepdims=True) + NORM_EPS) * ng
            g = p_scr[rows, 1024 + hd * GLA_DV:1024 + (hd + 1) * GLA_DV]
            o_ref[rows, cols] = (y * _silu(g)).astype(o_ref.dtype)
        k_st = k * jnp.exp(last - cum)
        s_scr[...] = st * jnp.exp(last) + jnp.where(bd_mask, _dot_tn(v, k_st), 0.0)
        return carry

    lax.fori_loop(0, tb // C, chunk, 0)


def _gla(h, w, w2p, gb, ng, batch, seq, interpret):
    tb = min(TOKEN_TILE, seq)
    nt = seq // tb
    kern = functools.partial(_gla_kernel, tb=tb)
    return pl.pallas_call(
        kern,
        grid=(batch, nt),
        in_specs=[pl.BlockSpec((tb, D_MODEL), lambda b, i: (b * nt + i, 0)),
                  _full_spec(w.shape), _full_spec(w2p.shape), _full_spec(gb.shape),
                  _full_spec(ng.shape)],
        out_specs=pl.BlockSpec((tb, WIDTH), lambda b, i: (b * nt + i, 0)),
        out_shape=jax.ShapeDtypeStruct((batch * seq, WIDTH), MXU_DTYPE),
        scratch_shapes=[pltpu.VMEM((tb, 1536), F32), pltpu.VMEM((tb, 256), F32),
                        pltpu.VMEM((WIDTH, 256), F32)],
        compiler_params=_params(("parallel", "arbitrary")),
        interpret=interpret,
        name="gla",
    )(h, w, w2p, gb, ng)


def _ssd_kernel(h_ref, w_ref, cw_ref, cb_ref, dtb_ref, a_ref, dskip_ref, ng_ref, exp_ref, o_ref,
                raw_scr, xc_scr, z_scr, dt_scr, s_scr, *, tb):
    C = SSD_CHUNK
    NCH = 1024

    @pl.when(pl.program_id(1) == 0)
    def _():
        s_scr[...] = jnp.zeros_like(s_scr)
        raw_scr[0:CARRY_ROWS, :] = jnp.zeros((CARRY_ROWS, NCH), F32)

    p = jnp.dot(h_ref[...], w_ref[...], preferred_element_type=F32)
    raw_scr[CARRY_ROWS:CARRY_ROWS + tb, :] = p[:, 0:NCH]
    z_scr[...] = p[:, NCH:NCH + WIDTH]
    dt_scr[...] = _softplus(p[:, 1536:1664] + dtb_ref[...])
    conv = cb_ref[...] + sum(
        raw_scr[CARRY_ROWS - (SSD_CONV - 1) + j:CARRY_ROWS - (SSD_CONV - 1) + j + tb, :] * cw_ref[j:j + 1, :]
        for j in range(SSD_CONV))
    xc_scr[...] = _silu(conv)
    raw_scr[0:CARRY_ROWS, :] = raw_scr[tb:tb + CARRY_ROWS, :]

    tri = _tril(C)
    lane_lo = _iota((1, LANES), 1) < SSD_P
    expand = exp_ref[...]
    a_row = a_ref[...]

    def chunk(c, carry):
        rows = _rows(c, C)
        xs = xc_scr[rows, 0:512]
        bm = xc_scr[rows, 512:768]
        cm = xc_scr[rows, 768:1024]
        dt = dt_scr[rows, :]
        cum = _sel_dot(tri, dt * a_row)
        cum_t = cum.T
        dt_e = _dot_sel(dt, expand)
        cum_e = _dot_sel(cum, expand)
        last_e = cum_e[C - 1:C, :]
        xdt = xs * dt_e
        v_st = xdt * jnp.exp(last_e - cum_e)
        e_cum = jnp.exp(cum_e)
        scores = [_dot_nt(cm[:, g * 128:(g + 1) * 128], bm[:, g * 128:(g + 1) * 128])
                  for g in range(SSD_GROUPS)]
        y_parts = []
        for pr in range(SSD_HEADS // 2):
            sc = scores[pr // 2]
            segs = []
            for hd in (2 * pr, 2 * pr + 1):
                d = jnp.minimum(cum[:, hd:hd + 1] - cum_t[hd:hd + 1, :], 0.0)
                segs.append(sc * jnp.where(tri, jnp.exp(d), 0.0))
            xp = xdt[:, pr * LANES:(pr + 1) * LANES]
            rhs = jnp.concatenate([jnp.where(lane_lo, xp, 0.0), jnp.where(lane_lo, 0.0, xp)], axis=0)
            y_parts.append(_dot(jnp.concatenate(segs, axis=1), rhs))
        y = jnp.concatenate(y_parts, axis=1)
        y_inter = []
        for g in range(SSD_GROUPS):
            cols = slice(g * 256, (g + 1) * 256)
            st = s_scr[g]
            y_inter.append(_dot(cm[:, g * 128:(g + 1) * 128], st) * e_cum[:, cols])
            s_scr[g] = st * jnp.exp(last_e[:, cols]) + _dot_tn(bm[:, g * 128:(g + 1) * 128], v_st[:, cols])
        y = y + jnp.concatenate(y_inter, axis=1) + dskip_ref[...] * xs
        y = y * _silu(z_scr[rows, :])
        for g in range(SSD_GROUPS):
            cols = slice(g * 256, (g + 1) * 256)
            yg = y[:, cols]
            yg = yg * lax.rsqrt(jnp.mean(yg * yg, axis=-1, keepdims=True) + NORM_EPS)
            o_ref[rows, cols] = (yg * ng_ref[:, cols]).astype(o_ref.dtype)
        return carry

    lax.fori_loop(0, tb // C, chunk, 0)


def _ssd(h, w, cw, cb, dtb, a_row, dskip, ng, expand, batch, seq, interpret):
    tb = min(TOKEN_TILE, seq)
    nt = seq // tb
    kern = functools.partial(_ssd_kernel, tb=tb)
    small = [cw, cb, dtb, a_row, dskip, ng, expand]
    return pl.pallas_call(
        kern,
        grid=(batch, nt),
        in_specs=[pl.BlockSpec((tb, D_MODEL), lambda b, i: (b * nt + i, 0)), _full_spec(w.shape)]
        + [_full_spec(a.shape) for a in small],
        out_specs=pl.BlockSpec((tb, WIDTH), lambda b, i: (b * nt + i, 0)),
        out_shape=jax.ShapeDtypeStruct((batch * seq, WIDTH), MXU_DTYPE),
        scratch_shapes=[pltpu.VMEM((tb + CARRY_ROWS, 1024), F32), pltpu.VMEM((tb, 1024), F32),
                        pltpu.VMEM((tb, WIDTH), F32), pltpu.VMEM((tb, LANES), F32),
                        pltpu.VMEM((SSD_GROUPS, SSD_STATE, 256), F32)],
        compiler_params=_params(("parallel", "arbitrary")),
        interpret=interpret,
        name="ssd",
    )(h, w, *small)


def _rwkv_kernel(h_ref, w_ref, mu_ref, w0_ref, w2_ref, a0_ref, a2_ref, kk_ref, ka_ref, rk_ref,
                 lng_ref, lnb_ref, seg_ref, o_ref,
                 u_scr, g_scr, r_scr, lw_scr, k_scr, v_scr, a_scr, b_scr, y_scr, bonus_scr, st_scr, *, tb):
    C = RWKV_CHUNK
    NU = 3 * WIDTH + 2 * RWKV_RANK

    @pl.when(pl.program_id(1) == 0)
    def _():
        st_scr[...] = jnp.zeros_like(st_scr)
        u_scr[0:CARRY_ROWS, :] = jnp.zeros((CARRY_ROWS, NU), F32)

    p = jnp.dot(h_ref[...], w_ref[...], preferred_element_type=F32)
    g_scr[...] = p[:, NU:NU + WIDTH]
    u = p[:, 0:NU]
    u_scr[CARRY_ROWS:CARRY_ROWS + tb, :] = u
    u_prev = u_scr[CARRY_ROWS - 1:CARRY_ROWS - 1 + tb, :]
    xs = u + (u_prev - u) * mu_ref[...]
    u_scr[0:CARRY_ROWS, :] = u_scr[tb:tb + CARRY_ROWS, :]

    seg = seg_ref[...]

    def seg_sum(x):
        return _dot_sel(x, seg, pieces=2)

    r = xs[:, 0:512]
    k = xs[:, 512:1024]
    v = xs[:, 1024:1536]
    wa = xs[:, 1536:1664]
    w_pre = w0_ref[...] + _dot(jnp.tanh(wa), w2_ref[...])
    a_gate = jax.nn.sigmoid(a0_ref[...] + _dot(wa, a2_ref[...]))
    lw_scr[...] = -jnp.exp(-_softplus(-w_pre) - 0.5)
    kk = k * kk_ref[...]
    k = k * (1.0 + (a_gate - 1.0) * ka_ref[...])
    kk = kk / jnp.maximum(jnp.sqrt(seg_sum(kk * kk)), 1e-12)
    r_scr[...] = r
    k_scr[...] = k
    v_scr[...] = v
    a_scr[...] = -kk
    b_scr[...] = kk * a_gate
    bonus_scr[...] = seg_sum(r * k * rk_ref[...]) * v

    tri = _tril(C)
    r2, c2 = _iota((2 * C, 2 * C), 0), _iota((2 * C, 2 * C), 1)
    same = (r2 >> 6) == (c2 >> 6)
    low_s = same & ((r2 & 63) > (c2 & 63))
    low_i = same & ((r2 & 63) >= (c2 & 63))
    eye = (r2 == c2).astype(F32)
    lane_lo = _iota((1, LANES), 1) < RWKV_N

    def chunk(c, carry):
        rows = _rows(c, C)
        lw = lw_scr[rows, :]
        cum = _sel_dot(tri, lw)
        cum_p = cum - lw
        ref = cum[C // 2:C // 2 + 1, :]
        last = cum[C - 1:C, :]
        e_fwd = jnp.exp(cum - ref)
        e_bwd = jnp.exp(ref - cum)
        r_c, k_c, v_c, a_c, b_c = r_scr[rows, :], k_scr[rows, :], v_scr[rows, :], a_scr[rows, :], b_scr[rows, :]
        r_t = r_c * e_fwd
        a_t = a_c * jnp.exp(cum_p - ref)
        b_t = b_c * e_bwd
        k_t = k_c * e_bwd
        r_0 = r_c * jnp.exp(cum)
        a_0 = a_c * jnp.exp(cum_p)
        e_end = jnp.exp(last - cum)
        b_h = b_c * e_end
        k_h = k_c * e_end
        e_last = jnp.exp(last)
        for pr in range(RWKV_HEADS // 2):
            cols = slice(pr * LANES, (pr + 1) * LANES)

            def stack(x):
                xp = x[:, cols]
                return jnp.concatenate([jnp.where(lane_lo, xp, 0.0), jnp.where(lane_lo, 0.0, xp)], axis=0)

            st = st_scr[pr]
            big = _dot_nt(jnp.concatenate([stack(a_t), stack(r_t)], axis=0),
                          jnp.concatenate([stack(b_t), stack(k_t)], axis=0))
            a_ab = jnp.where(low_s, big[0:128, 0:128], 0.0)
            a_ak = jnp.where(low_s, big[0:128, 128:256], 0.0)
            a_rb = jnp.where(low_i, big[128:256, 0:128], 0.0)
            a_rk = jnp.where(low_i, big[128:256, 128:256], 0.0)
            inv = eye + a_ab
            pw = a_ab
            for _ in range(5):
                pw = _dot(pw, pw)
                inv = inv + _dot(inv, pw)
            vs = stack(v_c)
            uu = _dot(inv, _dot_nt(stack(a_0), st) + _dot(a_ak, vs))
            oo = _dot_nt(stack(r_0), st) + _dot(a_rb, uu) + _dot(a_rk, vs)
            y_scr[rows, cols] = oo[0:C, :] + oo[C:2 * C, :]
            st_scr[pr] = st * e_last[:, cols] + _dot_tn(uu, stack(b_h)) + _dot_tn(vs, stack(k_h))
        return carry

    lax.fori_loop(0, tb // C, chunk, 0)

    y = y_scr[...]
    mean = seg_sum(y) * (1.0 / RWKV_N)
    d = y - mean
    var = seg_sum(d * d) * (1.0 / RWKV_N)
    y = d * lax.rsqrt(var + RWKV_LN_EPS) * lng_ref[...] + lnb_ref[...] + bonus_scr[...]
    o_ref[...] = (y * _silu(g_scr[...])).astype(o_ref.dtype)


def _rwkv(h, w, vecs, w2p, a2p, seg, batch, seq, interpret):
    tb = min(TOKEN_TILE, seq)
    nt = seq // tb
    kern = functools.partial(_rwkv_kernel, tb=tb)
    mu, w0, a0, kk, ka, rk, lng, lnb = vecs
    ins = [mu, w0, w2p, a0, a2p, kk, ka, rk, lng, lnb, seg]
    wide = lambda: pltpu.VMEM((tb, WIDTH), F32)
    return pl.pallas_call(
        kern,
        grid=(batch, nt),
        in_specs=[pl.BlockSpec((tb, D_MODEL), lambda b, i: (b * nt + i, 0)), _full_spec(w.shape)]
        + [_full_spec(a.shape) for a in ins],
        out_specs=pl.BlockSpec((tb, WIDTH), lambda b, i: (b * nt + i, 0)),
        out_shape=jax.ShapeDtypeStruct((batch * seq, WIDTH), MXU_DTYPE),
        scratch_shapes=[pltpu.VMEM((tb + CARRY_ROWS, 3 * WIDTH + 2 * RWKV_RANK), F32)]
        + [wide() for _ in range(9)]
        + [pltpu.VMEM((RWKV_HEADS // 2, LANES, LANES), F32)],
        compiler_params=_params(("parallel", "arbitrary")),
        interpret=interpret,
        name="rwkv7",
    )(h, w, *ins)


def _memkv_kernel(mem_ref, g_ref, w_ref, k_ref, v_ref):
    x = mem_ref[0]
    y = x * lax.rsqrt(jnp.mean(x * x, axis=-1, keepdims=True) + NORM_EPS) * g_ref[...]
    kv = _dot(y, w_ref[...])
    head = _iota((1, MEM_WIDTH), 1) >> 6
    for hd in range(MEM_HEADS):
        k_ref[0, hd] = jnp.where(head == hd, kv[:, 0:MEM_WIDTH], 0.0).astype(k_ref.dtype)
        v_ref[0, hd] = jnp.where(head == hd, kv[:, MEM_WIDTH:2 * MEM_WIDTH], 0.0).astype(v_ref.dtype)


def _memkv(mem, g, w, interpret):
    b, m, d = mem.shape
    out = jax.ShapeDtypeStruct((b, MEM_HEADS, m, MEM_WIDTH), MXU_DTYPE)
    return pl.pallas_call(
        _memkv_kernel,
        grid=(b,),
        in_specs=[pl.BlockSpec((1, m, d), lambda i: (i, 0, 0)), _full_spec((1, d)), _full_spec(w.shape)],
        out_specs=[pl.BlockSpec((1, MEM_HEADS, m, MEM_WIDTH), lambda i: (i, 0, 0, 0))] * 2,
        out_shape=[out, out],
        compiler_params=_params(("parallel",)),
        interpret=interpret,
        name="mem_kv",
    )(mem, g.reshape(1, d), w)


def _merge_kernel(x_ref, h_ref, oret_ref, ogla_ref, ossd_ref, orwkv_ref, km_ref, vm_ref,
                  wq_ref, wg_ref, uret_ref, ugla_ref, ussd_ref, urwkv_ref, umem_ref, wout_ref, gn_ref,
                  xo_ref, ho_ref):
    h = h_ref[...]
    q = jnp.dot(h, wq_ref[...], preferred_element_type=F32) * MEM_HEAD_DIM ** -0.5
    o_mem = jnp.zeros(q.shape, F32)
    for hd in range(MEM_HEADS):
        s = _dot_nt(q, km_ref[0, hd])
        s = jnp.exp(s - jnp.max(s, axis=-1, keepdims=True))
        prob = s / jnp.sum(s, axis=-1, keepdims=True)
        o_mem = o_mem + _dot(prob, vm_ref[0, hd])
    branches = ((oret_ref, uret_ref), (ogla_ref, ugla_ref), (ossd_ref, ussd_ref), (orwkv_ref, urwkv_ref))
    merged = None
    for i in range(N_BRANCHES):
        gate = jax.nn.sigmoid(jnp.dot(h, wg_ref[:, i * D_MODEL:(i + 1) * D_MODEL], preferred_element_type=F32))
        if i < 4:
            o_ref, u_ref = branches[i]
            up = jnp.dot(o_ref[...], u_ref[...], preferred_element_type=F32)
        else:
            up = _dot(o_mem, umem_ref[...])
        merged = gate * up if merged is None else merged + gate * up
    x = x_ref[...] + _dot(merged, wout_ref[...])
    xo_ref[...] = x
    y = x * lax.rsqrt(jnp.mean(x * x, axis=-1, keepdims=True) + NORM_EPS) * gn_ref[...]
    ho_ref[...] = y.astype(ho_ref.dtype)


def _merge(x2d, h, o_ret, o_gla, o_ssd, o_rwkv, km, vm, wq, wg, ups, wout, g_next, h_dtype,
           batch, seq, interpret):
    tm = min(TOKEN_TILE, seq)
    nt = seq // tm
    row = lambda w: pl.BlockSpec((tm, w), lambda b, i: (b * nt + i, 0))
    kvspec = pl.BlockSpec((1,) + km.shape[1:], lambda b, i: (b, 0, 0, 0))
    weights = [wq, wg, *ups, wout, g_next]
    return pl.pallas_call(
        _merge_kernel,
        grid=(batch, nt),
        in_specs=[row(D_MODEL), row(D_MODEL), row(WIDTH), row(WIDTH), row(WIDTH), row(WIDTH), kvspec, kvspec]
        + [pl.BlockSpec(w.shape, lambda b, i: (0, 0), pipeline_mode=pl.Buffered(1)) for w in weights],
        out_specs=[row(D_MODEL), row(D_MODEL)],
        out_shape=[jax.ShapeDtypeStruct(x2d.shape, F32), jax.ShapeDtypeStruct(x2d.shape, h_dtype)],
        compiler_params=_params(("parallel", "arbitrary")),
        interpret=interpret,
        name="merge",
    )(x2d, h, o_ret, o_gla, o_ssd, o_rwkv, km, vm, *weights)


def _pad_rows(w, rows):
    return jnp.pad(w, ((0, rows - w.shape[0]), (0, 0)))


def _pad_cols(w, cols):
    return jnp.pad(w, ((0, 0), (0, cols - w.shape[1])))


def _row(v, width=None):
    v = v.reshape(1, -1).astype(F32)
    return v if width is None else _pad_cols(v, width)


def _forward(x, mem, positions, norm_g, w_in, gla_gk_w2, gla_gk_b, gla_norm_g,
             ssd_conv_w, ssd_conv_b, ssd_dt_bias, ssd_a_log, ssd_d, ssd_norm_g,
             rwkv_mu, rwkv_w0, rwkv_w2, rwkv_a0, rwkv_a2, rwkv_k_k, rwkv_k_a, rwkv_r_k,
             rwkv_ln_g, rwkv_ln_b, mem_norm_g, w_mem_kv,
             w_up_ret, w_up_gla, w_up_ssd, w_up_rwkv, w_up_mem, w_out, final_norm_g, interpret=False):
    batch, seq, d = x.shape
    depth = w_in.shape[0]
    cdt = MXU_DTYPE
    o = _IN_OFFS

    half = np.arange(RET_DK // 2)
    ret_perm = np.concatenate([hd * RET_DK + 2 * half + par for par in (0, 1) for hd in range(RET_HEADS)])
    inv = 1.0 / (ROPE_BASE ** jnp.linspace(0.0, 1.0, RET_DK // 2, dtype=F32))
    inv_row = jnp.tile(inv, RET_HEADS).reshape(1, LANES)
    pos_col = positions.reshape(batch * seq, 1)
    head_of_lane = np.arange(WIDTH) // SSD_P
    ssd_expand = jnp.asarray(np.arange(LANES)[:, None] == head_of_lane[None, :], F32)
    rwkv_seg = jnp.asarray(head_of_lane[:, None] == head_of_lane[None, :], F32)

    x2d = x.reshape(batch * seq, d)
    h = _rmsnorm(x2d, norm_g[0], cdt, interpret)
    out = None
    for l in range(depth):
        wl = w_in[l]
        col = lambda i, j=None: wl[:, o[i]:o[(i if j is None else j) + 1]]
        w_ret = jnp.concatenate([col(0)[:, ret_perm], col(1)[:, ret_perm], col(2), col(3)], axis=1).astype(cdt)
        w_gla = jnp.concatenate([col(4), col(5), col(6), col(8), _pad_cols(col(7), LANES)], axis=1).astype(cdt)
        w_ssd = jnp.concatenate([col(9), col(11), _pad_cols(col(10), LANES)], axis=1).astype(cdt)
        w_rwkv = col(12, 13).astype(cdt)
        w_q = col(14).astype(cdt)
        w_g = col(15).astype(cdt)

        o_ret = _retention(h, pos_col, inv_row, w_ret, batch, seq, interpret)
        o_gla = _gla(h, w_gla, _pad_rows(gla_gk_w2[l], LANES).astype(cdt), _row(gla_gk_b[l]),
                     _row(gla_norm_g[l]), batch, seq, interpret)
        o_ssd = _ssd(h, w_ssd, ssd_conv_w[l].astype(F32), _row(ssd_conv_b[l]), _row(ssd_dt_bias[l], LANES),
                     _row(-jnp.exp(ssd_a_log[l].astype(F32)), LANES),
                     _row(jnp.repeat(ssd_d[l], SSD_P)), _row(ssd_norm_g[l]), ssd_expand, batch, seq, interpret)
        zeros_rank = jnp.zeros((RWKV_RANK, WIDTH), F32)
        w2p = jnp.concatenate([rwkv_w2[l], zeros_rank], axis=0).astype(cdt)
        a2p = jnp.concatenate([zeros_rank, rwkv_a2[l]], axis=0).astype(cdt)
        vecs = [_row(v) for v in (rwkv_mu[l], rwkv_w0[l], rwkv_a0[l], rwkv_k_k[l], rwkv_k_a[l],
                                  rwkv_r_k[l], rwkv_ln_g[l], rwkv_ln_b[l])]
        o_rwkv = _rwkv(h, w_rwkv, vecs, w2p, a2p, rwkv_seg, batch, seq, interpret)
        km, vm = _memkv(mem, mem_norm_g[l], w_mem_kv[l].astype(cdt), interpret)

        last = l == depth - 1
        g_next = final_norm_g if last else norm_g[l + 1]
        ups = [w.astype(cdt) for w in (w_up_ret[l], w_up_gla[l], w_up_ssd[l], w_up_rwkv[l], w_up_mem[l])]
        x2d, h = _merge(x2d, h, o_ret, o_gla, o_ssd, o_rwkv, km, vm, w_q, w_g, ups, w_out[l].astype(cdt),
                        _row(g_next), F32 if last else cdt, batch, seq, interpret)
        out = h
    return out.reshape(batch, seq, d)


def kernel(x, mem, positions, norm_g, w_in, gla_gk_w2, gla_gk_b, gla_norm_g, ssd_conv_w, ssd_conv_b, ssd_dt_bias, ssd_a_log, ssd_d, ssd_norm_g, rwkv_mu, rwkv_w0, rwkv_w2, rwkv_a0, rwkv_a2, rwkv_k_k, rwkv_k_a, rwkv_r_k, rwkv_ln_g, rwkv_ln_b, mem_norm_g, w_mem_kv, w_up_ret, w_up_gla, w_up_ssd, w_up_rwkv, w_up_mem, w_out, final_norm_g):
    return _forward(x, mem, positions, norm_g, w_in, gla_gk_w2, gla_gk_b, gla_norm_g,
                    ssd_conv_w, ssd_conv_b, ssd_dt_bias, ssd_a_log, ssd_d, ssd_norm_g,
                    rwkv_mu, rwkv_w0, rwkv_w2, rwkv_a0, rwkv_a2, rwkv_k_k, rwkv_k_a, rwkv_r_k,
                    rwkv_ln_g, rwkv_ln_b, mem_norm_g, w_mem_kv,
                    w_up_ret, w_up_gla, w_up_ssd, w_up_rwkv, w_up_mem, w_out, final_norm_g)
```

```python
import functools
import math

import jax
import jax.numpy as jnp
import numpy as np
from jax import lax
from jax.experimental import pallas as pl
from jax.experimental.pallas import tpu as pltpu

F32 = jnp.float32
BF16 = jnp.bfloat16
MXU_DTYPE = jnp.bfloat16

D_MODEL = 1024
WIDTH = 512
NORM_EPS = 1e-6
N_BRANCHES = 5

RET_HEADS, RET_DK, RET_DV, RET_CHUNK = 4, 64, 128, 128
ROPE_BASE = 10000.0
GLA_HEADS, GLA_DK, GLA_DV, GLA_RANK, GLA_NORMALIZER, GLA_CHUNK = 4, 64, 128, 16, 16.0, 64
SSD_HEADS, SSD_P, SSD_GROUPS, SSD_STATE, SSD_CONV, SSD_CHUNK = 8, 64, 2, 128, 4, 128
RWKV_HEADS, RWKV_N, RWKV_RANK, RWKV_CHUNK = 8, 64, 64, 64
RWKV_LN_EPS = 64e-5
RWKV_GROUP = 8
MEM_HEADS, MEM_HEAD_DIM, MEM_WIDTH = 4, 64, 256

LANES = 128
CARRY_ROWS = 8
TOKEN_TILE = 512
VMEM_LIMIT = 56 * 1024 * 1024

_IN_SIZES = (256, 256, 512, 512, 256, 256, 512, 16, 512, 1024, 8, 512, 1664, 512, 256, 5120)
_IN_OFFS = tuple(int(v) for v in np.cumsum((0,) + _IN_SIZES))


def _dot(a, b):
    return jnp.dot(a.astype(MXU_DTYPE), b.astype(MXU_DTYPE), preferred_element_type=F32)


def _dot_nt(a, b):
    return lax.dot_general(a.astype(MXU_DTYPE), b.astype(MXU_DTYPE), (((1,), (1,)), ((), ())),
                           preferred_element_type=F32)


def _dot_tn(a, b):
    return lax.dot_general(a.astype(MXU_DTYPE), b.astype(MXU_DTYPE), (((0,), (0,)), ((), ())),
                           preferred_element_type=F32)


def _split3(x):
    hi = x.astype(BF16)
    r1 = x - hi.astype(F32)
    mid = r1.astype(BF16)
    lo = (r1 - mid.astype(F32)).astype(BF16)
    return hi, mid, lo


def _sel_dot(sel, x):
    s = sel.astype(BF16)
    return sum(jnp.dot(s, p, preferred_element_type=F32) for p in _split3(x))


def _dot_sel(x, sel, pieces=3):
    s = sel.astype(BF16)
    return sum(jnp.dot(p, s, preferred_element_type=F32) for p in _split3(x)[:pieces])


def _iota(shape, dim):
    return lax.broadcasted_iota(jnp.int32, shape, dim)


def _tril(n, strict=False):
    r, c = _iota((n, n), 0), _iota((n, n), 1)
    return (r > c) if strict else (r >= c)


def _silu(x):
    return x * jax.nn.sigmoid(x)


def _softplus(x):
    return jnp.maximum(x, 0.0) + jnp.log1p(jnp.exp(-jnp.abs(x)))


def _rows(c, n):
    return pl.ds(pl.multiple_of(c * n, n), n)


def _full_spec(shape):
    zeros = (0,) * len(shape)
    return pl.BlockSpec(shape, lambda *_: zeros)


def _params(semantics):
    return pltpu.CompilerParams(dimension_semantics=semantics, vmem_limit_bytes=VMEM_LIMIT)


def _rmsnorm_kernel(x_ref, g_ref, o_ref):
    x = x_ref[...]
    y = x * lax.rsqrt(jnp.mean(x * x, axis=-1, keepdims=True) + NORM_EPS)
    o_ref[...] = (y * g_ref[...]).astype(o_ref.dtype)


def _rmsnorm(x2d, g, out_dtype, interpret):
    m, d = x2d.shape
    tm = min(1024, m)
    return pl.pallas_call(
        _rmsnorm_kernel,
        grid=(m // tm,),
        in_specs=[pl.BlockSpec((tm, d), lambda i: (i, 0)), _full_spec((1, d))],
        out_specs=pl.BlockSpec((tm, d), lambda i: (i, 0)),
        out_shape=jax.ShapeDtypeStruct((m, d), out_dtype),
        compiler_params=_params(("parallel",)),
        interpret=interpret,
        name="rmsnorm",
    )(x2d, g.reshape(1, d))


def _ret_kernel(h_ref, pos_ref, inv_ref, w_ref, o_ref, p_scr, s_scr, *, tb):
    C = RET_CHUNK

    @pl.when(pl.program_id(1) == 0)
    def _():
        s_scr[...] = jnp.zeros_like(s_scr)

    p = jnp.dot(h_ref[...], w_ref[...], preferred_element_type=F32)
    ang = pos_ref[...].astype(F32) * inv_ref[...]
    cos, sin = jnp.cos(ang), jnp.sin(ang)
    q1, q2 = p[:, 0:128], p[:, 128:256]
    k1, k2 = p[:, 256:384] * RET_DK ** -0.5, p[:, 384:512] * RET_DK ** -0.5
    p_scr[:, 0:128] = q1 * cos - q2 * sin
    p_scr[:, 128:256] = q2 * cos + q1 * sin
    p_scr[:, 256:384] = k1 * cos - k2 * sin
    p_scr[:, 384:512] = k2 * cos + k1 * sin
    p_scr[:, 512:1536] = p[:, 512:1536]

    def log_gamma(head):
        return jnp.log(1.0 - jnp.exp2(-5.0 - head.astype(F32)))

    qk_head = (_iota((1, 256), 1) >> 5) & 3
    lg_lane = log_gamma(qk_head)
    tau = _iota((C, 1), 0).astype(F32)
    dq = jnp.exp(lg_lane * (tau + 1.0))
    dk = jnp.exp(lg_lane * (C - 1.0 - tau))
    ds = jnp.exp(lg_lane * float(C))
    diff = (_iota((C, C), 0) - _iota((C, C), 1)).astype(F32)
    causal = _tril(C)
    bd_mask = (_iota((WIDTH, 256), 0) >> 7) == ((_iota((WIDTH, 256), 1) >> 5) & 3)

    def chunk(c, carry):
        rows = _rows(c, C)
        q = p_scr[rows, 0:256]
        k = p_scr[rows, 256:512]
        v = p_scr[rows, 512:1024]
        st = s_scr[...]
        y_inter = _dot_nt(q * dq, st)
        for hd in range(RET_HEADS):
            lg = math.log(1.0 - 2.0 ** (-5.0 - hd))
            seg = jnp.where(causal, jnp.exp(lg * diff), 0.0)
            kh = jnp.where(qk_head == hd, k, 0.0)
            sc = _dot_nt(q, kh) * seg
            cols = slice(hd * RET_DV, (hd + 1) * RET_DV)
            y = _dot(sc, v[:, cols]) + y_inter[:, cols]
            y = y * lax.rsqrt(jnp.mean(y * y, axis=-1, keepdims=True) + NORM_EPS)
            g = p_scr[rows, 1024 + hd * RET_DV:1024 + (hd + 1) * RET_DV]
            o_ref[rows, cols] = (y * _silu(g)).astype(o_ref.dtype)
        s_scr[...] = st * ds + jnp.where(bd_mask, _dot_tn(v, k * dk), 0.0)
        return carry

    lax.fori_loop(0, tb // C, chunk, 0)


def _retention(h, pos_col, inv_row, w, batch, seq, interpret):
    tb = min(TOKEN_TILE, seq)
    nt = seq // tb
    kern = functools.partial(_ret_kernel, tb=tb)
    return pl.pallas_call(
        kern,
        grid=(batch, nt),
        in_specs=[pl.BlockSpec((tb, D_MODEL), lambda b, i: (b * nt + i, 0)),
                  pl.BlockSpec((tb, 1), lambda b, i: (b * nt + i, 0)),
                  _full_spec((1, LANES)),
                  _full_spec(w.shape)],
        out_specs=pl.BlockSpec((tb, WIDTH), lambda b, i: (b * nt + i, 0)),
        out_shape=jax.ShapeDtypeStruct((batch * seq, WIDTH), MXU_DTYPE),
        scratch_shapes=[pltpu.VMEM((tb, 1536), F32), pltpu.VMEM((WIDTH, 256), F32)],
        compiler_params=_params(("parallel", "arbitrary")),
        interpret=interpret,
        name="retention",
    )(h, pos_col, inv_row, w)


def _gla_kernel(h_ref, w_ref, w2_ref, gb_ref, ng_ref, o_ref, p_scr, lg_scr, s_scr, *, tb):
    C = GLA_CHUNK

    @pl.when(pl.program_id(1) == 0)
    def _():
        s_scr[...] = jnp.zeros_like(s_scr)

    p = jnp.dot(h_ref[...], w_ref[...], preferred_element_type=F32)
    p_scr[...] = p[:, 0:1536]
    pre = _dot(p[:, 1536:1664], w2_ref[...]) + gb_ref[...]
    lg_scr[...] = -_softplus(-pre) / GLA_NORMALIZER

    tri = _tril(C)
    k_head = _iota((1, 256), 1) >> 6
    bd_mask = (_iota((WIDTH, 256), 0) >> 7) == (_iota((WIDTH, 256), 1) >> 6)
    ng = ng_ref[...]

    def chunk(c, carry):
        rows = _rows(c, C)
        q = p_scr[rows, 0:256] * GLA_DK ** -0.5
        k = p_scr[rows, 256:512]
        v = p_scr[rows, 512:1024]
        cum = _sel_dot(tri, lg_scr[rows, :])
        ref = cum[C // 2:C // 2 + 1, :]
        last = cum[C - 1:C, :]
        q_in = q * jnp.exp(cum - ref)
        k_in = k * jnp.exp(ref - cum)
        st = s_scr[...]
        y_inter = _dot_nt(q * jnp.exp(cum), st)
        for hd in range(GLA_HEADS):
            kh = jnp.where(k_head == hd, k_in, 0.0)
            sc = jnp.where(tri, _dot_nt(q_in, kh), 0.0)
            cols = slice(hd * GLA_DV, (hd + 1) * GLA_DV)
            y = _dot(sc, v[:, cols]) + y_inter[:, cols]
            y = y * lax.rsqrt(jnp.mean(y * y, axis=-1, keepdims=True) + NORM_EPS) * ng
            g = p_scr[rows, 1024 + hd * GLA_DV:1024 + (hd + 1) * GLA_DV]
            o_ref[rows, cols] = (y * _silu(g)).astype(o_ref.dtype)
        k_st = k * jnp.exp(last - cum)
        s_scr[...] = st * jnp.exp(last) + jnp.where(bd_mask, _dot_tn(v, k_st), 0.0)
        return carry

    lax.fori_loop(0, tb // C, chunk, 0)


def _gla(h, w, w2p, gb, ng, batch, seq, interpret):
    tb = min(TOKEN_TILE, seq)
    nt = seq // tb
    kern = functools.partial(_gla_kernel, tb=tb)
    return pl.pallas_call(
        kern,
        grid=(batch, nt),
        in_specs=[pl.BlockSpec((tb, D_MODEL), lambda b, i: (b * nt + i, 0)),
                  _full_spec(w.shape), _full_spec(w2p.shape), _full_spec(gb.shape),
                  _full_spec(ng.shape)],
        out_specs=pl.BlockSpec((tb, WIDTH), lambda b, i: (b * nt + i, 0)),
        out_shape=jax.ShapeDtypeStruct((batch * seq, WIDTH), MXU_DTYPE),
        scratch_shapes=[pltpu.VMEM((tb, 1536), F32), pltpu.VMEM((tb, 256), F32),
                        pltpu.VMEM((WIDTH, 256), F32)],
        compiler_params=_params(("parallel", "arbitrary")),
        interpret=interpret,
        name="gla",
    )(h, w, w2p, gb, ng)


def _ssd_kernel(h_ref, w_ref, cw_ref, cb_ref, dtb_ref, a_ref, dskip_ref, ng_ref, exp_ref, o_ref,
                raw_scr, xc_scr, z_scr, dt_scr, s_scr, *, tb):
    C = SSD_CHUNK
    NCH = 1024

    @pl.when(pl.program_id(1) == 0)
    def _():
        s_scr[...] = jnp.zeros_like(s_scr)
        raw_scr[0:CARRY_ROWS, :] = jnp.zeros((CARRY_ROWS, NCH), F32)

    p = jnp.dot(h_ref[...], w_ref[...], preferred_element_type=F32)
    raw_scr[CARRY_ROWS:CARRY_ROWS + tb, :] = p[:, 0:NCH]
    z_scr[...] = p[:, NCH:NCH + WIDTH]
    dt_scr[...] = _softplus(p[:, 1536:1664] + dtb_ref[...])
    conv = cb_ref[...] + sum(
        raw_scr[CARRY_ROWS - (SSD_CONV - 1) + j:CARRY_ROWS - (SSD_CONV - 1) + j + tb, :] * cw_ref[j:j + 1, :]
        for j in range(SSD_CONV))
    xc_scr[...] = _silu(conv)
    raw_scr[0:CARRY_ROWS, :] = raw_scr[tb:tb + CARRY_ROWS, :]

    tri = _tril(C)
    lane_lo = _iota((1, LANES), 1) < SSD_P
    expand = exp_ref[...]
    a_row = a_ref[...]

    def chunk(c, carry):
        rows = _rows(c, C)
        xs = xc_scr[rows, 0:512]
        bm = xc_scr[rows, 512:768]
        cm = xc_scr[rows, 768:1024]
        dt = dt_scr[rows, :]
        cum = _sel_dot(tri, dt * a_row)
        cum_t = cum.T
        dt_e = _dot_sel(dt, expand)
        cum_e = _dot_sel(cum, expand)
        last_e = cum_e[C - 1:C, :]
        xdt = xs * dt_e
        v_st = xdt * jnp.exp(last_e - cum_e)
        e_cum = jnp.exp(cum_e)
        scores = [_dot_nt(cm[:, g * 128:(g + 1) * 128], bm[:, g * 128:(g + 1) * 128])
                  for g in range(SSD_GROUPS)]
        y_parts = []
        for pr in range(SSD_HEADS // 2):
            sc = scores[pr // 2]
            segs = []
            for hd in (2 * pr, 2 * pr + 1):
                d = jnp.minimum(cum[:, hd:hd + 1] - cum_t[hd:hd + 1, :], 0.0)
                segs.append(sc * jnp.where(tri, jnp.exp(d), 0.0))
            xp = xdt[:, pr * LANES:(pr + 1) * LANES]
            rhs = jnp.concatenate([jnp.where(lane_lo, xp, 0.0), jnp.where(lane_lo, 0.0, xp)], axis=0)
            y_parts.append(_dot(jnp.concatenate(segs, axis=1), rhs))
        y = jnp.concatenate(y_parts, axis=1)
        y_inter = []
        for g in range(SSD_GROUPS):
            cols = slice(g * 256, (g + 1) * 256)
            st = s_scr[g]
            y_inter.append(_dot(cm[:, g * 128:(g + 1) * 128], st) * e_cum[:, cols])
            s_scr[g] = st * jnp.exp(last_e[:, cols]) + _dot_tn(bm[:, g * 128:(g + 1) * 128], v_st[:, cols])
        y = y + jnp.concatenate(y_inter, axis=1) + dskip_ref[...] * xs
        y = y * _silu(z_scr[rows, :])
        for g in range(SSD_GROUPS):
            cols = slice(g * 256, (g + 1) * 256)
            yg = y[:, cols]
            yg = yg * lax.rsqrt(jnp.mean(yg * yg, axis=-1, keepdims=True) + NORM_EPS)
            o_ref[rows, cols] = (yg * ng_ref[:, cols]).astype(o_ref.dtype)
        return carry

    lax.fori_loop(0, tb // C, chunk, 0)


def _ssd(h, w, cw, cb, dtb, a_row, dskip, ng, expand, batch, seq, interpret):
    tb = min(TOKEN_TILE, seq)
    nt = seq // tb
    kern = functools.partial(_ssd_kernel, tb=tb)
    small = [cw, cb, dtb, a_row, dskip, ng, expand]
    return pl.pallas_call(
        kern,
        grid=(batch, nt),
        in_specs=[pl.BlockSpec((tb, D_MODEL), lambda b, i: (b * nt + i, 0)), _full_spec(w.shape)]
        + [_full_spec(a.shape) for a in small],
        out_specs=pl.BlockSpec((tb, WIDTH), lambda b, i: (b * nt + i, 0)),
        out_shape=jax.ShapeDtypeStruct((batch * seq, WIDTH), MXU_DTYPE),
        scratch_shapes=[pltpu.VMEM((tb + CARRY_ROWS, 1024), F32), pltpu.VMEM((tb, 1024), F32),
                        pltpu.VMEM((tb, WIDTH), F32), pltpu.VMEM((tb, LANES), F32),
                        pltpu.VMEM((SSD_GROUPS, SSD_STATE, 256), F32)],
        compiler_params=_params(("parallel", "arbitrary")),
        interpret=interpret,
        name="ssd",
    )(h, w, *small)


def _rwkv_kernel(h_ref, w_ref, mu_ref, w0_ref, w2_ref, a0_ref, a2_ref, kk_ref, ka_ref, rk_ref,
                 lng_ref, lnb_ref, seg_ref, o_ref,
                 u_scr, g_scr, r_scr, lw_scr, k_scr, v_scr, a_scr, b_scr, y_scr, bonus_scr, st_scr, *, tb):
    C = RWKV_CHUNK
    NU = 3 * WIDTH + 2 * RWKV_RANK

    @pl.when(pl.program_id(1) == 0)
    def _():
        st_scr[...] = jnp.zeros_like(st_scr)
        u_scr[0:CARRY_ROWS, :] = jnp.zeros((CARRY_ROWS, NU), F32)

    p = jnp.dot(h_ref[...], w_ref[...], preferred_element_type=F32)
    g_scr[...] = p[:, NU:NU + WIDTH]
    u = p[:, 0:NU]
    u_scr[CARRY_ROWS:CARRY_ROWS + tb, :] = u
    u_prev = u_scr[CARRY_ROWS - 1:CARRY_ROWS - 1 + tb, :]
    xs = u + (u_prev - u) * mu_ref[...]
    u_scr[0:CARRY_ROWS, :] = u_scr[tb:tb + CARRY_ROWS, :]

    seg = seg_ref[...]

    def seg_sum(x, pieces):
        return jnp.concatenate([_dot_sel(x[:, j * 256:(j + 1) * 256], seg, pieces) for j in range(2)], axis=1)

    r = xs[:, 0:512]
    k = xs[:, 512:1024]
    v = xs[:, 1024:1536]
    wa = xs[:, 1536:1664]
    w_pre = w0_ref[...] + _dot(jnp.tanh(wa), w2_ref[...])
    a_gate = jax.nn.sigmoid(a0_ref[...] + _dot(wa, a2_ref[...]))
    lw_scr[...] = -jnp.exp(-_softplus(-w_pre) - 0.5)
    kk = k * kk_ref[...]
    k = k * (1.0 + (a_gate - 1.0) * ka_ref[...])
    kk = kk / jnp.maximum(jnp.sqrt(seg_sum(kk * kk, 1)), 1e-12)
    r_scr[...] = r
    k_scr[...] = k
    v_scr[...] = v
    a_scr[...] = -kk
    b_scr[...] = kk * a_gate
    bonus_scr[...] = seg_sum(r * k * rk_ref[...], 1) * v

    tri = _tril(C)
    r2, c2 = _iota((2 * C, 2 * C), 0), _iota((2 * C, 2 * C), 1)
    same = (r2 >> 6) == (c2 >> 6)
    low_s = same & ((r2 & 63) > (c2 & 63))
    low_i = same & ((r2 & 63) >= (c2 & 63))
    eye = (r2 == c2).astype(F32)
    lane_lo = _iota((1, LANES), 1) < RWKV_N

    pairs = range(RWKV_HEADS // 2)

    def stack(x, pr):
        xp = x[:, pr * LANES:(pr + 1) * LANES]
        return jnp.concatenate([jnp.where(lane_lo, xp, 0.0), jnp.where(lane_lo, 0.0, xp)], axis=0)

    def group(gi, carry):
        lanes = [(u, pr) for u in range(RWKV_GROUP) for pr in pairs]
        rows, a_0, r_0, v_c, b_h, k_h, e_last, lhs, rhs_t = [], [], [], [], [], [], [], {}, {}
        for u in range(RWKV_GROUP):
            rw = _rows(gi * RWKV_GROUP + u, C)
            lw = lw_scr[rw, :]
            cum = _sel_dot(tri, lw)
            cum_p = cum - lw
            ref = cum[C // 2:C // 2 + 1, :]
            last = cum[C - 1:C, :]
            e_fwd = jnp.exp(cum - ref)
            e_bwd = jnp.exp(ref - cum)
            e_end = jnp.exp(last - cum)
            r_c, k_c, a_c, b_c = r_scr[rw, :], k_scr[rw, :], a_scr[rw, :], b_scr[rw, :]
            r_t = r_c * e_fwd
            a_t = a_c * jnp.exp(cum_p - ref)
            b_t = b_c * e_bwd
            k_t = k_c * e_bwd
            for pr in pairs:
                lhs[u, pr] = jnp.concatenate([stack(a_t, pr), stack(r_t, pr)], axis=0)
                rhs_t[u, pr] = jnp.concatenate([stack(b_t, pr), stack(k_t, pr)], axis=0)
            rows.append(rw)
            a_0.append(a_c * jnp.exp(cum_p))
            r_0.append(r_c * jnp.exp(cum))
            v_c.append(v_scr[rw, :])
            b_h.append(b_c * e_end)
            k_h.append(k_c * e_end)
            e_last.append(jnp.exp(last))
        big = [_dot_nt(lhs[ln], rhs_t[ln]) for ln in lanes]
        a_ab = [jnp.where(low_s, m[0:128, 0:128], 0.0) for m in big]
        a_ak = [jnp.where(low_s, m[0:128, 128:256], 0.0) for m in big]
        a_rb = [jnp.where(low_i, m[128:256, 0:128], 0.0) for m in big]
        a_rk = [jnp.where(low_i, m[128:256, 128:256], 0.0) for m in big]
        inv = [eye + m for m in a_ab]
        pw = [_dot(m, m) for m in a_ab]
        for _ in range(4):
            prod = [_dot(p, jnp.concatenate([p, t], axis=1)) for p, t in zip(pw, inv)]
            pw = [m[:, 0:128] for m in prod]
            inv = [t + m[:, 128:256] for t, m in zip(inv, prod)]
        inv = [t + _dot(p, t) for p, t in zip(pw, inv)]
        vs = [stack(v_c[u], pr) for u, pr in lanes]
        akv = [_dot(m, x) for m, x in zip(a_ak, vs)]
        a_r = [jnp.concatenate([m, n], axis=1) for m, n in zip(a_rb, a_rk)]
        bk_h = [jnp.concatenate([stack(b_h[u], pr), stack(k_h[u], pr)], axis=0) for u, pr in lanes]

        st = [st_scr[pr] for pr in pairs]
        for u in range(RWKV_GROUP):
            ix = [u * len(pairs) + pr for pr in pairs]
            rhs = [_dot_nt(stack(a_0[u], pr), st[pr]) + akv[i] for pr, i in zip(pairs, ix)]
            uv = [jnp.concatenate([_dot(inv[i], m), vs[i]], axis=0) for m, i in zip(rhs, ix)]
            oo = [_dot_nt(stack(r_0[u], pr), st[pr]) + _dot(a_r[i], uv[pr]) for pr, i in zip(pairs, ix)]
            st = [st[pr] * e_last[u][:, pr * LANES:(pr + 1) * LANES] + _dot_tn(uv[pr], bk_h[i])
                  for pr, i in zip(pairs, ix)]
            for pr in pairs:
                y_scr[rows[u], pr * LANES:(pr + 1) * LANES] = oo[pr][0:C, :] + oo[pr][C:2 * C, :]
        for pr in pairs:
            st_scr[pr] = st[pr]
        return carry

    lax.fori_loop(0, tb // (C * RWKV_GROUP), group, 0)

    y = y_scr[...]
    mean = seg_sum(y, 2) * (1.0 / RWKV_N)
    d = y - mean
    var = seg_sum(d * d, 1) * (1.0 / RWKV_N)
    y = d * lax.rsqrt(var + RWKV_LN_EPS) * lng_ref[...] + lnb_ref[...] + bonus_scr[...]
    o_ref[...] = (y * _silu(g_scr[...])).astype(o_ref.dtype)


def _rwkv(h, w, vecs, w2p, a2p, seg, batch, seq, interpret):
    tb = min(TOKEN_TILE, seq)
    nt = seq // tb
    kern = functools.partial(_rwkv_kernel, tb=tb)
    mu, w0, a0, kk, ka, rk, lng, lnb = vecs
    ins = [mu, w0, w2p, a0, a2p, kk, ka, rk, lng, lnb, seg]
    wide = lambda: pltpu.VMEM((tb, WIDTH), F32)
    return pl.pallas_call(
        kern,
        grid=(batch, nt),
        in_specs=[pl.BlockSpec((tb, D_MODEL), lambda b, i: (b * nt + i, 0)), _full_spec(w.shape)]
        + [_full_spec(a.shape) for a in ins],
        out_specs=pl.BlockSpec((tb, WIDTH), lambda b, i: (b * nt + i, 0)),
        out_shape=jax.ShapeDtypeStruct((batch * seq, WIDTH), MXU_DTYPE),
        scratch_shapes=[pltpu.VMEM((tb + CARRY_ROWS, 3 * WIDTH + 2 * RWKV_RANK), F32)]
        + [wide() for _ in range(9)]
        + [pltpu.VMEM((RWKV_HEADS // 2, LANES, LANES), F32)],
        compiler_params=_params(("parallel", "arbitrary")),
        interpret=interpret,
        name="rwkv7",
    )(h, w, *ins)


def _memkv_kernel(mem_ref, g_ref, w_ref, k_ref, v_ref):
    x = mem_ref[0]
    y = x * lax.rsqrt(jnp.mean(x * x, axis=-1, keepdims=True) + NORM_EPS) * g_ref[...]
    kv = _dot(y, w_ref[...])
    head = _iota((1, MEM_WIDTH), 1) >> 6
    for hd in range(MEM_HEADS):
        k_ref[0, hd] = jnp.where(head == hd, kv[:, 0:MEM_WIDTH], 0.0).astype(k_ref.dtype)
        v_ref[0, hd] = jnp.where(head == hd, kv[:, MEM_WIDTH:2 * MEM_WIDTH], 0.0).astype(v_ref.dtype)


def _memkv(mem, g, w, interpret):
    b, m, d = mem.shape
    out = jax.ShapeDtypeStruct((b, MEM_HEADS, m, MEM_WIDTH), MXU_DTYPE)
    return pl.pallas_call(
        _memkv_kernel,
        grid=(b,),
        in_specs=[pl.BlockSpec((1, m, d), lambda i: (i, 0, 0)), _full_spec((1, d)), _full_spec(w.shape)],
        out_specs=[pl.BlockSpec((1, MEM_HEADS, m, MEM_WIDTH), lambda i: (i, 0, 0, 0))] * 2,
        out_shape=[out, out],
        compiler_params=_params(("parallel",)),
        interpret=interpret,
        name="mem_kv",
    )(mem, g.reshape(1, d), w)


def _merge_kernel(x_ref, h_ref, oret_ref, ogla_ref, ossd_ref, orwkv_ref, km_ref, vm_ref,
                  wq_ref, wg_ref, uret_ref, ugla_ref, ussd_ref, urwkv_ref, umem_ref, wout_ref, gn_ref,
                  xo_ref, ho_ref):
    h = h_ref[...]
    q = jnp.dot(h, wq_ref[...], preferred_element_type=F32) * MEM_HEAD_DIM ** -0.5
    o_mem = jnp.zeros(q.shape, F32)
    for hd in range(MEM_HEADS):
        s = _dot_nt(q, km_ref[0, hd])
        s = jnp.exp(s - jnp.max(s, axis=-1, keepdims=True))
        prob = s / jnp.sum(s, axis=-1, keepdims=True)
        o_mem = o_mem + _dot(prob, vm_ref[0, hd])
    branches = ((oret_ref, uret_ref), (ogla_ref, ugla_ref), (ossd_ref, ussd_ref), (orwkv_ref, urwkv_ref))
    merged = None
    for i in range(N_BRANCHES):
        gate = jax.nn.sigmoid(jnp.dot(h, wg_ref[:, i * D_MODEL:(i + 1) * D_MODEL], preferred_element_type=F32))
        if i < 4:
            o_ref, u_ref = branches[i]
            up = jnp.dot(o_ref[...], u_ref[...], preferred_element_type=F32)
        else:
            up = _dot(o_mem, umem_ref[...])
        merged = gate * up if merged is None else merged + gate * up
    x = x_ref[...] + _dot(merged, wout_ref[...])
    xo_ref[...] = x
    y = x * lax.rsqrt(jnp.mean(x * x, axis=-1, keepdims=True) + NORM_EPS) * gn_ref[...]
    ho_ref[...] = y.astype(ho_ref.dtype)


def _merge(x2d, h, o_ret, o_gla, o_ssd, o_rwkv, km, vm, wq, wg, ups, wout, g_next, h_dtype,
           batch, seq, interpret):
    tm = min(TOKEN_TILE, seq)
    nt = seq // tm
    row = lambda w: pl.BlockSpec((tm, w), lambda b, i: (b * nt + i, 0))
    kvspec = pl.BlockSpec((1,) + km.shape[1:], lambda b, i: (b, 0, 0, 0))
    weights = [wq, wg, *ups, wout, g_next]
    return pl.pallas_call(
        _merge_kernel,
        grid=(batch, nt),
        in_specs=[row(D_MODEL), row(D_MODEL), row(WIDTH), row(WIDTH), row(WIDTH), row(WIDTH), kvspec, kvspec]
        + [pl.BlockSpec(w.shape, lambda b, i: (0, 0), pipeline_mode=pl.Buffered(1)) for w in weights],
        out_specs=[row(D_MODEL), row(D_MODEL)],
        out_shape=[jax.ShapeDtypeStruct(x2d.shape, F32), jax.ShapeDtypeStruct(x2d.shape, h_dtype)],
        compiler_params=_params(("parallel", "arbitrary")),
        interpret=interpret,
        name="merge",
    )(x2d, h, o_ret, o_gla, o_ssd, o_rwkv, km, vm, *weights)


def _pad_rows(w, rows):
    return jnp.pad(w, ((0, rows - w.shape[0]), (0, 0)))


def _pad_cols(w, cols):
    return jnp.pad(w, ((0, 0), (0, cols - w.shape[1])))


def _row(v, width=None):
    v = v.reshape(1, -1).astype(F32)
    return v if width is None else _pad_cols(v, width)


def _forward(x, mem, positions, norm_g, w_in, gla_gk_w2, gla_gk_b, gla_norm_g,
             ssd_conv_w, ssd_conv_b, ssd_dt_bias, ssd_a_log, ssd_d, ssd_norm_g,
             rwkv_mu, rwkv_w0, rwkv_w2, rwkv_a0, rwkv_a2, rwkv_k_k, rwkv_k_a, rwkv_r_k,
             rwkv_ln_g, rwkv_ln_b, mem_norm_g, w_mem_kv,
             w_up_ret, w_up_gla, w_up_ssd, w_up_rwkv, w_up_mem, w_out, final_norm_g, interpret=False):
    batch, seq, d = x.shape
    depth = w_in.shape[0]
    cdt = MXU_DTYPE
    o = _IN_OFFS

    half = np.arange(RET_DK // 2)
    ret_perm = np.concatenate([hd * RET_DK + 2 * half + par for par in (0, 1) for hd in range(RET_HEADS)])
    inv = 1.0 / (ROPE_BASE ** jnp.linspace(0.0, 1.0, RET_DK // 2, dtype=F32))
    inv_row = jnp.tile(inv, RET_HEADS).reshape(1, LANES)
    pos_col = positions.reshape(batch * seq, 1)
    head_of_lane = np.arange(WIDTH) // SSD_P
    ssd_expand = jnp.asarray(np.arange(LANES)[:, None] == head_of_lane[None, :], F32)
    rwkv_seg = jnp.asarray(head_of_lane[:256, None] == head_of_lane[None, :256], F32)

    x2d = x.reshape(batch * seq, d)
    h = _rmsnorm(x2d, norm_g[0], cdt, interpret)
    out = None
    for l in range(depth):
        wl = w_in[l]
        col = lambda i, j=None: wl[:, o[i]:o[(i if j is None else j) + 1]]
        w_ret = jnp.concatenate([col(0)[:, ret_perm], col(1)[:, ret_perm], col(2), col(3)], axis=1).astype(cdt)
        w_gla = jnp.concatenate([col(4), col(5), col(6), col(8), _pad_cols(col(7), LANES)], axis=1).astype(cdt)
        w_ssd = jnp.concatenate([col(9), col(11), _pad_cols(col(10), LANES)], axis=1).astype(cdt)
        w_rwkv = col(12, 13).astype(cdt)
        w_q = col(14).astype(cdt)
        w_g = col(15).astype(cdt)

        o_ret = _retention(h, pos_col, inv_row, w_ret, batch, seq, interpret)
        o_gla = _gla(h, w_gla, _pad_rows(gla_gk_w2[l], LANES).astype(cdt), _row(gla_gk_b[l]),
                     _row(gla_norm_g[l]), batch, seq, interpret)
        o_ssd = _ssd(h, w_ssd, ssd_conv_w[l].astype(F32), _row(ssd_conv_b[l]), _row(ssd_dt_bias[l], LANES),
                     _row(-jnp.exp(ssd_a_log[l].astype(F32)), LANES),
                     _row(jnp.repeat(ssd_d[l], SSD_P)), _row(ssd_norm_g[l]), ssd_expand, batch, seq, interpret)
        zeros_rank = jnp.zeros((RWKV_RANK, WIDTH), F32)
        w2p = jnp.concatenate([rwkv_w2[l], zeros_rank], axis=0).astype(cdt)
        a2p = jnp.concatenate([zeros_rank, rwkv_a2[l]], axis=0).astype(cdt)
        vecs = [_row(v) for v in (rwkv_mu[l], rwkv_w0[l], rwkv_a0[l], rwkv_k_k[l], rwkv_k_a[l],
                                  rwkv_r_k[l], rwkv_ln_g[l], rwkv_ln_b[l])]
        o_rwkv = _rwkv(h, w_rwkv, vecs, w2p, a2p, rwkv_seg, batch, seq, interpret)
        km, vm = _memkv(mem, mem_norm_g[l], w_mem_kv[l].astype(cdt), interpret)

        last = l == depth - 1
        g_next = final_norm_g if last else norm_g[l + 1]
        ups = [w.astype(cdt) for w in (w_up_ret[l], w_up_gla[l], w_up_ssd[l], w_up_rwkv[l], w_up_mem[l])]
        x2d, h = _merge(x2d, h, o_ret, o_gla, o_ssd, o_rwkv, km, vm, w_q, w_g, ups, w_out[l].astype(cdt),
                        _row(g_next), F32 if last else cdt, batch, seq, interpret)
        out = h
    return out.reshape(batch, seq, d)


def kernel(x, mem, positions, norm_g, w_in, gla_gk_w2, gla_gk_b, gla_norm_g, ssd_conv_w, ssd_conv_b, ssd_dt_bias, ssd_a_log, ssd_d, ssd_norm_g, rwkv_mu, rwkv_w0, rwkv_w2, rwkv_a0, rwkv_a2, rwkv_k_k, rwkv_k_a, rwkv_r_k, rwkv_ln_g, rwkv_ln_b, mem_norm_g, w_mem_kv, w_up_ret, w_up_gla, w_up_ssd, w_up_rwkv, w_up_mem, w_out, final_norm_g):
    return _forward(x, mem, positions, norm_g, w_in, gla_gk_w2, gla_gk_b, gla_norm_g,
                    ssd_conv_w, ssd_conv_b, ssd_dt_bias, ssd_a_log, ssd_d, ssd_norm_g,
                    rwkv_mu, rwkv_w0, rwkv_w2, rwkv_a0, rwkv_a2, rwkv_k_k, rwkv_k_a, rwkv_r_k,
                    rwkv_ln_g, rwkv_ln_b, mem_norm_g, w_mem_kv,
                    w_up_ret, w_up_gla, w_up_ssd, w_up_rwkv, w_up_mem, w_out, final_norm_g)
```

```python
import functools
import math

import jax
import jax.numpy as jnp
import numpy as np
from jax import lax
from jax.experimental import pallas as pl
from jax.experimental.pallas import tpu as pltpu

F32 = jnp.float32
BF16 = jnp.bfloat16
MXU_DTYPE = jnp.bfloat16

D_MODEL = 1024
WIDTH = 512
NORM_EPS = 1e-6
N_BRANCHES = 5

RET_HEADS, RET_DK, RET_DV, RET_CHUNK = 4, 64, 128, 128
ROPE_BASE = 10000.0
GLA_HEADS, GLA_DK, GLA_DV, GLA_RANK, GLA_NORMALIZER, GLA_CHUNK = 4, 64, 128, 16, 16.0, 64
SSD_HEADS, SSD_P, SSD_GROUPS, SSD_STATE, SSD_CONV, SSD_CHUNK = 8, 64, 2, 128, 4, 128
RWKV_HEADS, RWKV_N, RWKV_RANK, RWKV_CHUNK = 8, 64, 64, 64
RWKV_LN_EPS = 64e-5
RET_GROUP = 4
SSD_GROUP = 4
GLA_GROUP = 8
RWKV_GROUP = 8
MEM_HEADS, MEM_HEAD_DIM, MEM_WIDTH = 4, 64, 256

LANES = 128
CARRY_ROWS = 8
TOKEN_TILE = 512
VMEM_LIMIT = 56 * 1024 * 1024

_IN_SIZES = (256, 256, 512, 512, 256, 256, 512, 16, 512, 1024, 8, 512, 1664, 512, 256, 5120)
_IN_OFFS = tuple(int(v) for v in np.cumsum((0,) + _IN_SIZES))


def _dot(a, b):
    return jnp.dot(a.astype(MXU_DTYPE), b.astype(MXU_DTYPE), preferred_element_type=F32)


def _dot_nt(a, b):
    return lax.dot_general(a.astype(MXU_DTYPE), b.astype(MXU_DTYPE), (((1,), (1,)), ((), ())),
                           preferred_element_type=F32)


def _dot_tn(a, b):
    return lax.dot_general(a.astype(MXU_DTYPE), b.astype(MXU_DTYPE), (((0,), (0,)), ((), ())),
                           preferred_element_type=F32)


def _split3(x):
    hi = x.astype(BF16)
    r1 = x - hi.astype(F32)
    mid = r1.astype(BF16)
    lo = (r1 - mid.astype(F32)).astype(BF16)
    return hi, mid, lo


def _sel_dot(sel, x):
    s = sel.astype(BF16)
    return sum(jnp.dot(s, p, preferred_element_type=F32) for p in _split3(x))


def _dot_sel(x, sel, pieces=3):
    s = sel.astype(BF16)
    return sum(jnp.dot(p, s, preferred_element_type=F32) for p in _split3(x)[:pieces])


def _iota(shape, dim):
    return lax.broadcasted_iota(jnp.int32, shape, dim)


def _tril(n, strict=False):
    r, c = _iota((n, n), 0), _iota((n, n), 1)
    return (r > c) if strict else (r >= c)


def _silu(x):
    return x * jax.nn.sigmoid(x)


def _softplus(x):
    return jnp.maximum(x, 0.0) + jnp.log1p(jnp.exp(-jnp.abs(x)))


def _rows(c, n):
    return pl.ds(pl.multiple_of(c * n, n), n)


def _full_spec(shape):
    zeros = (0,) * len(shape)
    return pl.BlockSpec(shape, lambda *_: zeros)


def _params(semantics):
    return pltpu.CompilerParams(dimension_semantics=semantics, vmem_limit_bytes=VMEM_LIMIT)


def _rmsnorm_kernel(x_ref, g_ref, o_ref):
    x = x_ref[...]
    y = x * lax.rsqrt(jnp.mean(x * x, axis=-1, keepdims=True) + NORM_EPS)
    o_ref[...] = (y * g_ref[...]).astype(o_ref.dtype)


def _rmsnorm(x2d, g, out_dtype, interpret):
    m, d = x2d.shape
    tm = min(1024, m)
    return pl.pallas_call(
        _rmsnorm_kernel,
        grid=(m // tm,),
        in_specs=[pl.BlockSpec((tm, d), lambda i: (i, 0)), _full_spec((1, d))],
        out_specs=pl.BlockSpec((tm, d), lambda i: (i, 0)),
        out_shape=jax.ShapeDtypeStruct((m, d), out_dtype),
        compiler_params=_params(("parallel",)),
        interpret=interpret,
        name="rmsnorm",
    )(x2d, g.reshape(1, d))


def _ret_kernel(h_ref, pos_ref, inv_ref, w_ref, o_ref, p_scr, s_scr, *, tb):
    C = RET_CHUNK

    @pl.when(pl.program_id(1) == 0)
    def _():
        s_scr[...] = jnp.zeros_like(s_scr)

    p = jnp.dot(h_ref[...], w_ref[...], preferred_element_type=F32)
    ang = pos_ref[...].astype(F32) * inv_ref[...]
    cos, sin = jnp.cos(ang), jnp.sin(ang)
    q1, q2 = p[:, 0:128], p[:, 128:256]
    k1, k2 = p[:, 256:384] * RET_DK ** -0.5, p[:, 384:512] * RET_DK ** -0.5
    p_scr[:, 0:128] = q1 * cos - q2 * sin
    p_scr[:, 128:256] = q2 * cos + q1 * sin
    p_scr[:, 256:384] = k1 * cos - k2 * sin
    p_scr[:, 384:512] = k2 * cos + k1 * sin
    p_scr[:, 512:1536] = p[:, 512:1536]

    def log_gamma(head):
        return jnp.log(1.0 - jnp.exp2(-5.0 - head.astype(F32)))

    qk_head = (_iota((1, 256), 1) >> 5) & 3
    lg_lane = log_gamma(qk_head)
    tau = _iota((C, 1), 0).astype(F32)
    dq = jnp.exp(lg_lane * (tau + 1.0))
    dk = jnp.exp(lg_lane * (C - 1.0 - tau))
    ds = jnp.exp(lg_lane * float(C))
    diff = (_iota((C, C), 0) - _iota((C, C), 1)).astype(F32)
    causal = _tril(C)
    bd_mask = (_iota((WIDTH, 256), 0) >> 7) == ((_iota((WIDTH, 256), 1) >> 5) & 3)

    heads = range(RET_HEADS)
    hcols = [slice(hd * RET_DV, (hd + 1) * RET_DV) for hd in heads]
    seg = [jnp.where(causal, jnp.exp(math.log(1.0 - 2.0 ** (-5.0 - hd)) * diff), 0.0) for hd in heads]

    def group(gi, carry):
        us = range(RET_GROUP)
        rows = [_rows(gi * RET_GROUP + u, C) for u in us]
        q = [p_scr[rw, 0:256] for rw in rows]
        k = [p_scr[rw, 256:512] for rw in rows]
        v = [p_scr[rw, 512:1024] for rw in rows]
        sc = [[_dot_nt(q[u], jnp.where(qk_head == hd, k[u], 0.0)) * seg[hd] for hd in heads] for u in us]
        y_intra = [[_dot(sc[u][hd], v[u][:, hcols[hd]]) for hd in heads] for u in us]
        kv = [jnp.where(bd_mask, _dot_tn(v[u], k[u] * dk), 0.0) for u in us]
        st = s_scr[...]
        y_inter = []
        for u in us:
            y_inter.append(_dot_nt(q[u] * dq, st))
            st = st * ds + kv[u]
        s_scr[...] = st
        for u in us:
            for hd in heads:
                y = y_intra[u][hd] + y_inter[u][:, hcols[hd]]
                y = y * lax.rsqrt(jnp.mean(y * y, axis=-1, keepdims=True) + NORM_EPS)
                g = p_scr[rows[u], 1024 + hd * RET_DV:1024 + (hd + 1) * RET_DV]
                o_ref[rows[u], hcols[hd]] = (y * _silu(g)).astype(o_ref.dtype)
        return carry

    lax.fori_loop(0, tb // (C * RET_GROUP), group, 0)


def _retention(h, pos_col, inv_row, w, batch, seq, interpret):
    tb = min(TOKEN_TILE, seq)
    nt = seq // tb
    kern = functools.partial(_ret_kernel, tb=tb)
    return pl.pallas_call(
        kern,
        grid=(batch, nt),
        in_specs=[pl.BlockSpec((tb, D_MODEL), lambda b, i: (b * nt + i, 0)),
                  pl.BlockSpec((tb, 1), lambda b, i: (b * nt + i, 0)),
                  _full_spec((1, LANES)),
                  _full_spec(w.shape)],
        out_specs=pl.BlockSpec((tb, WIDTH), lambda b, i: (b * nt + i, 0)),
        out_shape=jax.ShapeDtypeStruct((batch * seq, WIDTH), MXU_DTYPE),
        scratch_shapes=[pltpu.VMEM((tb, 1536), F32), pltpu.VMEM((WIDTH, 256), F32)],
        compiler_params=_params(("parallel", "arbitrary")),
        interpret=interpret,
        name="retention",
    )(h, pos_col, inv_row, w)


def _gla_kernel(h_ref, w_ref, w2_ref, gb_ref, ng_ref, o_ref, p_scr, lg_scr, s_scr, *, tb):
    C = GLA_CHUNK

    @pl.when(pl.program_id(1) == 0)
    def _():
        s_scr[...] = jnp.zeros_like(s_scr)

    p = jnp.dot(h_ref[...], w_ref[...], preferred_element_type=F32)
    p_scr[...] = p[:, 0:1536]
    pre = _dot(p[:, 1536:1664], w2_ref[...]) + gb_ref[...]
    lg_scr[...] = -_softplus(-pre) / GLA_NORMALIZER

    tri = _tril(C)
    k_head = _iota((1, 256), 1) >> 6
    bd_mask = (_iota((WIDTH, 256), 0) >> 7) == (_iota((WIDTH, 256), 1) >> 6)
    ng = ng_ref[...]

    heads = range(GLA_HEADS)
    hcols = [slice(hd * GLA_DV, (hd + 1) * GLA_DV) for hd in heads]

    def group(gi, carry):
        us = range(GLA_GROUP)
        rows = [_rows(gi * GLA_GROUP + u, C) for u in us]
        q = [p_scr[rw, 0:256] * GLA_DK ** -0.5 for rw in rows]
        k = [p_scr[rw, 256:512] for rw in rows]
        v = [p_scr[rw, 512:1024] for rw in rows]
        cum = [_sel_dot(tri, lg_scr[rw, :]) for rw in rows]
        ref = [c[C // 2:C // 2 + 1, :] for c in cum]
        last = [c[C - 1:C, :] for c in cum]
        q_in = [q[u] * jnp.exp(cum[u] - ref[u]) for u in us]
        k_in = [k[u] * jnp.exp(ref[u] - cum[u]) for u in us]
        q_dec = [q[u] * jnp.exp(cum[u]) for u in us]
        k_st = [k[u] * jnp.exp(last[u] - cum[u]) for u in us]
        sc = [[jnp.where(tri, _dot_nt(q_in[u], jnp.where(k_head == hd, k_in[u], 0.0)), 0.0) for hd in heads]
              for u in us]
        y_intra = [[_dot(sc[u][hd], v[u][:, hcols[hd]]) for hd in heads] for u in us]
        kv = [jnp.where(bd_mask, _dot_tn(v[u], k_st[u]), 0.0) for u in us]
        st = s_scr[...]
        y_inter = []
        for u in us:
            y_inter.append(_dot_nt(q_dec[u], st))
            st = st * jnp.exp(last[u]) + kv[u]
        s_scr[...] = st
        for u in us:
            for hd in heads:
                y = y_intra[u][hd] + y_inter[u][:, hcols[hd]]
                y = y * lax.rsqrt(jnp.mean(y * y, axis=-1, keepdims=True) + NORM_EPS) * ng
                g = p_scr[rows[u], 1024 + hd * GLA_DV:1024 + (hd + 1) * GLA_DV]
                o_ref[rows[u], hcols[hd]] = (y * _silu(g)).astype(o_ref.dtype)
        return carry

    lax.fori_loop(0, tb // (C * GLA_GROUP), group, 0)


def _gla(h, w, w2p, gb, ng, batch, seq, interpret):
    tb = min(TOKEN_TILE, seq)
    nt = seq // tb
    kern = functools.partial(_gla_kernel, tb=tb)
    return pl.pallas_call(
        kern,
        grid=(batch, nt),
        in_specs=[pl.BlockSpec((tb, D_MODEL), lambda b, i: (b * nt + i, 0)),
                  _full_spec(w.shape), _full_spec(w2p.shape), _full_spec(gb.shape),
                  _full_spec(ng.shape)],
        out_specs=pl.BlockSpec((tb, WIDTH), lambda b, i: (b * nt + i, 0)),
        out_shape=jax.ShapeDtypeStruct((batch * seq, WIDTH), MXU_DTYPE),
        scratch_shapes=[pltpu.VMEM((tb, 1536), F32), pltpu.VMEM((tb, 256), F32),
                        pltpu.VMEM((WIDTH, 256), F32)],
        compiler_params=_params(("parallel", "arbitrary")),
        interpret=interpret,
        name="gla",
    )(h, w, w2p, gb, ng)


def _ssd_kernel(h_ref, w_ref, cw_ref, cb_ref, dtb_ref, a_ref, dskip_ref, ng_ref, exp_ref, o_ref,
                raw_scr, xc_scr, z_scr, dt_scr, s_scr, *, tb):
    C = SSD_CHUNK
    NCH = 1024

    @pl.when(pl.program_id(1) == 0)
    def _():
        s_scr[...] = jnp.zeros_like(s_scr)
        raw_scr[0:CARRY_ROWS, :] = jnp.zeros((CARRY_ROWS, NCH), F32)

    p = jnp.dot(h_ref[...], w_ref[...], preferred_element_type=F32)
    raw_scr[CARRY_ROWS:CARRY_ROWS + tb, :] = p[:, 0:NCH]
    z_scr[...] = p[:, NCH:NCH + WIDTH]
    dt_scr[...] = _softplus(p[:, 1536:1664] + dtb_ref[...])
    conv = cb_ref[...] + sum(
        raw_scr[CARRY_ROWS - (SSD_CONV - 1) + j:CARRY_ROWS - (SSD_CONV - 1) + j + tb, :] * cw_ref[j:j + 1, :]
        for j in range(SSD_CONV))
    xc_scr[...] = _silu(conv)
    raw_scr[0:CARRY_ROWS, :] = raw_scr[tb:tb + CARRY_ROWS, :]

    tri = _tril(C)
    lane_lo = _iota((1, LANES), 1) < SSD_P
    expand = exp_ref[...]
    a_row = a_ref[...]

    groups = range(SSD_GROUPS)
    gcols = [slice(g * 256, (g + 1) * 256) for g in groups]
    ncols = [slice(g * SSD_STATE, (g + 1) * SSD_STATE) for g in groups]

    def group(gi, carry):
        us = range(SSD_GROUP)
        rows = [_rows(gi * SSD_GROUP + u, C) for u in us]
        xs = [xc_scr[rw, 0:512] for rw in rows]
        bm = [xc_scr[rw, 512:768] for rw in rows]
        cm = [xc_scr[rw, 768:1024] for rw in rows]
        dt = [dt_scr[rw, :] for rw in rows]
        cum = [_sel_dot(tri, dt[u] * a_row) for u in us]
        cum_t = [c.T for c in cum]
        dt_e = [_dot_sel(d, expand) for d in dt]
        cum_e = [_dot_sel(c, expand) for c in cum]
        last_e = [c[C - 1:C, :] for c in cum_e]
        xdt = [xs[u] * dt_e[u] for u in us]
        v_st = [xdt[u] * jnp.exp(last_e[u] - cum_e[u]) for u in us]
        e_cum = [jnp.exp(c) for c in cum_e]
        scores = [[_dot_nt(cm[u][:, ncols[g]], bm[u][:, ncols[g]]) for g in groups] for u in us]
        y_intra = []
        for u in us:
            parts = []
            for pr in range(SSD_HEADS // 2):
                sc = scores[u][pr // 2]
                segs = []
                for hd in (2 * pr, 2 * pr + 1):
                    d = jnp.minimum(cum[u][:, hd:hd + 1] - cum_t[u][hd:hd + 1, :], 0.0)
                    segs.append(sc * jnp.where(tri, jnp.exp(d), 0.0))
                xp = xdt[u][:, pr * LANES:(pr + 1) * LANES]
                rhs = jnp.concatenate([jnp.where(lane_lo, xp, 0.0), jnp.where(lane_lo, 0.0, xp)], axis=0)
                parts.append(_dot(jnp.concatenate(segs, axis=1), rhs))
            y_intra.append(jnp.concatenate(parts, axis=1))
        kv = [[_dot_tn(bm[u][:, ncols[g]], v_st[u][:, gcols[g]]) for g in groups] for u in us]
        st = [s_scr[g] for g in groups]
        y_inter = []
        for u in us:
            y_inter.append(jnp.concatenate(
                [_dot(cm[u][:, ncols[g]], st[g]) * e_cum[u][:, gcols[g]] for g in groups], axis=1))
            st = [st[g] * jnp.exp(last_e[u][:, gcols[g]]) + kv[u][g] for g in groups]
        for g in groups:
            s_scr[g] = st[g]
        for u in us:
            y = y_intra[u] + y_inter[u] + dskip_ref[...] * xs[u]
            y = y * _silu(z_scr[rows[u], :])
            for g in groups:
                yg = y[:, gcols[g]]
                yg = yg * lax.rsqrt(jnp.mean(yg * yg, axis=-1, keepdims=True) + NORM_EPS)
                o_ref[rows[u], gcols[g]] = (yg * ng_ref[:, gcols[g]]).astype(o_ref.dtype)
        return carry

    lax.fori_loop(0, tb // (C * SSD_GROUP), group, 0)


def _ssd(h, w, cw, cb, dtb, a_row, dskip, ng, expand, batch, seq, interpret):
    tb = min(TOKEN_TILE, seq)
    nt = seq // tb
    kern = functools.partial(_ssd_kernel, tb=tb)
    small = [cw, cb, dtb, a_row, dskip, ng, expand]
    return pl.pallas_call(
        kern,
        grid=(batch, nt),
        in_specs=[pl.BlockSpec((tb, D_MODEL), lambda b, i: (b * nt + i, 0)), _full_spec(w.shape)]
        + [_full_spec(a.shape) for a in small],
        out_specs=pl.BlockSpec((tb, WIDTH), lambda b, i: (b * nt + i, 0)),
        out_shape=jax.ShapeDtypeStruct((batch * seq, WIDTH), MXU_DTYPE),
        scratch_shapes=[pltpu.VMEM((tb + CARRY_ROWS, 1024), F32), pltpu.VMEM((tb, 1024), F32),
                        pltpu.VMEM((tb, WIDTH), F32), pltpu.VMEM((tb, LANES), F32),
                        pltpu.VMEM((SSD_GROUPS, SSD_STATE, 256), F32)],
        compiler_params=_params(("parallel", "arbitrary")),
        interpret=interpret,
        name="ssd",
    )(h, w, *small)


def _rwkv_kernel(h_ref, w_ref, mu_ref, w0_ref, w2_ref, a0_ref, a2_ref, kk_ref, ka_ref, rk_ref,
                 lng_ref, lnb_ref, seg_ref, o_ref,
                 u_scr, g_scr, r_scr, lw_scr, k_scr, v_scr, a_scr, b_scr, y_scr, bonus_scr, st_scr, *, tb):
    C = RWKV_CHUNK
    NU = 3 * WIDTH + 2 * RWKV_RANK

    @pl.when(pl.program_id(1) == 0)
    def _():
        st_scr[...] = jnp.zeros_like(st_scr)
        u_scr[0:CARRY_ROWS, :] = jnp.zeros((CARRY_ROWS, NU), F32)

    p = jnp.dot(h_ref[...], w_ref[...], preferred_element_type=F32)
    g_scr[...] = p[:, NU:NU + WIDTH]
    u = p[:, 0:NU]
    u_scr[CARRY_ROWS:CARRY_ROWS + tb, :] = u
    u_prev = u_scr[CARRY_ROWS - 1:CARRY_ROWS - 1 + tb, :]
    xs = u + (u_prev - u) * mu_ref[...]
    u_scr[0:CARRY_ROWS, :] = u_scr[tb:tb + CARRY_ROWS, :]

    seg = seg_ref[...]

    def seg_sum(x, pieces):
        return jnp.concatenate([_dot_sel(x[:, j * 256:(j + 1) * 256], seg, pieces) for j in range(2)], axis=1)

    r = xs[:, 0:512]
    k = xs[:, 512:1024]
    v = xs[:, 1024:1536]
    wa = xs[:, 1536:1664]
    w_pre = w0_ref[...] + _dot(jnp.tanh(wa), w2_ref[...])
    a_gate = jax.nn.sigmoid(a0_ref[...] + _dot(wa, a2_ref[...]))
    lw_scr[...] = -jnp.exp(-_softplus(-w_pre) - 0.5)
    kk = k * kk_ref[...]
    k = k * (1.0 + (a_gate - 1.0) * ka_ref[...])
    kk = kk / jnp.maximum(jnp.sqrt(seg_sum(kk * kk, 1)), 1e-12)
    r_scr[...] = r
    k_scr[...] = k
    v_scr[...] = v
    a_scr[...] = -kk
    b_scr[...] = kk * a_gate
    bonus_scr[...] = seg_sum(r * k * rk_ref[...], 1) * v

    tri = _tril(C)
    r2, c2 = _iota((2 * C, 2 * C), 0), _iota((2 * C, 2 * C), 1)
    same = (r2 >> 6) == (c2 >> 6)
    low_s = same & ((r2 & 63) > (c2 & 63))
    low_i = same & ((r2 & 63) >= (c2 & 63))
    eye = (r2 == c2).astype(F32)
    lane_lo = _iota((1, LANES), 1) < RWKV_N

    pairs = range(RWKV_HEADS // 2)

    def stack(x, pr):
        xp = x[:, pr * LANES:(pr + 1) * LANES]
        return jnp.concatenate([jnp.where(lane_lo, xp, 0.0), jnp.where(lane_lo, 0.0, xp)], axis=0)

    def group(gi, carry):
        lanes = [(u, pr) for u in range(RWKV_GROUP) for pr in pairs]
        rows, a_0, r_0, v_c, b_h, k_h, e_last, lhs, rhs_t = [], [], [], [], [], [], [], {}, {}
        for u in range(RWKV_GROUP):
            rw = _rows(gi * RWKV_GROUP + u, C)
            lw = lw_scr[rw, :]
            cum = _sel_dot(tri, lw)
            cum_p = cum - lw
            ref = cum[C // 2:C // 2 + 1, :]
            last = cum[C - 1:C, :]
            e_fwd = jnp.exp(cum - ref)
            e_bwd = jnp.exp(ref - cum)
            e_end = jnp.exp(last - cum)
            r_c, k_c, a_c, b_c = r_scr[rw, :], k_scr[rw, :], a_scr[rw, :], b_scr[rw, :]
            r_t = r_c * e_fwd
            a_t = a_c * jnp.exp(cum_p - ref)
            b_t = b_c * e_bwd
            k_t = k_c * e_bwd
            for pr in pairs:
                lhs[u, pr] = jnp.concatenate([stack(a_t, pr), stack(r_t, pr)], axis=0)
                rhs_t[u, pr] = jnp.concatenate([stack(b_t, pr), stack(k_t, pr)], axis=0)
            rows.append(rw)
            a_0.append(a_c * jnp.exp(cum_p))
            r_0.append(r_c * jnp.exp(cum))
            v_c.append(v_scr[rw, :])
            b_h.append(b_c * e_end)
            k_h.append(k_c * e_end)
            e_last.append(jnp.exp(last))
        big = [_dot_nt(lhs[ln], rhs_t[ln]) for ln in lanes]
        a_ab = [jnp.where(low_s, m[0:128, 0:128], 0.0) for m in big]
        a_ak = [jnp.where(low_s, m[0:128, 128:256], 0.0) for m in big]
        a_rb = [jnp.where(low_i, m[128:256, 0:128], 0.0) for m in big]
        a_rk = [jnp.where(low_i, m[128:256, 128:256], 0.0) for m in big]
        inv = [eye + m for m in a_ab]
        pw = [_dot(m, m) for m in a_ab]
        for _ in range(4):
            prod = [_dot(p, jnp.concatenate([p, t], axis=1)) for p, t in zip(pw, inv)]
            pw = [m[:, 0:128] for m in prod]
            inv = [t + m[:, 128:256] for t, m in zip(inv, prod)]
        inv = [t + _dot(p, t) for p, t in zip(pw, inv)]
        vs = [stack(v_c[u], pr) for u, pr in lanes]
        akv = [_dot(m, x) for m, x in zip(a_ak, vs)]
        a_r = [jnp.concatenate([m, n], axis=1) for m, n in zip(a_rb, a_rk)]
        bk_h = [jnp.concatenate([stack(b_h[u], pr), stack(k_h[u], pr)], axis=0) for u, pr in lanes]

        st = [st_scr[pr] for pr in pairs]
        for u in range(RWKV_GROUP):
            ix = [u * len(pairs) + pr for pr in pairs]
            rhs = [_dot_nt(stack(a_0[u], pr), st[pr]) + akv[i] for pr, i in zip(pairs, ix)]
            uv = [jnp.concatenate([_dot(inv[i], m), vs[i]], axis=0) for m, i in zip(rhs, ix)]
            oo = [_dot_nt(stack(r_0[u], pr), st[pr]) + _dot(a_r[i], uv[pr]) for pr, i in zip(pairs, ix)]
            st = [st[pr] * e_last[u][:, pr * LANES:(pr + 1) * LANES] + _dot_tn(uv[pr], bk_h[i])
                  for pr, i in zip(pairs, ix)]
            for pr in pairs:
                y_scr[rows[u], pr * LANES:(pr + 1) * LANES] = oo[pr][0:C, :] + oo[pr][C:2 * C, :]
        for pr in pairs:
            st_scr[pr] = st[pr]
        return carry

    lax.fori_loop(0, tb // (C * RWKV_GROUP), group, 0)

    y = y_scr[...]
    mean = seg_sum(y, 2) * (1.0 / RWKV_N)
    d = y - mean
    var = seg_sum(d * d, 1) * (1.0 / RWKV_N)
    y = d * lax.rsqrt(var + RWKV_LN_EPS) * lng_ref[...] + lnb_ref[...] + bonus_scr[...]
    o_ref[...] = (y * _silu(g_scr[...])).astype(o_ref.dtype)


def _rwkv(h, w, vecs, w2p, a2p, seg, batch, seq, interpret):
    tb = min(TOKEN_TILE, seq)
    nt = seq // tb
    kern = functools.partial(_rwkv_kernel, tb=tb)
    mu, w0, a0, kk, ka, rk, lng, lnb = vecs
    ins = [mu, w0, w2p, a0, a2p, kk, ka, rk, lng, lnb, seg]
    wide = lambda: pltpu.VMEM((tb, WIDTH), F32)
    return pl.pallas_call(
        kern,
        grid=(batch, nt),
        in_specs=[pl.BlockSpec((tb, D_MODEL), lambda b, i: (b * nt + i, 0)), _full_spec(w.shape)]
        + [_full_spec(a.shape) for a in ins],
        out_specs=pl.BlockSpec((tb, WIDTH), lambda b, i: (b * nt + i, 0)),
        out_shape=jax.ShapeDtypeStruct((batch * seq, WIDTH), MXU_DTYPE),
        scratch_shapes=[pltpu.VMEM((tb + CARRY_ROWS, 3 * WIDTH + 2 * RWKV_RANK), F32)]
        + [wide() for _ in range(9)]
        + [pltpu.VMEM((RWKV_HEADS // 2, LANES, LANES), F32)],
        compiler_params=_params(("parallel", "arbitrary")),
        interpret=interpret,
        name="rwkv7",
    )(h, w, *ins)


def _memkv_kernel(mem_ref, g_ref, w_ref, k_ref, v_ref):
    x = mem_ref[0]
    y = x * lax.rsqrt(jnp.mean(x * x, axis=-1, keepdims=True) + NORM_EPS) * g_ref[...]
    kv = _dot(y, w_ref[...])
    head = _iota((1, MEM_WIDTH), 1) >> 6
    for hd in range(MEM_HEADS):
        k_ref[0, hd] = jnp.where(head == hd, kv[:, 0:MEM_WIDTH], 0.0).astype(k_ref.dtype)
        v_ref[0, hd] = jnp.where(head == hd, kv[:, MEM_WIDTH:2 * MEM_WIDTH], 0.0).astype(v_ref.dtype)


def _memkv(mem, g, w, interpret):
    b, m, d = mem.shape
    out = jax.ShapeDtypeStruct((b, MEM_HEADS, m, MEM_WIDTH), MXU_DTYPE)
    return pl.pallas_call(
        _memkv_kernel,
        grid=(b,),
        in_specs=[pl.BlockSpec((1, m, d), lambda i: (i, 0, 0)), _full_spec((1, d)), _full_spec(w.shape)],
        out_specs=[pl.BlockSpec((1, MEM_HEADS, m, MEM_WIDTH), lambda i: (i, 0, 0, 0))] * 2,
        out_shape=[out, out],
        compiler_params=_params(("parallel",)),
        interpret=interpret,
        name="mem_kv",
    )(mem, g.reshape(1, d), w)


def _merge_kernel(x_ref, h_ref, oret_ref, ogla_ref, ossd_ref, orwkv_ref, km_ref, vm_ref,
                  wq_ref, wg_ref, uret_ref, ugla_ref, ussd_ref, urwkv_ref, umem_ref, wout_ref, gn_ref,
                  xo_ref, ho_ref):
    h = h_ref[...]
    q = jnp.dot(h, wq_ref[...], preferred_element_type=F32) * MEM_HEAD_DIM ** -0.5
    o_mem = jnp.zeros(q.shape, F32)
    for hd in range(MEM_HEADS):
        s = _dot_nt(q, km_ref[0, hd])
        s = jnp.exp(s - jnp.max(s, axis=-1, keepdims=True))
        prob = s / jnp.sum(s, axis=-1, keepdims=True)
        o_mem = o_mem + _dot(prob, vm_ref[0, hd])
    branches = ((oret_ref, uret_ref), (ogla_ref, ugla_ref), (ossd_ref, ussd_ref), (orwkv_ref, urwkv_ref))
    merged = None
    for i in range(N_BRANCHES):
        gate = jax.nn.sigmoid(jnp.dot(h, wg_ref[:, i * D_MODEL:(i + 1) * D_MODEL], preferred_element_type=F32))
        if i < 4:
            o_ref, u_ref = branches[i]
            up = jnp.dot(o_ref[...], u_ref[...], preferred_element_type=F32)
        else:
            up = _dot(o_mem, umem_ref[...])
        merged = gate * up if merged is None else merged + gate * up
    x = x_ref[...] + _dot(merged, wout_ref[...])
    xo_ref[...] = x
    y = x * lax.rsqrt(jnp.mean(x * x, axis=-1, keepdims=True) + NORM_EPS) * gn_ref[...]
    ho_ref[...] = y.astype(ho_ref.dtype)


def _merge(x2d, h, o_ret, o_gla, o_ssd, o_rwkv, km, vm, wq, wg, ups, wout, g_next, h_dtype,
           batch, seq, interpret):
    tm = min(TOKEN_TILE, seq)
    nt = seq // tm
    row = lambda w: pl.BlockSpec((tm, w), lambda b, i: (b * nt + i, 0))
    kvspec = pl.BlockSpec((1,) + km.shape[1:], lambda b, i: (b, 0, 0, 0))
    weights = [wq, wg, *ups, wout, g_next]
    return pl.pallas_call(
        _merge_kernel,
        grid=(batch, nt),
        in_specs=[row(D_MODEL), row(D_MODEL), row(WIDTH), row(WIDTH), row(WIDTH), row(WIDTH), kvspec, kvspec]
        + [pl.BlockSpec(w.shape, lambda b, i: (0, 0), pipeline_mode=pl.Buffered(1)) for w in weights],
        out_specs=[row(D_MODEL), row(D_MODEL)],
        out_shape=[jax.ShapeDtypeStruct(x2d.shape, F32), jax.ShapeDtypeStruct(x2d.shape, h_dtype)],
        compiler_params=_params(("parallel", "arbitrary")),
        interpret=interpret,
        name="merge",
    )(x2d, h, o_ret, o_gla, o_ssd, o_rwkv, km, vm, *weights)


def _pad_rows(w, rows):
    return jnp.pad(w, ((0, rows - w.shape[0]), (0, 0)))


def _pad_cols(w, cols):
    return jnp.pad(w, ((0, 0), (0, cols - w.shape[1])))


def _row(v, width=None):
    v = v.reshape(1, -1).astype(F32)
    return v if width is None else _pad_cols(v, width)


def _forward(x, mem, positions, norm_g, w_in, gla_gk_w2, gla_gk_b, gla_norm_g,
             ssd_conv_w, ssd_conv_b, ssd_dt_bias, ssd_a_log, ssd_d, ssd_norm_g,
             rwkv_mu, rwkv_w0, rwkv_w2, rwkv_a0, rwkv_a2, rwkv_k_k, rwkv_k_a, rwkv_r_k,
             rwkv_ln_g, rwkv_ln_b, mem_norm_g, w_mem_kv,
             w_up_ret, w_up_gla, w_up_ssd, w_up_rwkv, w_up_mem, w_out, final_norm_g, interpret=False):
    batch, seq, d = x.shape
    depth = w_in.shape[0]
    cdt = MXU_DTYPE
    o = _IN_OFFS

    half = np.arange(RET_DK // 2)
    ret_perm = np.concatenate([hd * RET_DK + 2 * half + par for par in (0, 1) for hd in range(RET_HEADS)])
    inv = 1.0 / (ROPE_BASE ** jnp.linspace(0.0, 1.0, RET_DK // 2, dtype=F32))
    inv_row = jnp.tile(inv, RET_HEADS).reshape(1, LANES)
    pos_col = positions.reshape(batch * seq, 1)
    head_of_lane = np.arange(WIDTH) // SSD_P
    ssd_expand = jnp.asarray(np.arange(LANES)[:, None] == head_of_lane[None, :], F32)
    rwkv_seg = jnp.asarray(head_of_lane[:256, None] == head_of_lane[None, :256], F32)

    x2d = x.reshape(batch * seq, d)
    h = _rmsnorm(x2d, norm_g[0], cdt, interpret)
    out = None
    for l in range(depth):
        wl = w_in[l]
        col = lambda i, j=None: wl[:, o[i]:o[(i if j is None else j) + 1]]
        w_ret = jnp.concatenate([col(0)[:, ret_perm], col(1)[:, ret_perm], col(2), col(3)], axis=1).astype(cdt)
        w_gla = jnp.concatenate([col(4), col(5), col(6), col(8), _pad_cols(col(7), LANES)], axis=1).astype(cdt)
        w_ssd = jnp.concatenate([col(9), col(11), _pad_cols(col(10), LANES)], axis=1).astype(cdt)
        w_rwkv = col(12, 13).astype(cdt)
        w_q = col(14).astype(cdt)
        w_g = col(15).astype(cdt)

        o_ret = _retention(h, pos_col, inv_row, w_ret, batch, seq, interpret)
        o_gla = _gla(h, w_gla, _pad_rows(gla_gk_w2[l], LANES).astype(cdt), _row(gla_gk_b[l]),
                     _row(gla_norm_g[l]), batch, seq, interpret)
        o_ssd = _ssd(h, w_ssd, ssd_conv_w[l].astype(F32), _row(ssd_conv_b[l]), _row(ssd_dt_bias[l], LANES),
                     _row(-jnp.exp(ssd_a_log[l].astype(F32)), LANES),
                     _row(jnp.repeat(ssd_d[l], SSD_P)), _row(ssd_norm_g[l]), ssd_expand, batch, seq, interpret)
        zeros_rank = jnp.zeros((RWKV_RANK, WIDTH), F32)
        w2p = jnp.concatenate([rwkv_w2[l], zeros_rank], axis=0).astype(cdt)
        a2p = jnp.concatenate([zeros_rank, rwkv_a2[l]], axis=0).astype(cdt)
        vecs = [_row(v) for v in (rwkv_mu[l], rwkv_w0[l], rwkv_a0[l], rwkv_k_k[l], rwkv_k_a[l],
                                  rwkv_r_k[l], rwkv_ln_g[l], rwkv_ln_b[l])]
        o_rwkv = _rwkv(h, w_rwkv, vecs, w2p, a2p, rwkv_seg, batch, seq, interpret)
        km, vm = _memkv(mem, mem_norm_g[l], w_mem_kv[l].astype(cdt), interpret)

        last = l == depth - 1
        g_next = final_norm_g if last else norm_g[l + 1]
        ups = [w.astype(cdt) for w in (w_up_ret[l], w_up_gla[l], w_up_ssd[l], w_up_rwkv[l], w_up_mem[l])]
        x2d, h = _merge(x2d, h, o_ret, o_gla, o_ssd, o_rwkv, km, vm, w_q, w_g, ups, w_out[l].astype(cdt),
                        _row(g_next), F32 if last else cdt, batch, seq, interpret)
        out = h
    return out.reshape(batch, seq, d)


def kernel(x, mem, positions, norm_g, w_in, gla_gk_w2, gla_gk_b, gla_norm_g, ssd_conv_w, ssd_conv_b, ssd_dt_bias, ssd_a_log, ssd_d, ssd_norm_g, rwkv_mu, rwkv_w0, rwkv_w2, rwkv_a0, rwkv_a2, rwkv_k_k, rwkv_k_a, rwkv_r_k, rwkv_ln_g, rwkv_ln_b, mem_norm_g, w_mem_kv, w_up_ret, w_up_gla, w_up_ssd, w_up_rwkv, w_up_mem, w_out, final_norm_g):
    return _forward(x, mem, positions, norm_g, w_in, gla_gk_w2, gla_gk_b, gla_norm_g,
                    ssd_conv_w, ssd_conv_b, ssd_dt_bias, ssd_a_log, ssd_d, ssd_norm_g,
                    rwkv_mu, rwkv_w0, rwkv_w2, rwkv_a0, rwkv_a2, rwkv_k_k, rwkv_k_a, rwkv_r_k,
                    rwkv_ln_g, rwkv_ln_b, mem_norm_g, w_mem_kv,
                    w_up_ret, w_up_gla, w_up_ssd, w_up_rwkv, w_up_mem, w_out, final_norm_g)
```

```python
import functools
import math

import jax
import jax.numpy as jnp
import numpy as np
from jax import lax
from jax.experimental import pallas as pl
from jax.experimental.pallas import tpu as pltpu

F32 = jnp.float32
BF16 = jnp.bfloat16
MXU_DTYPE = jnp.bfloat16

D_MODEL = 1024
WIDTH = 512
NORM_EPS = 1e-6
N_BRANCHES = 5

RET_HEADS, RET_DK, RET_DV, RET_CHUNK = 4, 64, 128, 128
ROPE_BASE = 10000.0
GLA_HEADS, GLA_DK, GLA_DV, GLA_RANK, GLA_NORMALIZER, GLA_CHUNK = 4, 64, 128, 16, 16.0, 64
SSD_HEADS, SSD_P, SSD_GROUPS, SSD_STATE, SSD_CONV, SSD_CHUNK = 8, 64, 2, 128, 4, 128
RWKV_HEADS, RWKV_N, RWKV_RANK, RWKV_CHUNK = 8, 64, 64, 64
RWKV_LN_EPS = 64e-5
RET_GROUP = 4
SSD_GROUP = 4
GLA_GROUP = 8
RWKV_GROUP = 8
MEM_HEADS, MEM_HEAD_DIM, MEM_WIDTH = 4, 64, 256

LANES = 128
CARRY_ROWS = 8
TOKEN_TILE = 512
VMEM_LIMIT = 56 * 1024 * 1024

_IN_SIZES = (256, 256, 512, 512, 256, 256, 512, 16, 512, 1024, 8, 512, 1664, 512, 256, 5120)
_IN_OFFS = tuple(int(v) for v in np.cumsum((0,) + _IN_SIZES))


def _dot(a, b):
    return jnp.dot(a.astype(MXU_DTYPE), b.astype(MXU_DTYPE), preferred_element_type=F32)


def _dot_nt(a, b):
    return lax.dot_general(a.astype(MXU_DTYPE), b.astype(MXU_DTYPE), (((1,), (1,)), ((), ())),
                           preferred_element_type=F32)


def _dot_tn(a, b):
    return lax.dot_general(a.astype(MXU_DTYPE), b.astype(MXU_DTYPE), (((0,), (0,)), ((), ())),
                           preferred_element_type=F32)


def _split3(x):
    hi = x.astype(BF16)
    r1 = x - hi.astype(F32)
    mid = r1.astype(BF16)
    lo = (r1 - mid.astype(F32)).astype(BF16)
    return hi, mid, lo


def _sel_dot(sel, x):
    s = sel.astype(BF16)
    return sum(jnp.dot(s, p, preferred_element_type=F32) for p in _split3(x))


def _dot_sel(x, sel, pieces=3):
    s = sel.astype(BF16)
    return sum(jnp.dot(p, s, preferred_element_type=F32) for p in _split3(x)[:pieces])


def _iota(shape, dim):
    return lax.broadcasted_iota(jnp.int32, shape, dim)


def _tril(n, strict=False):
    r, c = _iota((n, n), 0), _iota((n, n), 1)
    return (r > c) if strict else (r >= c)


def _silu(x):
    return x * jax.nn.sigmoid(x)


def _softplus(x):
    return jnp.maximum(x, 0.0) + jnp.log1p(jnp.exp(-jnp.abs(x)))


def _rows(c, n):
    return pl.ds(pl.multiple_of(c * n, n), n)


def _full_spec(shape):
    zeros = (0,) * len(shape)
    return pl.BlockSpec(shape, lambda *_: zeros)


def _layer_spec(arr, layer, **kwargs):
    tail = tuple(arr.shape[1:])
    index = (layer,) + (0,) * len(tail)
    return pl.BlockSpec((None,) + tail, lambda *_: index, **kwargs)


def _params(semantics):
    return pltpu.CompilerParams(dimension_semantics=semantics, vmem_limit_bytes=VMEM_LIMIT)


def _rmsnorm_kernel(x_ref, g_ref, o_ref):
    x = x_ref[...]
    y = x * lax.rsqrt(jnp.mean(x * x, axis=-1, keepdims=True) + NORM_EPS)
    o_ref[...] = (y * g_ref[...]).astype(o_ref.dtype)


def _rmsnorm(x2d, g, out_dtype, interpret):
    m, d = x2d.shape
    tm = min(1024, m)
    return pl.pallas_call(
        _rmsnorm_kernel,
        grid=(m // tm,),
        in_specs=[pl.BlockSpec((tm, d), lambda i: (i, 0)), _full_spec((1, d))],
        out_specs=pl.BlockSpec((tm, d), lambda i: (i, 0)),
        out_shape=jax.ShapeDtypeStruct((m, d), out_dtype),
        compiler_params=_params(("parallel",)),
        interpret=interpret,
        name="rmsnorm",
    )(x2d, g.reshape(1, d))


def _ret_kernel(h_ref, pos_ref, inv_ref, w_ref, o_ref, p_scr, s_scr, *, tb):
    C = RET_CHUNK

    @pl.when(pl.program_id(1) == 0)
    def _():
        s_scr[...] = jnp.zeros_like(s_scr)

    p = jnp.dot(h_ref[...], w_ref[...], preferred_element_type=F32)
    ang = pos_ref[...].astype(F32) * inv_ref[...]
    cos, sin = jnp.cos(ang), jnp.sin(ang)
    q1, q2 = p[:, 0:128], p[:, 128:256]
    k1, k2 = p[:, 256:384] * RET_DK ** -0.5, p[:, 384:512] * RET_DK ** -0.5
    p_scr[:, 0:128] = q1 * cos - q2 * sin
    p_scr[:, 128:256] = q2 * cos + q1 * sin
    p_scr[:, 256:384] = k1 * cos - k2 * sin
    p_scr[:, 384:512] = k2 * cos + k1 * sin
    p_scr[:, 512:1536] = p[:, 512:1536]

    def log_gamma(head):
        return jnp.log(1.0 - jnp.exp2(-5.0 - head.astype(F32)))

    qk_head = (_iota((1, 256), 1) >> 5) & 3
    lg_lane = log_gamma(qk_head)
    tau = _iota((C, 1), 0).astype(F32)
    dq = jnp.exp(lg_lane * (tau + 1.0))
    dk = jnp.exp(lg_lane * (C - 1.0 - tau))
    ds = jnp.exp(lg_lane * float(C))
    diff = (_iota((C, C), 0) - _iota((C, C), 1)).astype(F32)
    causal = _tril(C)
    bd_mask = (_iota((WIDTH, 256), 0) >> 7) == ((_iota((WIDTH, 256), 1) >> 5) & 3)

    heads = range(RET_HEADS)
    hcols = [slice(hd * RET_DV, (hd + 1) * RET_DV) for hd in heads]
    seg = [jnp.where(causal, jnp.exp(math.log(1.0 - 2.0 ** (-5.0 - hd)) * diff), 0.0) for hd in heads]

    def group(gi, carry):
        us = range(RET_GROUP)
        rows = [_rows(gi * RET_GROUP + u, C) for u in us]
        q = [p_scr[rw, 0:256] for rw in rows]
        k = [p_scr[rw, 256:512] for rw in rows]
        v = [p_scr[rw, 512:1024] for rw in rows]
        sc = [[_dot_nt(q[u], jnp.where(qk_head == hd, k[u], 0.0)) * seg[hd] for hd in heads] for u in us]
        y_intra = [[_dot(sc[u][hd], v[u][:, hcols[hd]]) for hd in heads] for u in us]
        kv = [jnp.where(bd_mask, _dot_tn(v[u], k[u] * dk), 0.0) for u in us]
        st = s_scr[...]
        y_inter = []
        for u in us:
            y_inter.append(_dot_nt(q[u] * dq, st))
            st = st * ds + kv[u]
        s_scr[...] = st
        for u in us:
            for hd in heads:
                y = y_intra[u][hd] + y_inter[u][:, hcols[hd]]
                y = y * lax.rsqrt(jnp.mean(y * y, axis=-1, keepdims=True) + NORM_EPS)
                g = p_scr[rows[u], 1024 + hd * RET_DV:1024 + (hd + 1) * RET_DV]
                o_ref[rows[u], hcols[hd]] = (y * _silu(g)).astype(o_ref.dtype)
        return carry

    lax.fori_loop(0, tb // (C * RET_GROUP), group, 0)


def _retention(h, pos_col, inv_row, w, layer, batch, seq, interpret):
    tb = min(TOKEN_TILE, seq)
    nt = seq // tb
    kern = functools.partial(_ret_kernel, tb=tb)
    return pl.pallas_call(
        kern,
        grid=(batch, nt),
        in_specs=[pl.BlockSpec((tb, D_MODEL), lambda b, i: (b * nt + i, 0)),
                  pl.BlockSpec((tb, 1), lambda b, i: (b * nt + i, 0)),
                  _full_spec((1, LANES)),
                  _layer_spec(w, layer)],
        out_specs=pl.BlockSpec((tb, WIDTH), lambda b, i: (b * nt + i, 0)),
        out_shape=jax.ShapeDtypeStruct((batch * seq, WIDTH), MXU_DTYPE),
        scratch_shapes=[pltpu.VMEM((tb, 1536), F32), pltpu.VMEM((WIDTH, 256), F32)],
        compiler_params=_params(("parallel", "arbitrary")),
        interpret=interpret,
        name="retention",
    )(h, pos_col, inv_row, w)


def _gla_kernel(h_ref, w_ref, w2_ref, gb_ref, ng_ref, o_ref, p_scr, lg_scr, s_scr, *, tb):
    C = GLA_CHUNK

    @pl.when(pl.program_id(1) == 0)
    def _():
        s_scr[...] = jnp.zeros_like(s_scr)

    p = jnp.dot(h_ref[...], w_ref[...], preferred_element_type=F32)
    p_scr[...] = p[:, 0:1536]
    pre = _dot(p[:, 1536:1664], w2_ref[...]) + gb_ref[...]
    lg_scr[...] = -_softplus(-pre) / GLA_NORMALIZER

    tri = _tril(C)
    k_head = _iota((1, 256), 1) >> 6
    bd_mask = (_iota((WIDTH, 256), 0) >> 7) == (_iota((WIDTH, 256), 1) >> 6)
    ng = ng_ref[...]

    heads = range(GLA_HEADS)
    hcols = [slice(hd * GLA_DV, (hd + 1) * GLA_DV) for hd in heads]

    def group(gi, carry):
        us = range(GLA_GROUP)
        rows = [_rows(gi * GLA_GROUP + u, C) for u in us]
        q = [p_scr[rw, 0:256] * GLA_DK ** -0.5 for rw in rows]
        k = [p_scr[rw, 256:512] for rw in rows]
        v = [p_scr[rw, 512:1024] for rw in rows]
        cum = [_sel_dot(tri, lg_scr[rw, :]) for rw in rows]
        ref = [c[C // 2:C // 2 + 1, :] for c in cum]
        last = [c[C - 1:C, :] for c in cum]
        q_in = [q[u] * jnp.exp(cum[u] - ref[u]) for u in us]
        k_in = [k[u] * jnp.exp(ref[u] - cum[u]) for u in us]
        q_dec = [q[u] * jnp.exp(cum[u]) for u in us]
        k_st = [k[u] * jnp.exp(last[u] - cum[u]) for u in us]
        sc = [[jnp.where(tri, _dot_nt(q_in[u], jnp.where(k_head == hd, k_in[u], 0.0)), 0.0) for hd in heads]
              for u in us]
        y_intra = [[_dot(sc[u][hd], v[u][:, hcols[hd]]) for hd in heads] for u in us]
        kv = [jnp.where(bd_mask, _dot_tn(v[u], k_st[u]), 0.0) for u in us]
        st = s_scr[...]
        y_inter = []
        for u in us:
            y_inter.append(_dot_nt(q_dec[u], st))
            st = st * jnp.exp(last[u]) + kv[u]
        s_scr[...] = st
        for u in us:
            for hd in heads:
                y = y_intra[u][hd] + y_inter[u][:, hcols[hd]]
                y = y * lax.rsqrt(jnp.mean(y * y, axis=-1, keepdims=True) + NORM_EPS) * ng
                g = p_scr[rows[u], 1024 + hd * GLA_DV:1024 + (hd + 1) * GLA_DV]
                o_ref[rows[u], hcols[hd]] = (y * _silu(g)).astype(o_ref.dtype)
        return carry

    lax.fori_loop(0, tb // (C * GLA_GROUP), group, 0)


def _gla(h, w, w2p, gb, ng, layer, batch, seq, interpret):
    tb = min(TOKEN_TILE, seq)
    nt = seq // tb
    kern = functools.partial(_gla_kernel, tb=tb)
    return pl.pallas_call(
        kern,
        grid=(batch, nt),
        in_specs=[pl.BlockSpec((tb, D_MODEL), lambda b, i: (b * nt + i, 0)),
                  _layer_spec(w, layer), _layer_spec(w2p, layer), _layer_spec(gb, layer),
                  _layer_spec(ng, layer)],
        out_specs=pl.BlockSpec((tb, WIDTH), lambda b, i: (b * nt + i, 0)),
        out_shape=jax.ShapeDtypeStruct((batch * seq, WIDTH), MXU_DTYPE),
        scratch_shapes=[pltpu.VMEM((tb, 1536), F32), pltpu.VMEM((tb, 256), F32),
                        pltpu.VMEM((WIDTH, 256), F32)],
        compiler_params=_params(("parallel", "arbitrary")),
        interpret=interpret,
        name="gla",
    )(h, w, w2p, gb, ng)


def _ssd_kernel(h_ref, w_ref, cw_ref, cb_ref, dtb_ref, a_ref, dskip_ref, ng_ref, exp_ref, o_ref,
                raw_scr, xc_scr, z_scr, dt_scr, s_scr, *, tb):
    C = SSD_CHUNK
    NCH = 1024

    @pl.when(pl.program_id(1) == 0)
    def _():
        s_scr[...] = jnp.zeros_like(s_scr)
        raw_scr[0:CARRY_ROWS, :] = jnp.zeros((CARRY_ROWS, NCH), F32)

    p = jnp.dot(h_ref[...], w_ref[...], preferred_element_type=F32)
    raw_scr[CARRY_ROWS:CARRY_ROWS + tb, :] = p[:, 0:NCH]
    z_scr[...] = p[:, NCH:NCH + WIDTH]
    dt_scr[...] = _softplus(p[:, 1536:1664] + dtb_ref[...])
    conv = cb_ref[...] + sum(
        raw_scr[CARRY_ROWS - (SSD_CONV - 1) + j:CARRY_ROWS - (SSD_CONV - 1) + j + tb, :] * cw_ref[j:j + 1, :]
        for j in range(SSD_CONV))
    xc_scr[...] = _silu(conv)
    raw_scr[0:CARRY_ROWS, :] = raw_scr[tb:tb + CARRY_ROWS, :]

    tri = _tril(C)
    lane_lo = _iota((1, LANES), 1) < SSD_P
    expand = exp_ref[...]
    a_row = a_ref[...]

    groups = range(SSD_GROUPS)
    gcols = [slice(g * 256, (g + 1) * 256) for g in groups]
    ncols = [slice(g * SSD_STATE, (g + 1) * SSD_STATE) for g in groups]

    def group(gi, carry):
        us = range(SSD_GROUP)
        rows = [_rows(gi * SSD_GROUP + u, C) for u in us]
        xs = [xc_scr[rw, 0:512] for rw in rows]
        bm = [xc_scr[rw, 512:768] for rw in rows]
        cm = [xc_scr[rw, 768:1024] for rw in rows]
        dt = [dt_scr[rw, :] for rw in rows]
        cum = [_sel_dot(tri, dt[u] * a_row) for u in us]
        cum_t = [c.T for c in cum]
        dt_e = [_dot_sel(d, expand) for d in dt]
        cum_e = [_dot_sel(c, expand) for c in cum]
        last_e = [c[C - 1:C, :] for c in cum_e]
        xdt = [xs[u] * dt_e[u] for u in us]
        v_st = [xdt[u] * jnp.exp(last_e[u] - cum_e[u]) for u in us]
        e_cum = [jnp.exp(c) for c in cum_e]
        scores = [[_dot_nt(cm[u][:, ncols[g]], bm[u][:, ncols[g]]) for g in groups] for u in us]
        y_intra = []
        for u in us:
            parts = []
            for pr in range(SSD_HEADS // 2):
                sc = scores[u][pr // 2]
                segs = []
                for hd in (2 * pr, 2 * pr + 1):
                    d = jnp.minimum(cum[u][:, hd:hd + 1] - cum_t[u][hd:hd + 1, :], 0.0)
                    segs.append(sc * jnp.where(tri, jnp.exp(d), 0.0))
                xp = xdt[u][:, pr * LANES:(pr + 1) * LANES]
                rhs = jnp.concatenate([jnp.where(lane_lo, xp, 0.0), jnp.where(lane_lo, 0.0, xp)], axis=0)
                parts.append(_dot(jnp.concatenate(segs, axis=1), rhs))
            y_intra.append(jnp.concatenate(parts, axis=1))
        kv = [[_dot_tn(bm[u][:, ncols[g]], v_st[u][:, gcols[g]]) for g in groups] for u in us]
        st = [s_scr[g] for g in groups]
        y_inter = []
        for u in us:
            y_inter.append(jnp.concatenate(
                [_dot(cm[u][:, ncols[g]], st[g]) * e_cum[u][:, gcols[g]] for g in groups], axis=1))
            st = [st[g] * jnp.exp(last_e[u][:, gcols[g]]) + kv[u][g] for g in groups]
        for g in groups:
            s_scr[g] = st[g]
        for u in us:
            y = y_intra[u] + y_inter[u] + dskip_ref[...] * xs[u]
            y = y * _silu(z_scr[rows[u], :])
            for g in groups:
                yg = y[:, gcols[g]]
                yg = yg * lax.rsqrt(jnp.mean(yg * yg, axis=-1, keepdims=True) + NORM_EPS)
                o_ref[rows[u], gcols[g]] = (yg * ng_ref[:, gcols[g]]).astype(o_ref.dtype)
        return carry

    lax.fori_loop(0, tb // (C * SSD_GROUP), group, 0)


def _ssd(h, w, cw, cb, dtb, a_row, dskip, ng, expand, layer, batch, seq, interpret):
    tb = min(TOKEN_TILE, seq)
    nt = seq // tb
    kern = functools.partial(_ssd_kernel, tb=tb)
    small = [cw, cb, dtb, a_row, dskip, ng]
    return pl.pallas_call(
        kern,
        grid=(batch, nt),
        in_specs=[pl.BlockSpec((tb, D_MODEL), lambda b, i: (b * nt + i, 0)), _layer_spec(w, layer)]
        + [_layer_spec(a, layer) for a in small] + [_full_spec(expand.shape)],
        out_specs=pl.BlockSpec((tb, WIDTH), lambda b, i: (b * nt + i, 0)),
        out_shape=jax.ShapeDtypeStruct((batch * seq, WIDTH), MXU_DTYPE),
        scratch_shapes=[pltpu.VMEM((tb + CARRY_ROWS, 1024), F32), pltpu.VMEM((tb, 1024), F32),
                        pltpu.VMEM((tb, WIDTH), F32), pltpu.VMEM((tb, LANES), F32),
                        pltpu.VMEM((SSD_GROUPS, SSD_STATE, 256), F32)],
        compiler_params=_params(("parallel", "arbitrary")),
        interpret=interpret,
        name="ssd",
    )(h, w, *small, expand)


def _rwkv_kernel(h_ref, w_ref, mu_ref, w0_ref, w2_ref, a0_ref, a2_ref, kk_ref, ka_ref, rk_ref,
                 lng_ref, lnb_ref, seg_ref, o_ref,
                 u_scr, g_scr, r_scr, lw_scr, k_scr, v_scr, a_scr, b_scr, y_scr, bonus_scr, st_scr, *, tb):
    C = RWKV_CHUNK
    NU = 3 * WIDTH + 2 * RWKV_RANK

    @pl.when(pl.program_id(1) == 0)
    def _():
        st_scr[...] = jnp.zeros_like(st_scr)
        u_scr[0:CARRY_ROWS, :] = jnp.zeros((CARRY_ROWS, NU), F32)

    p = jnp.dot(h_ref[...], w_ref[...], preferred_element_type=F32)
    g_scr[...] = p[:, NU:NU + WIDTH]
    u = p[:, 0:NU]
    u_scr[CARRY_ROWS:CARRY_ROWS + tb, :] = u
    u_prev = u_scr[CARRY_ROWS - 1:CARRY_ROWS - 1 + tb, :]
    xs = u + (u_prev - u) * mu_ref[...]
    u_scr[0:CARRY_ROWS, :] = u_scr[tb:tb + CARRY_ROWS, :]

    seg = seg_ref[...]

    def seg_sum(x, pieces):
        return jnp.concatenate([_dot_sel(x[:, j * 256:(j + 1) * 256], seg, pieces) for j in range(2)], axis=1)

    r = xs[:, 0:512]
    k = xs[:, 512:1024]
    v = xs[:, 1024:1536]
    wa = xs[:, 1536:1664]
    w_pre = w0_ref[...] + _dot(jnp.tanh(wa), w2_ref[...])
    a_gate = jax.nn.sigmoid(a0_ref[...] + _dot(wa, a2_ref[...]))
    lw_scr[...] = -jnp.exp(-_softplus(-w_pre) - 0.5)
    kk = k * kk_ref[...]
    k = k * (1.0 + (a_gate - 1.0) * ka_ref[...])
    kk = kk / jnp.maximum(jnp.sqrt(seg_sum(kk * kk, 1)), 1e-12)
    r_scr[...] = r
    k_scr[...] = k
    v_scr[...] = v
    a_scr[...] = -kk
    b_scr[...] = kk * a_gate
    bonus_scr[...] = seg_sum(r * k * rk_ref[...], 1) * v

    tri = _tril(C)
    r2, c2 = _iota((2 * C, 2 * C), 0), _iota((2 * C, 2 * C), 1)
    same = (r2 >> 6) == (c2 >> 6)
    low_s = same & ((r2 & 63) > (c2 & 63))
    low_i = same & ((r2 & 63) >= (c2 & 63))
    eye = (r2 == c2).astype(F32)
    lane_lo = _iota((1, LANES), 1) < RWKV_N

    pairs = range(RWKV_HEADS // 2)

    def stack(x, pr):
        xp = x[:, pr * LANES:(pr + 1) * LANES]
        return jnp.concatenate([jnp.where(lane_lo, xp, 0.0), jnp.where(lane_lo, 0.0, xp)], axis=0)

    def group(gi, carry):
        lanes = [(u, pr) for u in range(RWKV_GROUP) for pr in pairs]
        rows, a_0, r_0, v_c, b_h, k_h, e_last, lhs, rhs_t = [], [], [], [], [], [], [], {}, {}
        for u in range(RWKV_GROUP):
            rw = _rows(gi * RWKV_GROUP + u, C)
            lw = lw_scr[rw, :]
            cum = _sel_dot(tri, lw)
            cum_p = cum - lw
            ref = cum[C // 2:C // 2 + 1, :]
            last = cum[C - 1:C, :]
            e_fwd = jnp.exp(cum - ref)
            e_bwd = jnp.exp(ref - cum)
            e_end = jnp.exp(last - cum)
            r_c, k_c, a_c, b_c = r_scr[rw, :], k_scr[rw, :], a_scr[rw, :], b_scr[rw, :]
            r_t = r_c * e_fwd
            a_t = a_c * jnp.exp(cum_p - ref)
            b_t = b_c * e_bwd
            k_t = k_c * e_bwd
            for pr in pairs:
                lhs[u, pr] = jnp.concatenate([stack(a_t, pr), stack(r_t, pr)], axis=0)
                rhs_t[u, pr] = jnp.concatenate([stack(b_t, pr), stack(k_t, pr)], axis=0)
            rows.append(rw)
            a_0.append(a_c * jnp.exp(cum_p))
            r_0.append(r_c * jnp.exp(cum))
            v_c.append(v_scr[rw, :])
            b_h.append(b_c * e_end)
            k_h.append(k_c * e_end)
            e_last.append(jnp.exp(last))
        big = [_dot_nt(lhs[ln], rhs_t[ln]) for ln in lanes]
        a_ab = [jnp.where(low_s, m[0:128, 0:128], 0.0) for m in big]
        a_ak = [jnp.where(low_s, m[0:128, 128:256], 0.0) for m in big]
        a_rb = [jnp.where(low_i, m[128:256, 0:128], 0.0) for m in big]
        a_rk = [jnp.where(low_i, m[128:256, 128:256], 0.0) for m in big]
        inv = [eye + m for m in a_ab]
        pw = [_dot(m, m) for m in a_ab]
        for _ in range(4):
            prod = [_dot(p, jnp.concatenate([p, t], axis=1)) for p, t in zip(pw, inv)]
            pw = [m[:, 0:128] for m in prod]
            inv = [t + m[:, 128:256] for t, m in zip(inv, prod)]
        inv = [t + _dot(p, t) for p, t in zip(pw, inv)]
        vs = [stack(v_c[u], pr) for u, pr in lanes]
        akv = [_dot(m, x) for m, x in zip(a_ak, vs)]
        a_r = [jnp.concatenate([m, n], axis=1) for m, n in zip(a_rb, a_rk)]
        bk_h = [jnp.concatenate([stack(b_h[u], pr), stack(k_h[u], pr)], axis=0) for u, pr in lanes]

        st = [st_scr[pr] for pr in pairs]
        for u in range(RWKV_GROUP):
            ix = [u * len(pairs) + pr for pr in pairs]
            rhs = [_dot_nt(stack(a_0[u], pr), st[pr]) + akv[i] for pr, i in zip(pairs, ix)]
            uv = [jnp.concatenate([_dot(inv[i], m), vs[i]], axis=0) for m, i in zip(rhs, ix)]
            oo = [_dot_nt(stack(r_0[u], pr), st[pr]) + _dot(a_r[i], uv[pr]) for pr, i in zip(pairs, ix)]
            st = [st[pr] * e_last[u][:, pr * LANES:(pr + 1) * LANES] + _dot_tn(uv[pr], bk_h[i])
                  for pr, i in zip(pairs, ix)]
            for pr in pairs:
                y_scr[rows[u], pr * LANES:(pr + 1) * LANES] = oo[pr][0:C, :] + oo[pr][C:2 * C, :]
        for pr in pairs:
            st_scr[pr] = st[pr]
        return carry

    lax.fori_loop(0, tb // (C * RWKV_GROUP), group, 0)

    y = y_scr[...]
    mean = seg_sum(y, 2) * (1.0 / RWKV_N)
    d = y - mean
    var = seg_sum(d * d, 1) * (1.0 / RWKV_N)
    y = d * lax.rsqrt(var + RWKV_LN_EPS) * lng_ref[...] + lnb_ref[...] + bonus_scr[...]
    o_ref[...] = (y * _silu(g_scr[...])).astype(o_ref.dtype)


def _rwkv(h, w, vecs, w2p, a2p, seg, layer, batch, seq, interpret):
    tb = min(TOKEN_TILE, seq)
    nt = seq // tb
    kern = functools.partial(_rwkv_kernel, tb=tb)
    mu, w0, a0, kk, ka, rk, lng, lnb = vecs
    ins = [mu, w0, w2p, a0, a2p, kk, ka, rk, lng, lnb]
    wide = lambda: pltpu.VMEM((tb, WIDTH), F32)
    return pl.pallas_call(
        kern,
        grid=(batch, nt),
        in_specs=[pl.BlockSpec((tb, D_MODEL), lambda b, i: (b * nt + i, 0)), _layer_spec(w, layer)]
        + [_layer_spec(a, layer) for a in ins] + [_full_spec(seg.shape)],
        out_specs=pl.BlockSpec((tb, WIDTH), lambda b, i: (b * nt + i, 0)),
        out_shape=jax.ShapeDtypeStruct((batch * seq, WIDTH), MXU_DTYPE),
        scratch_shapes=[pltpu.VMEM((tb + CARRY_ROWS, 3 * WIDTH + 2 * RWKV_RANK), F32)]
        + [wide() for _ in range(9)]
        + [pltpu.VMEM((RWKV_HEADS // 2, LANES, LANES), F32)],
        compiler_params=_params(("parallel", "arbitrary")),
        interpret=interpret,
        name="rwkv7",
    )(h, w, *ins, seg)


def _memkv_kernel(mem_ref, g_ref, w_ref, k_ref, v_ref):
    x = mem_ref[0]
    y = x * lax.rsqrt(jnp.mean(x * x, axis=-1, keepdims=True) + NORM_EPS) * g_ref[...]
    kv = _dot(y, w_ref[...])
    head = _iota((1, MEM_WIDTH), 1) >> 6
    for hd in range(MEM_HEADS):
        k_ref[0, hd] = jnp.where(head == hd, kv[:, 0:MEM_WIDTH], 0.0).astype(k_ref.dtype)
        v_ref[0, hd] = jnp.where(head == hd, kv[:, MEM_WIDTH:2 * MEM_WIDTH], 0.0).astype(v_ref.dtype)


def _memkv(mem, g, w, layer, interpret):
    b, m, d = mem.shape
    out = jax.ShapeDtypeStruct((b, MEM_HEADS, m, MEM_WIDTH), MXU_DTYPE)
    return pl.pallas_call(
        _memkv_kernel,
        grid=(b,),
        in_specs=[pl.BlockSpec((1, m, d), lambda i: (i, 0, 0)), _layer_spec(g, layer), _layer_spec(w, layer)],
        out_specs=[pl.BlockSpec((1, MEM_HEADS, m, MEM_WIDTH), lambda i: (i, 0, 0, 0))] * 2,
        out_shape=[out, out],
        compiler_params=_params(("parallel",)),
        interpret=interpret,
        name="mem_kv",
    )(mem, g, w)


def _merge_kernel(x_ref, h_ref, oret_ref, ogla_ref, ossd_ref, orwkv_ref, km_ref, vm_ref,
                  wq_ref, wg_ref, uret_ref, ugla_ref, ussd_ref, urwkv_ref, umem_ref, wout_ref, gn_ref,
                  *out_refs):
    h = h_ref[...]
    q = jnp.dot(h, wq_ref[...], preferred_element_type=F32) * MEM_HEAD_DIM ** -0.5
    o_mem = jnp.zeros(q.shape, F32)
    for hd in range(MEM_HEADS):
        s = _dot_nt(q, km_ref[0, hd])
        s = jnp.exp(s - jnp.max(s, axis=-1, keepdims=True))
        prob = s / jnp.sum(s, axis=-1, keepdims=True)
        o_mem = o_mem + _dot(prob, vm_ref[0, hd])
    branches = ((oret_ref, uret_ref), (ogla_ref, ugla_ref), (ossd_ref, ussd_ref), (orwkv_ref, urwkv_ref))
    merged = None
    for i in range(N_BRANCHES):
        gate = jax.nn.sigmoid(jnp.dot(h, wg_ref[:, i * D_MODEL:(i + 1) * D_MODEL], preferred_element_type=F32))
        if i < 4:
            o_ref, u_ref = branches[i]
            up = jnp.dot(o_ref[...], u_ref[...], preferred_element_type=F32)
        else:
            up = _dot(o_mem, umem_ref[...])
        merged = gate * up if merged is None else merged + gate * up
    x = x_ref[...] + _dot(merged, wout_ref[...])
    if len(out_refs) == 2:
        out_refs[0][...] = x
    y = x * lax.rsqrt(jnp.mean(x * x, axis=-1, keepdims=True) + NORM_EPS) * gn_ref[...]
    out_refs[-1][...] = y.astype(out_refs[-1].dtype)


def _merge(x2d, h, o_ret, o_gla, o_ssd, o_rwkv, km, vm, wq, wg, ups, wout, g_next, layer, last,
           batch, seq, interpret):
    tm = min(TOKEN_TILE, seq)
    nt = seq // tm
    row = lambda w: pl.BlockSpec((tm, w), lambda b, i: (b * nt + i, 0))
    kvspec = pl.BlockSpec((1,) + km.shape[1:], lambda b, i: (b, 0, 0, 0))
    weights = [wq, wg, *ups, wout, g_next]
    return pl.pallas_call(
        _merge_kernel,
        grid=(batch, nt),
        in_specs=[row(D_MODEL), row(D_MODEL), row(WIDTH), row(WIDTH), row(WIDTH), row(WIDTH), kvspec, kvspec]
        + [_layer_spec(w, layer, pipeline_mode=pl.Buffered(1)) for w in weights],
        out_specs=[row(D_MODEL)] if last else [row(D_MODEL), row(D_MODEL)],
        out_shape=[jax.ShapeDtypeStruct(x2d.shape, F32)] if last else
        [jax.ShapeDtypeStruct(x2d.shape, F32), jax.ShapeDtypeStruct(x2d.shape, MXU_DTYPE)],
        compiler_params=_params(("parallel", "arbitrary")),
        interpret=interpret,
        name="merge",
    )(x2d, h, o_ret, o_gla, o_ssd, o_rwkv, km, vm, *weights)


def _pad_last(a, width):
    return jnp.pad(a, [(0, 0)] * (a.ndim - 1) + [(0, width - a.shape[-1])])


def _rows3(v, width=None):
    v = v.reshape(v.shape[0], 1, -1).astype(F32)
    return v if width is None else _pad_last(v, width)


def _forward(x, mem, positions, norm_g, w_in, gla_gk_w2, gla_gk_b, gla_norm_g,
             ssd_conv_w, ssd_conv_b, ssd_dt_bias, ssd_a_log, ssd_d, ssd_norm_g,
             rwkv_mu, rwkv_w0, rwkv_w2, rwkv_a0, rwkv_a2, rwkv_k_k, rwkv_k_a, rwkv_r_k,
             rwkv_ln_g, rwkv_ln_b, mem_norm_g, w_mem_kv,
             w_up_ret, w_up_gla, w_up_ssd, w_up_rwkv, w_up_mem, w_out, final_norm_g, interpret=False):
    batch, seq, d = x.shape
    depth = w_in.shape[0]
    cdt = MXU_DTYPE
    o = _IN_OFFS

    half = np.arange(RET_DK // 2)
    ret_perm = np.concatenate([hd * RET_DK + 2 * half + par for par in (0, 1) for hd in range(RET_HEADS)])
    inv = 1.0 / (ROPE_BASE ** jnp.linspace(0.0, 1.0, RET_DK // 2, dtype=F32))
    inv_row = jnp.tile(inv, RET_HEADS).reshape(1, LANES)
    pos_col = positions.reshape(batch * seq, 1)
    head_of_lane = np.arange(WIDTH) // SSD_P
    ssd_expand = jnp.asarray(np.arange(LANES)[:, None] == head_of_lane[None, :], F32)
    rwkv_seg = jnp.asarray(head_of_lane[:256, None] == head_of_lane[None, :256], F32)

    col = lambda i, j=None: w_in[:, :, o[i]:o[(i if j is None else j) + 1]]
    w_ret = jnp.concatenate([col(0)[:, :, ret_perm], col(1)[:, :, ret_perm], col(2), col(3)], axis=2).astype(cdt)
    w_gla = jnp.concatenate([col(4), col(5), col(6), col(8), _pad_last(col(7), LANES)], axis=2).astype(cdt)
    w_ssd = jnp.concatenate([col(9), col(11), _pad_last(col(10), LANES)], axis=2).astype(cdt)
    w_rwkv = col(12, 13).astype(cdt)
    w_q = col(14).astype(cdt)
    w_g = col(15).astype(cdt)
    gla_w2p = jnp.pad(gla_gk_w2, ((0, 0), (0, LANES - GLA_RANK), (0, 0))).astype(cdt)
    zeros_rank = jnp.zeros((depth, RWKV_RANK, WIDTH), F32)
    rwkv_w2p = jnp.concatenate([rwkv_w2, zeros_rank], axis=1).astype(cdt)
    rwkv_a2p = jnp.concatenate([zeros_rank, rwkv_a2], axis=1).astype(cdt)
    rwkv_vecs = [_rows3(v) for v in (rwkv_mu, rwkv_w0, rwkv_a0, rwkv_k_k, rwkv_k_a, rwkv_r_k,
                                     rwkv_ln_g, rwkv_ln_b)]
    ssd_small = [ssd_conv_w.astype(F32), _rows3(ssd_conv_b), _rows3(ssd_dt_bias, LANES),
                 _rows3(-jnp.exp(ssd_a_log.astype(F32)), LANES), _rows3(jnp.repeat(ssd_d, SSD_P, axis=1)),
                 _rows3(ssd_norm_g)]
    gla_b, gla_ng = _rows3(gla_gk_b), _rows3(gla_norm_g)
    mem_g, w_kv = _rows3(mem_norm_g), w_mem_kv.astype(cdt)
    ups = [w.astype(cdt) for w in (w_up_ret, w_up_gla, w_up_ssd, w_up_rwkv, w_up_mem)]
    w_o = w_out.astype(cdt)
    g_next = _rows3(jnp.concatenate([norm_g[1:], final_norm_g[None]], axis=0))

    x2d = x.reshape(batch * seq, d)
    h = _rmsnorm(x2d, norm_g[0], cdt, interpret)
    for l in range(depth):
        o_ret = _retention(h, pos_col, inv_row, w_ret, l, batch, seq, interpret)
        o_gla = _gla(h, w_gla, gla_w2p, gla_b, gla_ng, l, batch, seq, interpret)
        o_ssd = _ssd(h, w_ssd, *ssd_small, ssd_expand, l, batch, seq, interpret)
        o_rwkv = _rwkv(h, w_rwkv, rwkv_vecs, rwkv_w2p, rwkv_a2p, rwkv_seg, l, batch, seq, interpret)
        km, vm = _memkv(mem, mem_g, w_kv, l, interpret)
        last = l == depth - 1
        outs = _merge(x2d, h, o_ret, o_gla, o_ssd, o_rwkv, km, vm, w_q, w_g, ups, w_o, g_next, l, last,
                      batch, seq, interpret)
        x2d, h = (None, outs[0]) if last else outs
    return h.reshape(batch, seq, d)


def kernel(x, mem, positions, norm_g, w_in, gla_gk_w2, gla_gk_b, gla_norm_g, ssd_conv_w, ssd_conv_b, ssd_dt_bias, ssd_a_log, ssd_d, ssd_norm_g, rwkv_mu, rwkv_w0, rwkv_w2, rwkv_a0, rwkv_a2, rwkv_k_k, rwkv_k_a, rwkv_r_k, rwkv_ln_g, rwkv_ln_b, mem_norm_g, w_mem_kv, w_up_ret, w_up_gla, w_up_ssd, w_up_rwkv, w_up_mem, w_out, final_norm_g):
    return _forward(x, mem, positions, norm_g, w_in, gla_gk_w2, gla_gk_b, gla_norm_g,
                    ssd_conv_w, ssd_conv_b, ssd_dt_bias, ssd_a_log, ssd_d, ssd_norm_g,
                    rwkv_mu, rwkv_w0, rwkv_w2, rwkv_a0, rwkv_a2, rwkv_k_k, rwkv_k_a, rwkv_r_k,
                    rwkv_ln_g, rwkv_ln_b, mem_norm_g, w_mem_kv,
                    w_up_ret, w_up_gla, w_up_ssd, w_up_rwkv, w_up_mem, w_out, final_norm_g)
```

```python
import functools
import math

import jax
import jax.numpy as jnp
import numpy as np
from jax import lax
from jax.experimental import pallas as pl
from jax.experimental.pallas import tpu as pltpu

F32 = jnp.float32
BF16 = jnp.bfloat16
MXU_DTYPE = jnp.bfloat16

D_MODEL = 1024
WIDTH = 512
NORM_EPS = 1e-6
N_BRANCHES = 5

RET_HEADS, RET_DK, RET_DV, RET_CHUNK = 4, 64, 128, 128
ROPE_BASE = 10000.0
GLA_HEADS, GLA_DK, GLA_DV, GLA_RANK, GLA_NORMALIZER, GLA_CHUNK = 4, 64, 128, 16, 16.0, 64
SSD_HEADS, SSD_P, SSD_GROUPS, SSD_STATE, SSD_CONV, SSD_CHUNK = 8, 64, 2, 128, 4, 128
RWKV_HEADS, RWKV_N, RWKV_RANK, RWKV_CHUNK = 8, 64, 64, 64
RWKV_LN_EPS = 64e-5
RET_GROUP = 4
SSD_GROUP = 4
GLA_GROUP = 8
RWKV_GROUP = 8
MEM_HEADS, MEM_HEAD_DIM, MEM_WIDTH = 4, 64, 256

LANES = 128
CARRY_ROWS = 8
TOKEN_TILE = 512
MIXER_TILE = 512
VMEM_LIMIT = 56 * 1024 * 1024

_IN_SIZES = (256, 256, 512, 512, 256, 256, 512, 16, 512, 1024, 8, 512, 1664, 512, 256, 5120)
_IN_OFFS = tuple(int(v) for v in np.cumsum((0,) + _IN_SIZES))


def _dot(a, b):
    return jnp.dot(a.astype(MXU_DTYPE), b.astype(MXU_DTYPE), preferred_element_type=F32)


def _dot_nt(a, b):
    return lax.dot_general(a.astype(MXU_DTYPE), b.astype(MXU_DTYPE), (((1,), (1,)), ((), ())),
                           preferred_element_type=F32)


def _dot_tn(a, b):
    return lax.dot_general(a.astype(MXU_DTYPE), b.astype(MXU_DTYPE), (((0,), (0,)), ((), ())),
                           preferred_element_type=F32)


def _split3(x):
    hi = x.astype(BF16)
    r1 = x - hi.astype(F32)
    mid = r1.astype(BF16)
    lo = (r1 - mid.astype(F32)).astype(BF16)
    return hi, mid, lo


def _sel_dot(sel, x):
    s = sel.astype(BF16)
    return sum(jnp.dot(s, p, preferred_element_type=F32) for p in _split3(x))


def _dot_sel(x, sel, pieces=3):
    s = sel.astype(BF16)
    return sum(jnp.dot(p, s, preferred_element_type=F32) for p in _split3(x)[:pieces])


def _iota(shape, dim):
    return lax.broadcasted_iota(jnp.int32, shape, dim)


def _tril(n, strict=False):
    r, c = _iota((n, n), 0), _iota((n, n), 1)
    return (r > c) if strict else (r >= c)


def _silu(x):
    return x * jax.nn.sigmoid(x)


def _softplus(x):
    return jnp.maximum(x, 0.0) + jnp.log1p(jnp.exp(-jnp.abs(x)))


def _rows(c, n):
    return pl.ds(pl.multiple_of(c * n, n), n)


def _loop(trips, body):
    if trips == 1:
        body(0, 0)
    else:
        lax.fori_loop(0, trips, body, 0)


def _full_spec(shape):
    zeros = (0,) * len(shape)
    return pl.BlockSpec(shape, lambda *_: zeros)


def _layer_spec(arr, layer, **kwargs):
    tail = tuple(arr.shape[1:])
    index = (layer,) + (0,) * len(tail)
    return pl.BlockSpec((None,) + tail, lambda *_: index, **kwargs)


def _params(semantics):
    return pltpu.CompilerParams(dimension_semantics=semantics, vmem_limit_bytes=VMEM_LIMIT)


def _rmsnorm_kernel(x_ref, g_ref, o_ref):
    x = x_ref[...]
    y = x * lax.rsqrt(jnp.mean(x * x, axis=-1, keepdims=True) + NORM_EPS)
    o_ref[...] = (y * g_ref[...]).astype(o_ref.dtype)


def _rmsnorm(x2d, g, out_dtype, interpret):
    m, d = x2d.shape
    tm = min(1024, m)
    return pl.pallas_call(
        _rmsnorm_kernel,
        grid=(m // tm,),
        in_specs=[pl.BlockSpec((tm, d), lambda i: (i, 0)), _full_spec((1, d))],
        out_specs=pl.BlockSpec((tm, d), lambda i: (i, 0)),
        out_shape=jax.ShapeDtypeStruct((m, d), out_dtype),
        compiler_params=_params(("parallel",)),
        interpret=interpret,
        name="rmsnorm",
    )(x2d, g.reshape(1, d))


PREP_ROWS = 128


def _prep_kernel(w_ref, perm_ref, ret_ref, gla_ref, ssd_ref, rwkv_ref, q_ref, g_ref):
    o = _IN_OFFS
    cast = lambda v: v.astype(MXU_DTYPE)
    cols = lambda i, j=None: w_ref[:, o[i]:o[(i if j is None else j) + 1]]
    lane = _iota((1, LANES), 1)

    def narrow(i):
        lo, width = o[i], o[i + 1] - o[i]
        base = lo - lo % LANES
        block = w_ref[:, base:base + LANES]
        if lo != base:
            block = pltpu.roll(block, LANES - (lo - base), 1)
        return cast(jnp.where(lane < width, block, 0.0))

    perm = perm_ref[...]
    ret_ref[:, 0:256] = jnp.dot(cast(cols(0)), perm, preferred_element_type=F32).astype(ret_ref.dtype)
    ret_ref[:, 256:512] = jnp.dot(cast(cols(1)), perm, preferred_element_type=F32).astype(ret_ref.dtype)
    ret_ref[:, 512:1536] = cast(cols(2, 3))
    gla_ref[:, 0:1024] = cast(cols(4, 6))
    gla_ref[:, 1024:1536] = cast(cols(8))
    gla_ref[:, 1536:1664] = narrow(7)
    ssd_ref[:, 0:1024] = cast(cols(9))
    ssd_ref[:, 1024:1536] = cast(cols(11))
    ssd_ref[:, 1536:1664] = narrow(10)
    rwkv_ref[...] = cast(cols(12, 13))
    q_ref[...] = cast(cols(14))
    g_ref[...] = cast(cols(15))


def _prep_weights(w_in, perm, interpret):
    depth, d, n = w_in.shape
    widths = (1536, 1664, 1664, 2176, MEM_WIDTH, N_BRANCHES * D_MODEL)
    return pl.pallas_call(
        _prep_kernel,
        grid=(depth, d // PREP_ROWS),
        in_specs=[pl.BlockSpec((None, PREP_ROWS, n), lambda l, i: (l, i, 0)), _full_spec(perm.shape)],
        out_specs=[pl.BlockSpec((None, PREP_ROWS, w), lambda l, i: (l, i, 0)) for w in widths],
        out_shape=[jax.ShapeDtypeStruct((depth, d, w), MXU_DTYPE) for w in widths],
        compiler_params=_params(("parallel", "parallel")),
        interpret=interpret,
        name="prep_weights",
    )(w_in, perm)


def _ret_kernel(h_ref, pos_ref, inv_ref, w_ref, o_ref, p_scr, s_scr, *, tb):
    C = RET_CHUNK

    @pl.when(pl.program_id(1) == 0)
    def _():
        s_scr[...] = jnp.zeros_like(s_scr)

    p = jnp.dot(h_ref[...], w_ref[...], preferred_element_type=F32)
    ang = pos_ref[...].astype(F32) * inv_ref[...]
    cos, sin = jnp.cos(ang), jnp.sin(ang)
    q1, q2 = p[:, 0:128], p[:, 128:256]
    k1, k2 = p[:, 256:384] * RET_DK ** -0.5, p[:, 384:512] * RET_DK ** -0.5
    p_scr[:, 0:128] = q1 * cos - q2 * sin
    p_scr[:, 128:256] = q2 * cos + q1 * sin
    p_scr[:, 256:384] = k1 * cos - k2 * sin
    p_scr[:, 384:512] = k2 * cos + k1 * sin
    p_scr[:, 512:1536] = p[:, 512:1536]

    def log_gamma(head):
        return jnp.log(1.0 - jnp.exp2(-5.0 - head.astype(F32)))

    qk_head = (_iota((1, 256), 1) >> 5) & 3
    lg_lane = log_gamma(qk_head)
    tau = _iota((C, 1), 0).astype(F32)
    dq = jnp.exp(lg_lane * (tau + 1.0))
    dk = jnp.exp(lg_lane * (C - 1.0 - tau))
    ds = jnp.exp(lg_lane * float(C))
    diff = (_iota((C, C), 0) - _iota((C, C), 1)).astype(F32)
    causal = _tril(C)
    bd_mask = (_iota((WIDTH, 256), 0) >> 7) == ((_iota((WIDTH, 256), 1) >> 5) & 3)

    heads = range(RET_HEADS)
    hcols = [slice(hd * RET_DV, (hd + 1) * RET_DV) for hd in heads]
    seg = [jnp.where(causal, jnp.exp(math.log(1.0 - 2.0 ** (-5.0 - hd)) * diff), 0.0) for hd in heads]

    def group(gi, carry):
        us = range(RET_GROUP)
        rows = [_rows(gi * RET_GROUP + u, C) for u in us]
        q = [p_scr[rw, 0:256] for rw in rows]
        k = [p_scr[rw, 256:512] for rw in rows]
        v = [p_scr[rw, 512:1024] for rw in rows]
        sc = [[_dot_nt(q[u], jnp.where(qk_head == hd, k[u], 0.0)) * seg[hd] for hd in heads] for u in us]
        y_intra = [[_dot(sc[u][hd], v[u][:, hcols[hd]]) for hd in heads] for u in us]
        kv = [jnp.where(bd_mask, _dot_tn(v[u], k[u] * dk), 0.0) for u in us]
        st = s_scr[...]
        y_inter = []
        for u in us:
            y_inter.append(_dot_nt(q[u] * dq, st))
            st = st * ds + kv[u]
        s_scr[...] = st
        for u in us:
            for hd in heads:
                y = y_intra[u][hd] + y_inter[u][:, hcols[hd]]
                y = y * lax.rsqrt(jnp.mean(y * y, axis=-1, keepdims=True) + NORM_EPS)
                g = p_scr[rows[u], 1024 + hd * RET_DV:1024 + (hd + 1) * RET_DV]
                o_ref[rows[u], hcols[hd]] = (y * _silu(g)).astype(o_ref.dtype)
        return carry

    _loop(tb // (C * RET_GROUP), group)


def _retention(h, pos_col, inv_row, w, layer, batch, seq, interpret):
    tb = min(MIXER_TILE, seq)
    nt = seq // tb
    kern = functools.partial(_ret_kernel, tb=tb)
    return pl.pallas_call(
        kern,
        grid=(batch, nt),
        in_specs=[pl.BlockSpec((tb, D_MODEL), lambda b, i: (b * nt + i, 0)),
                  pl.BlockSpec((tb, 1), lambda b, i: (b * nt + i, 0)),
                  _full_spec((1, LANES)),
                  _layer_spec(w, layer)],
        out_specs=pl.BlockSpec((tb, WIDTH), lambda b, i: (b * nt + i, 0)),
        out_shape=jax.ShapeDtypeStruct((batch * seq, WIDTH), MXU_DTYPE),
        scratch_shapes=[pltpu.VMEM((tb, 1536), F32), pltpu.VMEM((WIDTH, 256), F32)],
        compiler_params=_params(("parallel", "arbitrary")),
        interpret=interpret,
        name="retention",
    )(h, pos_col, inv_row, w)


def _gla_kernel(h_ref, w_ref, w2_ref, gb_ref, ng_ref, o_ref, p_scr, lg_scr, s_scr, *, tb):
    C = GLA_CHUNK

    @pl.when(pl.program_id(1) == 0)
    def _():
        s_scr[...] = jnp.zeros_like(s_scr)

    p = jnp.dot(h_ref[...], w_ref[...], preferred_element_type=F32)
    p_scr[...] = p[:, 0:1536]
    pre = _dot(p[:, 1536:1664], w2_ref[...]) + gb_ref[...]
    lg_scr[...] = -_softplus(-pre) / GLA_NORMALIZER

    tri = _tril(C)
    k_head = _iota((1, 256), 1) >> 6
    bd_mask = (_iota((WIDTH, 256), 0) >> 7) == (_iota((WIDTH, 256), 1) >> 6)
    ng = ng_ref[...]

    heads = range(GLA_HEADS)
    hcols = [slice(hd * GLA_DV, (hd + 1) * GLA_DV) for hd in heads]

    def group(gi, carry):
        us = range(GLA_GROUP)
        rows = [_rows(gi * GLA_GROUP + u, C) for u in us]
        q = [p_scr[rw, 0:256] * GLA_DK ** -0.5 for rw in rows]
        k = [p_scr[rw, 256:512] for rw in rows]
        v = [p_scr[rw, 512:1024] for rw in rows]
        cum = [_sel_dot(tri, lg_scr[rw, :]) for rw in rows]
        ref = [c[C // 2:C // 2 + 1, :] for c in cum]
        last = [c[C - 1:C, :] for c in cum]
        q_in = [q[u] * jnp.exp(cum[u] - ref[u]) for u in us]
        k_in = [k[u] * jnp.exp(ref[u] - cum[u]) for u in us]
        q_dec = [q[u] * jnp.exp(cum[u]) for u in us]
        k_st = [k[u] * jnp.exp(last[u] - cum[u]) for u in us]
        sc = [[jnp.where(tri, _dot_nt(q_in[u], jnp.where(k_head == hd, k_in[u], 0.0)), 0.0) for hd in heads]
              for u in us]
        y_intra = [[_dot(sc[u][hd], v[u][:, hcols[hd]]) for hd in heads] for u in us]
        kv = [jnp.where(bd_mask, _dot_tn(v[u], k_st[u]), 0.0) for u in us]
        st = s_scr[...]
        y_inter = []
        for u in us:
            y_inter.append(_dot_nt(q_dec[u], st))
            st = st * jnp.exp(last[u]) + kv[u]
        s_scr[...] = st
        for u in us:
            for hd in heads:
                y = y_intra[u][hd] + y_inter[u][:, hcols[hd]]
                y = y * lax.rsqrt(jnp.mean(y * y, axis=-1, keepdims=True) + NORM_EPS) * ng
                g = p_scr[rows[u], 1024 + hd * GLA_DV:1024 + (hd + 1) * GLA_DV]
                o_ref[rows[u], hcols[hd]] = (y * _silu(g)).astype(o_ref.dtype)
        return carry

    _loop(tb // (C * GLA_GROUP), group)


def _gla(h, w, w2p, gb, ng, layer, batch, seq, interpret):
    tb = min(MIXER_TILE, seq)
    nt = seq // tb
    kern = functools.partial(_gla_kernel, tb=tb)
    return pl.pallas_call(
        kern,
        grid=(batch, nt),
        in_specs=[pl.BlockSpec((tb, D_MODEL), lambda b, i: (b * nt + i, 0)),
                  _layer_spec(w, layer), _layer_spec(w2p, layer), _layer_spec(gb, layer),
                  _layer_spec(ng, layer)],
        out_specs=pl.BlockSpec((tb, WIDTH), lambda b, i: (b * nt + i, 0)),
        out_shape=jax.ShapeDtypeStruct((batch * seq, WIDTH), MXU_DTYPE),
        scratch_shapes=[pltpu.VMEM((tb, 1536), F32), pltpu.VMEM((tb, 256), F32),
                        pltpu.VMEM((WIDTH, 256), F32)],
        compiler_params=_params(("parallel", "arbitrary")),
        interpret=interpret,
        name="gla",
    )(h, w, w2p, gb, ng)


def _ssd_kernel(h_ref, w_ref, cw_ref, cb_ref, dtb_ref, a_ref, dskip_ref, ng_ref, exp_ref, o_ref,
                raw_scr, xc_scr, z_scr, dt_scr, s_scr, *, tb):
    C = SSD_CHUNK
    NCH = 1024

    @pl.when(pl.program_id(1) == 0)
    def _():
        s_scr[...] = jnp.zeros_like(s_scr)
        raw_scr[0:CARRY_ROWS, :] = jnp.zeros((CARRY_ROWS, NCH), F32)

    p = jnp.dot(h_ref[...], w_ref[...], preferred_element_type=F32)
    raw_scr[CARRY_ROWS:CARRY_ROWS + tb, :] = p[:, 0:NCH]
    z_scr[...] = p[:, NCH:NCH + WIDTH]
    dt_scr[...] = _softplus(p[:, 1536:1664] + dtb_ref[...])
    conv = cb_ref[...] + sum(
        raw_scr[CARRY_ROWS - (SSD_CONV - 1) + j:CARRY_ROWS - (SSD_CONV - 1) + j + tb, :] * cw_ref[j:j + 1, :]
        for j in range(SSD_CONV))
    xc_scr[...] = _silu(conv)
    raw_scr[0:CARRY_ROWS, :] = raw_scr[tb:tb + CARRY_ROWS, :]

    tri = _tril(C)
    lane_lo = _iota((1, LANES), 1) < SSD_P
    expand = exp_ref[...]
    a_row = a_ref[...]

    groups = range(SSD_GROUPS)
    gcols = [slice(g * 256, (g + 1) * 256) for g in groups]
    ncols = [slice(g * SSD_STATE, (g + 1) * SSD_STATE) for g in groups]

    def group(gi, carry):
        us = range(SSD_GROUP)
        rows = [_rows(gi * SSD_GROUP + u, C) for u in us]
        xs = [xc_scr[rw, 0:512] for rw in rows]
        bm = [xc_scr[rw, 512:768] for rw in rows]
        cm = [xc_scr[rw, 768:1024] for rw in rows]
        dt = [dt_scr[rw, :] for rw in rows]
        cum = [_sel_dot(tri, dt[u] * a_row) for u in us]
        cum_t = [c.T for c in cum]
        dt_e = [_dot_sel(d, expand) for d in dt]
        cum_e = [_dot_sel(c, expand) for c in cum]
        last_e = [c[C - 1:C, :] for c in cum_e]
        xdt = [xs[u] * dt_e[u] for u in us]
        v_st = [xdt[u] * jnp.exp(last_e[u] - cum_e[u]) for u in us]
        e_cum = [jnp.exp(c) for c in cum_e]
        scores = [[_dot_nt(cm[u][:, ncols[g]], bm[u][:, ncols[g]]) for g in groups] for u in us]
        y_intra = []
        for u in us:
            parts = []
            for pr in range(SSD_HEADS // 2):
                sc = scores[u][pr // 2]
                segs = []
                for hd in (2 * pr, 2 * pr + 1):
                    d = jnp.minimum(cum[u][:, hd:hd + 1] - cum_t[u][hd:hd + 1, :], 0.0)
                    segs.append(sc * jnp.where(tri, jnp.exp(d), 0.0))
                xp = xdt[u][:, pr * LANES:(pr + 1) * LANES]
                rhs = jnp.concatenate([jnp.where(lane_lo, xp, 0.0), jnp.where(lane_lo, 0.0, xp)], axis=0)
                parts.append(_dot(jnp.concatenate(segs, axis=1), rhs))
            y_intra.append(jnp.concatenate(parts, axis=1))
        kv = [[_dot_tn(bm[u][:, ncols[g]], v_st[u][:, gcols[g]]) for g in groups] for u in us]
        st = [s_scr[g] for g in groups]
        y_inter = []
        for u in us:
            y_inter.append(jnp.concatenate(
                [_dot(cm[u][:, ncols[g]], st[g]) * e_cum[u][:, gcols[g]] for g in groups], axis=1))
            st = [st[g] * jnp.exp(last_e[u][:, gcols[g]]) + kv[u][g] for g in groups]
        for g in groups:
            s_scr[g] = st[g]
        for u in us:
            y = y_intra[u] + y_inter[u] + dskip_ref[...] * xs[u]
            y = y * _silu(z_scr[rows[u], :])
            for g in groups:
                yg = y[:, gcols[g]]
                yg = yg * lax.rsqrt(jnp.mean(yg * yg, axis=-1, keepdims=True) + NORM_EPS)
                o_ref[rows[u], gcols[g]] = (yg * ng_ref[:, gcols[g]]).astype(o_ref.dtype)
        return carry

    _loop(tb // (C * SSD_GROUP), group)


def _ssd(h, w, cw, cb, dtb, a_row, dskip, ng, expand, layer, batch, seq, interpret):
    tb = min(MIXER_TILE, seq)
    nt = seq // tb
    kern = functools.partial(_ssd_kernel, tb=tb)
    small = [cw, cb, dtb, a_row, dskip, ng]
    return pl.pallas_call(
        kern,
        grid=(batch, nt),
        in_specs=[pl.BlockSpec((tb, D_MODEL), lambda b, i: (b * nt + i, 0)), _layer_spec(w, layer)]
        + [_layer_spec(a, layer) for a in small] + [_full_spec(expand.shape)],
        out_specs=pl.BlockSpec((tb, WIDTH), lambda b, i: (b * nt + i, 0)),
        out_shape=jax.ShapeDtypeStruct((batch * seq, WIDTH), MXU_DTYPE),
        scratch_shapes=[pltpu.VMEM((tb + CARRY_ROWS, 1024), F32), pltpu.VMEM((tb, 1024), F32),
                        pltpu.VMEM((tb, WIDTH), F32), pltpu.VMEM((tb, LANES), F32),
                        pltpu.VMEM((SSD_GROUPS, SSD_STATE, 256), F32)],
        compiler_params=_params(("parallel", "arbitrary")),
        interpret=interpret,
        name="ssd",
    )(h, w, *small, expand)


def _rwkv_kernel(h_ref, w_ref, mu_ref, w0_ref, w2_ref, a0_ref, a2_ref, kk_ref, ka_ref, rk_ref,
                 lng_ref, lnb_ref, seg_ref, o_ref,
                 u_scr, g_scr, r_scr, lw_scr, k_scr, v_scr, a_scr, b_scr, y_scr, bonus_scr, st_scr, *, tb):
    C = RWKV_CHUNK
    NU = 3 * WIDTH + 2 * RWKV_RANK

    @pl.when(pl.program_id(1) == 0)
    def _():
        st_scr[...] = jnp.zeros_like(st_scr)
        u_scr[0:CARRY_ROWS, :] = jnp.zeros((CARRY_ROWS, NU), F32)

    p = jnp.dot(h_ref[...], w_ref[...], preferred_element_type=F32)
    g_scr[...] = p[:, NU:NU + WIDTH]
    u = p[:, 0:NU]
    u_scr[CARRY_ROWS:CARRY_ROWS + tb, :] = u
    u_prev = u_scr[CARRY_ROWS - 1:CARRY_ROWS - 1 + tb, :]
    xs = u + (u_prev - u) * mu_ref[...]
    u_scr[0:CARRY_ROWS, :] = u_scr[tb:tb + CARRY_ROWS, :]

    seg = seg_ref[...]

    def seg_sum(x, pieces):
        return jnp.concatenate([_dot_sel(x[:, j * 256:(j + 1) * 256], seg, pieces) for j in range(2)], axis=1)

    r = xs[:, 0:512]
    k = xs[:, 512:1024]
    v = xs[:, 1024:1536]
    wa = xs[:, 1536:1664]
    w_pre = w0_ref[...] + _dot(jnp.tanh(wa), w2_ref[...])
    a_gate = jax.nn.sigmoid(a0_ref[...] + _dot(wa, a2_ref[...]))
    lw_scr[...] = -jnp.exp(-_softplus(-w_pre) - 0.5)
    kk = k * kk_ref[...]
    k = k * (1.0 + (a_gate - 1.0) * ka_ref[...])
    kk = kk / jnp.maximum(jnp.sqrt(seg_sum(kk * kk, 1)), 1e-12)
    r_scr[...] = r
    k_scr[...] = k
    v_scr[...] = v
    a_scr[...] = -kk
    b_scr[...] = kk * a_gate
    bonus_scr[...] = seg_sum(r * k * rk_ref[...], 1) * v

    tri = _tril(C)
    r2, c2 = _iota((2 * C, 2 * C), 0), _iota((2 * C, 2 * C), 1)
    same = (r2 >> 6) == (c2 >> 6)
    low_s = same & ((r2 & 63) > (c2 & 63))
    low_i = same & ((r2 & 63) >= (c2 & 63))
    eye = (r2 == c2).astype(F32)
    lane_lo = _iota((1, LANES), 1) < RWKV_N

    pairs = range(RWKV_HEADS // 2)

    def stack_pair(xp):
        return jnp.concatenate([jnp.where(lane_lo, xp, 0.0), jnp.where(lane_lo, 0.0, xp)], axis=0)

    def stack(x, pr):
        return stack_pair(x[:, pr * LANES:(pr + 1) * LANES])

    def group(gi, carry):
        lanes = [(u, pr) for u in range(RWKV_GROUP) for pr in pairs]
        rows, a_0, r_0, v_c, b_h, k_h, e_last, lhs, rhs_t = [], [], [], [], [], [], [], {}, {}
        for u in range(RWKV_GROUP):
            rw = _rows(gi * RWKV_GROUP + u, C)
            lw = lw_scr[rw, :]
            cum = _sel_dot(tri, lw)
            cum_p = cum - lw
            ref = cum[C // 2:C // 2 + 1, :]
            last = cum[C - 1:C, :]
            e_fwd = jnp.exp(cum - ref)
            e_bwd = jnp.exp(ref - cum)
            e_end = jnp.exp(last - cum)
            r_c, k_c, a_c, b_c = r_scr[rw, :], k_scr[rw, :], a_scr[rw, :], b_scr[rw, :]
            r_t = r_c * e_fwd
            a_t = a_c * jnp.exp(cum_p - ref)
            b_t = b_c * e_bwd
            k_t = k_c * e_bwd
            for pr in pairs:
                lhs[u, pr] = jnp.concatenate([stack(a_t, pr), stack(r_t, pr)], axis=0)
                rhs_t[u, pr] = jnp.concatenate([stack(b_t, pr), stack(k_t, pr)], axis=0)
            rows.append(rw)
            a_0.append(a_c * jnp.exp(cum_p))
            r_0.append(r_c * jnp.exp(cum))
            v_c.append(v_scr[rw, :])
            b_h.append(b_c * e_end)
            k_h.append(k_c * e_end)
            e_last.append(jnp.exp(last))
        big = [_dot_nt(lhs[ln], rhs_t[ln]) for ln in lanes]
        a_ab = [jnp.where(low_s, m[0:128, 0:128], 0.0) for m in big]
        a_ak = [jnp.where(low_s, m[0:128, 128:256], 0.0) for m in big]
        a_rb = [jnp.where(low_i, m[128:256, 0:128], 0.0) for m in big]
        a_rk = [jnp.where(low_i, m[128:256, 128:256], 0.0) for m in big]
        inv = [eye + m for m in a_ab]
        pw = [_dot(m, m) for m in a_ab]
        for _ in range(4):
            prod = [_dot(p, jnp.concatenate([p, t], axis=1)) for p, t in zip(pw, inv)]
            pw = [m[:, 0:128] for m in prod]
            inv = [t + m[:, 128:256] for t, m in zip(inv, prod)]
        inv = [t + _dot(p, t) for p, t in zip(pw, inv)]
        vs = [stack(v_c[u], pr) for u, pr in lanes]
        akv = [_dot(m, x) for m, x in zip(a_ak, vs)]
        a_r = [jnp.concatenate([m, n], axis=1) for m, n in zip(a_rb, a_rk)]
        bk_h = [jnp.concatenate([stack(b_h[u], pr), stack(k_h[u], pr)], axis=0) for u, pr in lanes]

        st = [st_scr[pr] for pr in pairs]
        for u in range(RWKV_GROUP):
            ix = [u * len(pairs) + pr for pr in pairs]
            pcols = [slice(pr * LANES, (pr + 1) * LANES) for pr in pairs]
            rhs = [stack_pair(_dot_nt(a_0[u][:, pcols[pr]], st[pr])) + akv[i] for pr, i in zip(pairs, ix)]
            uv = [jnp.concatenate([_dot(inv[i], m), vs[i]], axis=0) for m, i in zip(rhs, ix)]
            oo = [_dot(a_r[i], uv[pr]) for pr, i in zip(pairs, ix)]
            y_st = [_dot_nt(r_0[u][:, pcols[pr]], st[pr]) for pr in pairs]
            st = [st[pr] * e_last[u][:, pcols[pr]] + _dot_tn(uv[pr], bk_h[i]) for pr, i in zip(pairs, ix)]
            for pr in pairs:
                y_scr[rows[u], pcols[pr]] = y_st[pr] + oo[pr][0:C, :] + oo[pr][C:2 * C, :]
        for pr in pairs:
            st_scr[pr] = st[pr]
        return carry

    _loop(tb // (C * RWKV_GROUP), group)

    y = y_scr[...]
    mean = seg_sum(y, 2) * (1.0 / RWKV_N)
    d = y - mean
    var = seg_sum(d * d, 1) * (1.0 / RWKV_N)
    y = d * lax.rsqrt(var + RWKV_LN_EPS) * lng_ref[...] + lnb_ref[...] + bonus_scr[...]
    o_ref[...] = (y * _silu(g_scr[...])).astype(o_ref.dtype)


def _rwkv(h, w, vecs, w2p, a2p, seg, layer, batch, seq, interpret):
    tb = min(TOKEN_TILE, seq)
    nt = seq // tb
    kern = functools.partial(_rwkv_kernel, tb=tb)
    mu, w0, a0, kk, ka, rk, lng, lnb = vecs
    ins = [mu, w0, w2p, a0, a2p, kk, ka, rk, lng, lnb]
    wide = lambda: pltpu.VMEM((tb, WIDTH), F32)
    return pl.pallas_call(
        kern,
        grid=(batch, nt),
        in_specs=[pl.BlockSpec((tb, D_MODEL), lambda b, i: (b * nt + i, 0)), _layer_spec(w, layer)]
        + [_layer_spec(a, layer) for a in ins] + [_full_spec(seg.shape)],
        out_specs=pl.BlockSpec((tb, WIDTH), lambda b, i: (b * nt + i, 0)),
        out_shape=jax.ShapeDtypeStruct((batch * seq, WIDTH), MXU_DTYPE),
        scratch_shapes=[pltpu.VMEM((tb + CARRY_ROWS, 3 * WIDTH + 2 * RWKV_RANK), F32)]
        + [wide() for _ in range(9)]
        + [pltpu.VMEM((RWKV_HEADS // 2, LANES, LANES), F32)],
        compiler_params=_params(("parallel", "arbitrary")),
        interpret=interpret,
        name="rwkv7",
    )(h, w, *ins, seg)


def _memkv_kernel(mem_ref, g_ref, w_ref, k_ref, v_ref):
    x = mem_ref[0]
    y = x * lax.rsqrt(jnp.mean(x * x, axis=-1, keepdims=True) + NORM_EPS) * g_ref[...]
    kv = _dot(y, w_ref[...])
    head = _iota((1, MEM_WIDTH), 1) >> 6
    for hd in range(MEM_HEADS):
        k_ref[0, hd] = jnp.where(head == hd, kv[:, 0:MEM_WIDTH], 0.0).astype(k_ref.dtype)
        v_ref[0, hd] = jnp.where(head == hd, kv[:, MEM_WIDTH:2 * MEM_WIDTH], 0.0).astype(v_ref.dtype)


def _memkv(mem, g, w, layer, interpret):
    b, m, d = mem.shape
    out = jax.ShapeDtypeStruct((b, MEM_HEADS, m, MEM_WIDTH), MXU_DTYPE)
    return pl.pallas_call(
        _memkv_kernel,
        grid=(b,),
        in_specs=[pl.BlockSpec((1, m, d), lambda i: (i, 0, 0)), _layer_spec(g, layer), _layer_spec(w, layer)],
        out_specs=[pl.BlockSpec((1, MEM_HEADS, m, MEM_WIDTH), lambda i: (i, 0, 0, 0))] * 2,
        out_shape=[out, out],
        compiler_params=_params(("parallel",)),
        interpret=interpret,
        name="mem_kv",
    )(mem, g, w)


def _merge_kernel(x_ref, h_ref, oret_ref, ogla_ref, ossd_ref, orwkv_ref, km_ref, vm_ref,
                  wq_ref, wg_ref, uret_ref, ugla_ref, ussd_ref, urwkv_ref, umem_ref, wout_ref, gn_ref,
                  *out_refs):
    h = h_ref[...]
    q = jnp.dot(h, wq_ref[...], preferred_element_type=F32) * MEM_HEAD_DIM ** -0.5
    o_mem = jnp.zeros(q.shape, F32)
    for hd in range(MEM_HEADS):
        s = _dot_nt(q, km_ref[0, hd])
        s = jnp.exp(s - jnp.max(s, axis=-1, keepdims=True))
        prob = s / jnp.sum(s, axis=-1, keepdims=True)
        o_mem = o_mem + _dot(prob, vm_ref[0, hd])
    branches = ((oret_ref, uret_ref), (ogla_ref, ugla_ref), (ossd_ref, ussd_ref), (orwkv_ref, urwkv_ref))
    merged = None
    for i in range(N_BRANCHES):
        gate = jax.nn.sigmoid(jnp.dot(h, wg_ref[:, i * D_MODEL:(i + 1) * D_MODEL], preferred_element_type=F32))
        if i < 4:
            o_ref, u_ref = branches[i]
            up = jnp.dot(o_ref[...], u_ref[...], preferred_element_type=F32)
        else:
            up = _dot(o_mem, umem_ref[...])
        merged = gate * up if merged is None else merged + gate * up
    x = x_ref[...] + _dot(merged, wout_ref[...])
    if len(out_refs) == 2:
        out_refs[0][...] = x
    y = x * lax.rsqrt(jnp.mean(x * x, axis=-1, keepdims=True) + NORM_EPS) * gn_ref[...]
    out_refs[-1][...] = y.astype(out_refs[-1].dtype)


def _merge(x2d, h, o_ret, o_gla, o_ssd, o_rwkv, km, vm, wq, wg, ups, wout, g_next, layer, last,
           batch, seq, interpret):
    tm = min(TOKEN_TILE, seq)
    nt = seq // tm
    row = lambda w: pl.BlockSpec((tm, w), lambda b, i: (b * nt + i, 0))
    kvspec = pl.BlockSpec((1,) + km.shape[1:], lambda b, i: (b, 0, 0, 0))
    weights = [wq, wg, *ups, wout, g_next]
    return pl.pallas_call(
        _merge_kernel,
        grid=(batch, nt),
        in_specs=[row(D_MODEL), row(D_MODEL), row(WIDTH), row(WIDTH), row(WIDTH), row(WIDTH), kvspec, kvspec]
        + [_layer_spec(w, layer, pipeline_mode=pl.Buffered(1)) for w in weights],
        out_specs=[row(D_MODEL)] if last else [row(D_MODEL), row(D_MODEL)],
        out_shape=[jax.ShapeDtypeStruct(x2d.shape, F32)] if last else
        [jax.ShapeDtypeStruct(x2d.shape, F32), jax.ShapeDtypeStruct(x2d.shape, MXU_DTYPE)],
        compiler_params=_params(("parallel", "arbitrary")),
        interpret=interpret,
        name="merge",
    )(x2d, h, o_ret, o_gla, o_ssd, o_rwkv, km, vm, *weights)


def _pad_last(a, width):
    return jnp.pad(a, [(0, 0)] * (a.ndim - 1) + [(0, width - a.shape[-1])])


def _rows3(v, width=None):
    v = v.reshape(v.shape[0], 1, -1).astype(F32)
    return v if width is None else _pad_last(v, width)


def _forward(x, mem, positions, norm_g, w_in, gla_gk_w2, gla_gk_b, gla_norm_g,
             ssd_conv_w, ssd_conv_b, ssd_dt_bias, ssd_a_log, ssd_d, ssd_norm_g,
             rwkv_mu, rwkv_w0, rwkv_w2, rwkv_a0, rwkv_a2, rwkv_k_k, rwkv_k_a, rwkv_r_k,
             rwkv_ln_g, rwkv_ln_b, mem_norm_g, w_mem_kv,
             w_up_ret, w_up_gla, w_up_ssd, w_up_rwkv, w_up_mem, w_out, final_norm_g, interpret=False):
    batch, seq, d = x.shape
    depth = w_in.shape[0]
    cdt = MXU_DTYPE
    o = _IN_OFFS

    half = np.arange(RET_DK // 2)
    ret_perm = np.concatenate([hd * RET_DK + 2 * half + par for par in (0, 1) for hd in range(RET_HEADS)])
    inv = 1.0 / (ROPE_BASE ** jnp.linspace(0.0, 1.0, RET_DK // 2, dtype=F32))
    inv_row = jnp.tile(inv, RET_HEADS).reshape(1, LANES)
    pos_col = positions.reshape(batch * seq, 1)
    head_of_lane = np.arange(WIDTH) // SSD_P
    ssd_expand = jnp.asarray(np.arange(LANES)[:, None] == head_of_lane[None, :], F32)
    rwkv_seg = jnp.asarray(head_of_lane[:256, None] == head_of_lane[None, :256], F32)

    perm_matrix = jnp.asarray(ret_perm[None, :] == np.arange(RET_HEADS * RET_DK)[:, None], cdt)
    w_ret, w_gla, w_ssd, w_rwkv, w_q, w_g = _prep_weights(w_in, perm_matrix, interpret)
    gla_w2p = jnp.pad(gla_gk_w2, ((0, 0), (0, LANES - GLA_RANK), (0, 0))).astype(cdt)
    zeros_rank = jnp.zeros((depth, RWKV_RANK, WIDTH), F32)
    rwkv_w2p = jnp.concatenate([rwkv_w2, zeros_rank], axis=1).astype(cdt)
    rwkv_a2p = jnp.concatenate([zeros_rank, rwkv_a2], axis=1).astype(cdt)
    rwkv_vecs = [_rows3(v) for v in (rwkv_mu, rwkv_w0, rwkv_a0, rwkv_k_k, rwkv_k_a, rwkv_r_k,
                                     rwkv_ln_g, rwkv_ln_b)]
    ssd_small = [ssd_conv_w.astype(F32), _rows3(ssd_conv_b), _rows3(ssd_dt_bias, LANES),
                 _rows3(-jnp.exp(ssd_a_log.astype(F32)), LANES), _rows3(jnp.repeat(ssd_d, SSD_P, axis=1)),
                 _rows3(ssd_norm_g)]
    gla_b, gla_ng = _rows3(gla_gk_b), _rows3(gla_norm_g)
    mem_g, w_kv = _rows3(mem_norm_g), w_mem_kv.astype(cdt)
    ups = [w.astype(cdt) for w in (w_up_ret, w_up_gla, w_up_ssd, w_up_rwkv, w_up_mem)]
    w_o = w_out.astype(cdt)
    g_next = _rows3(jnp.concatenate([norm_g[1:], final_norm_g[None]], axis=0))

    x2d = x.reshape(batch * seq, d)
    h = _rmsnorm(x2d, norm_g[0], cdt, interpret)
    for l in range(depth):
        o_ret = _retention(h, pos_col, inv_row, w_ret, l, batch, seq, interpret)
        o_gla = _gla(h, w_gla, gla_w2p, gla_b, gla_ng, l, batch, seq, interpret)
        o_ssd = _ssd(h, w_ssd, *ssd_small, ssd_expand, l, batch, seq, interpret)
        o_rwkv = _rwkv(h, w_rwkv, rwkv_vecs, rwkv_w2p, rwkv_a2p, rwkv_seg, l, batch, seq, interpret)
        km, vm = _memkv(mem, mem_g, w_kv, l, interpret)
        last = l == depth - 1
        outs = _merge(x2d, h, o_ret, o_gla, o_ssd, o_rwkv, km, vm, w_q, w_g, ups, w_o, g_next, l, last,
                      batch, seq, interpret)
        x2d, h = (None, outs[0]) if last else outs
    return h.reshape(batch, seq, d)


def kernel(x, mem, positions, norm_g, w_in, gla_gk_w2, gla_gk_b, gla_norm_g, ssd_conv_w, ssd_conv_b, ssd_dt_bias, ssd_a_log, ssd_d, ssd_norm_g, rwkv_mu, rwkv_w0, rwkv_w2, rwkv_a0, rwkv_a2, rwkv_k_k, rwkv_k_a, rwkv_r_k, rwkv_ln_g, rwkv_ln_b, mem_norm_g, w_mem_kv, w_up_ret, w_up_gla, w_up_ssd, w_up_rwkv, w_up_mem, w_out, final_norm_g):
    return _forward(x, mem, positions, norm_g, w_in, gla_gk_w2, gla_gk_b, gla_norm_g,
                    ssd_conv_w, ssd_conv_b, ssd_dt_bias, ssd_a_log, ssd_d, ssd_norm_g,
                    rwkv_mu, rwkv_w0, rwkv_w2, rwkv_a0, rwkv_a2, rwkv_k_k, rwkv_k_a, rwkv_r_k,
                    rwkv_ln_g, rwkv_ln_b, mem_norm_g, w_mem_kv,
                    w_up_ret, w_up_gla, w_up_ssd, w_up_rwkv, w_up_mem, w_out, final_norm_g)
```

```python
import functools
import math

import jax
import jax.numpy as jnp
import numpy as np
from jax import lax
from jax.experimental import pallas as pl
from jax.experimental.pallas import tpu as pltpu

F32 = jnp.float32
BF16 = jnp.bfloat16
MXU_DTYPE = jnp.bfloat16

D_MODEL = 1024
WIDTH = 512
NORM_EPS = 1e-6
N_BRANCHES = 5

RET_HEADS, RET_DK, RET_DV, RET_CHUNK = 4, 64, 128, 128
ROPE_BASE = 10000.0
GLA_HEADS, GLA_DK, GLA_DV, GLA_RANK, GLA_NORMALIZER, GLA_CHUNK = 4, 64, 128, 16, 16.0, 64
SSD_HEADS, SSD_P, SSD_GROUPS, SSD_STATE, SSD_CONV, SSD_CHUNK = 8, 64, 2, 128, 4, 128
RWKV_HEADS, RWKV_N, RWKV_RANK, RWKV_CHUNK = 8, 64, 64, 64
RWKV_LN_EPS = 64e-5
RET_GROUP = 4
SSD_GROUP = 4
GLA_GROUP = 8
RWKV_GROUP = 8
MEM_HEADS, MEM_HEAD_DIM, MEM_WIDTH = 4, 64, 256

LANES = 128
CARRY_ROWS = 8
TOKEN_TILE = 512
MIXER_TILE = 512
VMEM_LIMIT = 56 * 1024 * 1024

_IN_SIZES = (256, 256, 512, 512, 256, 256, 512, 16, 512, 1024, 8, 512, 1664, 512, 256, 5120)
_IN_OFFS = tuple(int(v) for v in np.cumsum((0,) + _IN_SIZES))


def _dot(a, b):
    return jnp.dot(a.astype(MXU_DTYPE), b.astype(MXU_DTYPE), preferred_element_type=F32)


def _dot_nt(a, b):
    return lax.dot_general(a.astype(MXU_DTYPE), b.astype(MXU_DTYPE), (((1,), (1,)), ((), ())),
                           preferred_element_type=F32)


def _dot_tn(a, b):
    return lax.dot_general(a.astype(MXU_DTYPE), b.astype(MXU_DTYPE), (((0,), (0,)), ((), ())),
                           preferred_element_type=F32)


def _proj(x, w_t):
    return lax.dot_general(x, w_t, (((1,), (1,)), ((), ())), preferred_element_type=F32)


def _split3(x):
    hi = x.astype(BF16)
    r1 = x - hi.astype(F32)
    mid = r1.astype(BF16)
    lo = (r1 - mid.astype(F32)).astype(BF16)
    return hi, mid, lo


def _sel_dot(sel, x):
    s = sel.astype(BF16)
    return sum(jnp.dot(s, p, preferred_element_type=F32) for p in _split3(x))


def _dot_sel(x, sel, pieces=3):
    s = sel.astype(BF16)
    return sum(jnp.dot(p, s, preferred_element_type=F32) for p in _split3(x)[:pieces])


def _iota(shape, dim):
    return lax.broadcasted_iota(jnp.int32, shape, dim)


def _tril(n, strict=False):
    r, c = _iota((n, n), 0), _iota((n, n), 1)
    return (r > c) if strict else (r >= c)


def _silu(x):
    return x * jax.nn.sigmoid(x)


def _softplus(x):
    return jnp.maximum(x, 0.0) + jnp.log1p(jnp.exp(-jnp.abs(x)))


def _rows(c, n):
    return pl.ds(pl.multiple_of(c * n, n), n)


def _loop(trips, body):
    if trips == 1:
        body(0, 0)
    else:
        lax.fori_loop(0, trips, body, 0)


def _full_spec(shape):
    zeros = (0,) * len(shape)
    return pl.BlockSpec(shape, lambda *_: zeros)


def _layer_spec(arr, layer, **kwargs):
    tail = tuple(arr.shape[1:])
    index = (layer,) + (0,) * len(tail)
    return pl.BlockSpec((None,) + tail, lambda *_: index, **kwargs)


def _params(semantics):
    return pltpu.CompilerParams(dimension_semantics=semantics, vmem_limit_bytes=VMEM_LIMIT)


def _rmsnorm_kernel(x_ref, g_ref, o_ref):
    x = x_ref[...]
    y = x * lax.rsqrt(jnp.mean(x * x, axis=-1, keepdims=True) + NORM_EPS)
    o_ref[...] = (y * g_ref[...]).astype(o_ref.dtype)


def _rmsnorm(x2d, g, out_dtype, interpret):
    m, d = x2d.shape
    tm = min(1024, m)
    return pl.pallas_call(
        _rmsnorm_kernel,
        grid=(m // tm,),
        in_specs=[pl.BlockSpec((tm, d), lambda i: (i, 0)), _full_spec((1, d))],
        out_specs=pl.BlockSpec((tm, d), lambda i: (i, 0)),
        out_shape=jax.ShapeDtypeStruct((m, d), out_dtype),
        compiler_params=_params(("parallel",)),
        interpret=interpret,
        name="rmsnorm",
    )(x2d, g.reshape(1, d))


def _ret_kernel(h_ref, pos_ref, inv_ref, w_ref, o_ref, p_scr, s_scr, *, tb):
    C = RET_CHUNK

    @pl.when(pl.program_id(1) == 0)
    def _():
        s_scr[...] = jnp.zeros_like(s_scr)

    p = _proj(h_ref[...], w_ref[...])
    ang = pos_ref[...].astype(F32) * inv_ref[...]
    cos, sin = jnp.cos(ang), jnp.sin(ang)
    q1, q2 = p[:, 0:128], p[:, 128:256]
    k1, k2 = p[:, 256:384] * RET_DK ** -0.5, p[:, 384:512] * RET_DK ** -0.5
    p_scr[:, 0:128] = q1 * cos - q2 * sin
    p_scr[:, 128:256] = q2 * cos + q1 * sin
    p_scr[:, 256:384] = k1 * cos - k2 * sin
    p_scr[:, 384:512] = k2 * cos + k1 * sin
    p_scr[:, 512:1536] = p[:, 512:1536]

    def log_gamma(head):
        return jnp.log(1.0 - jnp.exp2(-5.0 - head.astype(F32)))

    qk_head = (_iota((1, 256), 1) >> 5) & 3
    lg_lane = log_gamma(qk_head)
    tau = _iota((C, 1), 0).astype(F32)
    dq = jnp.exp(lg_lane * (tau + 1.0))
    dk = jnp.exp(lg_lane * (C - 1.0 - tau))
    ds = jnp.exp(lg_lane * float(C))
    diff = (_iota((C, C), 0) - _iota((C, C), 1)).astype(F32)
    causal = _tril(C)
    bd_mask = (_iota((WIDTH, 256), 0) >> 7) == ((_iota((WIDTH, 256), 1) >> 5) & 3)

    heads = range(RET_HEADS)
    hcols = [slice(hd * RET_DV, (hd + 1) * RET_DV) for hd in heads]
    seg = [jnp.where(causal, jnp.exp(math.log(1.0 - 2.0 ** (-5.0 - hd)) * diff), 0.0) for hd in heads]

    def group(gi, carry):
        us = range(RET_GROUP)
        rows = [_rows(gi * RET_GROUP + u, C) for u in us]
        q = [p_scr[rw, 0:256] for rw in rows]
        k = [p_scr[rw, 256:512] for rw in rows]
        v = [p_scr[rw, 512:1024] for rw in rows]
        sc = [[_dot_nt(q[u], jnp.where(qk_head == hd, k[u], 0.0)) * seg[hd] for hd in heads] for u in us]
        y_intra = [[_dot(sc[u][hd], v[u][:, hcols[hd]]) for hd in heads] for u in us]
        kv = [jnp.where(bd_mask, _dot_tn(v[u], k[u] * dk), 0.0) for u in us]
        st = s_scr[...]
        y_inter = []
        for u in us:
            y_inter.append(_dot_nt(q[u] * dq, st))
            st = st * ds + kv[u]
        s_scr[...] = st
        for u in us:
            for hd in heads:
                y = y_intra[u][hd] + y_inter[u][:, hcols[hd]]
                y = y * lax.rsqrt(jnp.mean(y * y, axis=-1, keepdims=True) + NORM_EPS)
                g = p_scr[rows[u], 1024 + hd * RET_DV:1024 + (hd + 1) * RET_DV]
                o_ref[rows[u], hcols[hd]] = (y * _silu(g)).astype(o_ref.dtype)
        return carry

    _loop(tb // (C * RET_GROUP), group)


def _retention(h, pos_col, inv_row, w, layer, batch, seq, interpret):
    tb = min(MIXER_TILE, seq)
    nt = seq // tb
    kern = functools.partial(_ret_kernel, tb=tb)
    return pl.pallas_call(
        kern,
        grid=(batch, nt),
        in_specs=[pl.BlockSpec((tb, D_MODEL), lambda b, i: (b * nt + i, 0)),
                  pl.BlockSpec((tb, 1), lambda b, i: (b * nt + i, 0)),
                  _full_spec((1, LANES)),
                  _layer_spec(w, layer)],
        out_specs=pl.BlockSpec((tb, WIDTH), lambda b, i: (b * nt + i, 0)),
        out_shape=jax.ShapeDtypeStruct((batch * seq, WIDTH), MXU_DTYPE),
        scratch_shapes=[pltpu.VMEM((tb, 1536), F32), pltpu.VMEM((WIDTH, 256), F32)],
        compiler_params=_params(("parallel", "arbitrary")),
        interpret=interpret,
        name="retention",
    )(h, pos_col, inv_row, w)


def _gla_kernel(h_ref, w_ref, w2_ref, gb_ref, ng_ref, o_ref, p_scr, lg_scr, s_scr, *, tb):
    C = GLA_CHUNK

    @pl.when(pl.program_id(1) == 0)
    def _():
        s_scr[...] = jnp.zeros_like(s_scr)

    p = _proj(h_ref[...], w_ref[...])
    p_scr[...] = p[:, 0:1536]
    pre = _dot(p[:, 1536:1664], w2_ref[...]) + gb_ref[...]
    lg_scr[...] = -_softplus(-pre) / GLA_NORMALIZER

    tri = _tril(C)
    k_head = _iota((1, 256), 1) >> 6
    bd_mask = (_iota((WIDTH, 256), 0) >> 7) == (_iota((WIDTH, 256), 1) >> 6)
    ng = ng_ref[...]

    heads = range(GLA_HEADS)
    hcols = [slice(hd * GLA_DV, (hd + 1) * GLA_DV) for hd in heads]

    def group(gi, carry):
        us = range(GLA_GROUP)
        rows = [_rows(gi * GLA_GROUP + u, C) for u in us]
        q = [p_scr[rw, 0:256] * GLA_DK ** -0.5 for rw in rows]
        k = [p_scr[rw, 256:512] for rw in rows]
        v = [p_scr[rw, 512:1024] for rw in rows]
        cum = [_sel_dot(tri, lg_scr[rw, :]) for rw in rows]
        ref = [c[C // 2:C // 2 + 1, :] for c in cum]
        last = [c[C - 1:C, :] for c in cum]
        q_in = [q[u] * jnp.exp(cum[u] - ref[u]) for u in us]
        k_in = [k[u] * jnp.exp(ref[u] - cum[u]) for u in us]
        q_dec = [q[u] * jnp.exp(cum[u]) for u in us]
        k_st = [k[u] * jnp.exp(last[u] - cum[u]) for u in us]
        sc = [[jnp.where(tri, _dot_nt(q_in[u], jnp.where(k_head == hd, k_in[u], 0.0)), 0.0) for hd in heads]
              for u in us]
        y_intra = [[_dot(sc[u][hd], v[u][:, hcols[hd]]) for hd in heads] for u in us]
        kv = [jnp.where(bd_mask, _dot_tn(v[u], k_st[u]), 0.0) for u in us]
        st = s_scr[...]
        y_inter = []
        for u in us:
            y_inter.append(_dot_nt(q_dec[u], st))
            st = st * jnp.exp(last[u]) + kv[u]
        s_scr[...] = st
        for u in us:
            for hd in heads:
                y = y_intra[u][hd] + y_inter[u][:, hcols[hd]]
                y = y * lax.rsqrt(jnp.mean(y * y, axis=-1, keepdims=True) + NORM_EPS) * ng
                g = p_scr[rows[u], 1024 + hd * GLA_DV:1024 + (hd + 1) * GLA_DV]
                o_ref[rows[u], hcols[hd]] = (y * _silu(g)).astype(o_ref.dtype)
        return carry

    _loop(tb // (C * GLA_GROUP), group)


def _gla(h, w, w2p, gb, ng, layer, batch, seq, interpret):
    tb = min(MIXER_TILE, seq)
    nt = seq // tb
    kern = functools.partial(_gla_kernel, tb=tb)
    return pl.pallas_call(
        kern,
        grid=(batch, nt),
        in_specs=[pl.BlockSpec((tb, D_MODEL), lambda b, i: (b * nt + i, 0)),
                  _layer_spec(w, layer), _layer_spec(w2p, layer), _layer_spec(gb, layer),
                  _layer_spec(ng, layer)],
        out_specs=pl.BlockSpec((tb, WIDTH), lambda b, i: (b * nt + i, 0)),
        out_shape=jax.ShapeDtypeStruct((batch * seq, WIDTH), MXU_DTYPE),
        scratch_shapes=[pltpu.VMEM((tb, 1536), F32), pltpu.VMEM((tb, 256), F32),
                        pltpu.VMEM((WIDTH, 256), F32)],
        compiler_params=_params(("parallel", "arbitrary")),
        interpret=interpret,
        name="gla",
    )(h, w, w2p, gb, ng)


def _ssd_kernel(h_ref, w_ref, cw_ref, cb_ref, dtb_ref, a_ref, dskip_ref, ng_ref, exp_ref, o_ref,
                raw_scr, xc_scr, z_scr, dt_scr, s_scr, *, tb):
    C = SSD_CHUNK
    NCH = 1024

    @pl.when(pl.program_id(1) == 0)
    def _():
        s_scr[...] = jnp.zeros_like(s_scr)
        raw_scr[0:CARRY_ROWS, :] = jnp.zeros((CARRY_ROWS, NCH), F32)

    p = _proj(h_ref[...], w_ref[...])
    raw_scr[CARRY_ROWS:CARRY_ROWS + tb, :] = p[:, 0:NCH]
    z_scr[...] = p[:, NCH:NCH + WIDTH]
    dt_scr[...] = _softplus(p[:, 1536:1664] + dtb_ref[...])
    conv = cb_ref[...] + sum(
        raw_scr[CARRY_ROWS - (SSD_CONV - 1) + j:CARRY_ROWS - (SSD_CONV - 1) + j + tb, :] * cw_ref[j:j + 1, :]
        for j in range(SSD_CONV))
    xc_scr[...] = _silu(conv)
    raw_scr[0:CARRY_ROWS, :] = raw_scr[tb:tb + CARRY_ROWS, :]

    tri = _tril(C)
    lane_lo = _iota((1, LANES), 1) < SSD_P
    expand = exp_ref[...]
    a_row = a_ref[...]

    groups = range(SSD_GROUPS)
    gcols = [slice(g * 256, (g + 1) * 256) for g in groups]
    ncols = [slice(g * SSD_STATE, (g + 1) * SSD_STATE) for g in groups]

    def group(gi, carry):
        us = range(SSD_GROUP)
        rows = [_rows(gi * SSD_GROUP + u, C) for u in us]
        xs = [xc_scr[rw, 0:512] for rw in rows]
        bm = [xc_scr[rw, 512:768] for rw in rows]
        cm = [xc_scr[rw, 768:1024] for rw in rows]
        dt = [dt_scr[rw, :] for rw in rows]
        cum = [_sel_dot(tri, dt[u] * a_row) for u in us]
        cum_t = [c.T for c in cum]
        dt_e = [_dot_sel(d, expand) for d in dt]
        cum_e = [_dot_sel(c, expand) for c in cum]
        last_e = [c[C - 1:C, :] for c in cum_e]
        xdt = [xs[u] * dt_e[u] for u in us]
        v_st = [xdt[u] * jnp.exp(last_e[u] - cum_e[u]) for u in us]
        e_cum = [jnp.exp(c) for c in cum_e]
        scores = [[_dot_nt(cm[u][:, ncols[g]], bm[u][:, ncols[g]]) for g in groups] for u in us]
        y_intra = []
        for u in us:
            parts = []
            for pr in range(SSD_HEADS // 2):
                sc = scores[u][pr // 2]
                segs = []
                for hd in (2 * pr, 2 * pr + 1):
                    d = jnp.minimum(cum[u][:, hd:hd + 1] - cum_t[u][hd:hd + 1, :], 0.0)
                    segs.append(sc * jnp.where(tri, jnp.exp(d), 0.0))
                xp = xdt[u][:, pr * LANES:(pr + 1) * LANES]
                rhs = jnp.concatenate([jnp.where(lane_lo, xp, 0.0), jnp.where(lane_lo, 0.0, xp)], axis=0)
                parts.append(_dot(jnp.concatenate(segs, axis=1), rhs))
            y_intra.append(jnp.concatenate(parts, axis=1))
        kv = [[_dot_tn(bm[u][:, ncols[g]], v_st[u][:, gcols[g]]) for g in groups] for u in us]
        st = [s_scr[g] for g in groups]
        y_inter = []
        for u in us:
            y_inter.append(jnp.concatenate(
                [_dot(cm[u][:, ncols[g]], st[g]) * e_cum[u][:, gcols[g]] for g in groups], axis=1))
            st = [st[g] * jnp.exp(last_e[u][:, gcols[g]]) + kv[u][g] for g in groups]
        for g in groups:
            s_scr[g] = st[g]
        for u in us:
            y = y_intra[u] + y_inter[u] + dskip_ref[...] * xs[u]
            y = y * _silu(z_scr[rows[u], :])
            for g in groups:
                yg = y[:, gcols[g]]
                yg = yg * lax.rsqrt(jnp.mean(yg * yg, axis=-1, keepdims=True) + NORM_EPS)
                o_ref[rows[u], gcols[g]] = (yg * ng_ref[:, gcols[g]]).astype(o_ref.dtype)
        return carry

    _loop(tb // (C * SSD_GROUP), group)


def _ssd(h, w, cw, cb, dtb, a_row, dskip, ng, expand, layer, batch, seq, interpret):
    tb = min(MIXER_TILE, seq)
    nt = seq // tb
    kern = functools.partial(_ssd_kernel, tb=tb)
    small = [cw, cb, dtb, a_row, dskip, ng]
    return pl.pallas_call(
        kern,
        grid=(batch, nt),
        in_specs=[pl.BlockSpec((tb, D_MODEL), lambda b, i: (b * nt + i, 0)), _layer_spec(w, layer)]
        + [_layer_spec(a, layer) for a in small] + [_full_spec(expand.shape)],
        out_specs=pl.BlockSpec((tb, WIDTH), lambda b, i: (b * nt + i, 0)),
        out_shape=jax.ShapeDtypeStruct((batch * seq, WIDTH), MXU_DTYPE),
        scratch_shapes=[pltpu.VMEM((tb + CARRY_ROWS, 1024), F32), pltpu.VMEM((tb, 1024), F32),
                        pltpu.VMEM((tb, WIDTH), F32), pltpu.VMEM((tb, LANES), F32),
                        pltpu.VMEM((SSD_GROUPS, SSD_STATE, 256), F32)],
        compiler_params=_params(("parallel", "arbitrary")),
        interpret=interpret,
        name="ssd",
    )(h, w, *small, expand)


def _rwkv_kernel(h_ref, w_ref, mu_ref, w0_ref, w2_ref, a0_ref, a2_ref, kk_ref, ka_ref, rk_ref,
                 lng_ref, lnb_ref, seg_ref, o_ref,
                 u_scr, g_scr, r_scr, lw_scr, k_scr, v_scr, a_scr, b_scr, y_scr, bonus_scr, st_scr, *, tb):
    C = RWKV_CHUNK
    NU = 3 * WIDTH + 2 * RWKV_RANK

    @pl.when(pl.program_id(1) == 0)
    def _():
        st_scr[...] = jnp.zeros_like(st_scr)
        u_scr[0:CARRY_ROWS, :] = jnp.zeros((CARRY_ROWS, NU), F32)

    p = _proj(h_ref[...], w_ref[...])
    g_scr[...] = p[:, NU:NU + WIDTH]
    u = p[:, 0:NU]
    u_scr[CARRY_ROWS:CARRY_ROWS + tb, :] = u
    u_prev = u_scr[CARRY_ROWS - 1:CARRY_ROWS - 1 + tb, :]
    xs = u + (u_prev - u) * mu_ref[...]
    u_scr[0:CARRY_ROWS, :] = u_scr[tb:tb + CARRY_ROWS, :]

    seg = seg_ref[...]

    def seg_sum(x, pieces):
        return jnp.concatenate([_dot_sel(x[:, j * 256:(j + 1) * 256], seg, pieces) for j in range(2)], axis=1)

    r = xs[:, 0:512]
    k = xs[:, 512:1024]
    v = xs[:, 1024:1536]
    wa = xs[:, 1536:1664]
    w_pre = w0_ref[...] + _dot(jnp.tanh(wa), w2_ref[...])
    a_gate = jax.nn.sigmoid(a0_ref[...] + _dot(wa, a2_ref[...]))
    lw_scr[...] = -jnp.exp(-_softplus(-w_pre) - 0.5)
    kk = k * kk_ref[...]
    k = k * (1.0 + (a_gate - 1.0) * ka_ref[...])
    kk = kk / jnp.maximum(jnp.sqrt(seg_sum(kk * kk, 1)), 1e-12)
    r_scr[...] = r
    k_scr[...] = k
    v_scr[...] = v
    a_scr[...] = -kk
    b_scr[...] = kk * a_gate
    bonus_scr[...] = seg_sum(r * k * rk_ref[...], 1) * v

    tri = _tril(C)
    r2, c2 = _iota((2 * C, 2 * C), 0), _iota((2 * C, 2 * C), 1)
    same = (r2 >> 6) == (c2 >> 6)
    low_s = same & ((r2 & 63) > (c2 & 63))
    low_i = same & ((r2 & 63) >= (c2 & 63))
    eye = (r2 == c2).astype(F32)
    lane_lo = _iota((1, LANES), 1) < RWKV_N

    pairs = range(RWKV_HEADS // 2)

    def stack_pair(xp):
        return jnp.concatenate([jnp.where(lane_lo, xp, 0.0), jnp.where(lane_lo, 0.0, xp)], axis=0)

    def stack(x, pr):
        return stack_pair(x[:, pr * LANES:(pr + 1) * LANES])

    def group(gi, carry):
        lanes = [(u, pr) for u in range(RWKV_GROUP) for pr in pairs]
        rows, a_0, r_0, v_c, b_h, k_h, e_last, lhs, rhs_t = [], [], [], [], [], [], [], {}, {}
        for u in range(RWKV_GROUP):
            rw = _rows(gi * RWKV_GROUP + u, C)
            lw = lw_scr[rw, :]
            cum = _sel_dot(tri, lw)
            cum_p = cum - lw
            ref = cum[C // 2:C // 2 + 1, :]
            last = cum[C - 1:C, :]
            e_fwd = jnp.exp(cum - ref)
            e_bwd = jnp.exp(ref - cum)
            e_end = jnp.exp(last - cum)
            r_c, k_c, a_c, b_c = r_scr[rw, :], k_scr[rw, :], a_scr[rw, :], b_scr[rw, :]
            r_t = r_c * e_fwd
            a_t = a_c * jnp.exp(cum_p - ref)
            b_t = b_c * e_bwd
            k_t = k_c * e_bwd
            for pr in pairs:
                lhs[u, pr] = jnp.concatenate([stack(a_t, pr), stack(r_t, pr)], axis=0)
                rhs_t[u, pr] = jnp.concatenate([stack(b_t, pr), stack(k_t, pr)], axis=0)
            rows.append(rw)
            a_0.append(a_c * jnp.exp(cum_p))
            r_0.append(r_c * jnp.exp(cum))
            v_c.append(v_scr[rw, :])
            b_h.append(b_c * e_end)
            k_h.append(k_c * e_end)
            e_last.append(jnp.exp(last))
        big = [_dot_nt(lhs[ln], rhs_t[ln]) for ln in lanes]
        a_ab = [jnp.where(low_s, m[0:128, 0:128], 0.0) for m in big]
        a_ak = [jnp.where(low_s, m[0:128, 128:256], 0.0) for m in big]
        a_rb = [jnp.where(low_i, m[128:256, 0:128], 0.0) for m in big]
        a_rk = [jnp.where(low_i, m[128:256, 128:256], 0.0) for m in big]
        inv = [eye + m for m in a_ab]
        pw = [_dot(m, m) for m in a_ab]
        for _ in range(4):
            prod = [_dot(p, jnp.concatenate([p, t], axis=1)) for p, t in zip(pw, inv)]
            pw = [m[:, 0:128] for m in prod]
            inv = [t + m[:, 128:256] for t, m in zip(inv, prod)]
        inv = [t + _dot(p, t) for p, t in zip(pw, inv)]
        vs = [stack(v_c[u], pr) for u, pr in lanes]
        akv = [_dot(m, x) for m, x in zip(a_ak, vs)]
        a_r = [jnp.concatenate([m, n], axis=1) for m, n in zip(a_rb, a_rk)]
        bk_h = [jnp.concatenate([stack(b_h[u], pr), stack(k_h[u], pr)], axis=0) for u, pr in lanes]

        st = [st_scr[pr] for pr in pairs]
        for u in range(RWKV_GROUP):
            ix = [u * len(pairs) + pr for pr in pairs]
            pcols = [slice(pr * LANES, (pr + 1) * LANES) for pr in pairs]
            rhs = [stack_pair(_dot_nt(a_0[u][:, pcols[pr]], st[pr])) + akv[i] for pr, i in zip(pairs, ix)]
            uv = [jnp.concatenate([_dot(inv[i], m), vs[i]], axis=0) for m, i in zip(rhs, ix)]
            oo = [_dot(a_r[i], uv[pr]) for pr, i in zip(pairs, ix)]
            y_st = [_dot_nt(r_0[u][:, pcols[pr]], st[pr]) for pr in pairs]
            st = [st[pr] * e_last[u][:, pcols[pr]] + _dot_tn(uv[pr], bk_h[i]) for pr, i in zip(pairs, ix)]
            for pr in pairs:
                y_scr[rows[u], pcols[pr]] = y_st[pr] + oo[pr][0:C, :] + oo[pr][C:2 * C, :]
        for pr in pairs:
            st_scr[pr] = st[pr]
        return carry

    _loop(tb // (C * RWKV_GROUP), group)

    y = y_scr[...]
    mean = seg_sum(y, 2) * (1.0 / RWKV_N)
    d = y - mean
    var = seg_sum(d * d, 1) * (1.0 / RWKV_N)
    y = d * lax.rsqrt(var + RWKV_LN_EPS) * lng_ref[...] + lnb_ref[...] + bonus_scr[...]
    o_ref[...] = (y * _silu(g_scr[...])).astype(o_ref.dtype)


def _rwkv(h, w, vecs, w2p, a2p, seg, layer, batch, seq, interpret):
    tb = min(TOKEN_TILE, seq)
    nt = seq // tb
    kern = functools.partial(_rwkv_kernel, tb=tb)
    mu, w0, a0, kk, ka, rk, lng, lnb = vecs
    ins = [mu, w0, w2p, a0, a2p, kk, ka, rk, lng, lnb]
    wide = lambda: pltpu.VMEM((tb, WIDTH), F32)
    return pl.pallas_call(
        kern,
        grid=(batch, nt),
        in_specs=[pl.BlockSpec((tb, D_MODEL), lambda b, i: (b * nt + i, 0)), _layer_spec(w, layer)]
        + [_layer_spec(a, layer) for a in ins] + [_full_spec(seg.shape)],
        out_specs=pl.BlockSpec((tb, WIDTH), lambda b, i: (b * nt + i, 0)),
        out_shape=jax.ShapeDtypeStruct((batch * seq, WIDTH), MXU_DTYPE),
        scratch_shapes=[pltpu.VMEM((tb + CARRY_ROWS, 3 * WIDTH + 2 * RWKV_RANK), F32)]
        + [wide() for _ in range(9)]
        + [pltpu.VMEM((RWKV_HEADS // 2, LANES, LANES), F32)],
        compiler_params=_params(("parallel", "arbitrary")),
        interpret=interpret,
        name="rwkv7",
    )(h, w, *ins, seg)


def _memkv_kernel(mem_ref, g_ref, w_ref, k_ref, v_ref):
    x = mem_ref[0]
    y = x * lax.rsqrt(jnp.mean(x * x, axis=-1, keepdims=True) + NORM_EPS) * g_ref[...]
    kv = _dot(y, w_ref[...])
    head = _iota((1, MEM_WIDTH), 1) >> 6
    for hd in range(MEM_HEADS):
        k_ref[0, hd] = jnp.where(head == hd, kv[:, 0:MEM_WIDTH], 0.0).astype(k_ref.dtype)
        v_ref[0, hd] = jnp.where(head == hd, kv[:, MEM_WIDTH:2 * MEM_WIDTH], 0.0).astype(v_ref.dtype)


def _memkv(mem, g, w, layer, interpret):
    b, m, d = mem.shape
    out = jax.ShapeDtypeStruct((b, MEM_HEADS, m, MEM_WIDTH), MXU_DTYPE)
    return pl.pallas_call(
        _memkv_kernel,
        grid=(b,),
        in_specs=[pl.BlockSpec((1, m, d), lambda i: (i, 0, 0)), _layer_spec(g, layer), _layer_spec(w, layer)],
        out_specs=[pl.BlockSpec((1, MEM_HEADS, m, MEM_WIDTH), lambda i: (i, 0, 0, 0))] * 2,
        out_shape=[out, out],
        compiler_params=_params(("parallel",)),
        interpret=interpret,
        name="mem_kv",
    )(mem, g, w)


def _merge_kernel(x_ref, h_ref, oret_ref, ogla_ref, ossd_ref, orwkv_ref, km_ref, vm_ref,
                  wq_ref, wg_ref, uret_ref, ugla_ref, ussd_ref, urwkv_ref, umem_ref, wout_ref, gn_ref,
                  *out_refs):
    h = h_ref[...]
    q = _proj(h, wq_ref[...]) * MEM_HEAD_DIM ** -0.5
    o_mem = jnp.zeros(q.shape, F32)
    for hd in range(MEM_HEADS):
        s = _dot_nt(q, km_ref[0, hd])
        s = jnp.exp(s - jnp.max(s, axis=-1, keepdims=True))
        prob = s / jnp.sum(s, axis=-1, keepdims=True)
        o_mem = o_mem + _dot(prob, vm_ref[0, hd])
    branches = ((oret_ref, uret_ref), (ogla_ref, ugla_ref), (ossd_ref, ussd_ref), (orwkv_ref, urwkv_ref))
    merged = None
    for i in range(N_BRANCHES):
        gate = jax.nn.sigmoid(_proj(h, wg_ref[i * D_MODEL:(i + 1) * D_MODEL, :]))
        if i < 4:
            o_ref, u_ref = branches[i]
            up = jnp.dot(o_ref[...], u_ref[...], preferred_element_type=F32)
        else:
            up = _dot(o_mem, umem_ref[...])
        merged = gate * up if merged is None else merged + gate * up
    x = x_ref[...] + _dot(merged, wout_ref[...])
    if len(out_refs) == 2:
        out_refs[0][...] = x
    y = x * lax.rsqrt(jnp.mean(x * x, axis=-1, keepdims=True) + NORM_EPS) * gn_ref[...]
    out_refs[-1][...] = y.astype(out_refs[-1].dtype)


def _merge(x2d, h, o_ret, o_gla, o_ssd, o_rwkv, km, vm, wq, wg, ups, wout, g_next, layer, last,
           batch, seq, interpret):
    tm = min(TOKEN_TILE, seq)
    nt = seq // tm
    row = lambda w: pl.BlockSpec((tm, w), lambda b, i: (b * nt + i, 0))
    kvspec = pl.BlockSpec((1,) + km.shape[1:], lambda b, i: (b, 0, 0, 0))
    weights = [wq, wg, *ups, wout, g_next]
    return pl.pallas_call(
        _merge_kernel,
        grid=(batch, nt),
        in_specs=[row(D_MODEL), row(D_MODEL), row(WIDTH), row(WIDTH), row(WIDTH), row(WIDTH), kvspec, kvspec]
        + [_layer_spec(w, layer, pipeline_mode=pl.Buffered(1)) for w in weights],
        out_specs=[row(D_MODEL)] if last else [row(D_MODEL), row(D_MODEL)],
        out_shape=[jax.ShapeDtypeStruct(x2d.shape, F32)] if last else
        [jax.ShapeDtypeStruct(x2d.shape, F32), jax.ShapeDtypeStruct(x2d.shape, MXU_DTYPE)],
        compiler_params=_params(("parallel", "arbitrary")),
        interpret=interpret,
        name="merge",
    )(x2d, h, o_ret, o_gla, o_ssd, o_rwkv, km, vm, *weights)


def _pad_last(a, width):
    return jnp.pad(a, [(0, 0)] * (a.ndim - 1) + [(0, width - a.shape[-1])])


def _rows3(v, width=None):
    v = v.reshape(v.shape[0], 1, -1).astype(F32)
    return v if width is None else _pad_last(v, width)


def _forward(x, mem, positions, norm_g, w_in, gla_gk_w2, gla_gk_b, gla_norm_g,
             ssd_conv_w, ssd_conv_b, ssd_dt_bias, ssd_a_log, ssd_d, ssd_norm_g,
             rwkv_mu, rwkv_w0, rwkv_w2, rwkv_a0, rwkv_a2, rwkv_k_k, rwkv_k_a, rwkv_r_k,
             rwkv_ln_g, rwkv_ln_b, mem_norm_g, w_mem_kv,
             w_up_ret, w_up_gla, w_up_ssd, w_up_rwkv, w_up_mem, w_out, final_norm_g, interpret=False):
    batch, seq, d = x.shape
    depth = w_in.shape[0]
    cdt = MXU_DTYPE
    o = _IN_OFFS

    half = np.arange(RET_DK // 2)
    ret_perm = np.concatenate([hd * RET_DK + 2 * half + par for par in (0, 1) for hd in range(RET_HEADS)])
    inv = 1.0 / (ROPE_BASE ** jnp.linspace(0.0, 1.0, RET_DK // 2, dtype=F32))
    inv_row = jnp.tile(inv, RET_HEADS).reshape(1, LANES)
    pos_col = positions.reshape(batch * seq, 1)
    head_of_lane = np.arange(WIDTH) // SSD_P
    ssd_expand = jnp.asarray(np.arange(LANES)[:, None] == head_of_lane[None, :], F32)
    rwkv_seg = jnp.asarray(head_of_lane[:256, None] == head_of_lane[None, :256], F32)

    w_t = jnp.swapaxes(w_in, 1, 2)
    row = lambda i, j=None: w_t[:, o[i]:o[(i if j is None else j) + 1], :]
    pad_rows = lambda a: jnp.pad(a, ((0, 0), (0, LANES - a.shape[1]), (0, 0)))
    w_ret = jnp.concatenate([row(0)[:, ret_perm], row(1)[:, ret_perm], row(2, 3)], axis=1).astype(cdt)
    w_gla = jnp.concatenate([row(4, 6), row(8), pad_rows(row(7))], axis=1).astype(cdt)
    w_ssd = jnp.concatenate([row(9), row(11), pad_rows(row(10))], axis=1).astype(cdt)
    w_rwkv = row(12, 13).astype(cdt)
    w_q = row(14).astype(cdt)
    w_g = row(15).astype(cdt)
    gla_w2p = jnp.pad(gla_gk_w2, ((0, 0), (0, LANES - GLA_RANK), (0, 0))).astype(cdt)
    zeros_rank = jnp.zeros((depth, RWKV_RANK, WIDTH), F32)
    rwkv_w2p = jnp.concatenate([rwkv_w2, zeros_rank], axis=1).astype(cdt)
    rwkv_a2p = jnp.concatenate([zeros_rank, rwkv_a2], axis=1).astype(cdt)
    rwkv_vecs = [_rows3(v) for v in (rwkv_mu, rwkv_w0, rwkv_a0, rwkv_k_k, rwkv_k_a, rwkv_r_k,
                                     rwkv_ln_g, rwkv_ln_b)]
    ssd_small = [ssd_conv_w.astype(F32), _rows3(ssd_conv_b), _rows3(ssd_dt_bias, LANES),
                 _rows3(-jnp.exp(ssd_a_log.astype(F32)), LANES), _rows3(jnp.repeat(ssd_d, SSD_P, axis=1)),
                 _rows3(ssd_norm_g)]
    gla_b, gla_ng = _rows3(gla_gk_b), _rows3(gla_norm_g)
    mem_g, w_kv = _rows3(mem_norm_g), w_mem_kv.astype(cdt)
    ups = [w.astype(cdt) for w in (w_up_ret, w_up_gla, w_up_ssd, w_up_rwkv, w_up_mem)]
    w_o = w_out.astype(cdt)
    g_next = _rows3(jnp.concatenate([norm_g[1:], final_norm_g[None]], axis=0))

    x2d = x.reshape(batch * seq, d)
    h = _rmsnorm(x2d, norm_g[0], cdt, interpret)
    for l in range(depth):
        o_ret = _retention(h, pos_col, inv_row, w_ret, l, batch, seq, interpret)
        o_gla = _gla(h, w_gla, gla_w2p, gla_b, gla_ng, l, batch, seq, interpret)
        o_ssd = _ssd(h, w_ssd, *ssd_small, ssd_expand, l, batch, seq, interpret)
        o_rwkv = _rwkv(h, w_rwkv, rwkv_vecs, rwkv_w2p, rwkv_a2p, rwkv_seg, l, batch, seq, interpret)
        km, vm = _memkv(mem, mem_g, w_kv, l, interpret)
        last = l == depth - 1
        outs = _merge(x2d, h, o_ret, o_gla, o_ssd, o_rwkv, km, vm, w_q, w_g, ups, w_o, g_next, l, last,
                      batch, seq, interpret)
        x2d, h = (None, outs[0]) if last else outs
    return h.reshape(batch, seq, d)


def kernel(x, mem, positions, norm_g, w_in, gla_gk_w2, gla_gk_b, gla_norm_g, ssd_conv_w, ssd_conv_b, ssd_dt_bias, ssd_a_log, ssd_d, ssd_norm_g, rwkv_mu, rwkv_w0, rwkv_w2, rwkv_a0, rwkv_a2, rwkv_k_k, rwkv_k_a, rwkv_r_k, rwkv_ln_g, rwkv_ln_b, mem_norm_g, w_mem_kv, w_up_ret, w_up_gla, w_up_ssd, w_up_rwkv, w_up_mem, w_out, final_norm_g):
    return _forward(x, mem, positions, norm_g, w_in, gla_gk_w2, gla_gk_b, gla_norm_g,
                    ssd_conv_w, ssd_conv_b, ssd_dt_bias, ssd_a_log, ssd_d, ssd_norm_g,
                    rwkv_mu, rwkv_w0, rwkv_w2, rwkv_a0, rwkv_a2, rwkv_k_k, rwkv_k_a, rwkv_r_k,
                    rwkv_ln_g, rwkv_ln_b, mem_norm_g, w_mem_kv,
                    w_up_ret, w_up_gla, w_up_ssd, w_up_rwkv, w_up_mem, w_out, final_norm_g)
```

```python
import functools
import math

import jax
import jax.numpy as jnp
import numpy as np
from jax import lax
from jax.experimental import pallas as pl
from jax.experimental.pallas import tpu as pltpu

F32 = jnp.float32
BF16 = jnp.bfloat16
MXU_DTYPE = jnp.bfloat16

D_MODEL = 1024
WIDTH = 512
NORM_EPS = 1e-6
N_BRANCHES = 5

RET_HEADS, RET_DK, RET_DV, RET_CHUNK = 4, 64, 128, 128
ROPE_BASE = 10000.0
GLA_HEADS, GLA_DK, GLA_DV, GLA_RANK, GLA_NORMALIZER, GLA_CHUNK = 4, 64, 128, 16, 16.0, 64
SSD_HEADS, SSD_P, SSD_GROUPS, SSD_STATE, SSD_CONV, SSD_CHUNK = 8, 64, 2, 128, 4, 128
RWKV_HEADS, RWKV_N, RWKV_RANK, RWKV_CHUNK = 8, 64, 64, 64
RWKV_LN_EPS = 64e-5
RET_GROUP = 4
SSD_GROUP = 4
GLA_GROUP = 8
RWKV_GROUP = 8
MEM_HEADS, MEM_HEAD_DIM, MEM_WIDTH = 4, 64, 256

LANES = 128
CARRY_ROWS = 8
TOKEN_TILE = 512
MIXER_TILE = 512
VMEM_LIMIT = 56 * 1024 * 1024

_IN_SIZES = (256, 256, 512, 512, 256, 256, 512, 16, 512, 1024, 8, 512, 1664, 512, 256, 5120)
_IN_OFFS = tuple(int(v) for v in np.cumsum((0,) + _IN_SIZES))


def _dot(a, b):
    return jnp.dot(a.astype(MXU_DTYPE), b.astype(MXU_DTYPE), preferred_element_type=F32)


def _dot_nt(a, b):
    return lax.dot_general(a.astype(MXU_DTYPE), b.astype(MXU_DTYPE), (((1,), (1,)), ((), ())),
                           preferred_element_type=F32)


def _dot_tn(a, b):
    return lax.dot_general(a.astype(MXU_DTYPE), b.astype(MXU_DTYPE), (((0,), (0,)), ((), ())),
                           preferred_element_type=F32)


def _proj(x, w_t):
    return lax.dot_general(x, w_t, (((1,), (1,)), ((), ())), preferred_element_type=F32)


def _split3(x):
    hi = x.astype(BF16)
    r1 = x - hi.astype(F32)
    mid = r1.astype(BF16)
    lo = (r1 - mid.astype(F32)).astype(BF16)
    return hi, mid, lo


def _sel_dot(sel, x):
    s = sel.astype(BF16)
    return sum(jnp.dot(s, p, preferred_element_type=F32) for p in _split3(x))


def _dot_sel(x, sel, pieces=3):
    s = sel.astype(BF16)
    return sum(jnp.dot(p, s, preferred_element_type=F32) for p in _split3(x)[:pieces])


def _iota(shape, dim):
    return lax.broadcasted_iota(jnp.int32, shape, dim)


def _tril(n, strict=False):
    r, c = _iota((n, n), 0), _iota((n, n), 1)
    return (r > c) if strict else (r >= c)


def _silu(x):
    return x * jax.nn.sigmoid(x)


def _softplus(x):
    return jnp.maximum(x, 0.0) + jnp.log1p(jnp.exp(-jnp.abs(x)))


def _rows(c, n):
    return pl.ds(pl.multiple_of(c * n, n), n)


def _loop(trips, body):
    if trips == 1:
        body(0, 0)
    else:
        lax.fori_loop(0, trips, body, 0)


def _full_spec(shape):
    zeros = (0,) * len(shape)
    return pl.BlockSpec(shape, lambda *_: zeros)


def _layer_spec(arr, layer, **kwargs):
    tail = tuple(arr.shape[1:])
    index = (layer,) + (0,) * len(tail)
    return pl.BlockSpec((None,) + tail, lambda *_: index, **kwargs)


def _w_rows_spec(w_t, layer, row0, rows):
    return pl.BlockSpec((pl.Element(1), pl.Element(rows), pl.Element(w_t.shape[2])),
                        lambda *_: (layer, row0, 0), pipeline_mode=pl.Buffered(1))


def _first_step():
    return (pl.program_id(0) == 0) & (pl.program_id(1) == 0)


def _params(semantics):
    return pltpu.CompilerParams(dimension_semantics=semantics, vmem_limit_bytes=VMEM_LIMIT)


def _rmsnorm_kernel(x_ref, g_ref, o_ref):
    x = x_ref[...]
    y = x * lax.rsqrt(jnp.mean(x * x, axis=-1, keepdims=True) + NORM_EPS)
    o_ref[...] = (y * g_ref[...]).astype(o_ref.dtype)


def _rmsnorm(x2d, g, out_dtype, interpret):
    m, d = x2d.shape
    tm = min(1024, m)
    return pl.pallas_call(
        _rmsnorm_kernel,
        grid=(m // tm,),
        in_specs=[pl.BlockSpec((tm, d), lambda i: (i, 0)), _full_spec((1, d))],
        out_specs=pl.BlockSpec((tm, d), lambda i: (i, 0)),
        out_shape=jax.ShapeDtypeStruct((m, d), out_dtype),
        compiler_params=_params(("parallel",)),
        interpret=interpret,
        name="rmsnorm",
    )(x2d, g.reshape(1, d))


def _ret_kernel(h_ref, pos_ref, inv_ref, w_ref, perm_ref, o_ref, p_scr, s_scr, wb_scr, *, tb):
    C = RET_CHUNK

    @pl.when(_first_step())
    def _():
        for lo in (0, 256):
            rows = w_ref[0, lo:lo + 256, :].astype(wb_scr.dtype)
            wb_scr[lo:lo + 256, :] = jnp.dot(perm_ref[...], rows, preferred_element_type=F32).astype(wb_scr.dtype)
        wb_scr[512:1536, :] = w_ref[0, 512:1536, :].astype(wb_scr.dtype)

    @pl.when(pl.program_id(1) == 0)
    def _():
        s_scr[...] = jnp.zeros_like(s_scr)

    p = _proj(h_ref[...], wb_scr[...])
    ang = pos_ref[...].astype(F32) * inv_ref[...]
    cos, sin = jnp.cos(ang), jnp.sin(ang)
    q1, q2 = p[:, 0:128], p[:, 128:256]
    k1, k2 = p[:, 256:384] * RET_DK ** -0.5, p[:, 384:512] * RET_DK ** -0.5
    p_scr[:, 0:128] = q1 * cos - q2 * sin
    p_scr[:, 128:256] = q2 * cos + q1 * sin
    p_scr[:, 256:384] = k1 * cos - k2 * sin
    p_scr[:, 384:512] = k2 * cos + k1 * sin
    p_scr[:, 512:1536] = p[:, 512:1536]

    def log_gamma(head):
        return jnp.log(1.0 - jnp.exp2(-5.0 - head.astype(F32)))

    qk_head = (_iota((1, 256), 1) >> 5) & 3
    lg_lane = log_gamma(qk_head)
    tau = _iota((C, 1), 0).astype(F32)
    dq = jnp.exp(lg_lane * (tau + 1.0))
    dk = jnp.exp(lg_lane * (C - 1.0 - tau))
    ds = jnp.exp(lg_lane * float(C))
    diff = (_iota((C, C), 0) - _iota((C, C), 1)).astype(F32)
    causal = _tril(C)
    bd_mask = (_iota((WIDTH, 256), 0) >> 7) == ((_iota((WIDTH, 256), 1) >> 5) & 3)

    heads = range(RET_HEADS)
    hcols = [slice(hd * RET_DV, (hd + 1) * RET_DV) for hd in heads]
    seg = [jnp.where(causal, jnp.exp(math.log(1.0 - 2.0 ** (-5.0 - hd)) * diff), 0.0) for hd in heads]

    def group(gi, carry):
        us = range(RET_GROUP)
        rows = [_rows(gi * RET_GROUP + u, C) for u in us]
        q = [p_scr[rw, 0:256] for rw in rows]
        k = [p_scr[rw, 256:512] for rw in rows]
        v = [p_scr[rw, 512:1024] for rw in rows]
        sc = [[_dot_nt(q[u], jnp.where(qk_head == hd, k[u], 0.0)) * seg[hd] for hd in heads] for u in us]
        y_intra = [[_dot(sc[u][hd], v[u][:, hcols[hd]]) for hd in heads] for u in us]
        kv = [jnp.where(bd_mask, _dot_tn(v[u], k[u] * dk), 0.0) for u in us]
        st = s_scr[...]
        y_inter = []
        for u in us:
            y_inter.append(_dot_nt(q[u] * dq, st))
            st = st * ds + kv[u]
        s_scr[...] = st
        for u in us:
            for hd in heads:
                y = y_intra[u][hd] + y_inter[u][:, hcols[hd]]
                y = y * lax.rsqrt(jnp.mean(y * y, axis=-1, keepdims=True) + NORM_EPS)
                g = p_scr[rows[u], 1024 + hd * RET_DV:1024 + (hd + 1) * RET_DV]
                o_ref[rows[u], hcols[hd]] = (y * _silu(g)).astype(o_ref.dtype)
        return carry

    _loop(tb // (C * RET_GROUP), group)


def _retention(h, pos_col, inv_row, w, perm, layer, batch, seq, interpret):
    tb = min(MIXER_TILE, seq)
    nt = seq // tb
    kern = functools.partial(_ret_kernel, tb=tb)
    return pl.pallas_call(
        kern,
        grid=(batch, nt),
        in_specs=[pl.BlockSpec((tb, D_MODEL), lambda b, i: (b * nt + i, 0)),
                  pl.BlockSpec((tb, 1), lambda b, i: (b * nt + i, 0)),
                  _full_spec((1, LANES)),
                  _w_rows_spec(w, layer, _IN_OFFS[0], 1536), _full_spec(perm.shape)],
        out_specs=pl.BlockSpec((tb, WIDTH), lambda b, i: (b * nt + i, 0)),
        out_shape=jax.ShapeDtypeStruct((batch * seq, WIDTH), MXU_DTYPE),
        scratch_shapes=[pltpu.VMEM((tb, 1536), F32), pltpu.VMEM((WIDTH, 256), F32),
                        pltpu.VMEM((1536, D_MODEL), MXU_DTYPE)],
        compiler_params=_params(("arbitrary", "arbitrary")),
        interpret=interpret,
        name="retention",
    )(h, pos_col, inv_row, w, perm)


def _gla_kernel(h_ref, w_ref, w2_ref, gb_ref, ng_ref, o_ref, p_scr, lg_scr, s_scr, wb_scr, *, tb):
    C = GLA_CHUNK

    @pl.when(_first_step())
    def _():
        cast = lambda v: v.astype(wb_scr.dtype)
        wb_scr[0:1024, :] = cast(w_ref[0, 0:1024, :])
        wb_scr[1024:1536, :] = cast(w_ref[0, 1024 + GLA_RANK:1536 + GLA_RANK, :])
        wb_scr[1536:1664, :] = cast(jnp.concatenate(
            [w_ref[0, 1024:1024 + GLA_RANK, :], jnp.zeros((LANES - GLA_RANK, D_MODEL), F32)], axis=0))

    @pl.when(pl.program_id(1) == 0)
    def _():
        s_scr[...] = jnp.zeros_like(s_scr)

    p = _proj(h_ref[...], wb_scr[...])
    p_scr[...] = p[:, 0:1536]
    pre = _dot(p[:, 1536:1664], w2_ref[...]) + gb_ref[...]
    lg_scr[...] = -_softplus(-pre) / GLA_NORMALIZER

    tri = _tril(C)
    k_head = _iota((1, 256), 1) >> 6
    bd_mask = (_iota((WIDTH, 256), 0) >> 7) == (_iota((WIDTH, 256), 1) >> 6)
    ng = ng_ref[...]

    heads = range(GLA_HEADS)
    hcols = [slice(hd * GLA_DV, (hd + 1) * GLA_DV) for hd in heads]

    def group(gi, carry):
        us = range(GLA_GROUP)
        rows = [_rows(gi * GLA_GROUP + u, C) for u in us]
        q = [p_scr[rw, 0:256] * GLA_DK ** -0.5 for rw in rows]
        k = [p_scr[rw, 256:512] for rw in rows]
        v = [p_scr[rw, 512:1024] for rw in rows]
        cum = [_sel_dot(tri, lg_scr[rw, :]) for rw in rows]
        ref = [c[C // 2:C // 2 + 1, :] for c in cum]
        last = [c[C - 1:C, :] for c in cum]
        q_in = [q[u] * jnp.exp(cum[u] - ref[u]) for u in us]
        k_in = [k[u] * jnp.exp(ref[u] - cum[u]) for u in us]
        q_dec = [q[u] * jnp.exp(cum[u]) for u in us]
        k_st = [k[u] * jnp.exp(last[u] - cum[u]) for u in us]
        sc = [[jnp.where(tri, _dot_nt(q_in[u], jnp.where(k_head == hd, k_in[u], 0.0)), 0.0) for hd in heads]
              for u in us]
        y_intra = [[_dot(sc[u][hd], v[u][:, hcols[hd]]) for hd in heads] for u in us]
        kv = [jnp.where(bd_mask, _dot_tn(v[u], k_st[u]), 0.0) for u in us]
        st = s_scr[...]
        y_inter = []
        for u in us:
            y_inter.append(_dot_nt(q_dec[u], st))
            st = st * jnp.exp(last[u]) + kv[u]
        s_scr[...] = st
        for u in us:
            for hd in heads:
                y = y_intra[u][hd] + y_inter[u][:, hcols[hd]]
                y = y * lax.rsqrt(jnp.mean(y * y, axis=-1, keepdims=True) + NORM_EPS) * ng
                g = p_scr[rows[u], 1024 + hd * GLA_DV:1024 + (hd + 1) * GLA_DV]
                o_ref[rows[u], hcols[hd]] = (y * _silu(g)).astype(o_ref.dtype)
        return carry

    _loop(tb // (C * GLA_GROUP), group)


def _gla(h, w, w2p, gb, ng, layer, batch, seq, interpret):
    tb = min(MIXER_TILE, seq)
    nt = seq // tb
    kern = functools.partial(_gla_kernel, tb=tb)
    return pl.pallas_call(
        kern,
        grid=(batch, nt),
        in_specs=[pl.BlockSpec((tb, D_MODEL), lambda b, i: (b * nt + i, 0)),
                  _w_rows_spec(w, layer, _IN_OFFS[4], 1536 + GLA_RANK), _layer_spec(w2p, layer),
                  _layer_spec(gb, layer), _layer_spec(ng, layer)],
        out_specs=pl.BlockSpec((tb, WIDTH), lambda b, i: (b * nt + i, 0)),
        out_shape=jax.ShapeDtypeStruct((batch * seq, WIDTH), MXU_DTYPE),
        scratch_shapes=[pltpu.VMEM((tb, 1536), F32), pltpu.VMEM((tb, 256), F32),
                        pltpu.VMEM((WIDTH, 256), F32), pltpu.VMEM((1664, D_MODEL), MXU_DTYPE)],
        compiler_params=_params(("arbitrary", "arbitrary")),
        interpret=interpret,
        name="gla",
    )(h, w, w2p, gb, ng)


def _ssd_kernel(h_ref, w_ref, cw_ref, cb_ref, dtb_ref, a_ref, dskip_ref, ng_ref, exp_ref, o_ref,
                raw_scr, xc_scr, z_scr, dt_scr, s_scr, wb_scr, *, tb):
    C = SSD_CHUNK
    NCH = 1024

    @pl.when(_first_step())
    def _():
        cast = lambda v: v.astype(wb_scr.dtype)
        wb_scr[0:NCH, :] = cast(w_ref[0, 0:NCH, :])
        wb_scr[NCH:NCH + WIDTH, :] = cast(w_ref[0, NCH + SSD_HEADS:NCH + SSD_HEADS + WIDTH, :])
        wb_scr[1536:1664, :] = cast(jnp.concatenate(
            [w_ref[0, NCH:NCH + SSD_HEADS, :], jnp.zeros((LANES - SSD_HEADS, D_MODEL), F32)], axis=0))

    @pl.when(pl.program_id(1) == 0)
    def _():
        s_scr[...] = jnp.zeros_like(s_scr)
        raw_scr[0:CARRY_ROWS, :] = jnp.zeros((CARRY_ROWS, NCH), F32)

    p = _proj(h_ref[...], wb_scr[...])
    raw_scr[CARRY_ROWS:CARRY_ROWS + tb, :] = p[:, 0:NCH]
    z_scr[...] = p[:, NCH:NCH + WIDTH]
    dt_scr[...] = _softplus(p[:, 1536:1664] + dtb_ref[...])
    conv = cb_ref[...] + sum(
        raw_scr[CARRY_ROWS - (SSD_CONV - 1) + j:CARRY_ROWS - (SSD_CONV - 1) + j + tb, :] * cw_ref[j:j + 1, :]
        for j in range(SSD_CONV))
    xc_scr[...] = _silu(conv)
    raw_scr[0:CARRY_ROWS, :] = raw_scr[tb:tb + CARRY_ROWS, :]

    tri = _tril(C)
    lane_lo = _iota((1, LANES), 1) < SSD_P
    expand = exp_ref[...]
    a_row = a_ref[...]

    groups = range(SSD_GROUPS)
    gcols = [slice(g * 256, (g + 1) * 256) for g in groups]
    ncols = [slice(g * SSD_STATE, (g + 1) * SSD_STATE) for g in groups]

    def group(gi, carry):
        us = range(SSD_GROUP)
        rows = [_rows(gi * SSD_GROUP + u, C) for u in us]
        xs = [xc_scr[rw, 0:512] for rw in rows]
        bm = [xc_scr[rw, 512:768] for rw in rows]
        cm = [xc_scr[rw, 768:1024] for rw in rows]
        dt = [dt_scr[rw, :] for rw in rows]
        cum = [_sel_dot(tri, dt[u] * a_row) for u in us]
        cum_t = [c.T for c in cum]
        dt_e = [_dot_sel(d, expand) for d in dt]
        cum_e = [_dot_sel(c, expand) for c in cum]
        last_e = [c[C - 1:C, :] for c in cum_e]
        xdt = [xs[u] * dt_e[u] for u in us]
        v_st = [xdt[u] * jnp.exp(last_e[u] - cum_e[u]) for u in us]
        e_cum = [jnp.exp(c) for c in cum_e]
        scores = [[_dot_nt(cm[u][:, ncols[g]], bm[u][:, ncols[g]]) for g in groups] for u in us]
        y_intra = []
        for u in us:
            parts = []
            for pr in range(SSD_HEADS // 2):
                sc = scores[u][pr // 2]
                segs = []
                for hd in (2 * pr, 2 * pr + 1):
                    d = jnp.minimum(cum[u][:, hd:hd + 1] - cum_t[u][hd:hd + 1, :], 0.0)
                    segs.append(sc * jnp.where(tri, jnp.exp(d), 0.0))
                xp = xdt[u][:, pr * LANES:(pr + 1) * LANES]
                rhs = jnp.concatenate([jnp.where(lane_lo, xp, 0.0), jnp.where(lane_lo, 0.0, xp)], axis=0)
                parts.append(_dot(jnp.concatenate(segs, axis=1), rhs))
            y_intra.append(jnp.concatenate(parts, axis=1))
        kv = [[_dot_tn(bm[u][:, ncols[g]], v_st[u][:, gcols[g]]) for g in groups] for u in us]
        st = [s_scr[g] for g in groups]
        y_inter = []
        for u in us:
            y_inter.append(jnp.concatenate(
                [_dot(cm[u][:, ncols[g]], st[g]) * e_cum[u][:, gcols[g]] for g in groups], axis=1))
            st = [st[g] * jnp.exp(last_e[u][:, gcols[g]]) + kv[u][g] for g in groups]
        for g in groups:
            s_scr[g] = st[g]
        for u in us:
            y = y_intra[u] + y_inter[u] + dskip_ref[...] * xs[u]
            y = y * _silu(z_scr[rows[u], :])
            for g in groups:
                yg = y[:, gcols[g]]
                yg = yg * lax.rsqrt(jnp.mean(yg * yg, axis=-1, keepdims=True) + NORM_EPS)
                o_ref[rows[u], gcols[g]] = (yg * ng_ref[:, gcols[g]]).astype(o_ref.dtype)
        return carry

    _loop(tb // (C * SSD_GROUP), group)


def _ssd(h, w, cw, cb, dtb, a_row, dskip, ng, expand, layer, batch, seq, interpret):
    tb = min(MIXER_TILE, seq)
    nt = seq // tb
    kern = functools.partial(_ssd_kernel, tb=tb)
    small = [cw, cb, dtb, a_row, dskip, ng]
    return pl.pallas_call(
        kern,
        grid=(batch, nt),
        in_specs=[pl.BlockSpec((tb, D_MODEL), lambda b, i: (b * nt + i, 0)),
                  _w_rows_spec(w, layer, _IN_OFFS[9], 1536 + SSD_HEADS)]
        + [_layer_spec(a, layer) for a in small] + [_full_spec(expand.shape)],
        out_specs=pl.BlockSpec((tb, WIDTH), lambda b, i: (b * nt + i, 0)),
        out_shape=jax.ShapeDtypeStruct((batch * seq, WIDTH), MXU_DTYPE),
        scratch_shapes=[pltpu.VMEM((tb + CARRY_ROWS, 1024), F32), pltpu.VMEM((tb, 1024), F32),
                        pltpu.VMEM((tb, WIDTH), F32), pltpu.VMEM((tb, LANES), F32),
                        pltpu.VMEM((SSD_GROUPS, SSD_STATE, 256), F32), pltpu.VMEM((1664, D_MODEL), MXU_DTYPE)],
        compiler_params=_params(("arbitrary", "arbitrary")),
        interpret=interpret,
        name="ssd",
    )(h, w, *small, expand)


def _rwkv_kernel(h_ref, w_ref, mu_ref, w0_ref, w2_ref, a0_ref, a2_ref, kk_ref, ka_ref, rk_ref,
                 lng_ref, lnb_ref, seg_ref, o_ref,
                 u_scr, g_scr, r_scr, lw_scr, k_scr, v_scr, a_scr, b_scr, y_scr, bonus_scr, st_scr, wb_scr,
                 *, tb):
    C = RWKV_CHUNK
    NU = 3 * WIDTH + 2 * RWKV_RANK

    @pl.when(_first_step())
    def _():
        wb_scr[...] = w_ref[0].astype(wb_scr.dtype)

    @pl.when(pl.program_id(1) == 0)
    def _():
        st_scr[...] = jnp.zeros_like(st_scr)
        u_scr[0:CARRY_ROWS, :] = jnp.zeros((CARRY_ROWS, NU), F32)

    p = _proj(h_ref[...], wb_scr[...])
    g_scr[...] = p[:, NU:NU + WIDTH]
    u = p[:, 0:NU]
    u_scr[CARRY_ROWS:CARRY_ROWS + tb, :] = u
    u_prev = u_scr[CARRY_ROWS - 1:CARRY_ROWS - 1 + tb, :]
    xs = u + (u_prev - u) * mu_ref[...]
    u_scr[0:CARRY_ROWS, :] = u_scr[tb:tb + CARRY_ROWS, :]

    seg = seg_ref[...]

    def seg_sum(x, pieces):
        return jnp.concatenate([_dot_sel(x[:, j * 256:(j + 1) * 256], seg, pieces) for j in range(2)], axis=1)

    r = xs[:, 0:512]
    k = xs[:, 512:1024]
    v = xs[:, 1024:1536]
    wa = xs[:, 1536:1664]
    w_pre = w0_ref[...] + _dot(jnp.tanh(wa), w2_ref[...])
    a_gate = jax.nn.sigmoid(a0_ref[...] + _dot(wa, a2_ref[...]))
    lw_scr[...] = -jnp.exp(-_softplus(-w_pre) - 0.5)
    kk = k * kk_ref[...]
    k = k * (1.0 + (a_gate - 1.0) * ka_ref[...])
    kk = kk / jnp.maximum(jnp.sqrt(seg_sum(kk * kk, 1)), 1e-12)
    r_scr[...] = r
    k_scr[...] = k
    v_scr[...] = v
    a_scr[...] = -kk
    b_scr[...] = kk * a_gate
    bonus_scr[...] = seg_sum(r * k * rk_ref[...], 1) * v

    tri = _tril(C)
    r2, c2 = _iota((2 * C, 2 * C), 0), _iota((2 * C, 2 * C), 1)
    same = (r2 >> 6) == (c2 >> 6)
    low_s = same & ((r2 & 63) > (c2 & 63))
    low_i = same & ((r2 & 63) >= (c2 & 63))
    eye = (r2 == c2).astype(F32)
    lane_lo = _iota((1, LANES), 1) < RWKV_N

    pairs = range(RWKV_HEADS // 2)

    def stack_pair(xp):
        return jnp.concatenate([jnp.where(lane_lo, xp, 0.0), jnp.where(lane_lo, 0.0, xp)], axis=0)

    def stack(x, pr):
        return stack_pair(x[:, pr * LANES:(pr + 1) * LANES])

    def group(gi, carry):
        lanes = [(u, pr) for u in range(RWKV_GROUP) for pr in pairs]
        rows, a_0, r_0, v_c, b_h, k_h, e_last, lhs, rhs_t = [], [], [], [], [], [], [], {}, {}
        for u in range(RWKV_GROUP):
            rw = _rows(gi * RWKV_GROUP + u, C)
            lw = lw_scr[rw, :]
            cum = _sel_dot(tri, lw)
            cum_p = cum - lw
            ref = cum[C // 2:C // 2 + 1, :]
            last = cum[C - 1:C, :]
            e_fwd = jnp.exp(cum - ref)
            e_bwd = jnp.exp(ref - cum)
            e_end = jnp.exp(last - cum)
            r_c, k_c, a_c, b_c = r_scr[rw, :], k_scr[rw, :], a_scr[rw, :], b_scr[rw, :]
            r_t = r_c * e_fwd
            a_t = a_c * jnp.exp(cum_p - ref)
            b_t = b_c * e_bwd
            k_t = k_c * e_bwd
            for pr in pairs:
                lhs[u, pr] = jnp.concatenate([stack(a_t, pr), stack(r_t, pr)], axis=0)
                rhs_t[u, pr] = jnp.concatenate([stack(b_t, pr), stack(k_t, pr)], axis=0)
            rows.append(rw)
            a_0.append(a_c * jnp.exp(cum_p))
            r_0.append(r_c * jnp.exp(cum))
            v_c.append(v_scr[rw, :])
            b_h.append(b_c * e_end)
            k_h.append(k_c * e_end)
            e_last.append(jnp.exp(last))
        big = [_dot_nt(lhs[ln], rhs_t[ln]) for ln in lanes]
        a_ab = [jnp.where(low_s, m[0:128, 0:128], 0.0) for m in big]
        a_ak = [jnp.where(low_s, m[0:128, 128:256], 0.0) for m in big]
        a_rb = [jnp.where(low_i, m[128:256, 0:128], 0.0) for m in big]
        a_rk = [jnp.where(low_i, m[128:256, 128:256], 0.0) for m in big]
        inv = [eye + m for m in a_ab]
        pw = [_dot(m, m) for m in a_ab]
        for _ in range(4):
            prod = [_dot(p, jnp.concatenate([p, t], axis=1)) for p, t in zip(pw, inv)]
            pw = [m[:, 0:128] for m in prod]
            inv = [t + m[:, 128:256] for t, m in zip(inv, prod)]
        inv = [t + _dot(p, t) for p, t in zip(pw, inv)]
        vs = [stack(v_c[u], pr) for u, pr in lanes]
        akv = [_dot(m, x) for m, x in zip(a_ak, vs)]
        a_r = [jnp.concatenate([m, n], axis=1) for m, n in zip(a_rb, a_rk)]
        bk_h = [jnp.concatenate([stack(b_h[u], pr), stack(k_h[u], pr)], axis=0) for u, pr in lanes]

        st = [st_scr[pr] for pr in pairs]
        for u in range(RWKV_GROUP):
            ix = [u * len(pairs) + pr for pr in pairs]
            pcols = [slice(pr * LANES, (pr + 1) * LANES) for pr in pairs]
            rhs = [stack_pair(_dot_nt(a_0[u][:, pcols[pr]], st[pr])) + akv[i] for pr, i in zip(pairs, ix)]
            uv = [jnp.concatenate([_dot(inv[i], m), vs[i]], axis=0) for m, i in zip(rhs, ix)]
            oo = [_dot(a_r[i], uv[pr]) for pr, i in zip(pairs, ix)]
            y_st = [_dot_nt(r_0[u][:, pcols[pr]], st[pr]) for pr in pairs]
            st = [st[pr] * e_last[u][:, pcols[pr]] + _dot_tn(uv[pr], bk_h[i]) for pr, i in zip(pairs, ix)]
            for pr in pairs:
                y_scr[rows[u], pcols[pr]] = y_st[pr] + oo[pr][0:C, :] + oo[pr][C:2 * C, :]
        for pr in pairs:
            st_scr[pr] = st[pr]
        return carry

    _loop(tb // (C * RWKV_GROUP), group)

    y = y_scr[...]
    mean = seg_sum(y, 2) * (1.0 / RWKV_N)
    d = y - mean
    var = seg_sum(d * d, 1) * (1.0 / RWKV_N)
    y = d * lax.rsqrt(var + RWKV_LN_EPS) * lng_ref[...] + lnb_ref[...] + bonus_scr[...]
    o_ref[...] = (y * _silu(g_scr[...])).astype(o_ref.dtype)


def _rwkv(h, w, vecs, w2p, a2p, seg, layer, batch, seq, interpret):
    tb = min(TOKEN_TILE, seq)
    nt = seq // tb
    kern = functools.partial(_rwkv_kernel, tb=tb)
    mu, w0, a0, kk, ka, rk, lng, lnb = vecs
    ins = [mu, w0, w2p, a0, a2p, kk, ka, rk, lng, lnb]
    wide = lambda: pltpu.VMEM((tb, WIDTH), F32)
    n_rows = _IN_OFFS[14] - _IN_OFFS[12]
    return pl.pallas_call(
        kern,
        grid=(batch, nt),
        in_specs=[pl.BlockSpec((tb, D_MODEL), lambda b, i: (b * nt + i, 0)),
                  _w_rows_spec(w, layer, _IN_OFFS[12], n_rows)]
        + [_layer_spec(a, layer) for a in ins] + [_full_spec(seg.shape)],
        out_specs=pl.BlockSpec((tb, WIDTH), lambda b, i: (b * nt + i, 0)),
        out_shape=jax.ShapeDtypeStruct((batch * seq, WIDTH), MXU_DTYPE),
        scratch_shapes=[pltpu.VMEM((tb + CARRY_ROWS, 3 * WIDTH + 2 * RWKV_RANK), F32)]
        + [wide() for _ in range(9)]
        + [pltpu.VMEM((RWKV_HEADS // 2, LANES, LANES), F32), pltpu.VMEM((n_rows, D_MODEL), MXU_DTYPE)],
        compiler_params=_params(("arbitrary", "arbitrary")),
        interpret=interpret,
        name="rwkv7",
    )(h, w, *ins, seg)


def _memkv_kernel(mem_ref, g_ref, w_ref, k_ref, v_ref):
    x = mem_ref[0]
    y = x * lax.rsqrt(jnp.mean(x * x, axis=-1, keepdims=True) + NORM_EPS) * g_ref[...]
    kv = _dot(y, w_ref[...])
    head = _iota((1, MEM_WIDTH), 1) >> 6
    for hd in range(MEM_HEADS):
        k_ref[0, hd] = jnp.where(head == hd, kv[:, 0:MEM_WIDTH], 0.0).astype(k_ref.dtype)
        v_ref[0, hd] = jnp.where(head == hd, kv[:, MEM_WIDTH:2 * MEM_WIDTH], 0.0).astype(v_ref.dtype)


def _memkv(mem, g, w, layer, interpret):
    b, m, d = mem.shape
    out = jax.ShapeDtypeStruct((b, MEM_HEADS, m, MEM_WIDTH), MXU_DTYPE)
    return pl.pallas_call(
        _memkv_kernel,
        grid=(b,),
        in_specs=[pl.BlockSpec((1, m, d), lambda i: (i, 0, 0)), _layer_spec(g, layer), _layer_spec(w, layer)],
        out_specs=[pl.BlockSpec((1, MEM_HEADS, m, MEM_WIDTH), lambda i: (i, 0, 0, 0))] * 2,
        out_shape=[out, out],
        compiler_params=_params(("parallel",)),
        interpret=interpret,
        name="mem_kv",
    )(mem, g, w)


def _merge_kernel(x_ref, h_ref, oret_ref, ogla_ref, ossd_ref, orwkv_ref, km_ref, vm_ref,
                  wq_ref, wg_ref, uret_ref, ugla_ref, ussd_ref, urwkv_ref, umem_ref, wout_ref, gn_ref,
                  *refs):
    out_refs, wqb_scr = refs[:-1], refs[-1]

    @pl.when(_first_step())
    def _():
        wqb_scr[...] = wq_ref[0].astype(wqb_scr.dtype)

    h = h_ref[...]
    q = _proj(h, wqb_scr[...]) * MEM_HEAD_DIM ** -0.5
    o_mem = jnp.zeros(q.shape, F32)
    for hd in range(MEM_HEADS):
        s = _dot_nt(q, km_ref[0, hd])
        s = jnp.exp(s - jnp.max(s, axis=-1, keepdims=True))
        prob = s / jnp.sum(s, axis=-1, keepdims=True)
        o_mem = o_mem + _dot(prob, vm_ref[0, hd])
    branches = ((oret_ref, uret_ref), (ogla_ref, ugla_ref), (ossd_ref, ussd_ref), (orwkv_ref, urwkv_ref))
    merged = None
    for i in range(N_BRANCHES):
        gate = jax.nn.sigmoid(_proj(h, wg_ref[i * D_MODEL:(i + 1) * D_MODEL, :]))
        if i < 4:
            o_ref, u_ref = branches[i]
            up = jnp.dot(o_ref[...], u_ref[...], preferred_element_type=F32)
        else:
            up = _dot(o_mem, umem_ref[...])
        merged = gate * up if merged is None else merged + gate * up
    x = x_ref[...] + _dot(merged, wout_ref[...])
    if len(out_refs) == 2:
        out_refs[0][...] = x
    y = x * lax.rsqrt(jnp.mean(x * x, axis=-1, keepdims=True) + NORM_EPS) * gn_ref[...]
    out_refs[-1][...] = y.astype(out_refs[-1].dtype)


def _merge(x2d, h, o_ret, o_gla, o_ssd, o_rwkv, km, vm, wq, wg, ups, wout, g_next, layer, last,
           batch, seq, interpret):
    tm = min(TOKEN_TILE, seq)
    nt = seq // tm
    row = lambda w: pl.BlockSpec((tm, w), lambda b, i: (b * nt + i, 0))
    kvspec = pl.BlockSpec((1,) + km.shape[1:], lambda b, i: (b, 0, 0, 0))
    weights = [wg, *ups, wout, g_next]
    return pl.pallas_call(
        _merge_kernel,
        grid=(batch, nt),
        in_specs=[row(D_MODEL), row(D_MODEL), row(WIDTH), row(WIDTH), row(WIDTH), row(WIDTH), kvspec, kvspec,
                  _w_rows_spec(wq, layer, _IN_OFFS[14], MEM_WIDTH)]
        + [_layer_spec(w, layer, pipeline_mode=pl.Buffered(1)) for w in weights],
        out_specs=[row(D_MODEL)] if last else [row(D_MODEL), row(D_MODEL)],
        out_shape=[jax.ShapeDtypeStruct(x2d.shape, F32)] if last else
        [jax.ShapeDtypeStruct(x2d.shape, F32), jax.ShapeDtypeStruct(x2d.shape, MXU_DTYPE)],
        scratch_shapes=[pltpu.VMEM((MEM_WIDTH, D_MODEL), MXU_DTYPE)],
        compiler_params=_params(("arbitrary", "arbitrary")),
        interpret=interpret,
        name="merge",
    )(x2d, h, o_ret, o_gla, o_ssd, o_rwkv, km, vm, wq, *weights)


def _pad_last(a, width):
    return jnp.pad(a, [(0, 0)] * (a.ndim - 1) + [(0, width - a.shape[-1])])


def _rows3(v, width=None):
    v = v.reshape(v.shape[0], 1, -1).astype(F32)
    return v if width is None else _pad_last(v, width)


def _forward(x, mem, positions, norm_g, w_in, gla_gk_w2, gla_gk_b, gla_norm_g,
             ssd_conv_w, ssd_conv_b, ssd_dt_bias, ssd_a_log, ssd_d, ssd_norm_g,
             rwkv_mu, rwkv_w0, rwkv_w2, rwkv_a0, rwkv_a2, rwkv_k_k, rwkv_k_a, rwkv_r_k,
             rwkv_ln_g, rwkv_ln_b, mem_norm_g, w_mem_kv,
             w_up_ret, w_up_gla, w_up_ssd, w_up_rwkv, w_up_mem, w_out, final_norm_g, interpret=False):
    batch, seq, d = x.shape
    depth = w_in.shape[0]
    cdt = MXU_DTYPE
    o = _IN_OFFS

    half = np.arange(RET_DK // 2)
    ret_perm = np.concatenate([hd * RET_DK + 2 * half + par for par in (0, 1) for hd in range(RET_HEADS)])
    inv = 1.0 / (ROPE_BASE ** jnp.linspace(0.0, 1.0, RET_DK // 2, dtype=F32))
    inv_row = jnp.tile(inv, RET_HEADS).reshape(1, LANES)
    pos_col = positions.reshape(batch * seq, 1)
    head_of_lane = np.arange(WIDTH) // SSD_P
    ssd_expand = jnp.asarray(np.arange(LANES)[:, None] == head_of_lane[None, :], F32)
    rwkv_seg = jnp.asarray(head_of_lane[:256, None] == head_of_lane[None, :256], F32)

    w_t = jnp.swapaxes(w_in, 1, 2)
    w_g = w_t[:, o[15]:o[16], :].astype(cdt)
    ret_perm_rows = jnp.asarray(ret_perm[:, None] == np.arange(RET_HEADS * RET_DK)[None, :], cdt)
    gla_w2p = jnp.pad(gla_gk_w2, ((0, 0), (0, LANES - GLA_RANK), (0, 0))).astype(cdt)
    zeros_rank = jnp.zeros((depth, RWKV_RANK, WIDTH), F32)
    rwkv_w2p = jnp.concatenate([rwkv_w2, zeros_rank], axis=1).astype(cdt)
    rwkv_a2p = jnp.concatenate([zeros_rank, rwkv_a2], axis=1).astype(cdt)
    rwkv_vecs = [_rows3(v) for v in (rwkv_mu, rwkv_w0, rwkv_a0, rwkv_k_k, rwkv_k_a, rwkv_r_k,
                                     rwkv_ln_g, rwkv_ln_b)]
    ssd_small = [ssd_conv_w.astype(F32), _rows3(ssd_conv_b), _rows3(ssd_dt_bias, LANES),
                 _rows3(-jnp.exp(ssd_a_log.astype(F32)), LANES), _rows3(jnp.repeat(ssd_d, SSD_P, axis=1)),
                 _rows3(ssd_norm_g)]
    gla_b, gla_ng = _rows3(gla_gk_b), _rows3(gla_norm_g)
    mem_g, w_kv = _rows3(mem_norm_g), w_mem_kv.astype(cdt)
    ups = [w.astype(cdt) for w in (w_up_ret, w_up_gla, w_up_ssd, w_up_rwkv, w_up_mem)]
    w_o = w_out.astype(cdt)
    g_next = _rows3(jnp.concatenate([norm_g[1:], final_norm_g[None]], axis=0))

    x2d = x.reshape(batch * seq, d)
    h = _rmsnorm(x2d, norm_g[0], cdt, interpret)
    for l in range(depth):
        o_ret = _retention(h, pos_col, inv_row, w_t, ret_perm_rows, l, batch, seq, interpret)
        o_gla = _gla(h, w_t, gla_w2p, gla_b, gla_ng, l, batch, seq, interpret)
        o_ssd = _ssd(h, w_t, *ssd_small, ssd_expand, l, batch, seq, interpret)
        o_rwkv = _rwkv(h, w_t, rwkv_vecs, rwkv_w2p, rwkv_a2p, rwkv_seg, l, batch, seq, interpret)
        km, vm = _memkv(mem, mem_g, w_kv, l, interpret)
        last = l == depth - 1
        outs = _merge(x2d, h, o_ret, o_gla, o_ssd, o_rwkv, km, vm, w_t, w_g, ups, w_o, g_next, l, last,
                      batch, seq, interpret)
        x2d, h = (None, outs[0]) if last else outs
    return h.reshape(batch, seq, d)


def kernel(x, mem, positions, norm_g, w_in, gla_gk_w2, gla_gk_b, gla_norm_g, ssd_conv_w, ssd_conv_b, ssd_dt_bias, ssd_a_log, ssd_d, ssd_norm_g, rwkv_mu, rwkv_w0, rwkv_w2, rwkv_a0, rwkv_a2, rwkv_k_k, rwkv_k_a, rwkv_r_k, rwkv_ln_g, rwkv_ln_b, mem_norm_g, w_mem_kv, w_up_ret, w_up_gla, w_up_ssd, w_up_rwkv, w_up_mem, w_out, final_norm_g):
    return _forward(x, mem, positions, norm_g, w_in, gla_gk_w2, gla_gk_b, gla_norm_g,
                    ssd_conv_w, ssd_conv_b, ssd_dt_bias, ssd_a_log, ssd_d, ssd_norm_g,
                    rwkv_mu, rwkv_w0, rwkv_w2, rwkv_a0, rwkv_a2, rwkv_k_k, rwkv_k_a, rwkv_r_k,
                    rwkv_ln_g, rwkv_ln_b, mem_norm_g, w_mem_kv,
                    w_up_ret, w_up_gla, w_up_ssd, w_up_rwkv, w_up_mem, w_out, final_norm_g)
```

```python
import functools
import math

import jax
import jax.numpy as jnp
import numpy as np
from jax import lax
from jax.experimental import pallas as pl
from jax.experimental.pallas import tpu as pltpu

F32 = jnp.float32
BF16 = jnp.bfloat16
MXU_DTYPE = jnp.bfloat16

D_MODEL = 1024
WIDTH = 512
NORM_EPS = 1e-6
N_BRANCHES = 5

RET_HEADS, RET_DK, RET_DV, RET_CHUNK = 4, 64, 128, 128
ROPE_BASE = 10000.0
GLA_HEADS, GLA_DK, GLA_DV, GLA_RANK, GLA_NORMALIZER, GLA_CHUNK = 4, 64, 128, 16, 16.0, 64
SSD_HEADS, SSD_P, SSD_GROUPS, SSD_STATE, SSD_CONV, SSD_CHUNK = 8, 64, 2, 128, 4, 128
RWKV_HEADS, RWKV_N, RWKV_RANK, RWKV_CHUNK = 8, 64, 64, 64
RWKV_LN_EPS = 64e-5
RET_GROUP = 4
SSD_GROUP = 4
GLA_GROUP = 8
RWKV_GROUP = 8
MEM_HEADS, MEM_HEAD_DIM, MEM_WIDTH = 4, 64, 256

LANES = 128
CARRY_ROWS = 8
TOKEN_TILE = 512
MIXER_TILE = 512
VMEM_LIMIT = 56 * 1024 * 1024

_IN_SIZES = (256, 256, 512, 512, 256, 256, 512, 16, 512, 1024, 8, 512, 1664, 512, 256, 5120)
_IN_OFFS = tuple(int(v) for v in np.cumsum((0,) + _IN_SIZES))


def _dot(a, b):
    return jnp.dot(a.astype(MXU_DTYPE), b.astype(MXU_DTYPE), preferred_element_type=F32)


def _dot_nt(a, b):
    return lax.dot_general(a.astype(MXU_DTYPE), b.astype(MXU_DTYPE), (((1,), (1,)), ((), ())),
                           preferred_element_type=F32)


def _dot_tn(a, b):
    return lax.dot_general(a.astype(MXU_DTYPE), b.astype(MXU_DTYPE), (((0,), (0,)), ((), ())),
                           preferred_element_type=F32)


def _proj(x, w_t):
    return lax.dot_general(x, w_t, (((1,), (1,)), ((), ())), preferred_element_type=F32)


def _split3(x):
    hi = x.astype(BF16)
    r1 = x - hi.astype(F32)
    mid = r1.astype(BF16)
    lo = (r1 - mid.astype(F32)).astype(BF16)
    return hi, mid, lo


def _sel_dot(sel, x):
    s = sel.astype(BF16)
    return sum(jnp.dot(s, p, preferred_element_type=F32) for p in _split3(x))


def _dot_sel(x, sel, pieces=3):
    s = sel.astype(BF16)
    return sum(jnp.dot(p, s, preferred_element_type=F32) for p in _split3(x)[:pieces])


def _iota(shape, dim):
    return lax.broadcasted_iota(jnp.int32, shape, dim)


def _tril(n, strict=False):
    r, c = _iota((n, n), 0), _iota((n, n), 1)
    return (r > c) if strict else (r >= c)


def _silu(x):
    return x * jax.nn.sigmoid(x)


def _softplus(x):
    return jnp.maximum(x, 0.0) + jnp.log1p(jnp.exp(-jnp.abs(x)))


def _rows(c, n):
    return pl.ds(pl.multiple_of(c * n, n), n)


def _loop(trips, body):
    if trips == 1:
        body(0, 0)
    else:
        lax.fori_loop(0, trips, body, 0)


def _full_spec(shape):
    zeros = (0,) * len(shape)
    return pl.BlockSpec(shape, lambda *_: zeros)


def _layer_spec(arr, layer, **kwargs):
    tail = tuple(arr.shape[1:])
    index = (layer,) + (0,) * len(tail)
    return pl.BlockSpec((None,) + tail, lambda *_: index, **kwargs)


def _w_rows_spec(w_t, layer, row0, rows):
    return pl.BlockSpec((pl.Element(1), pl.Element(rows), pl.Element(w_t.shape[2])),
                        lambda *_: (layer, row0, 0), pipeline_mode=pl.Buffered(1))


def _first_step():
    return (pl.program_id(0) == 0) & (pl.program_id(1) == 0)


def _params(semantics):
    return pltpu.CompilerParams(dimension_semantics=semantics, vmem_limit_bytes=VMEM_LIMIT)


def _rmsnorm_kernel(x_ref, g_ref, o_ref):
    x = x_ref[...]
    y = x * lax.rsqrt(jnp.mean(x * x, axis=-1, keepdims=True) + NORM_EPS)
    o_ref[...] = (y * g_ref[...]).astype(o_ref.dtype)


def _rmsnorm(x2d, g, out_dtype, interpret):
    m, d = x2d.shape
    tm = min(1024, m)
    return pl.pallas_call(
        _rmsnorm_kernel,
        grid=(m // tm,),
        in_specs=[pl.BlockSpec((tm, d), lambda i: (i, 0)), _full_spec((1, d))],
        out_specs=pl.BlockSpec((tm, d), lambda i: (i, 0)),
        out_shape=jax.ShapeDtypeStruct((m, d), out_dtype),
        compiler_params=_params(("parallel",)),
        interpret=interpret,
        name="rmsnorm",
    )(x2d, g.reshape(1, d))


def _ret_kernel(h_ref, pos_ref, inv_ref, w_ref, perm_ref, o_ref, p_scr, s_scr, wb_scr, *, tb):
    C = RET_CHUNK

    @pl.when(_first_step())
    def _():
        for lo in (0, 256):
            rows = w_ref[0, lo:lo + 256, :].astype(wb_scr.dtype)
            wb_scr[lo:lo + 256, :] = jnp.dot(perm_ref[...], rows, preferred_element_type=F32).astype(wb_scr.dtype)
        wb_scr[512:1536, :] = w_ref[0, 512:1536, :].astype(wb_scr.dtype)

    @pl.when(pl.program_id(1) == 0)
    def _():
        s_scr[...] = jnp.zeros_like(s_scr)

    ang = pos_ref[...].astype(F32) * inv_ref[...]
    cos, sin = jnp.cos(ang), jnp.sin(ang)
    p_scr[:, 512:1536] = _proj(h_ref[...], wb_scr[512:1536, :])
    p = _proj(h_ref[...], wb_scr[0:512, :])
    q1, q2 = p[:, 0:128], p[:, 128:256]
    k1, k2 = p[:, 256:384] * RET_DK ** -0.5, p[:, 384:512] * RET_DK ** -0.5
    p_scr[:, 0:128] = q1 * cos - q2 * sin
    p_scr[:, 128:256] = q2 * cos + q1 * sin
    p_scr[:, 256:384] = k1 * cos - k2 * sin
    p_scr[:, 384:512] = k2 * cos + k1 * sin

    def log_gamma(head):
        return jnp.log(1.0 - jnp.exp2(-5.0 - head.astype(F32)))

    qk_head = (_iota((1, 256), 1) >> 5) & 3
    lg_lane = log_gamma(qk_head)
    tau = _iota((C, 1), 0).astype(F32)
    dq = jnp.exp(lg_lane * (tau + 1.0))
    dk = jnp.exp(lg_lane * (C - 1.0 - tau))
    ds = jnp.exp(lg_lane * float(C))
    diff = (_iota((C, C), 0) - _iota((C, C), 1)).astype(F32)
    causal = _tril(C)
    bd_mask = (_iota((WIDTH, 256), 0) >> 7) == ((_iota((WIDTH, 256), 1) >> 5) & 3)

    heads = range(RET_HEADS)
    hcols = [slice(hd * RET_DV, (hd + 1) * RET_DV) for hd in heads]
    seg = [jnp.where(causal, jnp.exp(math.log(1.0 - 2.0 ** (-5.0 - hd)) * diff), 0.0) for hd in heads]

    def group(gi, carry):
        us = range(RET_GROUP)
        rows = [_rows(gi * RET_GROUP + u, C) for u in us]
        q = [p_scr[rw, 0:256] for rw in rows]
        k = [p_scr[rw, 256:512] for rw in rows]
        v = [p_scr[rw, 512:1024] for rw in rows]
        sc = [[_dot_nt(q[u], jnp.where(qk_head == hd, k[u], 0.0)) * seg[hd] for hd in heads] for u in us]
        y_intra = [[_dot(sc[u][hd], v[u][:, hcols[hd]]) for hd in heads] for u in us]
        kv = [jnp.where(bd_mask, _dot_tn(v[u], k[u] * dk), 0.0) for u in us]
        st = s_scr[...]
        y_inter = []
        for u in us:
            y_inter.append(_dot_nt(q[u] * dq, st))
            st = st * ds + kv[u]
        s_scr[...] = st
        for u in us:
            for hd in heads:
                y = y_intra[u][hd] + y_inter[u][:, hcols[hd]]
                y = y * lax.rsqrt(jnp.mean(y * y, axis=-1, keepdims=True) + NORM_EPS)
                g = p_scr[rows[u], 1024 + hd * RET_DV:1024 + (hd + 1) * RET_DV]
                o_ref[rows[u], hcols[hd]] = (y * _silu(g)).astype(o_ref.dtype)
        return carry

    _loop(tb // (C * RET_GROUP), group)


def _retention(h, pos_col, inv_row, w, perm, layer, batch, seq, interpret):
    tb = min(MIXER_TILE, seq)
    nt = seq // tb
    kern = functools.partial(_ret_kernel, tb=tb)
    return pl.pallas_call(
        kern,
        grid=(batch, nt),
        in_specs=[pl.BlockSpec((tb, D_MODEL), lambda b, i: (b * nt + i, 0)),
                  pl.BlockSpec((tb, 1), lambda b, i: (b * nt + i, 0)),
                  _full_spec((1, LANES)),
                  _w_rows_spec(w, layer, _IN_OFFS[0], 1536), _full_spec(perm.shape)],
        out_specs=pl.BlockSpec((tb, WIDTH), lambda b, i: (b * nt + i, 0)),
        out_shape=jax.ShapeDtypeStruct((batch * seq, WIDTH), MXU_DTYPE),
        scratch_shapes=[pltpu.VMEM((tb, 1536), F32), pltpu.VMEM((WIDTH, 256), F32),
                        pltpu.VMEM((1536, D_MODEL), MXU_DTYPE)],
        compiler_params=_params(("arbitrary", "arbitrary")),
        interpret=interpret,
        name="retention",
    )(h, pos_col, inv_row, w, perm)


def _gla_kernel(h_ref, w_ref, w2_ref, gb_ref, ng_ref, o_ref, p_scr, lg_scr, s_scr, wb_scr, *, tb):
    C = GLA_CHUNK

    @pl.when(_first_step())
    def _():
        cast = lambda v: v.astype(wb_scr.dtype)
        wb_scr[0:1024, :] = cast(w_ref[0, 0:1024, :])
        wb_scr[1024:1536, :] = cast(w_ref[0, 1024 + GLA_RANK:1536 + GLA_RANK, :])
        wb_scr[1536:1664, :] = cast(jnp.concatenate(
            [w_ref[0, 1024:1024 + GLA_RANK, :], jnp.zeros((LANES - GLA_RANK, D_MODEL), F32)], axis=0))

    @pl.when(pl.program_id(1) == 0)
    def _():
        s_scr[...] = jnp.zeros_like(s_scr)

    pre = _dot(_proj(h_ref[...], wb_scr[1536:1664, :]), w2_ref[...]) + gb_ref[...]
    lg_scr[...] = -_softplus(-pre) / GLA_NORMALIZER
    tri = _tril(C)
    assert tb == C * GLA_GROUP
    cum_all = [_sel_dot(tri, lg_scr[u * C:(u + 1) * C, :]) for u in range(GLA_GROUP)]
    p_scr[:, 0:512] = _proj(h_ref[...], wb_scr[0:512, :])
    p_scr[:, 512:1536] = _proj(h_ref[...], wb_scr[512:1536, :])

    k_head = _iota((1, 256), 1) >> 6
    bd_mask = (_iota((WIDTH, 256), 0) >> 7) == (_iota((WIDTH, 256), 1) >> 6)
    ng = ng_ref[...]

    heads = range(GLA_HEADS)
    hcols = [slice(hd * GLA_DV, (hd + 1) * GLA_DV) for hd in heads]

    def group(gi, carry):
        us = range(GLA_GROUP)
        rows = [_rows(gi * GLA_GROUP + u, C) for u in us]
        q = [p_scr[rw, 0:256] * GLA_DK ** -0.5 for rw in rows]
        k = [p_scr[rw, 256:512] for rw in rows]
        v = [p_scr[rw, 512:1024] for rw in rows]
        cum = cum_all
        ref = [c[C // 2:C // 2 + 1, :] for c in cum]
        last = [c[C - 1:C, :] for c in cum]
        q_in = [q[u] * jnp.exp(cum[u] - ref[u]) for u in us]
        k_in = [k[u] * jnp.exp(ref[u] - cum[u]) for u in us]
        q_dec = [q[u] * jnp.exp(cum[u]) for u in us]
        k_st = [k[u] * jnp.exp(last[u] - cum[u]) for u in us]
        sc = [[jnp.where(tri, _dot_nt(q_in[u], jnp.where(k_head == hd, k_in[u], 0.0)), 0.0) for hd in heads]
              for u in us]
        y_intra = [[_dot(sc[u][hd], v[u][:, hcols[hd]]) for hd in heads] for u in us]
        kv = [jnp.where(bd_mask, _dot_tn(v[u], k_st[u]), 0.0) for u in us]
        st = s_scr[...]
        y_inter = []
        for u in us:
            y_inter.append(_dot_nt(q_dec[u], st))
            st = st * jnp.exp(last[u]) + kv[u]
        s_scr[...] = st
        for u in us:
            for hd in heads:
                y = y_intra[u][hd] + y_inter[u][:, hcols[hd]]
                y = y * lax.rsqrt(jnp.mean(y * y, axis=-1, keepdims=True) + NORM_EPS) * ng
                g = p_scr[rows[u], 1024 + hd * GLA_DV:1024 + (hd + 1) * GLA_DV]
                o_ref[rows[u], hcols[hd]] = (y * _silu(g)).astype(o_ref.dtype)
        return carry

    _loop(tb // (C * GLA_GROUP), group)


def _gla(h, w, w2p, gb, ng, layer, batch, seq, interpret):
    tb = min(MIXER_TILE, seq)
    nt = seq // tb
    kern = functools.partial(_gla_kernel, tb=tb)
    return pl.pallas_call(
        kern,
        grid=(batch, nt),
        in_specs=[pl.BlockSpec((tb, D_MODEL), lambda b, i: (b * nt + i, 0)),
                  _w_rows_spec(w, layer, _IN_OFFS[4], 1536 + GLA_RANK), _layer_spec(w2p, layer),
                  _layer_spec(gb, layer), _layer_spec(ng, layer)],
        out_specs=pl.BlockSpec((tb, WIDTH), lambda b, i: (b * nt + i, 0)),
        out_shape=jax.ShapeDtypeStruct((batch * seq, WIDTH), MXU_DTYPE),
        scratch_shapes=[pltpu.VMEM((tb, 1536), F32), pltpu.VMEM((tb, 256), F32),
                        pltpu.VMEM((WIDTH, 256), F32), pltpu.VMEM((1664, D_MODEL), MXU_DTYPE)],
        compiler_params=_params(("arbitrary", "arbitrary")),
        interpret=interpret,
        name="gla",
    )(h, w, w2p, gb, ng)


def _ssd_kernel(h_ref, w_ref, cw_ref, cb_ref, dtb_ref, a_ref, dskip_ref, ng_ref, exp_ref, o_ref,
                raw_scr, xc_scr, z_scr, dt_scr, s_scr, wb_scr, *, tb):
    C = SSD_CHUNK
    NCH = 1024

    @pl.when(_first_step())
    def _():
        cast = lambda v: v.astype(wb_scr.dtype)
        wb_scr[0:NCH, :] = cast(w_ref[0, 0:NCH, :])
        wb_scr[NCH:NCH + WIDTH, :] = cast(w_ref[0, NCH + SSD_HEADS:NCH + SSD_HEADS + WIDTH, :])
        wb_scr[1536:1664, :] = cast(jnp.concatenate(
            [w_ref[0, NCH:NCH + SSD_HEADS, :], jnp.zeros((LANES - SSD_HEADS, D_MODEL), F32)], axis=0))

    @pl.when(pl.program_id(1) == 0)
    def _():
        s_scr[...] = jnp.zeros_like(s_scr)
        raw_scr[0:CARRY_ROWS, :] = jnp.zeros((CARRY_ROWS, NCH), F32)

    dt_scr[...] = _softplus(_proj(h_ref[...], wb_scr[1536:1664, :]) + dtb_ref[...])
    first_tap = CARRY_ROWS - (SSD_CONV - 1)
    for lo in range(0, NCH, 256):
        cols = slice(lo, lo + 256)
        raw_scr[CARRY_ROWS:CARRY_ROWS + tb, cols] = _proj(h_ref[...], wb_scr[cols, :])
        conv = cb_ref[:, cols] + sum(raw_scr[first_tap + j:first_tap + j + tb, cols] * cw_ref[j:j + 1, cols]
                                     for j in range(SSD_CONV))
        xc_scr[:, cols] = _silu(conv)
        raw_scr[0:CARRY_ROWS, cols] = raw_scr[tb:tb + CARRY_ROWS, cols]
    z_scr[...] = _proj(h_ref[...], wb_scr[NCH:NCH + WIDTH, :])

    tri = _tril(C)
    lane_lo = _iota((1, LANES), 1) < SSD_P
    expand = exp_ref[...]
    a_row = a_ref[...]

    groups = range(SSD_GROUPS)
    gcols = [slice(g * 256, (g + 1) * 256) for g in groups]
    ncols = [slice(g * SSD_STATE, (g + 1) * SSD_STATE) for g in groups]

    def group(gi, carry):
        us = range(SSD_GROUP)
        rows = [_rows(gi * SSD_GROUP + u, C) for u in us]
        xs = [xc_scr[rw, 0:512] for rw in rows]
        bm = [xc_scr[rw, 512:768] for rw in rows]
        cm = [xc_scr[rw, 768:1024] for rw in rows]
        dt = [dt_scr[rw, :] for rw in rows]
        cum = [_sel_dot(tri, dt[u] * a_row) for u in us]
        cum_t = [c.T for c in cum]
        dt_e = [_dot_sel(d, expand) for d in dt]
        cum_e = [_dot_sel(c, expand) for c in cum]
        last_e = [c[C - 1:C, :] for c in cum_e]
        xdt = [xs[u] * dt_e[u] for u in us]
        v_st = [xdt[u] * jnp.exp(last_e[u] - cum_e[u]) for u in us]
        e_cum = [jnp.exp(c) for c in cum_e]
        scores = [[_dot_nt(cm[u][:, ncols[g]], bm[u][:, ncols[g]]) for g in groups] for u in us]
        y_intra = []
        for u in us:
            parts = []
            for pr in range(SSD_HEADS // 2):
                sc = scores[u][pr // 2]
                segs = []
                for hd in (2 * pr, 2 * pr + 1):
                    d = jnp.minimum(cum[u][:, hd:hd + 1] - cum_t[u][hd:hd + 1, :], 0.0)
                    segs.append(sc * jnp.where(tri, jnp.exp(d), 0.0))
                xp = xdt[u][:, pr * LANES:(pr + 1) * LANES]
                rhs = jnp.concatenate([jnp.where(lane_lo, xp, 0.0), jnp.where(lane_lo, 0.0, xp)], axis=0)
                parts.append(_dot(jnp.concatenate(segs, axis=1), rhs))
            y_intra.append(jnp.concatenate(parts, axis=1))
        kv = [[_dot_tn(bm[u][:, ncols[g]], v_st[u][:, gcols[g]]) for g in groups] for u in us]
        st = [s_scr[g] for g in groups]
        y_inter = []
        for u in us:
            y_inter.append(jnp.concatenate(
                [_dot(cm[u][:, ncols[g]], st[g]) * e_cum[u][:, gcols[g]] for g in groups], axis=1))
            st = [st[g] * jnp.exp(last_e[u][:, gcols[g]]) + kv[u][g] for g in groups]
        for g in groups:
            s_scr[g] = st[g]
        for u in us:
            y = y_intra[u] + y_inter[u] + dskip_ref[...] * xs[u]
            y = y * _silu(z_scr[rows[u], :])
            for g in groups:
                yg = y[:, gcols[g]]
                yg = yg * lax.rsqrt(jnp.mean(yg * yg, axis=-1, keepdims=True) + NORM_EPS)
                o_ref[rows[u], gcols[g]] = (yg * ng_ref[:, gcols[g]]).astype(o_ref.dtype)
        return carry

    _loop(tb // (C * SSD_GROUP), group)


def _ssd(h, w, cw, cb, dtb, a_row, dskip, ng, expand, layer, batch, seq, interpret):
    tb = min(MIXER_TILE, seq)
    nt = seq // tb
    kern = functools.partial(_ssd_kernel, tb=tb)
    small = [cw, cb, dtb, a_row, dskip, ng]
    return pl.pallas_call(
        kern,
        grid=(batch, nt),
        in_specs=[pl.BlockSpec((tb, D_MODEL), lambda b, i: (b * nt + i, 0)),
                  _w_rows_spec(w, layer, _IN_OFFS[9], 1536 + SSD_HEADS)]
        + [_layer_spec(a, layer) for a in small] + [_full_spec(expand.shape)],
        out_specs=pl.BlockSpec((tb, WIDTH), lambda b, i: (b * nt + i, 0)),
        out_shape=jax.ShapeDtypeStruct((batch * seq, WIDTH), MXU_DTYPE),
        scratch_shapes=[pltpu.VMEM((tb + CARRY_ROWS, 1024), F32), pltpu.VMEM((tb, 1024), F32),
                        pltpu.VMEM((tb, WIDTH), F32), pltpu.VMEM((tb, LANES), F32),
                        pltpu.VMEM((SSD_GROUPS, SSD_STATE, 256), F32), pltpu.VMEM((1664, D_MODEL), MXU_DTYPE)],
        compiler_params=_params(("arbitrary", "arbitrary")),
        interpret=interpret,
        name="ssd",
    )(h, w, *small, expand)


def _rwkv_kernel(h_ref, w_ref, mu_ref, w0_ref, w2_ref, a0_ref, a2_ref, kk_ref, ka_ref, rk_ref,
                 lng_ref, lnb_ref, seg_ref, o_ref,
                 u_scr, g_scr, r_scr, lw_scr, k_scr, v_scr, a_scr, b_scr, y_scr, bonus_scr, st_scr, wb_scr,
                 *, tb):
    C = RWKV_CHUNK
    NU = 3 * WIDTH + 2 * RWKV_RANK

    @pl.when(_first_step())
    def _():
        wb_scr[...] = w_ref[0].astype(wb_scr.dtype)

    @pl.when(pl.program_id(1) == 0)
    def _():
        st_scr[...] = jnp.zeros_like(st_scr)
        u_scr[0:CARRY_ROWS, :] = jnp.zeros((CARRY_ROWS, NU), F32)

    seg = seg_ref[...]

    def seg_sum(x, pieces):
        return jnp.concatenate([_dot_sel(x[:, j * 256:(j + 1) * 256], seg, pieces) for j in range(2)], axis=1)

    def shifted(lo, hi):
        u = _proj(h_ref[...], wb_scr[lo:hi, :])
        u_scr[CARRY_ROWS:CARRY_ROWS + tb, lo:hi] = u
        u_prev = u_scr[CARRY_ROWS - 1:CARRY_ROWS - 1 + tb, lo:hi]
        u_scr[0:CARRY_ROWS, lo:hi] = u_scr[tb:tb + CARRY_ROWS, lo:hi]
        return u + (u_prev - u) * mu_ref[:, lo:hi]

    wa = shifted(1536, 1664)
    w_pre = w0_ref[...] + _dot(jnp.tanh(wa), w2_ref[...])
    a_gate = jax.nn.sigmoid(a0_ref[...] + _dot(wa, a2_ref[...]))
    lw_scr[...] = -jnp.exp(-_softplus(-w_pre) - 0.5)
    k = shifted(512, 1024)
    kk = k * kk_ref[...]
    k = k * (1.0 + (a_gate - 1.0) * ka_ref[...])
    kk = kk / jnp.maximum(jnp.sqrt(seg_sum(kk * kk, 1)), 1e-12)
    k_scr[...] = k
    a_scr[...] = -kk
    b_scr[...] = kk * a_gate
    r = shifted(0, 512)
    r_scr[...] = r
    v = shifted(1024, 1536)
    v_scr[...] = v
    bonus_scr[...] = seg_sum(r * k * rk_ref[...], 1) * v
    g_scr[...] = _proj(h_ref[...], wb_scr[NU:NU + WIDTH, :])

    tri = _tril(C)
    r2, c2 = _iota((2 * C, 2 * C), 0), _iota((2 * C, 2 * C), 1)
    same = (r2 >> 6) == (c2 >> 6)
    low_s = same & ((r2 & 63) > (c2 & 63))
    low_i = same & ((r2 & 63) >= (c2 & 63))
    eye = (r2 == c2).astype(F32)
    lane_lo = _iota((1, LANES), 1) < RWKV_N

    pairs = range(RWKV_HEADS // 2)

    def stack_pair(xp):
        return jnp.concatenate([jnp.where(lane_lo, xp, 0.0), jnp.where(lane_lo, 0.0, xp)], axis=0)

    def stack(x, pr):
        return stack_pair(x[:, pr * LANES:(pr + 1) * LANES])

    def group(gi, carry):
        lanes = [(u, pr) for u in range(RWKV_GROUP) for pr in pairs]
        rows, a_0, r_0, v_c, b_h, k_h, e_last, lhs, rhs_t = [], [], [], [], [], [], [], {}, {}
        for u in range(RWKV_GROUP):
            rw = _rows(gi * RWKV_GROUP + u, C)
            lw = lw_scr[rw, :]
            cum = _sel_dot(tri, lw)
            cum_p = cum - lw
            ref = cum[C // 2:C // 2 + 1, :]
            last = cum[C - 1:C, :]
            e_fwd = jnp.exp(cum - ref)
            e_bwd = jnp.exp(ref - cum)
            e_end = jnp.exp(last - cum)
            r_c, k_c, a_c, b_c = r_scr[rw, :], k_scr[rw, :], a_scr[rw, :], b_scr[rw, :]
            r_t = r_c * e_fwd
            a_t = a_c * jnp.exp(cum_p - ref)
            b_t = b_c * e_bwd
            k_t = k_c * e_bwd
            for pr in pairs:
                lhs[u, pr] = jnp.concatenate([stack(a_t, pr), stack(r_t, pr)], axis=0)
                rhs_t[u, pr] = jnp.concatenate([stack(b_t, pr), stack(k_t, pr)], axis=0)
            rows.append(rw)
            a_0.append(a_c * jnp.exp(cum_p))
            r_0.append(r_c * jnp.exp(cum))
            v_c.append(v_scr[rw, :])
            b_h.append(b_c * e_end)
            k_h.append(k_c * e_end)
            e_last.append(jnp.exp(last))
        big = [_dot_nt(lhs[ln], rhs_t[ln]) for ln in lanes]
        a_ab = [jnp.where(low_s, m[0:128, 0:128], 0.0) for m in big]
        a_ak = [jnp.where(low_s, m[0:128, 128:256], 0.0) for m in big]
        a_rb = [jnp.where(low_i, m[128:256, 0:128], 0.0) for m in big]
        a_rk = [jnp.where(low_i, m[128:256, 128:256], 0.0) for m in big]
        inv = [eye + m for m in a_ab]
        pw = [_dot(m, m) for m in a_ab]
        for _ in range(4):
            prod = [_dot(p, jnp.concatenate([p, t], axis=1)) for p, t in zip(pw, inv)]
            pw = [m[:, 0:128] for m in prod]
            inv = [t + m[:, 128:256] for t, m in zip(inv, prod)]
        inv = [t + _dot(p, t) for p, t in zip(pw, inv)]
        vs = [stack(v_c[u], pr) for u, pr in lanes]
        akv = [_dot(m, x) for m, x in zip(a_ak, vs)]
        wu = [_dot(inv[i], jnp.concatenate([stack(a_0[u], pr), akv[i]], axis=1)) for i, (u, pr) in enumerate(lanes)]
        w1 = [m[:, 0:LANES] for m in wu]
        u0v = [jnp.concatenate([m[:, LANES:2 * LANES], x], axis=0) for m, x in zip(wu, vs)]
        zero = jnp.zeros((2 * C, LANES), F32)
        xo = [_dot(jnp.concatenate([a_rb[i], a_rk[i]], axis=1),
                   jnp.concatenate([jnp.concatenate([w1[i], zero], axis=0), u0v[i]], axis=1))
              for i in range(len(lanes))]
        fold = lambda m: m[0:C, :] + m[C:2 * C, :]
        w2 = [r_0[u][:, pr * LANES:(pr + 1) * LANES] + fold(xo[i][:, 0:LANES]) for i, (u, pr) in enumerate(lanes)]
        y0 = [fold(m[:, LANES:2 * LANES]) for m in xo]
        bh = [stack(b_h[u], pr) for u, pr in lanes]
        m_st = [_dot_tn(w1[i], bh[i]) for i in range(len(lanes))]
        n_st = [_dot_tn(u0v[i], jnp.concatenate([bh[i], stack(k_h[u], pr)], axis=0))
                for i, (u, pr) in enumerate(lanes)]

        st = [st_scr[pr] for pr in pairs]
        for u in range(RWKV_GROUP):
            for pr in pairs:
                i = u * len(pairs) + pr
                cols = slice(pr * LANES, (pr + 1) * LANES)
                y_scr[rows[u], cols] = _dot_nt(w2[i], st[pr]) + y0[i]
                st[pr] = st[pr] * e_last[u][:, cols] + _dot(st[pr], m_st[i]) + n_st[i]
        for pr in pairs:
            st_scr[pr] = st[pr]
        return carry

    _loop(tb // (C * RWKV_GROUP), group)

    y = y_scr[...]
    mean = seg_sum(y, 2) * (1.0 / RWKV_N)
    d = y - mean
    var = seg_sum(d * d, 1) * (1.0 / RWKV_N)
    y = d * lax.rsqrt(var + RWKV_LN_EPS) * lng_ref[...] + lnb_ref[...] + bonus_scr[...]
    o_ref[...] = (y * _silu(g_scr[...])).astype(o_ref.dtype)


def _rwkv(h, w, vecs, w2p, a2p, seg, layer, batch, seq, interpret):
    tb = min(TOKEN_TILE, seq)
    nt = seq // tb
    kern = functools.partial(_rwkv_kernel, tb=tb)
    mu, w0, a0, kk, ka, rk, lng, lnb = vecs
    ins = [mu, w0, w2p, a0, a2p, kk, ka, rk, lng, lnb]
    wide = lambda: pltpu.VMEM((tb, WIDTH), F32)
    n_rows = _IN_OFFS[14] - _IN_OFFS[12]
    return pl.pallas_call(
        kern,
        grid=(batch, nt),
        in_specs=[pl.BlockSpec((tb, D_MODEL), lambda b, i: (b * nt + i, 0)),
                  _w_rows_spec(w, layer, _IN_OFFS[12], n_rows)]
        + [_layer_spec(a, layer) for a in ins] + [_full_spec(seg.shape)],
        out_specs=pl.BlockSpec((tb, WIDTH), lambda b, i: (b * nt + i, 0)),
        out_shape=jax.ShapeDtypeStruct((batch * seq, WIDTH), MXU_DTYPE),
        scratch_shapes=[pltpu.VMEM((tb + CARRY_ROWS, 3 * WIDTH + 2 * RWKV_RANK), F32)]
        + [wide() for _ in range(9)]
        + [pltpu.VMEM((RWKV_HEADS // 2, LANES, LANES), F32), pltpu.VMEM((n_rows, D_MODEL), MXU_DTYPE)],
        compiler_params=_params(("arbitrary", "arbitrary")),
        interpret=interpret,
        name="rwkv7",
    )(h, w, *ins, seg)


def _memkv_kernel(mem_ref, g_ref, w_ref, k_ref, v_ref):
    x = mem_ref[0]
    y = x * lax.rsqrt(jnp.mean(x * x, axis=-1, keepdims=True) + NORM_EPS) * g_ref[...]
    kv = _dot(y, w_ref[...])
    head = _iota((1, MEM_WIDTH), 1) >> 6
    for hd in range(MEM_HEADS):
        k_ref[0, hd] = jnp.where(head == hd, kv[:, 0:MEM_WIDTH], 0.0).astype(k_ref.dtype)
        v_ref[0, hd] = jnp.where(head == hd, kv[:, MEM_WIDTH:2 * MEM_WIDTH], 0.0).astype(v_ref.dtype)


def _memkv(mem, g, w, layer, interpret):
    b, m, d = mem.shape
    out = jax.ShapeDtypeStruct((b, MEM_HEADS, m, MEM_WIDTH), MXU_DTYPE)
    return pl.pallas_call(
        _memkv_kernel,
        grid=(b,),
        in_specs=[pl.BlockSpec((1, m, d), lambda i: (i, 0, 0)), _layer_spec(g, layer), _layer_spec(w, layer)],
        out_specs=[pl.BlockSpec((1, MEM_HEADS, m, MEM_WIDTH), lambda i: (i, 0, 0, 0))] * 2,
        out_shape=[out, out],
        compiler_params=_params(("parallel",)),
        interpret=interpret,
        name="mem_kv",
    )(mem, g, w)


def _merge_kernel(x_ref, h_ref, oret_ref, ogla_ref, ossd_ref, orwkv_ref, km_ref, vm_ref,
                  wq_ref, wg_ref, uret_ref, ugla_ref, ussd_ref, urwkv_ref, umem_ref, wout_ref, gn_ref,
                  *refs):
    out_refs, wqb_scr = refs[:-1], refs[-1]

    @pl.when(_first_step())
    def _():
        wqb_scr[...] = wq_ref[0].astype(wqb_scr.dtype)

    h = h_ref[...]
    q = _proj(h, wqb_scr[...]) * MEM_HEAD_DIM ** -0.5
    o_mem = jnp.zeros(q.shape, F32)
    for hd in range(MEM_HEADS):
        s = _dot_nt(q, km_ref[0, hd])
        s = jnp.exp(s - jnp.max(s, axis=-1, keepdims=True))
        prob = s / jnp.sum(s, axis=-1, keepdims=True)
        o_mem = o_mem + _dot(prob, vm_ref[0, hd])
    branches = ((oret_ref, uret_ref), (ogla_ref, ugla_ref), (ossd_ref, ussd_ref), (orwkv_ref, urwkv_ref))
    merged = None
    for i in range(N_BRANCHES):
        gate = jax.nn.sigmoid(_proj(h, wg_ref[i * D_MODEL:(i + 1) * D_MODEL, :]))
        if i < 4:
            o_ref, u_ref = branches[i]
            up = jnp.dot(o_ref[...], u_ref[...], preferred_element_type=F32)
        else:
            up = _dot(o_mem, umem_ref[...])
        merged = gate * up if merged is None else merged + gate * up
    x = x_ref[...] + _dot(merged, wout_ref[...])
    if len(out_refs) == 2:
        out_refs[0][...] = x
    y = x * lax.rsqrt(jnp.mean(x * x, axis=-1, keepdims=True) + NORM_EPS) * gn_ref[...]
    out_refs[-1][...] = y.astype(out_refs[-1].dtype)


def _merge(x2d, h, o_ret, o_gla, o_ssd, o_rwkv, km, vm, wq, wg, ups, wout, g_next, layer, last,
           batch, seq, interpret):
    tm = min(TOKEN_TILE, seq)
    nt = seq // tm
    row = lambda w: pl.BlockSpec((tm, w), lambda b, i: (b * nt + i, 0))
    kvspec = pl.BlockSpec((1,) + km.shape[1:], lambda b, i: (b, 0, 0, 0))
    weights = [wg, *ups, wout, g_next]
    return pl.pallas_call(
        _merge_kernel,
        grid=(batch, nt),
        in_specs=[row(D_MODEL), row(D_MODEL), row(WIDTH), row(WIDTH), row(WIDTH), row(WIDTH), kvspec, kvspec,
                  _w_rows_spec(wq, layer, _IN_OFFS[14], MEM_WIDTH)]
        + [_layer_spec(w, layer, pipeline_mode=pl.Buffered(1)) for w in weights],
        out_specs=[row(D_MODEL)] if last else [row(D_MODEL), row(D_MODEL)],
        out_shape=[jax.ShapeDtypeStruct(x2d.shape, F32)] if last else
        [jax.ShapeDtypeStruct(x2d.shape, F32), jax.ShapeDtypeStruct(x2d.shape, MXU_DTYPE)],
        scratch_shapes=[pltpu.VMEM((MEM_WIDTH, D_MODEL), MXU_DTYPE)],
        compiler_params=_params(("arbitrary", "arbitrary")),
        interpret=interpret,
        name="merge",
    )(x2d, h, o_ret, o_gla, o_ssd, o_rwkv, km, vm, wq, *weights)


def _pad_last(a, width):
    return jnp.pad(a, [(0, 0)] * (a.ndim - 1) + [(0, width - a.shape[-1])])


def _rows3(v, width=None):
    v = v.reshape(v.shape[0], 1, -1).astype(F32)
    return v if width is None else _pad_last(v, width)


def _forward(x, mem, positions, norm_g, w_in, gla_gk_w2, gla_gk_b, gla_norm_g,
             ssd_conv_w, ssd_conv_b, ssd_dt_bias, ssd_a_log, ssd_d, ssd_norm_g,
             rwkv_mu, rwkv_w0, rwkv_w2, rwkv_a0, rwkv_a2, rwkv_k_k, rwkv_k_a, rwkv_r_k,
             rwkv_ln_g, rwkv_ln_b, mem_norm_g, w_mem_kv,
             w_up_ret, w_up_gla, w_up_ssd, w_up_rwkv, w_up_mem, w_out, final_norm_g, interpret=False):
    batch, seq, d = x.shape
    depth = w_in.shape[0]
    cdt = MXU_DTYPE
    o = _IN_OFFS

    half = np.arange(RET_DK // 2)
    ret_perm = np.concatenate([hd * RET_DK + 2 * half + par for par in (0, 1) for hd in range(RET_HEADS)])
    inv = 1.0 / (ROPE_BASE ** jnp.linspace(0.0, 1.0, RET_DK // 2, dtype=F32))
    inv_row = jnp.tile(inv, RET_HEADS).reshape(1, LANES)
    pos_col = positions.reshape(batch * seq, 1)
    head_of_lane = np.arange(WIDTH) // SSD_P
    ssd_expand = jnp.asarray(np.arange(LANES)[:, None] == head_of_lane[None, :], F32)
    rwkv_seg = jnp.asarray(head_of_lane[:256, None] == head_of_lane[None, :256], F32)

    w_t = jnp.swapaxes(w_in, 1, 2)
    w_g = w_t[:, o[15]:o[16], :].astype(cdt)
    ret_perm_rows = jnp.asarray(ret_perm[:, None] == np.arange(RET_HEADS * RET_DK)[None, :], cdt)
    gla_w2p = jnp.pad(gla_gk_w2, ((0, 0), (0, LANES - GLA_RANK), (0, 0))).astype(cdt)
    zeros_rank = jnp.zeros((depth, RWKV_RANK, WIDTH), F32)
    rwkv_w2p = jnp.concatenate([rwkv_w2, zeros_rank], axis=1).astype(cdt)
    rwkv_a2p = jnp.concatenate([zeros_rank, rwkv_a2], axis=1).astype(cdt)
    rwkv_vecs = [_rows3(v) for v in (rwkv_mu, rwkv_w0, rwkv_a0, rwkv_k_k, rwkv_k_a, rwkv_r_k,
                                     rwkv_ln_g, rwkv_ln_b)]
    ssd_small = [ssd_conv_w.astype(F32), _rows3(ssd_conv_b), _rows3(ssd_dt_bias, LANES),
                 _rows3(-jnp.exp(ssd_a_log.astype(F32)), LANES), _rows3(jnp.repeat(ssd_d, SSD_P, axis=1)),
                 _rows3(ssd_norm_g)]
    gla_b, gla_ng = _rows3(gla_gk_b), _rows3(gla_norm_g)
    mem_g, w_kv = _rows3(mem_norm_g), w_mem_kv.astype(cdt)
    ups = [w.astype(cdt) for w in (w_up_ret, w_up_gla, w_up_ssd, w_up_rwkv, w_up_mem)]
    w_o = w_out.astype(cdt)
    g_next = _rows3(jnp.concatenate([norm_g[1:], final_norm_g[None]], axis=0))

    x2d = x.reshape(batch * seq, d)
    h = _rmsnorm(x2d, norm_g[0], cdt, interpret)
    for l in range(depth):
        o_ret = _retention(h, pos_col, inv_row, w_t, ret_perm_rows, l, batch, seq, interpret)
        o_gla = _gla(h, w_t, gla_w2p, gla_b, gla_ng, l, batch, seq, interpret)
        o_ssd = _ssd(h, w_t, *ssd_small, ssd_expand, l, batch, seq, interpret)
        o_rwkv = _rwkv(h, w_t, rwkv_vecs, rwkv_w2p, rwkv_a2p, rwkv_seg, l, batch, seq, interpret)
        km, vm = _memkv(mem, mem_g, w_kv, l, interpret)
        last = l == depth - 1
        outs = _merge(x2d, h, o_ret, o_gla, o_ssd, o_rwkv, km, vm, w_t, w_g, ups, w_o, g_next, l, last,
                      batch, seq, interpret)
        x2d, h = (None, outs[0]) if last else outs
    return h.reshape(batch, seq, d)


def kernel(x, mem, positions, norm_g, w_in, gla_gk_w2, gla_gk_b, gla_norm_g, ssd_conv_w, ssd_conv_b, ssd_dt_bias, ssd_a_log, ssd_d, ssd_norm_g, rwkv_mu, rwkv_w0, rwkv_w2, rwkv_a0, rwkv_a2, rwkv_k_k, rwkv_k_a, rwkv_r_k, rwkv_ln_g, rwkv_ln_b, mem_norm_g, w_mem_kv, w_up_ret, w_up_gla, w_up_ssd, w_up_rwkv, w_up_mem, w_out, final_norm_g):
    return _forward(x, mem, positions, norm_g, w_in, gla_gk_w2, gla_gk_b, gla_norm_g,
                    ssd_conv_w, ssd_conv_b, ssd_dt_bias, ssd_a_log, ssd_d, ssd_norm_g,
                    rwkv_mu, rwkv_w0, rwkv_w2, rwkv_a0, rwkv_a2, rwkv_k_k, rwkv_k_a, rwkv_r_k,
                    rwkv_ln_g, rwkv_ln_b, mem_norm_g, w_mem_kv,
                    w_up_ret, w_up_gla, w_up_ssd, w_up_rwkv, w_up_mem, w_out, final_norm_g)
```

```python
import functools
import math

import jax
import jax.numpy as jnp
import numpy as np
from jax import lax
from jax.experimental import pallas as pl
from jax.experimental.pallas import tpu as pltpu

F32 = jnp.float32
BF16 = jnp.bfloat16
MXU_DTYPE = jnp.bfloat16

D_MODEL = 1024
WIDTH = 512
NORM_EPS = 1e-6
N_BRANCHES = 5

RET_HEADS, RET_DK, RET_DV, RET_CHUNK = 4, 64, 128, 128
ROPE_BASE = 10000.0
GLA_HEADS, GLA_DK, GLA_DV, GLA_RANK, GLA_NORMALIZER, GLA_CHUNK = 4, 64, 128, 16, 16.0, 64
SSD_HEADS, SSD_P, SSD_GROUPS, SSD_STATE, SSD_CONV, SSD_CHUNK = 8, 64, 2, 128, 4, 128
RWKV_HEADS, RWKV_N, RWKV_RANK, RWKV_CHUNK = 8, 64, 64, 64
RWKV_LN_EPS = 64e-5
RET_GROUP = 4
SSD_GROUP = 4
GLA_GROUP = 8
MEM_HEADS, MEM_HEAD_DIM, MEM_WIDTH = 4, 64, 256

LANES = 128
CARRY_ROWS = 8
TOKEN_TILE = 512
MIXER_TILE = 512
VMEM_LIMIT = 56 * 1024 * 1024

_IN_SIZES = (256, 256, 512, 512, 256, 256, 512, 16, 512, 1024, 8, 512, 1664, 512, 256, 5120)
_IN_OFFS = tuple(int(v) for v in np.cumsum((0,) + _IN_SIZES))


def _dot(a, b):
    return jnp.dot(a.astype(MXU_DTYPE), b.astype(MXU_DTYPE), preferred_element_type=F32)


def _dot_nt(a, b):
    return lax.dot_general(a.astype(MXU_DTYPE), b.astype(MXU_DTYPE), (((1,), (1,)), ((), ())),
                           preferred_element_type=F32)


def _dot_tn(a, b):
    return lax.dot_general(a.astype(MXU_DTYPE), b.astype(MXU_DTYPE), (((0,), (0,)), ((), ())),
                           preferred_element_type=F32)


def _proj(x, w_t):
    return lax.dot_general(x, w_t, (((1,), (1,)), ((), ())), preferred_element_type=F32)


def _split3(x):
    hi = x.astype(BF16)
    r1 = x - hi.astype(F32)
    mid = r1.astype(BF16)
    lo = (r1 - mid.astype(F32)).astype(BF16)
    return hi, mid, lo


def _sel_dot(sel, x):
    s = sel.astype(BF16)
    return sum(jnp.dot(s, p, preferred_element_type=F32) for p in _split3(x))


def _dot_sel(x, sel, pieces=3):
    s = sel.astype(BF16)
    return sum(jnp.dot(p, s, preferred_element_type=F32) for p in _split3(x)[:pieces])


def _iota(shape, dim):
    return lax.broadcasted_iota(jnp.int32, shape, dim)


def _tril(n, strict=False):
    r, c = _iota((n, n), 0), _iota((n, n), 1)
    return (r > c) if strict else (r >= c)


def _silu(x):
    return x * jax.nn.sigmoid(x)


def _softplus(x):
    return jnp.maximum(x, 0.0) + jnp.log1p(jnp.exp(-jnp.abs(x)))


def _rows(c, n):
    return pl.ds(pl.multiple_of(c * n, n), n)


def _loop(trips, body):
    if trips == 1:
        body(0, 0)
    else:
        lax.fori_loop(0, trips, body, 0)


def _full_spec(shape):
    zeros = (0,) * len(shape)
    return pl.BlockSpec(shape, lambda *_: zeros)


def _layer_spec(arr, layer, **kwargs):
    tail = tuple(arr.shape[1:])
    index = (layer,) + (0,) * len(tail)
    return pl.BlockSpec((None,) + tail, lambda *_: index, **kwargs)


def _w_rows_spec(w_t, layer, row0, rows):
    return pl.BlockSpec((pl.Element(1), pl.Element(rows), pl.Element(w_t.shape[2])),
                        lambda *_: (layer, row0, 0), pipeline_mode=pl.Buffered(1))


def _first_step():
    return (pl.program_id(0) == 0) & (pl.program_id(1) == 0)


def _params(semantics):
    return pltpu.CompilerParams(dimension_semantics=semantics, vmem_limit_bytes=VMEM_LIMIT)


def _rmsnorm_kernel(x_ref, g_ref, o_ref):
    x = x_ref[...]
    y = x * lax.rsqrt(jnp.mean(x * x, axis=-1, keepdims=True) + NORM_EPS)
    o_ref[...] = (y * g_ref[...]).astype(o_ref.dtype)


def _rmsnorm(x2d, g, out_dtype, interpret):
    m, d = x2d.shape
    tm = min(1024, m)
    return pl.pallas_call(
        _rmsnorm_kernel,
        grid=(m // tm,),
        in_specs=[pl.BlockSpec((tm, d), lambda i: (i, 0)), _full_spec((1, d))],
        out_specs=pl.BlockSpec((tm, d), lambda i: (i, 0)),
        out_shape=jax.ShapeDtypeStruct((m, d), out_dtype),
        compiler_params=_params(("parallel",)),
        interpret=interpret,
        name="rmsnorm",
    )(x2d, g.reshape(1, d))


def _ret_kernel(h_ref, pos_ref, inv_ref, w_ref, perm_ref, o_ref, p_scr, s_scr, wb_scr, *, tb):
    C = RET_CHUNK

    @pl.when(_first_step())
    def _():
        for lo in (0, 256):
            rows = w_ref[0, lo:lo + 256, :].astype(wb_scr.dtype)
            wb_scr[lo:lo + 256, :] = jnp.dot(perm_ref[...], rows, preferred_element_type=F32).astype(wb_scr.dtype)
        wb_scr[512:1536, :] = w_ref[0, 512:1536, :].astype(wb_scr.dtype)

    @pl.when(pl.program_id(1) == 0)
    def _():
        s_scr[...] = jnp.zeros_like(s_scr)

    ang = pos_ref[...].astype(F32) * inv_ref[...]
    cos, sin = jnp.cos(ang), jnp.sin(ang)
    p_scr[:, 512:1536] = _proj(h_ref[...], wb_scr[512:1536, :])
    p = _proj(h_ref[...], wb_scr[0:512, :])
    q1, q2 = p[:, 0:128], p[:, 128:256]
    k1, k2 = p[:, 256:384] * RET_DK ** -0.5, p[:, 384:512] * RET_DK ** -0.5
    p_scr[:, 0:128] = q1 * cos - q2 * sin
    p_scr[:, 128:256] = q2 * cos + q1 * sin
    p_scr[:, 256:384] = k1 * cos - k2 * sin
    p_scr[:, 384:512] = k2 * cos + k1 * sin

    def log_gamma(head):
        return jnp.log(1.0 - jnp.exp2(-5.0 - head.astype(F32)))

    qk_head = (_iota((1, 256), 1) >> 5) & 3
    lg_lane = log_gamma(qk_head)
    tau = _iota((C, 1), 0).astype(F32)
    dq = jnp.exp(lg_lane * (tau + 1.0))
    dk = jnp.exp(lg_lane * (C - 1.0 - tau))
    ds = jnp.exp(lg_lane * float(C))
    diff = (_iota((C, C), 0) - _iota((C, C), 1)).astype(F32)
    causal = _tril(C)
    bd_mask = (_iota((WIDTH, 256), 0) >> 7) == ((_iota((WIDTH, 256), 1) >> 5) & 3)

    heads = range(RET_HEADS)
    hcols = [slice(hd * RET_DV, (hd + 1) * RET_DV) for hd in heads]
    seg = [jnp.where(causal, jnp.exp(math.log(1.0 - 2.0 ** (-5.0 - hd)) * diff), 0.0) for hd in heads]

    def group(gi, carry):
        us = range(RET_GROUP)
        rows = [_rows(gi * RET_GROUP + u, C) for u in us]
        q = [p_scr[rw, 0:256] for rw in rows]
        k = [p_scr[rw, 256:512] for rw in rows]
        v = [p_scr[rw, 512:1024] for rw in rows]
        sc = [[_dot_nt(q[u], jnp.where(qk_head == hd, k[u], 0.0)) * seg[hd] for hd in heads] for u in us]
        y_intra = [[_dot(sc[u][hd], v[u][:, hcols[hd]]) for hd in heads] for u in us]
        kv = [jnp.where(bd_mask, _dot_tn(v[u], k[u] * dk), 0.0) for u in us]
        st = s_scr[...]
        y_inter = []
        for u in us:
            y_inter.append(_dot_nt(q[u] * dq, st))
            st = st * ds + kv[u]
        s_scr[...] = st
        for u in us:
            for hd in heads:
                y = y_intra[u][hd] + y_inter[u][:, hcols[hd]]
                y = y * lax.rsqrt(jnp.mean(y * y, axis=-1, keepdims=True) + NORM_EPS)
                g = p_scr[rows[u], 1024 + hd * RET_DV:1024 + (hd + 1) * RET_DV]
                o_ref[rows[u], hcols[hd]] = (y * _silu(g)).astype(o_ref.dtype)
        return carry

    _loop(tb // (C * RET_GROUP), group)


def _retention(h, pos_col, inv_row, w, perm, layer, batch, seq, interpret):
    tb = min(MIXER_TILE, seq)
    nt = seq // tb
    kern = functools.partial(_ret_kernel, tb=tb)
    return pl.pallas_call(
        kern,
        grid=(batch, nt),
        in_specs=[pl.BlockSpec((tb, D_MODEL), lambda b, i: (b * nt + i, 0)),
                  pl.BlockSpec((tb, 1), lambda b, i: (b * nt + i, 0)),
                  _full_spec((1, LANES)),
                  _w_rows_spec(w, layer, _IN_OFFS[0], 1536), _full_spec(perm.shape)],
        out_specs=pl.BlockSpec((tb, WIDTH), lambda b, i: (b * nt + i, 0)),
        out_shape=jax.ShapeDtypeStruct((batch * seq, WIDTH), MXU_DTYPE),
        scratch_shapes=[pltpu.VMEM((tb, 1536), F32), pltpu.VMEM((WIDTH, 256), F32),
                        pltpu.VMEM((1536, D_MODEL), MXU_DTYPE)],
        compiler_params=_params(("arbitrary", "arbitrary")),
        interpret=interpret,
        name="retention",
    )(h, pos_col, inv_row, w, perm)


def _gla_kernel(h_ref, w_ref, w2_ref, gb_ref, ng_ref, o_ref, p_scr, lg_scr, s_scr, wb_scr, *, tb):
    C = GLA_CHUNK

    @pl.when(_first_step())
    def _():
        cast = lambda v: v.astype(wb_scr.dtype)
        wb_scr[0:1024, :] = cast(w_ref[0, 0:1024, :])
        wb_scr[1024:1536, :] = cast(w_ref[0, 1024 + GLA_RANK:1536 + GLA_RANK, :])
        wb_scr[1536:1664, :] = cast(jnp.concatenate(
            [w_ref[0, 1024:1024 + GLA_RANK, :], jnp.zeros((LANES - GLA_RANK, D_MODEL), F32)], axis=0))

    @pl.when(pl.program_id(1) == 0)
    def _():
        s_scr[...] = jnp.zeros_like(s_scr)

    pre = _dot(_proj(h_ref[...], wb_scr[1536:1664, :]), w2_ref[...]) + gb_ref[...]
    lg_scr[...] = -_softplus(-pre) / GLA_NORMALIZER
    tri = _tril(C)
    assert tb == C * GLA_GROUP
    cum_all = [_sel_dot(tri, lg_scr[u * C:(u + 1) * C, :]) for u in range(GLA_GROUP)]
    p_scr[:, 0:512] = _proj(h_ref[...], wb_scr[0:512, :])
    p_scr[:, 512:1536] = _proj(h_ref[...], wb_scr[512:1536, :])

    k_head = _iota((1, 256), 1) >> 6
    bd_mask = (_iota((WIDTH, 256), 0) >> 7) == (_iota((WIDTH, 256), 1) >> 6)
    ng = ng_ref[...]

    heads = range(GLA_HEADS)
    hcols = [slice(hd * GLA_DV, (hd + 1) * GLA_DV) for hd in heads]

    def group(gi, carry):
        us = range(GLA_GROUP)
        rows = [_rows(gi * GLA_GROUP + u, C) for u in us]
        q = [p_scr[rw, 0:256] * GLA_DK ** -0.5 for rw in rows]
        k = [p_scr[rw, 256:512] for rw in rows]
        v = [p_scr[rw, 512:1024] for rw in rows]
        cum = cum_all
        ref = [c[C // 2:C // 2 + 1, :] for c in cum]
        last = [c[C - 1:C, :] for c in cum]
        q_in = [q[u] * jnp.exp(cum[u] - ref[u]) for u in us]
        k_in = [k[u] * jnp.exp(ref[u] - cum[u]) for u in us]
        q_dec = [q[u] * jnp.exp(cum[u]) for u in us]
        k_st = [k[u] * jnp.exp(last[u] - cum[u]) for u in us]
        sc = [[jnp.where(tri, _dot_nt(q_in[u], jnp.where(k_head == hd, k_in[u], 0.0)), 0.0) for hd in heads]
              for u in us]
        y_intra = [[_dot(sc[u][hd], v[u][:, hcols[hd]]) for hd in heads] for u in us]
        kv = [jnp.where(bd_mask, _dot_tn(v[u], k_st[u]), 0.0) for u in us]
        st = s_scr[...]
        y_inter = []
        for u in us:
            y_inter.append(_dot_nt(q_dec[u], st))
            st = st * jnp.exp(last[u]) + kv[u]
        s_scr[...] = st
        for u in us:
            for hd in heads:
                y = y_intra[u][hd] + y_inter[u][:, hcols[hd]]
                y = y * lax.rsqrt(jnp.mean(y * y, axis=-1, keepdims=True) + NORM_EPS) * ng
                g = p_scr[rows[u], 1024 + hd * GLA_DV:1024 + (hd + 1) * GLA_DV]
                o_ref[rows[u], hcols[hd]] = (y * _silu(g)).astype(o_ref.dtype)
        return carry

    _loop(tb // (C * GLA_GROUP), group)


def _gla(h, w, w2p, gb, ng, layer, batch, seq, interpret):
    tb = min(MIXER_TILE, seq)
    nt = seq // tb
    kern = functools.partial(_gla_kernel, tb=tb)
    return pl.pallas_call(
        kern,
        grid=(batch, nt),
        in_specs=[pl.BlockSpec((tb, D_MODEL), lambda b, i: (b * nt + i, 0)),
                  _w_rows_spec(w, layer, _IN_OFFS[4], 1536 + GLA_RANK), _layer_spec(w2p, layer),
                  _layer_spec(gb, layer), _layer_spec(ng, layer)],
        out_specs=pl.BlockSpec((tb, WIDTH), lambda b, i: (b * nt + i, 0)),
        out_shape=jax.ShapeDtypeStruct((batch * seq, WIDTH), MXU_DTYPE),
        scratch_shapes=[pltpu.VMEM((tb, 1536), F32), pltpu.VMEM((tb, 256), F32),
                        pltpu.VMEM((WIDTH, 256), F32), pltpu.VMEM((1664, D_MODEL), MXU_DTYPE)],
        compiler_params=_params(("arbitrary", "arbitrary")),
        interpret=interpret,
        name="gla",
    )(h, w, w2p, gb, ng)


def _ssd_kernel(h_ref, w_ref, cw_ref, cb_ref, dtb_ref, a_ref, dskip_ref, ng_ref, exp_ref, o_ref,
                raw_scr, xc_scr, z_scr, dt_scr, s_scr, wb_scr, *, tb):
    C = SSD_CHUNK
    NCH = 1024

    @pl.when(_first_step())
    def _():
        cast = lambda v: v.astype(wb_scr.dtype)
        wb_scr[0:NCH, :] = cast(w_ref[0, 0:NCH, :])
        wb_scr[NCH:NCH + WIDTH, :] = cast(w_ref[0, NCH + SSD_HEADS:NCH + SSD_HEADS + WIDTH, :])
        wb_scr[1536:1664, :] = cast(jnp.concatenate(
            [w_ref[0, NCH:NCH + SSD_HEADS, :], jnp.zeros((LANES - SSD_HEADS, D_MODEL), F32)], axis=0))

    @pl.when(pl.program_id(1) == 0)
    def _():
        s_scr[...] = jnp.zeros_like(s_scr)
        raw_scr[0:CARRY_ROWS, :] = jnp.zeros((CARRY_ROWS, NCH), F32)

    dt_scr[...] = _softplus(_proj(h_ref[...], wb_scr[1536:1664, :]) + dtb_ref[...])
    first_tap = CARRY_ROWS - (SSD_CONV - 1)
    for lo in range(0, NCH, 256):
        cols = slice(lo, lo + 256)
        raw_scr[CARRY_ROWS:CARRY_ROWS + tb, cols] = _proj(h_ref[...], wb_scr[cols, :])
        conv = cb_ref[:, cols] + sum(raw_scr[first_tap + j:first_tap + j + tb, cols] * cw_ref[j:j + 1, cols]
                                     for j in range(SSD_CONV))
        xc_scr[:, cols] = _silu(conv)
        raw_scr[0:CARRY_ROWS, cols] = raw_scr[tb:tb + CARRY_ROWS, cols]
    z_scr[...] = _proj(h_ref[...], wb_scr[NCH:NCH + WIDTH, :])

    tri = _tril(C)
    lane_lo = _iota((1, LANES), 1) < SSD_P
    expand = exp_ref[...]
    a_row = a_ref[...]

    groups = range(SSD_GROUPS)
    gcols = [slice(g * 256, (g + 1) * 256) for g in groups]
    ncols = [slice(g * SSD_STATE, (g + 1) * SSD_STATE) for g in groups]

    def group(gi, carry):
        us = range(SSD_GROUP)
        rows = [_rows(gi * SSD_GROUP + u, C) for u in us]
        xs = [xc_scr[rw, 0:512] for rw in rows]
        bm = [xc_scr[rw, 512:768] for rw in rows]
        cm = [xc_scr[rw, 768:1024] for rw in rows]
        dt = [dt_scr[rw, :] for rw in rows]
        cum = [_sel_dot(tri, dt[u] * a_row) for u in us]
        cum_t = [c.T for c in cum]
        dt_e = [_dot_sel(d, expand) for d in dt]
        cum_e = [_dot_sel(c, expand) for c in cum]
        last_e = [c[C - 1:C, :] for c in cum_e]
        xdt = [xs[u] * dt_e[u] for u in us]
        v_st = [xdt[u] * jnp.exp(last_e[u] - cum_e[u]) for u in us]
        e_cum = [jnp.exp(c) for c in cum_e]
        scores = [[_dot_nt(cm[u][:, ncols[g]], bm[u][:, ncols[g]]) for g in groups] for u in us]
        y_intra = []
        for u in us:
            parts = []
            for pr in range(SSD_HEADS // 2):
                sc = scores[u][pr // 2]
                segs = []
                for hd in (2 * pr, 2 * pr + 1):
                    d = jnp.minimum(cum[u][:, hd:hd + 1] - cum_t[u][hd:hd + 1, :], 0.0)
                    segs.append(sc * jnp.where(tri, jnp.exp(d), 0.0))
                xp = xdt[u][:, pr * LANES:(pr + 1) * LANES]
                rhs = jnp.concatenate([jnp.where(lane_lo, xp, 0.0), jnp.where(lane_lo, 0.0, xp)], axis=0)
                parts.append(_dot(jnp.concatenate(segs, axis=1), rhs))
            y_intra.append(jnp.concatenate(parts, axis=1))
        kv = [[_dot_tn(bm[u][:, ncols[g]], v_st[u][:, gcols[g]]) for g in groups] for u in us]
        st = [s_scr[g] for g in groups]
        y_inter = []
        for u in us:
            y_inter.append(jnp.concatenate(
                [_dot(cm[u][:, ncols[g]], st[g]) * e_cum[u][:, gcols[g]] for g in groups], axis=1))
            st = [st[g] * jnp.exp(last_e[u][:, gcols[g]]) + kv[u][g] for g in groups]
        for g in groups:
            s_scr[g] = st[g]
        for u in us:
            y = y_intra[u] + y_inter[u] + dskip_ref[...] * xs[u]
            y = y * _silu(z_scr[rows[u], :])
            for g in groups:
                yg = y[:, gcols[g]]
                yg = yg * lax.rsqrt(jnp.mean(yg * yg, axis=-1, keepdims=True) + NORM_EPS)
                o_ref[rows[u], gcols[g]] = (yg * ng_ref[:, gcols[g]]).astype(o_ref.dtype)
        return carry

    _loop(tb // (C * SSD_GROUP), group)


def _ssd(h, w, cw, cb, dtb, a_row, dskip, ng, expand, layer, batch, seq, interpret):
    tb = min(MIXER_TILE, seq)
    nt = seq // tb
    kern = functools.partial(_ssd_kernel, tb=tb)
    small = [cw, cb, dtb, a_row, dskip, ng]
    return pl.pallas_call(
        kern,
        grid=(batch, nt),
        in_specs=[pl.BlockSpec((tb, D_MODEL), lambda b, i: (b * nt + i, 0)),
                  _w_rows_spec(w, layer, _IN_OFFS[9], 1536 + SSD_HEADS)]
        + [_layer_spec(a, layer) for a in small] + [_full_spec(expand.shape)],
        out_specs=pl.BlockSpec((tb, WIDTH), lambda b, i: (b * nt + i, 0)),
        out_shape=jax.ShapeDtypeStruct((batch * seq, WIDTH), MXU_DTYPE),
        scratch_shapes=[pltpu.VMEM((tb + CARRY_ROWS, 1024), F32), pltpu.VMEM((tb, 1024), F32),
                        pltpu.VMEM((tb, WIDTH), F32), pltpu.VMEM((tb, LANES), F32),
                        pltpu.VMEM((SSD_GROUPS, SSD_STATE, 256), F32), pltpu.VMEM((1664, D_MODEL), MXU_DTYPE)],
        compiler_params=_params(("arbitrary", "arbitrary")),
        interpret=interpret,
        name="ssd",
    )(h, w, *small, expand)


def _rwkv_kernel(h_ref, w_ref, mu_ref, w0_ref, w2_ref, a0_ref, a2_ref, kk_ref, ka_ref, rk_ref,
                 lng_ref, lnb_ref, seg_ref, o_ref,
                 u_scr, g_scr, r_scr, lw_scr, k_scr, v_scr, a_scr, b_scr, y_scr, bonus_scr, st_scr, wb_scr,
                 *, tb):
    C = RWKV_CHUNK
    NU = 3 * WIDTH + 2 * RWKV_RANK

    @pl.when(_first_step())
    def _():
        wb_scr[...] = w_ref[0].astype(wb_scr.dtype)

    @pl.when(pl.program_id(1) == 0)
    def _():
        st_scr[...] = jnp.zeros_like(st_scr)
        u_scr[0:CARRY_ROWS, :] = jnp.zeros((CARRY_ROWS, NU), F32)

    seg = seg_ref[...]

    def seg_sum(x, pieces):
        return jnp.concatenate([_dot_sel(x[:, j * 256:(j + 1) * 256], seg, pieces) for j in range(2)], axis=1)

    def shifted(lo, hi):
        u = _proj(h_ref[...], wb_scr[lo:hi, :])
        u_scr[CARRY_ROWS:CARRY_ROWS + tb, lo:hi] = u
        u_prev = u_scr[CARRY_ROWS - 1:CARRY_ROWS - 1 + tb, lo:hi]
        u_scr[0:CARRY_ROWS, lo:hi] = u_scr[tb:tb + CARRY_ROWS, lo:hi]
        return u + (u_prev - u) * mu_ref[:, lo:hi]

    wa = shifted(1536, 1664)
    w_pre = w0_ref[...] + _dot(jnp.tanh(wa), w2_ref[...])
    a_gate = jax.nn.sigmoid(a0_ref[...] + _dot(wa, a2_ref[...]))
    lw_scr[...] = -jnp.exp(-_softplus(-w_pre) - 0.5)
    k = shifted(512, 1024)
    kk = k * kk_ref[...]
    k = k * (1.0 + (a_gate - 1.0) * ka_ref[...])
    kk = kk / jnp.maximum(jnp.sqrt(seg_sum(kk * kk, 1)), 1e-12)
    k_scr[...] = k
    a_scr[...] = -kk
    b_scr[...] = kk * a_gate
    r = shifted(0, 512)
    r_scr[...] = r
    v = shifted(1024, 1536)
    v_scr[...] = v
    bonus_scr[...] = seg_sum(r * k * rk_ref[...], 1) * v
    g_scr[...] = _proj(h_ref[...], wb_scr[NU:NU + WIDTH, :])

    tri = _tril(C)
    r2, c2 = _iota((2 * C, 2 * C), 0), _iota((2 * C, 2 * C), 1)
    same = (r2 >> 6) == (c2 >> 6)
    low_s = same & ((r2 & 63) > (c2 & 63))
    low_i = same & ((r2 & 63) >= (c2 & 63))
    eye = (r2 == c2).astype(F32)
    lane_lo = _iota((1, LANES), 1) < RWKV_N

    pairs = range(RWKV_HEADS // 2)

    def stack_pair(xp):
        return jnp.concatenate([jnp.where(lane_lo, xp, 0.0), jnp.where(lane_lo, 0.0, xp)], axis=0)

    def stack(x, pr):
        return stack_pair(x[:, pr * LANES:(pr + 1) * LANES])

    def state_free_part(chunks, tick):
        lanes = [(j, pr) for j in range(len(chunks)) for pr in pairs]
        rows, a_0, r_0, v_c, b_h, k_h, e_last, lhs, rhs_t = [], [], [], [], [], [], [], {}, {}
        for j, u in enumerate(chunks):
            rw = _rows(u, C)
            lw = lw_scr[rw, :]
            cum = _sel_dot(tri, lw)
            cum_p = cum - lw
            ref = cum[C // 2:C // 2 + 1, :]
            last = cum[C - 1:C, :]
            e_fwd = jnp.exp(cum - ref)
            e_bwd = jnp.exp(ref - cum)
            e_end = jnp.exp(last - cum)
            r_c, k_c, a_c, b_c = r_scr[rw, :], k_scr[rw, :], a_scr[rw, :], b_scr[rw, :]
            r_t = r_c * e_fwd
            a_t = a_c * jnp.exp(cum_p - ref)
            b_t = b_c * e_bwd
            k_t = k_c * e_bwd
            for pr in pairs:
                lhs[j, pr] = jnp.concatenate([stack(a_t, pr), stack(r_t, pr)], axis=0)
                rhs_t[j, pr] = jnp.concatenate([stack(b_t, pr), stack(k_t, pr)], axis=0)
            rows.append(rw)
            a_0.append(a_c * jnp.exp(cum_p))
            r_0.append(r_c * jnp.exp(cum))
            v_c.append(v_scr[rw, :])
            b_h.append(b_c * e_end)
            k_h.append(k_c * e_end)
            e_last.append(jnp.exp(last))
        tick()
        big = [_dot_nt(lhs[ln], rhs_t[ln]) for ln in lanes]
        tick()
        a_ab = [jnp.where(low_s, m[0:128, 0:128], 0.0) for m in big]
        a_ak = [jnp.where(low_s, m[0:128, 128:256], 0.0) for m in big]
        a_rb = [jnp.where(low_i, m[128:256, 0:128], 0.0) for m in big]
        a_rk = [jnp.where(low_i, m[128:256, 128:256], 0.0) for m in big]
        inv = [eye + m for m in a_ab]
        pw = [_dot(m, m) for m in a_ab]
        tick()
        for _ in range(4):
            prod = [_dot(p, jnp.concatenate([p, t], axis=1)) for p, t in zip(pw, inv)]
            pw = [m[:, 0:128] for m in prod]
            inv = [t + m[:, 128:256] for t, m in zip(inv, prod)]
            tick()
        inv = [t + _dot(p, t) for p, t in zip(pw, inv)]
        tick()
        vs = [stack(v_c[j], pr) for j, pr in lanes]
        akv = [_dot(m, x) for m, x in zip(a_ak, vs)]
        tick()
        wu = [_dot(inv[i], jnp.concatenate([stack(a_0[j], pr), akv[i]], axis=1)) for i, (j, pr) in enumerate(lanes)]
        tick()
        w1 = [m[:, 0:LANES] for m in wu]
        u0v = [jnp.concatenate([m[:, LANES:2 * LANES], x], axis=0) for m, x in zip(wu, vs)]
        zero = jnp.zeros((2 * C, LANES), F32)
        xo = [_dot(jnp.concatenate([a_rb[i], a_rk[i]], axis=1),
                   jnp.concatenate([jnp.concatenate([w1[i], zero], axis=0), u0v[i]], axis=1))
              for i in range(len(lanes))]
        tick()
        fold = lambda m: m[0:C, :] + m[C:2 * C, :]
        w2 = [r_0[j][:, pr * LANES:(pr + 1) * LANES] + fold(xo[i][:, 0:LANES]) for i, (j, pr) in enumerate(lanes)]
        y0 = [fold(m[:, LANES:2 * LANES]) for m in xo]
        bh = [stack(b_h[j], pr) for j, pr in lanes]
        m_st = [_dot_tn(w1[i], bh[i]) for i in range(len(lanes))]
        tick()
        n_st = [_dot_tn(u0v[i], jnp.concatenate([bh[i], stack(k_h[j], pr)], axis=0))
                for i, (j, pr) in enumerate(lanes)]
        tick()
        return rows, e_last, w2, y0, m_st, n_st

    st = [st_scr[pr] for pr in pairs]

    def chain_steps(part):
        rows, e_last, w2, y0, m_st, n_st = part

        def step(j):
            for pr in pairs:
                i = j * len(pairs) + pr
                cols = slice(pr * LANES, (pr + 1) * LANES)
                y_scr[rows[j], cols] = _dot_nt(w2[i], st[pr]) + y0[i]
                st[pr] = st[pr] * e_last[j][:, cols] + _dot(st[pr], m_st[i]) + n_st[i]

        return [functools.partial(step, j) for j in range(len(rows))]

    n_chunks = tb // C
    first, second = list(range(n_chunks // 2)), list(range(n_chunks // 2, n_chunks))
    queue = chain_steps(state_free_part(first, lambda: None))

    def tick():
        if queue:
            queue.pop(0)()

    second_part = state_free_part(second, tick)
    for step in queue + chain_steps(second_part):
        step()
    for pr in pairs:
        st_scr[pr] = st[pr]

    y = y_scr[...]
    mean = seg_sum(y, 2) * (1.0 / RWKV_N)
    d = y - mean
    var = seg_sum(d * d, 1) * (1.0 / RWKV_N)
    y = d * lax.rsqrt(var + RWKV_LN_EPS) * lng_ref[...] + lnb_ref[...] + bonus_scr[...]
    o_ref[...] = (y * _silu(g_scr[...])).astype(o_ref.dtype)


def _rwkv(h, w, vecs, w2p, a2p, seg, layer, batch, seq, interpret):
    tb = min(TOKEN_TILE, seq)
    nt = seq // tb
    kern = functools.partial(_rwkv_kernel, tb=tb)
    mu, w0, a0, kk, ka, rk, lng, lnb = vecs
    ins = [mu, w0, w2p, a0, a2p, kk, ka, rk, lng, lnb]
    wide = lambda: pltpu.VMEM((tb, WIDTH), F32)
    n_rows = _IN_OFFS[14] - _IN_OFFS[12]
    return pl.pallas_call(
        kern,
        grid=(batch, nt),
        in_specs=[pl.BlockSpec((tb, D_MODEL), lambda b, i: (b * nt + i, 0)),
                  _w_rows_spec(w, layer, _IN_OFFS[12], n_rows)]
        + [_layer_spec(a, layer) for a in ins] + [_full_spec(seg.shape)],
        out_specs=pl.BlockSpec((tb, WIDTH), lambda b, i: (b * nt + i, 0)),
        out_shape=jax.ShapeDtypeStruct((batch * seq, WIDTH), MXU_DTYPE),
        scratch_shapes=[pltpu.VMEM((tb + CARRY_ROWS, 3 * WIDTH + 2 * RWKV_RANK), F32)]
        + [wide() for _ in range(9)]
        + [pltpu.VMEM((RWKV_HEADS // 2, LANES, LANES), F32), pltpu.VMEM((n_rows, D_MODEL), MXU_DTYPE)],
        compiler_params=_params(("arbitrary", "arbitrary")),
        interpret=interpret,
        name="rwkv7",
    )(h, w, *ins, seg)


def _memkv_kernel(mem_ref, g_ref, w_ref, k_ref, v_ref):
    x = mem_ref[0]
    y = x * lax.rsqrt(jnp.mean(x * x, axis=-1, keepdims=True) + NORM_EPS) * g_ref[...]
    kv = _dot(y, w_ref[...])
    head = _iota((1, MEM_WIDTH), 1) >> 6
    for hd in range(MEM_HEADS):
        k_ref[0, hd] = jnp.where(head == hd, kv[:, 0:MEM_WIDTH], 0.0).astype(k_ref.dtype)
        v_ref[0, hd] = jnp.where(head == hd, kv[:, MEM_WIDTH:2 * MEM_WIDTH], 0.0).astype(v_ref.dtype)


def _memkv(mem, g, w, layer, interpret):
    b, m, d = mem.shape
    out = jax.ShapeDtypeStruct((b, MEM_HEADS, m, MEM_WIDTH), MXU_DTYPE)
    return pl.pallas_call(
        _memkv_kernel,
        grid=(b,),
        in_specs=[pl.BlockSpec((1, m, d), lambda i: (i, 0, 0)), _layer_spec(g, layer), _layer_spec(w, layer)],
        out_specs=[pl.BlockSpec((1, MEM_HEADS, m, MEM_WIDTH), lambda i: (i, 0, 0, 0))] * 2,
        out_shape=[out, out],
        compiler_params=_params(("parallel",)),
        interpret=interpret,
        name="mem_kv",
    )(mem, g, w)


def _merge_kernel(x_ref, h_ref, oret_ref, ogla_ref, ossd_ref, orwkv_ref, km_ref, vm_ref,
                  wq_ref, wg_ref, uret_ref, ugla_ref, ussd_ref, urwkv_ref, umem_ref, wout_ref, gn_ref,
                  *refs):
    out_refs, wqb_scr = refs[:-1], refs[-1]

    @pl.when(_first_step())
    def _():
        wqb_scr[...] = wq_ref[0].astype(wqb_scr.dtype)

    h = h_ref[...]
    q = _proj(h, wqb_scr[...]) * MEM_HEAD_DIM ** -0.5
    scores = [_dot_nt(q, km_ref[0, hd]) for hd in range(MEM_HEADS)]
    branches = ((oret_ref, uret_ref), (ogla_ref, ugla_ref), (ossd_ref, ussd_ref), (orwkv_ref, urwkv_ref))
    merged = None
    for i in range(N_BRANCHES):
        gate = jax.nn.sigmoid(_proj(h, wg_ref[i * D_MODEL:(i + 1) * D_MODEL, :]))
        if i < 4:
            o_ref, u_ref = branches[i]
            up = jnp.dot(o_ref[...], u_ref[...], preferred_element_type=F32)
        else:
            up = _dot(o_mem, umem_ref[...])
        merged = gate * up if merged is None else merged + gate * up
        if i == 0:
            o_mem = jnp.zeros(q.shape, F32)
            for hd in range(MEM_HEADS):
                s = jnp.exp(scores[hd] - jnp.max(scores[hd], axis=-1, keepdims=True))
                prob = s / jnp.sum(s, axis=-1, keepdims=True)
                o_mem = o_mem + _dot(prob, vm_ref[0, hd])
    x = x_ref[...] + _dot(merged, wout_ref[...])
    if len(out_refs) == 2:
        out_refs[0][...] = x
    y = x * lax.rsqrt(jnp.mean(x * x, axis=-1, keepdims=True) + NORM_EPS) * gn_ref[...]
    out_refs[-1][...] = y.astype(out_refs[-1].dtype)


def _merge(x2d, h, o_ret, o_gla, o_ssd, o_rwkv, km, vm, wq, wg, ups, wout, g_next, layer, last,
           batch, seq, interpret):
    tm = min(TOKEN_TILE, seq)
    nt = seq // tm
    row = lambda w: pl.BlockSpec((tm, w), lambda b, i: (b * nt + i, 0))
    kvspec = pl.BlockSpec((1,) + km.shape[1:], lambda b, i: (b, 0, 0, 0))
    weights = [wg, *ups, wout, g_next]
    return pl.pallas_call(
        _merge_kernel,
        grid=(batch, nt),
        in_specs=[row(D_MODEL), row(D_MODEL), row(WIDTH), row(WIDTH), row(WIDTH), row(WIDTH), kvspec, kvspec,
                  _w_rows_spec(wq, layer, _IN_OFFS[14], MEM_WIDTH)]
        + [_layer_spec(w, layer, pipeline_mode=pl.Buffered(1)) for w in weights],
        out_specs=[row(D_MODEL)] if last else [row(D_MODEL), row(D_MODEL)],
        out_shape=[jax.ShapeDtypeStruct(x2d.shape, F32)] if last else
        [jax.ShapeDtypeStruct(x2d.shape, F32), jax.ShapeDtypeStruct(x2d.shape, MXU_DTYPE)],
        scratch_shapes=[pltpu.VMEM((MEM_WIDTH, D_MODEL), MXU_DTYPE)],
        compiler_params=_params(("arbitrary", "arbitrary")),
        interpret=interpret,
        name="merge",
    )(x2d, h, o_ret, o_gla, o_ssd, o_rwkv, km, vm, wq, *weights)


def _pad_last(a, width):
    return jnp.pad(a, [(0, 0)] * (a.ndim - 1) + [(0, width - a.shape[-1])])


def _rows3(v, width=None):
    v = v.reshape(v.shape[0], 1, -1).astype(F32)
    return v if width is None else _pad_last(v, width)


def _forward(x, mem, positions, norm_g, w_in, gla_gk_w2, gla_gk_b, gla_norm_g,
             ssd_conv_w, ssd_conv_b, ssd_dt_bias, ssd_a_log, ssd_d, ssd_norm_g,
             rwkv_mu, rwkv_w0, rwkv_w2, rwkv_a0, rwkv_a2, rwkv_k_k, rwkv_k_a, rwkv_r_k,
             rwkv_ln_g, rwkv_ln_b, mem_norm_g, w_mem_kv,
             w_up_ret, w_up_gla, w_up_ssd, w_up_rwkv, w_up_mem, w_out, final_norm_g, interpret=False):
    batch, seq, d = x.shape
    depth = w_in.shape[0]
    cdt = MXU_DTYPE
    o = _IN_OFFS

    half = np.arange(RET_DK // 2)
    ret_perm = np.concatenate([hd * RET_DK + 2 * half + par for par in (0, 1) for hd in range(RET_HEADS)])
    inv = 1.0 / (ROPE_BASE ** jnp.linspace(0.0, 1.0, RET_DK // 2, dtype=F32))
    inv_row = jnp.tile(inv, RET_HEADS).reshape(1, LANES)
    pos_col = positions.reshape(batch * seq, 1)
    head_of_lane = np.arange(WIDTH) // SSD_P
    ssd_expand = jnp.asarray(np.arange(LANES)[:, None] == head_of_lane[None, :], F32)
    rwkv_seg = jnp.asarray(head_of_lane[:256, None] == head_of_lane[None, :256], F32)

    w_t = jnp.swapaxes(w_in, 1, 2)
    w_g = w_t[:, o[15]:o[16], :].astype(cdt)
    ret_perm_rows = jnp.asarray(ret_perm[:, None] == np.arange(RET_HEADS * RET_DK)[None, :], cdt)
    gla_w2p = jnp.pad(gla_gk_w2, ((0, 0), (0, LANES - GLA_RANK), (0, 0))).astype(cdt)
    zeros_rank = jnp.zeros((depth, RWKV_RANK, WIDTH), F32)
    rwkv_w2p = jnp.concatenate([rwkv_w2, zeros_rank], axis=1).astype(cdt)
    rwkv_a2p = jnp.concatenate([zeros_rank, rwkv_a2], axis=1).astype(cdt)
    rwkv_vecs = [_rows3(v) for v in (rwkv_mu, rwkv_w0, rwkv_a0, rwkv_k_k, rwkv_k_a, rwkv_r_k,
                                     rwkv_ln_g, rwkv_ln_b)]
    ssd_small = [ssd_conv_w.astype(F32), _rows3(ssd_conv_b), _rows3(ssd_dt_bias, LANES),
                 _rows3(-jnp.exp(ssd_a_log.astype(F32)), LANES), _rows3(jnp.repeat(ssd_d, SSD_P, axis=1)),
                 _rows3(ssd_norm_g)]
    gla_b, gla_ng = _rows3(gla_gk_b), _rows3(gla_norm_g)
    mem_g, w_kv = _rows3(mem_norm_g), w_mem_kv.astype(cdt)
    ups = [w.astype(cdt) for w in (w_up_ret, w_up_gla, w_up_ssd, w_up_rwkv, w_up_mem)]
    w_o = w_out.astype(cdt)
    g_next = _rows3(jnp.concatenate([norm_g[1:], final_norm_g[None]], axis=0))

    x2d = x.reshape(batch * seq, d)
    h = _rmsnorm(x2d, norm_g[0], cdt, interpret)
    for l in range(depth):
        o_ret = _retention(h, pos_col, inv_row, w_t, ret_perm_rows, l, batch, seq, interpret)
        o_gla = _gla(h, w_t, gla_w2p, gla_b, gla_ng, l, batch, seq, interpret)
        o_ssd = _ssd(h, w_t, *ssd_small, ssd_expand, l, batch, seq, interpret)
        o_rwkv = _rwkv(h, w_t, rwkv_vecs, rwkv_w2p, rwkv_a2p, rwkv_seg, l, batch, seq, interpret)
        km, vm = _memkv(mem, mem_g, w_kv, l, interpret)
        last = l == depth - 1
        outs = _merge(x2d, h, o_ret, o_gla, o_ssd, o_rwkv, km, vm, w_t, w_g, ups, w_o, g_next, l, last,
                      batch, seq, interpret)
        x2d, h = (None, outs[0]) if last else outs
    return h.reshape(batch, seq, d)


def kernel(x, mem, positions, norm_g, w_in, gla_gk_w2, gla_gk_b, gla_norm_g, ssd_conv_w, ssd_conv_b, ssd_dt_bias, ssd_a_log, ssd_d, ssd_norm_g, rwkv_mu, rwkv_w0, rwkv_w2, rwkv_a0, rwkv_a2, rwkv_k_k, rwkv_k_a, rwkv_r_k, rwkv_ln_g, rwkv_ln_b, mem_norm_g, w_mem_kv, w_up_ret, w_up_gla, w_up_ssd, w_up_rwkv, w_up_mem, w_out, final_norm_g):
    return _forward(x, mem, positions, norm_g, w_in, gla_gk_w2, gla_gk_b, gla_norm_g,
                    ssd_conv_w, ssd_conv_b, ssd_dt_bias, ssd_a_log, ssd_d, ssd_norm_g,
                    rwkv_mu, rwkv_w0, rwkv_w2, rwkv_a0, rwkv_a2, rwkv_k_k, rwkv_k_a, rwkv_r_k,
                    rwkv_ln_g, rwkv_ln_b, mem_norm_g, w_mem_kv,
                    w_up_ret, w_up_gla, w_up_ssd, w_up_rwkv, w_up_mem, w_out, final_norm_g)
```

```python
import functools
import math

import jax
import jax.numpy as jnp
import numpy as np
from jax import lax
from jax.experimental import pallas as pl
from jax.experimental.pallas import tpu as pltpu

F32 = jnp.float32
BF16 = jnp.bfloat16
MXU_DTYPE = jnp.bfloat16

D_MODEL = 1024
WIDTH = 512
NORM_EPS = 1e-6
N_BRANCHES = 5

RET_HEADS, RET_DK, RET_DV, RET_CHUNK = 4, 64, 128, 128
ROPE_BASE = 10000.0
GLA_HEADS, GLA_DK, GLA_DV, GLA_RANK, GLA_NORMALIZER, GLA_CHUNK = 4, 64, 128, 16, 16.0, 64
SSD_HEADS, SSD_P, SSD_GROUPS, SSD_STATE, SSD_CONV, SSD_CHUNK = 8, 64, 2, 128, 4, 128
RWKV_HEADS, RWKV_N, RWKV_RANK, RWKV_CHUNK = 8, 64, 64, 64
RWKV_LN_EPS = 64e-5
RET_GROUP = 8
SSD_GROUP = 4
GLA_GROUP = 16
MEM_HEADS, MEM_HEAD_DIM, MEM_WIDTH = 4, 64, 256

LANES = 128
CARRY_ROWS = 8
TOKEN_TILE = 512
VMEM_LIMIT = 56 * 1024 * 1024

_IN_SIZES = (256, 256, 512, 512, 256, 256, 512, 16, 512, 1024, 8, 512, 1664, 512, 256, 5120)
_IN_OFFS = tuple(int(v) for v in np.cumsum((0,) + _IN_SIZES))


def _dot(a, b):
    return jnp.dot(a.astype(MXU_DTYPE), b.astype(MXU_DTYPE), preferred_element_type=F32)


def _dot_nt(a, b):
    return lax.dot_general(a.astype(MXU_DTYPE), b.astype(MXU_DTYPE), (((1,), (1,)), ((), ())),
                           preferred_element_type=F32)


def _dot_tn(a, b):
    return lax.dot_general(a.astype(MXU_DTYPE), b.astype(MXU_DTYPE), (((0,), (0,)), ((), ())),
                           preferred_element_type=F32)


def _proj(x, w_t):
    return lax.dot_general(x, w_t, (((1,), (1,)), ((), ())), preferred_element_type=F32)


def _split3(x):
    hi = x.astype(BF16)
    r1 = x - hi.astype(F32)
    mid = r1.astype(BF16)
    lo = (r1 - mid.astype(F32)).astype(BF16)
    return hi, mid, lo


def _sel_dot(sel, x):
    s = sel.astype(BF16)
    return sum(jnp.dot(s, p, preferred_element_type=F32) for p in _split3(x))


def _dot_sel(x, sel, pieces=3):
    s = sel.astype(BF16)
    return sum(jnp.dot(p, s, preferred_element_type=F32) for p in _split3(x)[:pieces])


def _iota(shape, dim):
    return lax.broadcasted_iota(jnp.int32, shape, dim)


def _tril(n, strict=False):
    r, c = _iota((n, n), 0), _iota((n, n), 1)
    return (r > c) if strict else (r >= c)


def _silu(x):
    return x * jax.nn.sigmoid(x)


def _softplus(x):
    return jnp.maximum(x, 0.0) + jnp.log1p(jnp.exp(-jnp.abs(x)))


def _rows(c, n):
    return pl.ds(pl.multiple_of(c * n, n), n)


def _loop(trips, body):
    if trips == 1:
        body(0, 0)
    else:
        lax.fori_loop(0, trips, body, 0)


def _full_spec(shape):
    zeros = (0,) * len(shape)
    return pl.BlockSpec(shape, lambda *_: zeros)


def _layer_spec(arr, layer, **kwargs):
    tail = tuple(arr.shape[1:])
    index = (layer,) + (0,) * len(tail)
    return pl.BlockSpec((None,) + tail, lambda *_: index, **kwargs)


def _w_rows_spec(w_t, layer, row0, rows):
    return pl.BlockSpec((pl.Element(1), pl.Element(rows), pl.Element(w_t.shape[2])),
                        lambda *_: (layer, row0, 0), pipeline_mode=pl.Buffered(1))


def _first_step():
    return (pl.program_id(0) == 0) & (pl.program_id(1) == 0)


def _params(semantics):
    return pltpu.CompilerParams(dimension_semantics=semantics, vmem_limit_bytes=VMEM_LIMIT)


def _rmsnorm_kernel(x_ref, g_ref, o_ref):
    x = x_ref[...]
    y = x * lax.rsqrt(jnp.mean(x * x, axis=-1, keepdims=True) + NORM_EPS)
    o_ref[...] = (y * g_ref[...]).astype(o_ref.dtype)


def _rmsnorm(x2d, g, out_dtype, interpret):
    m, d = x2d.shape
    tm = min(1024, m)
    return pl.pallas_call(
        _rmsnorm_kernel,
        grid=(m // tm,),
        in_specs=[pl.BlockSpec((tm, d), lambda i: (i, 0)), _full_spec((1, d))],
        out_specs=pl.BlockSpec((tm, d), lambda i: (i, 0)),
        out_shape=jax.ShapeDtypeStruct((m, d), out_dtype),
        compiler_params=_params(("parallel",)),
        interpret=interpret,
        name="rmsnorm",
    )(x2d, g.reshape(1, d))


def _ret_kernel(h_ref, pos_ref, inv_ref, w_ref, perm_ref, o_ref, p_scr, s_scr, wb_scr, *, tb):
    C = RET_CHUNK

    @pl.when(_first_step())
    def _():
        for lo in (0, 256):
            rows = w_ref[0, lo:lo + 256, :].astype(wb_scr.dtype)
            wb_scr[lo:lo + 256, :] = jnp.dot(perm_ref[...], rows, preferred_element_type=F32).astype(wb_scr.dtype)
        wb_scr[512:1536, :] = w_ref[0, 512:1536, :].astype(wb_scr.dtype)

    @pl.when(pl.program_id(1) == 0)
    def _():
        s_scr[...] = jnp.zeros_like(s_scr)

    ang = pos_ref[...].astype(F32) * inv_ref[...]
    cos, sin = jnp.cos(ang), jnp.sin(ang)
    p_scr[:, 512:1536] = _proj(h_ref[...], wb_scr[512:1536, :])
    p = _proj(h_ref[...], wb_scr[0:512, :])
    q1, q2 = p[:, 0:128], p[:, 128:256]
    k1, k2 = p[:, 256:384] * RET_DK ** -0.5, p[:, 384:512] * RET_DK ** -0.5
    p_scr[:, 0:128] = q1 * cos - q2 * sin
    p_scr[:, 128:256] = q2 * cos + q1 * sin
    p_scr[:, 256:384] = k1 * cos - k2 * sin
    p_scr[:, 384:512] = k2 * cos + k1 * sin

    def log_gamma(head):
        return jnp.log(1.0 - jnp.exp2(-5.0 - head.astype(F32)))

    qk_head = (_iota((1, 256), 1) >> 5) & 3
    lg_lane = log_gamma(qk_head)
    tau = _iota((C, 1), 0).astype(F32)
    dq = jnp.exp(lg_lane * (tau + 1.0))
    dk = jnp.exp(lg_lane * (C - 1.0 - tau))
    ds = jnp.exp(lg_lane * float(C))
    diff = (_iota((C, C), 0) - _iota((C, C), 1)).astype(F32)
    causal = _tril(C)
    bd_mask = (_iota((WIDTH, 256), 0) >> 7) == ((_iota((WIDTH, 256), 1) >> 5) & 3)

    heads = range(RET_HEADS)
    hcols = [slice(hd * RET_DV, (hd + 1) * RET_DV) for hd in heads]
    seg = [jnp.where(causal, jnp.exp(math.log(1.0 - 2.0 ** (-5.0 - hd)) * diff), 0.0) for hd in heads]

    def group(gi, carry):
        us = range(RET_GROUP)
        rows = [_rows(gi * RET_GROUP + u, C) for u in us]
        q = [p_scr[rw, 0:256] for rw in rows]
        k = [p_scr[rw, 256:512] for rw in rows]
        v = [p_scr[rw, 512:1024] for rw in rows]
        sc = [[_dot_nt(q[u], jnp.where(qk_head == hd, k[u], 0.0)) * seg[hd] for hd in heads] for u in us]
        y_intra = [[_dot(sc[u][hd], v[u][:, hcols[hd]]) for hd in heads] for u in us]
        kv = [jnp.where(bd_mask, _dot_tn(v[u], k[u] * dk), 0.0) for u in us]
        st = s_scr[...]
        y_inter = []
        for u in us:
            y_inter.append(_dot_nt(q[u] * dq, st))
            st = st * ds + kv[u]
        s_scr[...] = st
        for u in us:
            for hd in heads:
                y = y_intra[u][hd] + y_inter[u][:, hcols[hd]]
                y = y * lax.rsqrt(jnp.mean(y * y, axis=-1, keepdims=True) + NORM_EPS)
                g = p_scr[rows[u], 1024 + hd * RET_DV:1024 + (hd + 1) * RET_DV]
                o_ref[rows[u], hcols[hd]] = (y * _silu(g)).astype(o_ref.dtype)
        return carry

    _loop(tb // (C * RET_GROUP), group)


def _retention(h, pos_col, inv_row, w, perm, layer, batch, seq, interpret):
    tb = RET_CHUNK * RET_GROUP
    nt = seq // tb
    kern = functools.partial(_ret_kernel, tb=tb)
    return pl.pallas_call(
        kern,
        grid=(batch, nt),
        in_specs=[pl.BlockSpec((tb, D_MODEL), lambda b, i: (b * nt + i, 0)),
                  pl.BlockSpec((tb, 1), lambda b, i: (b * nt + i, 0)),
                  _full_spec((1, LANES)),
                  _w_rows_spec(w, layer, _IN_OFFS[0], 1536), _full_spec(perm.shape)],
        out_specs=pl.BlockSpec((tb, WIDTH), lambda b, i: (b * nt + i, 0)),
        out_shape=jax.ShapeDtypeStruct((batch * seq, WIDTH), MXU_DTYPE),
        scratch_shapes=[pltpu.VMEM((tb, 1536), F32), pltpu.VMEM((WIDTH, 256), F32),
                        pltpu.VMEM((1536, D_MODEL), MXU_DTYPE)],
        compiler_params=_params(("arbitrary", "arbitrary")),
        interpret=interpret,
        name="retention",
    )(h, pos_col, inv_row, w, perm)


def _gla_kernel(h_ref, w_ref, w2_ref, gb_ref, ng_ref, o_ref, p_scr, lg_scr, s_scr, wb_scr, *, tb):
    C = GLA_CHUNK

    @pl.when(_first_step())
    def _():
        cast = lambda v: v.astype(wb_scr.dtype)
        wb_scr[0:1024, :] = cast(w_ref[0, 0:1024, :])
        wb_scr[1024:1536, :] = cast(w_ref[0, 1024 + GLA_RANK:1536 + GLA_RANK, :])
        wb_scr[1536:1664, :] = cast(jnp.concatenate(
            [w_ref[0, 1024:1024 + GLA_RANK, :], jnp.zeros((LANES - GLA_RANK, D_MODEL), F32)], axis=0))

    @pl.when(pl.program_id(1) == 0)
    def _():
        s_scr[...] = jnp.zeros_like(s_scr)

    pre = _dot(_proj(h_ref[...], wb_scr[1536:1664, :]), w2_ref[...]) + gb_ref[...]
    lg_scr[...] = -_softplus(-pre) / GLA_NORMALIZER
    tri = _tril(C)
    assert tb == C * GLA_GROUP
    cum_all = [_sel_dot(tri, lg_scr[u * C:(u + 1) * C, :]) for u in range(GLA_GROUP)]
    p_scr[:, 0:512] = _proj(h_ref[...], wb_scr[0:512, :])
    p_scr[:, 512:1536] = _proj(h_ref[...], wb_scr[512:1536, :])

    k_head = _iota((1, 256), 1) >> 6
    bd_mask = (_iota((WIDTH, 256), 0) >> 7) == (_iota((WIDTH, 256), 1) >> 6)
    ng = ng_ref[...]

    heads = range(GLA_HEADS)
    hcols = [slice(hd * GLA_DV, (hd + 1) * GLA_DV) for hd in heads]

    def group(gi, carry):
        us = range(GLA_GROUP)
        rows = [_rows(gi * GLA_GROUP + u, C) for u in us]
        q = [p_scr[rw, 0:256] * GLA_DK ** -0.5 for rw in rows]
        k = [p_scr[rw, 256:512] for rw in rows]
        v = [p_scr[rw, 512:1024] for rw in rows]
        cum = cum_all
        ref = [c[C // 2:C // 2 + 1, :] for c in cum]
        last = [c[C - 1:C, :] for c in cum]
        q_in = [q[u] * jnp.exp(cum[u] - ref[u]) for u in us]
        k_in = [k[u] * jnp.exp(ref[u] - cum[u]) for u in us]
        q_dec = [q[u] * jnp.exp(cum[u]) for u in us]
        k_st = [k[u] * jnp.exp(last[u] - cum[u]) for u in us]
        sc = [[jnp.where(tri, _dot_nt(q_in[u], jnp.where(k_head == hd, k_in[u], 0.0)), 0.0) for hd in heads]
              for u in us]
        y_intra = [[_dot(sc[u][hd], v[u][:, hcols[hd]]) for hd in heads] for u in us]
        kv = [jnp.where(bd_mask, _dot_tn(v[u], k_st[u]), 0.0) for u in us]
        st = s_scr[...]
        y_inter = []
        for u in us:
            y_inter.append(_dot_nt(q_dec[u], st))
            st = st * jnp.exp(last[u]) + kv[u]
        s_scr[...] = st
        for u in us:
            for hd in heads:
                y = y_intra[u][hd] + y_inter[u][:, hcols[hd]]
                y = y * lax.rsqrt(jnp.mean(y * y, axis=-1, keepdims=True) + NORM_EPS) * ng
                g = p_scr[rows[u], 1024 + hd * GLA_DV:1024 + (hd + 1) * GLA_DV]
                o_ref[rows[u], hcols[hd]] = (y * _silu(g)).astype(o_ref.dtype)
        return carry

    _loop(tb // (C * GLA_GROUP), group)


def _gla(h, w, w2p, gb, ng, layer, batch, seq, interpret):
    tb = GLA_CHUNK * GLA_GROUP
    nt = seq // tb
    kern = functools.partial(_gla_kernel, tb=tb)
    return pl.pallas_call(
        kern,
        grid=(batch, nt),
        in_specs=[pl.BlockSpec((tb, D_MODEL), lambda b, i: (b * nt + i, 0)),
                  _w_rows_spec(w, layer, _IN_OFFS[4], 1536 + GLA_RANK), _layer_spec(w2p, layer),
                  _layer_spec(gb, layer), _layer_spec(ng, layer)],
        out_specs=pl.BlockSpec((tb, WIDTH), lambda b, i: (b * nt + i, 0)),
        out_shape=jax.ShapeDtypeStruct((batch * seq, WIDTH), MXU_DTYPE),
        scratch_shapes=[pltpu.VMEM((tb, 1536), F32), pltpu.VMEM((tb, 256), F32),
                        pltpu.VMEM((WIDTH, 256), F32), pltpu.VMEM((1664, D_MODEL), MXU_DTYPE)],
        compiler_params=_params(("arbitrary", "arbitrary")),
        interpret=interpret,
        name="gla",
    )(h, w, w2p, gb, ng)


def _ssd_kernel(h_ref, w_ref, cw_ref, cb_ref, dtb_ref, a_ref, dskip_ref, ng_ref, exp_ref, o_ref,
                raw_scr, xc_scr, z_scr, dt_scr, s_scr, wb_scr, *, tb):
    C = SSD_CHUNK
    NCH = 1024

    @pl.when(_first_step())
    def _():
        cast = lambda v: v.astype(wb_scr.dtype)
        wb_scr[0:NCH, :] = cast(w_ref[0, 0:NCH, :])
        wb_scr[NCH:NCH + WIDTH, :] = cast(w_ref[0, NCH + SSD_HEADS:NCH + SSD_HEADS + WIDTH, :])
        wb_scr[1536:1664, :] = cast(jnp.concatenate(
            [w_ref[0, NCH:NCH + SSD_HEADS, :], jnp.zeros((LANES - SSD_HEADS, D_MODEL), F32)], axis=0))

    @pl.when(pl.program_id(1) == 0)
    def _():
        s_scr[...] = jnp.zeros_like(s_scr)
        raw_scr[0:CARRY_ROWS, :] = jnp.zeros((CARRY_ROWS, NCH), F32)

    dt_scr[...] = _softplus(_proj(h_ref[...], wb_scr[1536:1664, :]) + dtb_ref[...])
    first_tap = CARRY_ROWS - (SSD_CONV - 1)
    for lo in range(0, NCH, 256):
        cols = slice(lo, lo + 256)
        raw_scr[CARRY_ROWS:CARRY_ROWS + tb, cols] = _proj(h_ref[...], wb_scr[cols, :])
        conv = cb_ref[:, cols] + sum(raw_scr[first_tap + j:first_tap + j + tb, cols] * cw_ref[j:j + 1, cols]
                                     for j in range(SSD_CONV))
        xc_scr[:, cols] = _silu(conv)
        raw_scr[0:CARRY_ROWS, cols] = raw_scr[tb:tb + CARRY_ROWS, cols]
    z_scr[...] = _proj(h_ref[...], wb_scr[NCH:NCH + WIDTH, :])

    tri = _tril(C)
    lane_lo = _iota((1, LANES), 1) < SSD_P
    expand = exp_ref[...]
    a_row = a_ref[...]

    groups = range(SSD_GROUPS)
    gcols = [slice(g * 256, (g + 1) * 256) for g in groups]
    ncols = [slice(g * SSD_STATE, (g + 1) * SSD_STATE) for g in groups]

    def group(gi, carry):
        us = range(SSD_GROUP)
        rows = [_rows(gi * SSD_GROUP + u, C) for u in us]
        xs = [xc_scr[rw, 0:512] for rw in rows]
        bm = [xc_scr[rw, 512:768] for rw in rows]
        cm = [xc_scr[rw, 768:1024] for rw in rows]
        dt = [dt_scr[rw, :] for rw in rows]
        cum = [_sel_dot(tri, dt[u] * a_row) for u in us]
        cum_t = [c.T for c in cum]
        dt_e = [_dot_sel(d, expand) for d in dt]
        cum_e = [_dot_sel(c, expand) for c in cum]
        last_e = [c[C - 1:C, :] for c in cum_e]
        xdt = [xs[u] * dt_e[u] for u in us]
        v_st = [xdt[u] * jnp.exp(last_e[u] - cum_e[u]) for u in us]
        e_cum = [jnp.exp(c) for c in cum_e]
        scores = [[_dot_nt(cm[u][:, ncols[g]], bm[u][:, ncols[g]]) for g in groups] for u in us]
        y_intra = []
        for u in us:
            parts = []
            for pr in range(SSD_HEADS // 2):
                sc = scores[u][pr // 2]
                segs = []
                for hd in (2 * pr, 2 * pr + 1):
                    d = jnp.minimum(cum[u][:, hd:hd + 1] - cum_t[u][hd:hd + 1, :], 0.0)
                    segs.append(sc * jnp.where(tri, jnp.exp(d), 0.0))
                xp = xdt[u][:, pr * LANES:(pr + 1) * LANES]
                rhs = jnp.concatenate([jnp.where(lane_lo, xp, 0.0), jnp.where(lane_lo, 0.0, xp)], axis=0)
                parts.append(_dot(jnp.concatenate(segs, axis=1), rhs))
            y_intra.append(jnp.concatenate(parts, axis=1))
        kv = [[_dot_tn(bm[u][:, ncols[g]], v_st[u][:, gcols[g]]) for g in groups] for u in us]
        st = [s_scr[g] for g in groups]
        y_inter = []
        for u in us:
            y_inter.append(jnp.concatenate(
                [_dot(cm[u][:, ncols[g]], st[g]) * e_cum[u][:, gcols[g]] for g in groups], axis=1))
            st = [st[g] * jnp.exp(last_e[u][:, gcols[g]]) + kv[u][g] for g in groups]
        for g in groups:
            s_scr[g] = st[g]
        for u in us:
            y = y_intra[u] + y_inter[u] + dskip_ref[...] * xs[u]
            y = y * _silu(z_scr[rows[u], :])
            for g in groups:
                yg = y[:, gcols[g]]
                yg = yg * lax.rsqrt(jnp.mean(yg * yg, axis=-1, keepdims=True) + NORM_EPS)
                o_ref[rows[u], gcols[g]] = (yg * ng_ref[:, gcols[g]]).astype(o_ref.dtype)
        return carry

    _loop(tb // (C * SSD_GROUP), group)


def _ssd(h, w, cw, cb, dtb, a_row, dskip, ng, expand, layer, batch, seq, interpret):
    tb = SSD_CHUNK * SSD_GROUP
    nt = seq // tb
    kern = functools.partial(_ssd_kernel, tb=tb)
    small = [cw, cb, dtb, a_row, dskip, ng]
    return pl.pallas_call(
        kern,
        grid=(batch, nt),
        in_specs=[pl.BlockSpec((tb, D_MODEL), lambda b, i: (b * nt + i, 0)),
                  _w_rows_spec(w, layer, _IN_OFFS[9], 1536 + SSD_HEADS)]
        + [_layer_spec(a, layer) for a in small] + [_full_spec(expand.shape)],
        out_specs=pl.BlockSpec((tb, WIDTH), lambda b, i: (b * nt + i, 0)),
        out_shape=jax.ShapeDtypeStruct((batch * seq, WIDTH), MXU_DTYPE),
        scratch_shapes=[pltpu.VMEM((tb + CARRY_ROWS, 1024), F32), pltpu.VMEM((tb, 1024), F32),
                        pltpu.VMEM((tb, WIDTH), F32), pltpu.VMEM((tb, LANES), F32),
                        pltpu.VMEM((SSD_GROUPS, SSD_STATE, 256), F32), pltpu.VMEM((1664, D_MODEL), MXU_DTYPE)],
        compiler_params=_params(("arbitrary", "arbitrary")),
        interpret=interpret,
        name="ssd",
    )(h, w, *small, expand)


def _rwkv_kernel(h_ref, w_ref, mu_ref, w0_ref, w2_ref, a0_ref, a2_ref, kk_ref, ka_ref, rk_ref,
                 lng_ref, lnb_ref, seg_ref, o_ref,
                 u_scr, g_scr, r_scr, lw_scr, k_scr, v_scr, a_scr, b_scr, y_scr, bonus_scr, st_scr, wb_scr,
                 *, tb):
    C = RWKV_CHUNK
    NU = 3 * WIDTH + 2 * RWKV_RANK

    @pl.when(_first_step())
    def _():
        wb_scr[...] = w_ref[0].astype(wb_scr.dtype)

    @pl.when(pl.program_id(1) == 0)
    def _():
        st_scr[...] = jnp.zeros_like(st_scr)
        u_scr[0:CARRY_ROWS, :] = jnp.zeros((CARRY_ROWS, NU), F32)

    seg = seg_ref[...]

    def seg_sum(x, pieces):
        return jnp.concatenate([_dot_sel(x[:, j * 256:(j + 1) * 256], seg, pieces) for j in range(2)], axis=1)

    def shifted(lo, hi):
        u = _proj(h_ref[...], wb_scr[lo:hi, :])
        u_scr[CARRY_ROWS:CARRY_ROWS + tb, lo:hi] = u
        u_prev = u_scr[CARRY_ROWS - 1:CARRY_ROWS - 1 + tb, lo:hi]
        u_scr[0:CARRY_ROWS, lo:hi] = u_scr[tb:tb + CARRY_ROWS, lo:hi]
        return u + (u_prev - u) * mu_ref[:, lo:hi]

    wa = shifted(1536, 1664)
    w_pre = w0_ref[...] + _dot(jnp.tanh(wa), w2_ref[...])
    a_gate = jax.nn.sigmoid(a0_ref[...] + _dot(wa, a2_ref[...]))
    lw_scr[...] = -jnp.exp(-_softplus(-w_pre) - 0.5)
    k = shifted(512, 1024)
    kk = k * kk_ref[...]
    k = k * (1.0 + (a_gate - 1.0) * ka_ref[...])
    kk = kk / jnp.maximum(jnp.sqrt(seg_sum(kk * kk, 1)), 1e-12)
    k_scr[...] = k
    a_scr[...] = -kk
    b_scr[...] = kk * a_gate
    r = shifted(0, 512)
    r_scr[...] = r
    v = shifted(1024, 1536)
    v_scr[...] = v
    bonus_scr[...] = seg_sum(r * k * rk_ref[...], 1) * v
    g_scr[...] = _proj(h_ref[...], wb_scr[NU:NU + WIDTH, :])

    tri = _tril(C)
    r2, c2 = _iota((2 * C, 2 * C), 0), _iota((2 * C, 2 * C), 1)
    same = (r2 >> 6) == (c2 >> 6)
    low_s = same & ((r2 & 63) > (c2 & 63))
    low_i = same & ((r2 & 63) >= (c2 & 63))
    eye = (r2 == c2).astype(F32)
    lane_lo = _iota((1, LANES), 1) < RWKV_N

    pairs = range(RWKV_HEADS // 2)

    def stack_pair(xp):
        return jnp.concatenate([jnp.where(lane_lo, xp, 0.0), jnp.where(lane_lo, 0.0, xp)], axis=0)

    def stack(x, pr):
        return stack_pair(x[:, pr * LANES:(pr + 1) * LANES])

    def state_free_part(chunks, tick):
        lanes = [(j, pr) for j in range(len(chunks)) for pr in pairs]
        rows, a_0, r_0, v_c, b_h, k_h, e_last, lhs, rhs_t = [], [], [], [], [], [], [], {}, {}
        for j, u in enumerate(chunks):
            rw = _rows(u, C)
            lw = lw_scr[rw, :]
            cum = _sel_dot(tri, lw)
            cum_p = cum - lw
            ref = cum[C // 2:C // 2 + 1, :]
            last = cum[C - 1:C, :]
            e_fwd = jnp.exp(cum - ref)
            e_bwd = jnp.exp(ref - cum)
            e_end = jnp.exp(last - cum)
            r_c, k_c, a_c, b_c = r_scr[rw, :], k_scr[rw, :], a_scr[rw, :], b_scr[rw, :]
            r_t = r_c * e_fwd
            a_t = a_c * jnp.exp(cum_p - ref)
            b_t = b_c * e_bwd
            k_t = k_c * e_bwd
            for pr in pairs:
                lhs[j, pr] = jnp.concatenate([stack(a_t, pr), stack(r_t, pr)], axis=0)
                rhs_t[j, pr] = jnp.concatenate([stack(b_t, pr), stack(k_t, pr)], axis=0)
            rows.append(rw)
            a_0.append(a_c * jnp.exp(cum_p))
            r_0.append(r_c * jnp.exp(cum))
            v_c.append(v_scr[rw, :])
            b_h.append(b_c * e_end)
            k_h.append(k_c * e_end)
            e_last.append(jnp.exp(last))
        tick()
        big = [_dot_nt(lhs[ln], rhs_t[ln]) for ln in lanes]
        tick()
        a_ab = [jnp.where(low_s, m[0:128, 0:128], 0.0) for m in big]
        a_ak = [jnp.where(low_s, m[0:128, 128:256], 0.0) for m in big]
        a_rb = [jnp.where(low_i, m[128:256, 0:128], 0.0) for m in big]
        a_rk = [jnp.where(low_i, m[128:256, 128:256], 0.0) for m in big]
        inv = [eye + m for m in a_ab]
        pw = [_dot(m, m) for m in a_ab]
        tick()
        for _ in range(4):
            prod = [_dot(p, jnp.concatenate([p, t], axis=1)) for p, t in zip(pw, inv)]
            pw = [m[:, 0:128] for m in prod]
            inv = [t + m[:, 128:256] for t, m in zip(inv, prod)]
            tick()
        inv = [t + _dot(p, t) for p, t in zip(pw, inv)]
        tick()
        vs = [stack(v_c[j], pr) for j, pr in lanes]
        akv = [_dot(m, x) for m, x in zip(a_ak, vs)]
        tick()
        wu = [_dot(inv[i], jnp.concatenate([stack(a_0[j], pr), akv[i]], axis=1)) for i, (j, pr) in enumerate(lanes)]
        tick()
        w1 = [m[:, 0:LANES] for m in wu]
        u0v = [jnp.concatenate([m[:, LANES:2 * LANES], x], axis=0) for m, x in zip(wu, vs)]
        zero = jnp.zeros((2 * C, LANES), F32)
        xo = [_dot(jnp.concatenate([a_rb[i], a_rk[i]], axis=1),
                   jnp.concatenate([jnp.concatenate([w1[i], zero], axis=0), u0v[i]], axis=1))
              for i in range(len(lanes))]
        tick()
        fold = lambda m: m[0:C, :] + m[C:2 * C, :]
        w2 = [r_0[j][:, pr * LANES:(pr + 1) * LANES] + fold(xo[i][:, 0:LANES]) for i, (j, pr) in enumerate(lanes)]
        y0 = [fold(m[:, LANES:2 * LANES]) for m in xo]
        bh = [stack(b_h[j], pr) for j, pr in lanes]
        m_st = [_dot_tn(w1[i], bh[i]) for i in range(len(lanes))]
        tick()
        n_st = [_dot_tn(u0v[i], jnp.concatenate([bh[i], stack(k_h[j], pr)], axis=0))
                for i, (j, pr) in enumerate(lanes)]
        tick()
        return rows, e_last, w2, y0, m_st, n_st

    st = [st_scr[pr] for pr in pairs]

    def chain_steps(part):
        rows, e_last, w2, y0, m_st, n_st = part

        def step(j):
            for pr in pairs:
                i = j * len(pairs) + pr
                cols = slice(pr * LANES, (pr + 1) * LANES)
                y_scr[rows[j], cols] = _dot_nt(w2[i], st[pr]) + y0[i]
                st[pr] = st[pr] * e_last[j][:, cols] + _dot(st[pr], m_st[i]) + n_st[i]

        return [functools.partial(step, j) for j in range(len(rows))]

    n_chunks = tb // C
    first, second = list(range(n_chunks // 2)), list(range(n_chunks // 2, n_chunks))
    queue = chain_steps(state_free_part(first, lambda: None))

    def tick():
        if queue:
            queue.pop(0)()

    second_part = state_free_part(second, tick)
    for step in queue + chain_steps(second_part):
        step()
    for pr in pairs:
        st_scr[pr] = st[pr]

    y = y_scr[...]
    mean = seg_sum(y, 2) * (1.0 / RWKV_N)
    d = y - mean
    var = seg_sum(d * d, 1) * (1.0 / RWKV_N)
    y = d * lax.rsqrt(var + RWKV_LN_EPS) * lng_ref[...] + lnb_ref[...] + bonus_scr[...]
    o_ref[...] = (y * _silu(g_scr[...])).astype(o_ref.dtype)


def _rwkv(h, w, vecs, w2p, a2p, seg, layer, batch, seq, interpret):
    tb = min(TOKEN_TILE, seq)
    nt = seq // tb
    kern = functools.partial(_rwkv_kernel, tb=tb)
    mu, w0, a0, kk, ka, rk, lng, lnb = vecs
    ins = [mu, w0, w2p, a0, a2p, kk, ka, rk, lng, lnb]
    wide = lambda: pltpu.VMEM((tb, WIDTH), F32)
    n_rows = _IN_OFFS[14] - _IN_OFFS[12]
    return pl.pallas_call(
        kern,
        grid=(batch, nt),
        in_specs=[pl.BlockSpec((tb, D_MODEL), lambda b, i: (b * nt + i, 0)),
                  _w_rows_spec(w, layer, _IN_OFFS[12], n_rows)]
        + [_layer_spec(a, layer) for a in ins] + [_full_spec(seg.shape)],
        out_specs=pl.BlockSpec((tb, WIDTH), lambda b, i: (b * nt + i, 0)),
        out_shape=jax.ShapeDtypeStruct((batch * seq, WIDTH), MXU_DTYPE),
        scratch_shapes=[pltpu.VMEM((tb + CARRY_ROWS, 3 * WIDTH + 2 * RWKV_RANK), F32)]
        + [wide() for _ in range(9)]
        + [pltpu.VMEM((RWKV_HEADS // 2, LANES, LANES), F32), pltpu.VMEM((n_rows, D_MODEL), MXU_DTYPE)],
        compiler_params=_params(("arbitrary", "arbitrary")),
        interpret=interpret,
        name="rwkv7",
    )(h, w, *ins, seg)


def _memkv_kernel(mem_ref, g_ref, w_ref, k_ref, v_ref):
    x = mem_ref[0]
    y = x * lax.rsqrt(jnp.mean(x * x, axis=-1, keepdims=True) + NORM_EPS) * g_ref[...]
    kv = _dot(y, w_ref[...])
    head = _iota((1, MEM_WIDTH), 1) >> 6
    for hd in range(MEM_HEADS):
        k_ref[0, hd] = jnp.where(head == hd, kv[:, 0:MEM_WIDTH], 0.0).astype(k_ref.dtype)
        v_ref[0, hd] = jnp.where(head == hd, kv[:, MEM_WIDTH:2 * MEM_WIDTH], 0.0).astype(v_ref.dtype)


def _memkv(mem, g, w, layer, interpret):
    b, m, d = mem.shape
    out = jax.ShapeDtypeStruct((b, MEM_HEADS, m, MEM_WIDTH), MXU_DTYPE)
    return pl.pallas_call(
        _memkv_kernel,
        grid=(b,),
        in_specs=[pl.BlockSpec((1, m, d), lambda i: (i, 0, 0)), _layer_spec(g, layer), _layer_spec(w, layer)],
        out_specs=[pl.BlockSpec((1, MEM_HEADS, m, MEM_WIDTH), lambda i: (i, 0, 0, 0))] * 2,
        out_shape=[out, out],
        compiler_params=_params(("parallel",)),
        interpret=interpret,
        name="mem_kv",
    )(mem, g, w)


def _merge_kernel(x_ref, h_ref, oret_ref, ogla_ref, ossd_ref, orwkv_ref, km_ref, vm_ref,
                  wq_ref, wg_ref, uret_ref, ugla_ref, ussd_ref, urwkv_ref, umem_ref, wout_ref, gn_ref,
                  *refs):
    out_refs, wqb_scr = refs[:-1], refs[-1]

    @pl.when(_first_step())
    def _():
        wqb_scr[...] = wq_ref[0].astype(wqb_scr.dtype)

    h = h_ref[...]
    q = _proj(h, wqb_scr[...]) * MEM_HEAD_DIM ** -0.5
    scores = [_dot_nt(q, km_ref[0, hd]) for hd in range(MEM_HEADS)]
    branches = ((oret_ref, uret_ref), (ogla_ref, ugla_ref), (ossd_ref, ussd_ref), (orwkv_ref, urwkv_ref))
    merged = None
    for i in range(N_BRANCHES):
        gate = jax.nn.sigmoid(_proj(h, wg_ref[i * D_MODEL:(i + 1) * D_MODEL, :]))
        if i < 4:
            o_ref, u_ref = branches[i]
            up = jnp.dot(o_ref[...], u_ref[...], preferred_element_type=F32)
        else:
            up = _dot(o_mem, umem_ref[...])
        merged = gate * up if merged is None else merged + gate * up
        if i == 0:
            o_mem = jnp.zeros(q.shape, F32)
            for hd in range(MEM_HEADS):
                s = jnp.exp(scores[hd] - jnp.max(scores[hd], axis=-1, keepdims=True))
                prob = s / jnp.sum(s, axis=-1, keepdims=True)
                o_mem = o_mem + _dot(prob, vm_ref[0, hd])
    x = x_ref[...] + _dot(merged, wout_ref[...])
    if len(out_refs) == 2:
        out_refs[0][...] = x
    y = x * lax.rsqrt(jnp.mean(x * x, axis=-1, keepdims=True) + NORM_EPS) * gn_ref[...]
    out_refs[-1][...] = y.astype(out_refs[-1].dtype)


def _merge(x2d, h, o_ret, o_gla, o_ssd, o_rwkv, km, vm, wq, wg, ups, wout, g_next, layer, last,
           batch, seq, interpret):
    tm = min(TOKEN_TILE, seq)
    nt = seq // tm
    row = lambda w: pl.BlockSpec((tm, w), lambda b, i: (b * nt + i, 0))
    kvspec = pl.BlockSpec((1,) + km.shape[1:], lambda b, i: (b, 0, 0, 0))
    weights = [wg, *ups, wout, g_next]
    return pl.pallas_call(
        _merge_kernel,
        grid=(batch, nt),
        in_specs=[row(D_MODEL), row(D_MODEL), row(WIDTH), row(WIDTH), row(WIDTH), row(WIDTH), kvspec, kvspec,
                  _w_rows_spec(wq, layer, _IN_OFFS[14], MEM_WIDTH)]
        + [_layer_spec(w, layer, pipeline_mode=pl.Buffered(1)) for w in weights],
        out_specs=[row(D_MODEL)] if last else [row(D_MODEL), row(D_MODEL)],
        out_shape=[jax.ShapeDtypeStruct(x2d.shape, F32)] if last else
        [jax.ShapeDtypeStruct(x2d.shape, F32), jax.ShapeDtypeStruct(x2d.shape, MXU_DTYPE)],
        scratch_shapes=[pltpu.VMEM((MEM_WIDTH, D_MODEL), MXU_DTYPE)],
        compiler_params=_params(("arbitrary", "arbitrary")),
        interpret=interpret,
        name="merge",
    )(x2d, h, o_ret, o_gla, o_ssd, o_rwkv, km, vm, wq, *weights)


def _pad_last(a, width):
    return jnp.pad(a, [(0, 0)] * (a.ndim - 1) + [(0, width - a.shape[-1])])


def _rows3(v, width=None):
    v = v.reshape(v.shape[0], 1, -1).astype(F32)
    return v if width is None else _pad_last(v, width)


def _forward(x, mem, positions, norm_g, w_in, gla_gk_w2, gla_gk_b, gla_norm_g,
             ssd_conv_w, ssd_conv_b, ssd_dt_bias, ssd_a_log, ssd_d, ssd_norm_g,
             rwkv_mu, rwkv_w0, rwkv_w2, rwkv_a0, rwkv_a2, rwkv_k_k, rwkv_k_a, rwkv_r_k,
             rwkv_ln_g, rwkv_ln_b, mem_norm_g, w_mem_kv,
             w_up_ret, w_up_gla, w_up_ssd, w_up_rwkv, w_up_mem, w_out, final_norm_g, interpret=False):
    batch, seq, d = x.shape
    assert d == D_MODEL and seq % (RET_CHUNK * RET_GROUP) == 0 and seq % (GLA_CHUNK * GLA_GROUP) == 0
    depth = w_in.shape[0]
    cdt = MXU_DTYPE
    o = _IN_OFFS

    half = np.arange(RET_DK // 2)
    ret_perm = np.concatenate([hd * RET_DK + 2 * half + par for par in (0, 1) for hd in range(RET_HEADS)])
    inv = 1.0 / (ROPE_BASE ** jnp.linspace(0.0, 1.0, RET_DK // 2, dtype=F32))
    inv_row = jnp.tile(inv, RET_HEADS).reshape(1, LANES)
    pos_col = positions.reshape(batch * seq, 1)
    head_of_lane = np.arange(WIDTH) // SSD_P
    ssd_expand = jnp.asarray(np.arange(LANES)[:, None] == head_of_lane[None, :], F32)
    rwkv_seg = jnp.asarray(head_of_lane[:256, None] == head_of_lane[None, :256], F32)

    w_t = jnp.swapaxes(w_in, 1, 2)
    w_g = w_t[:, o[15]:o[16], :].astype(cdt)
    ret_perm_rows = jnp.asarray(ret_perm[:, None] == np.arange(RET_HEADS * RET_DK)[None, :], cdt)
    gla_w2p = jnp.pad(gla_gk_w2, ((0, 0), (0, LANES - GLA_RANK), (0, 0))).astype(cdt)
    zeros_rank = jnp.zeros((depth, RWKV_RANK, WIDTH), F32)
    rwkv_w2p = jnp.concatenate([rwkv_w2, zeros_rank], axis=1).astype(cdt)
    rwkv_a2p = jnp.concatenate([zeros_rank, rwkv_a2], axis=1).astype(cdt)
    rwkv_vecs = [_rows3(v) for v in (rwkv_mu, rwkv_w0, rwkv_a0, rwkv_k_k, rwkv_k_a, rwkv_r_k,
                                     rwkv_ln_g, rwkv_ln_b)]
    ssd_small = [ssd_conv_w.astype(F32), _rows3(ssd_conv_b), _rows3(ssd_dt_bias, LANES),
                 _rows3(-jnp.exp(ssd_a_log.astype(F32)), LANES), _rows3(jnp.repeat(ssd_d, SSD_P, axis=1)),
                 _rows3(ssd_norm_g)]
    gla_b, gla_ng = _rows3(gla_gk_b), _rows3(gla_norm_g)
    mem_g, w_kv = _rows3(mem_norm_g), w_mem_kv.astype(cdt)
    ups = [w.astype(cdt) for w in (w_up_ret, w_up_gla, w_up_ssd, w_up_rwkv, w_up_mem)]
    w_o = w_out.astype(cdt)
    g_next = _rows3(jnp.concatenate([norm_g[1:], final_norm_g[None]], axis=0))

    x2d = x.reshape(batch * seq, d)
    h = _rmsnorm(x2d, norm_g[0], cdt, interpret)
    for l in range(depth):
        o_ret = _retention(h, pos_col, inv_row, w_t, ret_perm_rows, l, batch, seq, interpret)
        o_gla = _gla(h, w_t, gla_w2p, gla_b, gla_ng, l, batch, seq, interpret)
        o_ssd = _ssd(h, w_t, *ssd_small, ssd_expand, l, batch, seq, interpret)
        o_rwkv = _rwkv(h, w_t, rwkv_vecs, rwkv_w2p, rwkv_a2p, rwkv_seg, l, batch, seq, interpret)
        km, vm = _memkv(mem, mem_g, w_kv, l, interpret)
        last = l == depth - 1
        outs = _merge(x2d, h, o_ret, o_gla, o_ssd, o_rwkv, km, vm, w_t, w_g, ups, w_o, g_next, l, last,
                      batch, seq, interpret)
        x2d, h = (None, outs[0]) if last else outs
    return h.reshape(batch, seq, d)


def kernel(x, mem, positions, norm_g, w_in, gla_gk_w2, gla_gk_b, gla_norm_g, ssd_conv_w, ssd_conv_b, ssd_dt_bias, ssd_a_log, ssd_d, ssd_norm_g, rwkv_mu, rwkv_w0, rwkv_w2, rwkv_a0, rwkv_a2, rwkv_k_k, rwkv_k_a, rwkv_r_k, rwkv_ln_g, rwkv_ln_b, mem_norm_g, w_mem_kv, w_up_ret, w_up_gla, w_up_ssd, w_up_rwkv, w_up_mem, w_out, final_norm_g):
    return _forward(x, mem, positions, norm_g, w_in, gla_gk_w2, gla_gk_b, gla_norm_g,
                    ssd_conv_w, ssd_conv_b, ssd_dt_bias, ssd_a_log, ssd_d, ssd_norm_g,
                    rwkv_mu, rwkv_w0, rwkv_w2, rwkv_a0, rwkv_a2, rwkv_k_k, rwkv_k_a, rwkv_r_k,
                    rwkv_ln_g, rwkv_ln_b, mem_norm_g, w_mem_kv,
                    w_up_ret, w_up_gla, w_up_ssd, w_up_rwkv, w_up_mem, w_out, final_norm_g)
```

```python
import functools
import math

import jax
import jax.numpy as jnp
import numpy as np
from jax import lax
from jax.experimental import pallas as pl
from jax.experimental.pallas import tpu as pltpu

F32 = jnp.float32
BF16 = jnp.bfloat16
MXU_DTYPE = jnp.bfloat16

D_MODEL = 1024
WIDTH = 512
NORM_EPS = 1e-6
N_BRANCHES = 5

RET_HEADS, RET_DK, RET_DV, RET_CHUNK = 4, 64, 128, 128
ROPE_BASE = 10000.0
GLA_HEADS, GLA_DK, GLA_DV, GLA_RANK, GLA_NORMALIZER, GLA_CHUNK = 4, 64, 128, 16, 16.0, 64
SSD_HEADS, SSD_P, SSD_GROUPS, SSD_STATE, SSD_CONV, SSD_CHUNK = 8, 64, 2, 128, 4, 128
RWKV_HEADS, RWKV_N, RWKV_RANK, RWKV_CHUNK = 8, 64, 64, 64
RWKV_LN_EPS = 64e-5
RET_GROUP = 8
SSD_GROUP = 4
GLA_GROUP = 16
MEM_HEADS, MEM_HEAD_DIM, MEM_WIDTH = 4, 64, 256

LANES = 128
CARRY_ROWS = 8
TOKEN_TILE = 512
VMEM_LIMIT = 56 * 1024 * 1024

_IN_SIZES = (256, 256, 512, 512, 256, 256, 512, 16, 512, 1024, 8, 512, 1664, 512, 256, 5120)
_IN_OFFS = tuple(int(v) for v in np.cumsum((0,) + _IN_SIZES))


def _dot(a, b):
    return jnp.dot(a.astype(MXU_DTYPE), b.astype(MXU_DTYPE), preferred_element_type=F32)


def _dot_nt(a, b):
    return lax.dot_general(a.astype(MXU_DTYPE), b.astype(MXU_DTYPE), (((1,), (1,)), ((), ())),
                           preferred_element_type=F32)


def _dot_tn(a, b):
    return lax.dot_general(a.astype(MXU_DTYPE), b.astype(MXU_DTYPE), (((0,), (0,)), ((), ())),
                           preferred_element_type=F32)


def _proj(x, w_t):
    return lax.dot_general(x, w_t, (((1,), (1,)), ((), ())), preferred_element_type=F32)


def _split3(x):
    hi = x.astype(BF16)
    r1 = x - hi.astype(F32)
    mid = r1.astype(BF16)
    lo = (r1 - mid.astype(F32)).astype(BF16)
    return hi, mid, lo


def _sel_dot(sel, x):
    return sum(jnp.dot(sel, p, preferred_element_type=F32) for p in _split3(x))


def _dot_sel(x, sel, pieces=3):
    return sum(jnp.dot(p, sel, preferred_element_type=F32) for p in _split3(x)[:pieces])


def _iota(shape, dim):
    return lax.broadcasted_iota(jnp.int32, shape, dim)


def _tril(n, strict=False):
    r, c = _iota((n, n), 0), _iota((n, n), 1)
    return (r > c) if strict else (r >= c)


def _silu(x):
    return x * jax.nn.sigmoid(x)


def _softplus(x):
    return jnp.maximum(x, 0.0) + jnp.log(1.0 + jnp.exp(-jnp.abs(x)))


def _rows(c, n):
    return pl.ds(pl.multiple_of(c * n, n), n)


def _loop(trips, body):
    if trips == 1:
        body(0, 0)
    else:
        lax.fori_loop(0, trips, body, 0)


def _full_spec(shape):
    zeros = (0,) * len(shape)
    return pl.BlockSpec(shape, lambda *_: zeros)


def _layer_spec(arr, layer, **kwargs):
    tail = tuple(arr.shape[1:])
    index = (layer,) + (0,) * len(tail)
    return pl.BlockSpec((None,) + tail, lambda *_: index, **kwargs)


def _w_rows_spec(w_t, layer, row0, rows):
    return pl.BlockSpec((pl.Element(1), pl.Element(rows), pl.Element(w_t.shape[2])),
                        lambda *_: (layer, row0, 0), pipeline_mode=pl.Buffered(1))


def _first_step():
    return (pl.program_id(0) == 0) & (pl.program_id(1) == 0)


def _params(semantics):
    return pltpu.CompilerParams(dimension_semantics=semantics, vmem_limit_bytes=VMEM_LIMIT)


def _rmsnorm_kernel(x_ref, g_ref, o_ref):
    x = x_ref[...]
    y = x * lax.rsqrt(jnp.mean(x * x, axis=-1, keepdims=True) + NORM_EPS)
    o_ref[...] = (y * g_ref[...]).astype(o_ref.dtype)


def _rmsnorm(x2d, g, out_dtype, interpret):
    m, d = x2d.shape
    tm = min(1024, m)
    return pl.pallas_call(
        _rmsnorm_kernel,
        grid=(m // tm,),
        in_specs=[pl.BlockSpec((tm, d), lambda i: (i, 0)), _full_spec((1, d))],
        out_specs=pl.BlockSpec((tm, d), lambda i: (i, 0)),
        out_shape=jax.ShapeDtypeStruct((m, d), out_dtype),
        compiler_params=_params(("parallel",)),
        interpret=interpret,
        name="rmsnorm",
    )(x2d, g.reshape(1, d))


def _ret_kernel(h_ref, pos_ref, inv_ref, w_ref, perm_ref, o_ref, p_scr, s_scr, wb_scr, *, tb):
    C = RET_CHUNK

    @pl.when(_first_step())
    def _():
        for lo in (0, 256):
            rows = w_ref[0, lo:lo + 256, :].astype(wb_scr.dtype)
            wb_scr[lo:lo + 256, :] = jnp.dot(perm_ref[...], rows, preferred_element_type=F32).astype(wb_scr.dtype)
        wb_scr[512:1536, :] = w_ref[0, 512:1536, :].astype(wb_scr.dtype)

    @pl.when(pl.program_id(1) == 0)
    def _():
        s_scr[...] = jnp.zeros_like(s_scr)

    ang = pos_ref[...].astype(F32) * inv_ref[...]
    lane_group = _iota((1, LANES), 1) >> 5

    def unpack(t):
        blocks = []
        for j in range(RET_HEADS):
            own = jnp.where(lane_group == j, t, 0.0)
            blocks.append(own + pltpu.roll(own, 32, 1) + pltpu.roll(own, 64, 1) + pltpu.roll(own, 96, 1))
        return jnp.concatenate(blocks, axis=0)

    cos, sin = unpack(jnp.cos(ang)), unpack(jnp.sin(ang))
    p_scr[:, 512:1536] = _proj(h_ref[...], wb_scr[512:1536, :])
    p = _proj(h_ref[...], wb_scr[0:512, :])
    q1, q2 = p[:, 0:128], p[:, 128:256]
    k1, k2 = p[:, 256:384] * RET_DK ** -0.5, p[:, 384:512] * RET_DK ** -0.5
    p_scr[:, 0:128] = q1 * cos - q2 * sin
    p_scr[:, 128:256] = q2 * cos + q1 * sin
    p_scr[:, 256:384] = k1 * cos - k2 * sin
    p_scr[:, 384:512] = k2 * cos + k1 * sin

    def log_gamma(head):
        return jnp.log(1.0 - jnp.exp2(-5.0 - head.astype(F32)))

    qk_head = (_iota((1, 256), 1) >> 5) & 3
    lg_lane = log_gamma(qk_head)
    tau = _iota((C, 1), 0).astype(F32)
    dq = jnp.exp(lg_lane * (tau + 1.0))
    dk = jnp.exp(lg_lane * (C - 1.0 - tau))
    ds = jnp.exp(lg_lane * float(C))
    diff = (_iota((C, C), 0) - _iota((C, C), 1)).astype(F32)
    causal = _tril(C)
    bd_mask = (_iota((WIDTH, 256), 0) >> 7) == ((_iota((WIDTH, 256), 1) >> 5) & 3)

    heads = range(RET_HEADS)
    hcols = [slice(hd * RET_DV, (hd + 1) * RET_DV) for hd in heads]
    seg = [jnp.where(causal, jnp.exp(math.log(1.0 - 2.0 ** (-5.0 - hd)) * diff), 0.0) for hd in heads]

    def group(gi, carry):
        us = range(RET_GROUP)
        rows = [_rows(gi * RET_GROUP + u, C) for u in us]
        q = [p_scr[rw, 0:256] for rw in rows]
        k = [p_scr[rw, 256:512] for rw in rows]
        v = [p_scr[rw, 512:1024] for rw in rows]
        sc = [[_dot_nt(q[u], jnp.where(qk_head == hd, k[u], 0.0)) * seg[hd] for hd in heads] for u in us]
        y_intra = [[_dot(sc[u][hd], v[u][:, hcols[hd]]) for hd in heads] for u in us]
        kv = [jnp.where(bd_mask, _dot_tn(v[u], k[u] * dk), 0.0) for u in us]
        st = s_scr[...]
        y_inter = []
        for u in us:
            y_inter.append(_dot_nt(q[u] * dq, st))
            st = st * ds + kv[u]
        s_scr[...] = st
        for u in us:
            for hd in heads:
                y = y_intra[u][hd] + y_inter[u][:, hcols[hd]]
                y = y * lax.rsqrt(jnp.mean(y * y, axis=-1, keepdims=True) + NORM_EPS)
                g = p_scr[rows[u], 1024 + hd * RET_DV:1024 + (hd + 1) * RET_DV]
                o_ref[rows[u], hcols[hd]] = (y * _silu(g)).astype(o_ref.dtype)
        return carry

    _loop(tb // (C * RET_GROUP), group)


def _retention(h, ret_pos, inv_row, w, perm, layer, batch, seq, interpret):
    tb = RET_CHUNK * RET_GROUP
    nt = seq // tb
    kern = functools.partial(_ret_kernel, tb=tb)
    return pl.pallas_call(
        kern,
        grid=(batch, nt),
        in_specs=[pl.BlockSpec((tb, D_MODEL), lambda b, i: (b * nt + i, 0)),
                  pl.BlockSpec((tb // RET_HEADS, LANES), lambda b, i: (b * nt + i, 0)),
                  _full_spec((1, LANES)),
                  _w_rows_spec(w, layer, _IN_OFFS[0], 1536), _full_spec(perm.shape)],
        out_specs=pl.BlockSpec((tb, WIDTH), lambda b, i: (b * nt + i, 0)),
        out_shape=jax.ShapeDtypeStruct((batch * seq, WIDTH), MXU_DTYPE),
        scratch_shapes=[pltpu.VMEM((tb, 1536), F32), pltpu.VMEM((WIDTH, 256), F32),
                        pltpu.VMEM((1536, D_MODEL), MXU_DTYPE)],
        compiler_params=_params(("arbitrary", "arbitrary")),
        interpret=interpret,
        name="retention",
    )(h, ret_pos, inv_row, w, perm)


def _gla_kernel(h_ref, w_ref, w2_ref, gb_ref, ng_ref, o_ref, p_scr, lg_scr, s_scr, wb_scr, *, tb):
    C = GLA_CHUNK

    @pl.when(_first_step())
    def _():
        cast = lambda v: v.astype(wb_scr.dtype)
        wb_scr[0:1024, :] = cast(w_ref[0, 0:1024, :])
        wb_scr[1024:1536, :] = cast(w_ref[0, 1024 + GLA_RANK:1536 + GLA_RANK, :])
        wb_scr[1536:1664, :] = cast(jnp.concatenate(
            [w_ref[0, 1024:1024 + GLA_RANK, :], jnp.zeros((LANES - GLA_RANK, D_MODEL), F32)], axis=0))

    @pl.when(pl.program_id(1) == 0)
    def _():
        s_scr[...] = jnp.zeros_like(s_scr)

    pre = _dot(_proj(h_ref[...], wb_scr[1536:1664, :]), w2_ref[...]) + gb_ref[...]
    lg_scr[...] = -_softplus(-pre) / GLA_NORMALIZER
    tri = _tril(C)
    assert tb == C * GLA_GROUP
    tri_sel = tri.astype(BF16)
    cum_all = [_sel_dot(tri_sel, lg_scr[u * C:(u + 1) * C, :]) for u in range(GLA_GROUP)]
    p_scr[:, 0:512] = _proj(h_ref[...], wb_scr[0:512, :])
    p_scr[:, 512:1536] = _proj(h_ref[...], wb_scr[512:1536, :])

    k_head = _iota((1, 256), 1) >> 6
    bd_mask = (_iota((WIDTH, 256), 0) >> 7) == (_iota((WIDTH, 256), 1) >> 6)
    ng = ng_ref[...]

    heads = range(GLA_HEADS)
    hcols = [slice(hd * GLA_DV, (hd + 1) * GLA_DV) for hd in heads]

    def group(gi, carry):
        us = range(GLA_GROUP)
        rows = [_rows(gi * GLA_GROUP + u, C) for u in us]
        q = [p_scr[rw, 0:256] * GLA_DK ** -0.5 for rw in rows]
        k = [p_scr[rw, 256:512] for rw in rows]
        v = [p_scr[rw, 512:1024] for rw in rows]
        cum = cum_all
        ref = [c[C // 2:C // 2 + 1, :] for c in cum]
        last = [c[C - 1:C, :] for c in cum]
        q_in = [q[u] * jnp.exp(cum[u] - ref[u]) for u in us]
        k_in = [k[u] * jnp.exp(ref[u] - cum[u]) for u in us]
        q_dec = [q[u] * jnp.exp(cum[u]) for u in us]
        k_st = [k[u] * jnp.exp(last[u] - cum[u]) for u in us]
        sc = [[jnp.where(tri, _dot_nt(q_in[u], jnp.where(k_head == hd, k_in[u], 0.0)), 0.0) for hd in heads]
              for u in us]
        y_intra = [[_dot(sc[u][hd], v[u][:, hcols[hd]]) for hd in heads] for u in us]
        kv = [jnp.where(bd_mask, _dot_tn(v[u], k_st[u]), 0.0) for u in us]
        st = s_scr[...]
        y_inter = []
        for u in us:
            y_inter.append(_dot_nt(q_dec[u], st))
            st = st * jnp.exp(last[u]) + kv[u]
        s_scr[...] = st
        for u in us:
            for hd in heads:
                y = y_intra[u][hd] + y_inter[u][:, hcols[hd]]
                y = y * lax.rsqrt(jnp.mean(y * y, axis=-1, keepdims=True) + NORM_EPS) * ng
                g = p_scr[rows[u], 1024 + hd * GLA_DV:1024 + (hd + 1) * GLA_DV]
                o_ref[rows[u], hcols[hd]] = (y * _silu(g)).astype(o_ref.dtype)
        return carry

    _loop(tb // (C * GLA_GROUP), group)


def _gla(h, w, w2p, gb, ng, layer, batch, seq, interpret):
    tb = GLA_CHUNK * GLA_GROUP
    nt = seq // tb
    kern = functools.partial(_gla_kernel, tb=tb)
    return pl.pallas_call(
        kern,
        grid=(batch, nt),
        in_specs=[pl.BlockSpec((tb, D_MODEL), lambda b, i: (b * nt + i, 0)),
                  _w_rows_spec(w, layer, _IN_OFFS[4], 1536 + GLA_RANK), _layer_spec(w2p, layer),
                  _layer_spec(gb, layer), _layer_spec(ng, layer)],
        out_specs=pl.BlockSpec((tb, WIDTH), lambda b, i: (b * nt + i, 0)),
        out_shape=jax.ShapeDtypeStruct((batch * seq, WIDTH), MXU_DTYPE),
        scratch_shapes=[pltpu.VMEM((tb, 1536), F32), pltpu.VMEM((tb, 256), F32),
                        pltpu.VMEM((WIDTH, 256), F32), pltpu.VMEM((1664, D_MODEL), MXU_DTYPE)],
        compiler_params=_params(("arbitrary", "arbitrary")),
        interpret=interpret,
        name="gla",
    )(h, w, w2p, gb, ng)


def _ssd_kernel(h_ref, w_ref, cw_ref, cb_ref, dtb_ref, a_ref, dskip_ref, ng_ref, exp_ref, o_ref,
                raw_scr, xc_scr, z_scr, dt_scr, s_scr, wb_scr, *, tb):
    C = SSD_CHUNK
    NCH = 1024

    @pl.when(_first_step())
    def _():
        cast = lambda v: v.astype(wb_scr.dtype)
        wb_scr[0:NCH, :] = cast(w_ref[0, 0:NCH, :])
        wb_scr[NCH:NCH + WIDTH, :] = cast(w_ref[0, NCH + SSD_HEADS:NCH + SSD_HEADS + WIDTH, :])
        wb_scr[1536:1664, :] = cast(jnp.concatenate(
            [w_ref[0, NCH:NCH + SSD_HEADS, :], jnp.zeros((LANES - SSD_HEADS, D_MODEL), F32)], axis=0))

    @pl.when(pl.program_id(1) == 0)
    def _():
        s_scr[...] = jnp.zeros_like(s_scr)
        raw_scr[0:CARRY_ROWS, :] = jnp.zeros((CARRY_ROWS, NCH), F32)

    dt_scr[...] = _softplus(_proj(h_ref[...], wb_scr[1536:1664, :]) + dtb_ref[...])
    first_tap = CARRY_ROWS - (SSD_CONV - 1)
    for lo in range(0, NCH, 256):
        cols = slice(lo, lo + 256)
        raw_scr[CARRY_ROWS:CARRY_ROWS + tb, cols] = _proj(h_ref[...], wb_scr[cols, :])
        conv = cb_ref[:, cols] + sum(raw_scr[first_tap + j:first_tap + j + tb, cols] * cw_ref[j:j + 1, cols]
                                     for j in range(SSD_CONV))
        xc_scr[:, cols] = _silu(conv)
        raw_scr[0:CARRY_ROWS, cols] = raw_scr[tb:tb + CARRY_ROWS, cols]
    z_scr[...] = _proj(h_ref[...], wb_scr[NCH:NCH + WIDTH, :])

    tri = _tril(C)
    lane_lo = _iota((1, LANES), 1) < SSD_P
    tri_sel = tri.astype(BF16)
    expand = exp_ref[...].astype(BF16)
    a_row = a_ref[...]

    groups = range(SSD_GROUPS)
    gcols = [slice(g * 256, (g + 1) * 256) for g in groups]
    ncols = [slice(g * SSD_STATE, (g + 1) * SSD_STATE) for g in groups]

    def group(gi, carry):
        us = range(SSD_GROUP)
        rows = [_rows(gi * SSD_GROUP + u, C) for u in us]
        xs = [xc_scr[rw, 0:512] for rw in rows]
        bm = [xc_scr[rw, 512:768] for rw in rows]
        cm = [xc_scr[rw, 768:1024] for rw in rows]
        dt = [dt_scr[rw, :] for rw in rows]
        cum = [_sel_dot(tri_sel, dt[u] * a_row) for u in us]
        cum_t = [c.T for c in cum]
        dt_e = [_dot_sel(d, expand) for d in dt]
        cum_e = [_dot_sel(c, expand) for c in cum]
        last_e = [c[C - 1:C, :] for c in cum_e]
        xdt = [xs[u] * dt_e[u] for u in us]
        v_st = [xdt[u] * jnp.exp(last_e[u] - cum_e[u]) for u in us]
        e_cum = [jnp.exp(c) for c in cum_e]
        scores = [[_dot_nt(cm[u][:, ncols[g]], bm[u][:, ncols[g]]) for g in groups] for u in us]
        y_intra = []
        for u in us:
            parts = []
            for pr in range(SSD_HEADS // 2):
                sc = scores[u][pr // 2]
                segs = []
                for hd in (2 * pr, 2 * pr + 1):
                    d = jnp.minimum(cum[u][:, hd:hd + 1] - cum_t[u][hd:hd + 1, :], 0.0)
                    segs.append(sc * jnp.where(tri, jnp.exp(d), 0.0))
                xp = xdt[u][:, pr * LANES:(pr + 1) * LANES]
                rhs = jnp.concatenate([jnp.where(lane_lo, xp, 0.0), jnp.where(lane_lo, 0.0, xp)], axis=0)
                parts.append(_dot(jnp.concatenate(segs, axis=1), rhs))
            y_intra.append(jnp.concatenate(parts, axis=1))
        kv = [[_dot_tn(bm[u][:, ncols[g]], v_st[u][:, gcols[g]]) for g in groups] for u in us]
        st = [s_scr[g] for g in groups]
        y_inter = []
        for u in us:
            y_inter.append(jnp.concatenate(
                [_dot(cm[u][:, ncols[g]], st[g]) * e_cum[u][:, gcols[g]] for g in groups], axis=1))
            st = [st[g] * jnp.exp(last_e[u][:, gcols[g]]) + kv[u][g] for g in groups]
        for g in groups:
            s_scr[g] = st[g]
        for u in us:
            y = y_intra[u] + y_inter[u] + dskip_ref[...] * xs[u]
            y = y * _silu(z_scr[rows[u], :])
            for g in groups:
                yg = y[:, gcols[g]]
                yg = yg * lax.rsqrt(jnp.mean(yg * yg, axis=-1, keepdims=True) + NORM_EPS)
                o_ref[rows[u], gcols[g]] = (yg * ng_ref[:, gcols[g]]).astype(o_ref.dtype)
        return carry

    _loop(tb // (C * SSD_GROUP), group)


def _ssd(h, w, cw, cb, dtb, a_row, dskip, ng, expand, layer, batch, seq, interpret):
    tb = SSD_CHUNK * SSD_GROUP
    nt = seq // tb
    kern = functools.partial(_ssd_kernel, tb=tb)
    small = [cw, cb, dtb, a_row, dskip, ng]
    return pl.pallas_call(
        kern,
        grid=(batch, nt),
        in_specs=[pl.BlockSpec((tb, D_MODEL), lambda b, i: (b * nt + i, 0)),
                  _w_rows_spec(w, layer, _IN_OFFS[9], 1536 + SSD_HEADS)]
        + [_layer_spec(a, layer) for a in small] + [_full_spec(expand.shape)],
        out_specs=pl.BlockSpec((tb, WIDTH), lambda b, i: (b * nt + i, 0)),
        out_shape=jax.ShapeDtypeStruct((batch * seq, WIDTH), MXU_DTYPE),
        scratch_shapes=[pltpu.VMEM((tb + CARRY_ROWS, 1024), F32), pltpu.VMEM((tb, 1024), F32),
                        pltpu.VMEM((tb, WIDTH), F32), pltpu.VMEM((tb, LANES), F32),
                        pltpu.VMEM((SSD_GROUPS, SSD_STATE, 256), F32), pltpu.VMEM((1664, D_MODEL), MXU_DTYPE)],
        compiler_params=_params(("arbitrary", "arbitrary")),
        interpret=interpret,
        name="ssd",
    )(h, w, *small, expand)


def _rwkv_kernel(h_ref, w_ref, mu_ref, w0_ref, w2_ref, a0_ref, a2_ref, kk_ref, ka_ref, rk_ref,
                 lng_ref, lnb_ref, seg_ref, o_ref,
                 u_scr, g_scr, r_scr, lw_scr, k_scr, v_scr, a_scr, b_scr, y_scr, bonus_scr, st_scr, wb_scr,
                 *, tb):
    C = RWKV_CHUNK
    NU = 3 * WIDTH + 2 * RWKV_RANK

    @pl.when(_first_step())
    def _():
        wb_scr[...] = w_ref[0].astype(wb_scr.dtype)

    @pl.when(pl.program_id(1) == 0)
    def _():
        st_scr[...] = jnp.zeros_like(st_scr)
        u_scr[0:CARRY_ROWS, :] = jnp.zeros((CARRY_ROWS, NU), F32)

    seg = seg_ref[...].astype(BF16)

    def seg_sum(x, pieces):
        return jnp.concatenate([_dot_sel(x[:, j * 256:(j + 1) * 256], seg, pieces) for j in range(2)], axis=1)

    def shifted(lo, hi):
        u = _proj(h_ref[...], wb_scr[lo:hi, :])
        u_scr[CARRY_ROWS:CARRY_ROWS + tb, lo:hi] = u
        u_prev = u_scr[CARRY_ROWS - 1:CARRY_ROWS - 1 + tb, lo:hi]
        u_scr[0:CARRY_ROWS, lo:hi] = u_scr[tb:tb + CARRY_ROWS, lo:hi]
        return u + (u_prev - u) * mu_ref[:, lo:hi]

    wa = shifted(1536, 1664)
    w_pre = w0_ref[...] + _dot(jnp.tanh(wa), w2_ref[...])
    a_gate = jax.nn.sigmoid(a0_ref[...] + _dot(wa, a2_ref[...]))
    lw_scr[...] = -jnp.exp(-_softplus(-w_pre) - 0.5)
    k = shifted(512, 1024)
    kk = k * kk_ref[...]
    k = k * (1.0 + (a_gate - 1.0) * ka_ref[...])
    kk = kk / jnp.maximum(jnp.sqrt(seg_sum(kk * kk, 1)), 1e-12)
    k_scr[...] = k
    a_scr[...] = -kk
    b_scr[...] = kk * a_gate
    r = shifted(0, 512)
    r_scr[...] = r
    v = shifted(1024, 1536)
    v_scr[...] = v
    bonus_scr[...] = seg_sum(r * k * rk_ref[...], 1) * v
    g_scr[...] = _proj(h_ref[...], wb_scr[NU:NU + WIDTH, :])

    tri_sel = _tril(C).astype(BF16)
    r2, c2 = _iota((2 * C, 2 * C), 0), _iota((2 * C, 2 * C), 1)
    same = (r2 >> 6) == (c2 >> 6)
    low_s = same & ((r2 & 63) > (c2 & 63))
    low_i = same & ((r2 & 63) >= (c2 & 63))
    eye = (r2 == c2).astype(F32)
    lane_lo = _iota((1, LANES), 1) < RWKV_N

    pairs = range(RWKV_HEADS // 2)

    def stack_pair(xp):
        return jnp.concatenate([jnp.where(lane_lo, xp, 0.0), jnp.where(lane_lo, 0.0, xp)], axis=0)

    def stack(x, pr):
        return stack_pair(x[:, pr * LANES:(pr + 1) * LANES])

    def state_free_part(chunks, tick):
        lanes = [(j, pr) for j in range(len(chunks)) for pr in pairs]
        rows, a_0, r_0, v_c, b_h, k_h, e_last, lhs, rhs_t = [], [], [], [], [], [], [], {}, {}
        for j, u in enumerate(chunks):
            rw = _rows(u, C)
            lw = lw_scr[rw, :]
            cum = _sel_dot(tri_sel, lw)
            cum_p = cum - lw
            ref = cum[C // 2:C // 2 + 1, :]
            last = cum[C - 1:C, :]
            e_fwd = jnp.exp(cum - ref)
            e_bwd = jnp.exp(ref - cum)
            e_end = jnp.exp(last - cum)
            r_c, k_c, a_c, b_c = r_scr[rw, :], k_scr[rw, :], a_scr[rw, :], b_scr[rw, :]
            r_t = r_c * e_fwd
            a_t = a_c * jnp.exp(cum_p - ref)
            b_t = b_c * e_bwd
            k_t = k_c * e_bwd
            for pr in pairs:
                lhs[j, pr] = jnp.concatenate([stack(a_t, pr), stack(r_t, pr)], axis=0)
                rhs_t[j, pr] = jnp.concatenate([stack(b_t, pr), stack(k_t, pr)], axis=0)
            rows.append(rw)
            a_0.append(a_c * jnp.exp(cum_p))
            r_0.append(r_c * jnp.exp(cum))
            v_c.append(v_scr[rw, :])
            b_h.append(b_c * e_end)
            k_h.append(k_c * e_end)
            e_last.append(jnp.exp(last))
        tick()
        big = [_dot_nt(lhs[ln], rhs_t[ln]) for ln in lanes]
        tick()
        a_ab = [jnp.where(low_s, m[0:128, 0:128], 0.0) for m in big]
        a_ak = [jnp.where(low_s, m[0:128, 128:256], 0.0) for m in big]
        a_rb = [jnp.where(low_i, m[128:256, 0:128], 0.0) for m in big]
        a_rk = [jnp.where(low_i, m[128:256, 128:256], 0.0) for m in big]
        inv = [eye + m for m in a_ab]
        pw = [_dot(m, m) for m in a_ab]
        tick()
        for _ in range(4):
            prod = [_dot(p, jnp.concatenate([p, t], axis=1)) for p, t in zip(pw, inv)]
            pw = [m[:, 0:128] for m in prod]
            inv = [t + m[:, 128:256] for t, m in zip(inv, prod)]
            tick()
        inv = [t + _dot(p, t) for p, t in zip(pw, inv)]
        tick()
        vs = [stack(v_c[j], pr) for j, pr in lanes]
        akv = [_dot(m, x) for m, x in zip(a_ak, vs)]
        tick()
        wu = [_dot(inv[i], jnp.concatenate([stack(a_0[j], pr), akv[i]], axis=1)) for i, (j, pr) in enumerate(lanes)]
        tick()
        w1 = [m[:, 0:LANES] for m in wu]
        u0v = [jnp.concatenate([m[:, LANES:2 * LANES], x], axis=0) for m, x in zip(wu, vs)]
        zero = jnp.zeros((2 * C, LANES), F32)
        xo = [_dot(jnp.concatenate([a_rb[i], a_rk[i]], axis=1),
                   jnp.concatenate([jnp.concatenate([w1[i], zero], axis=0), u0v[i]], axis=1))
              for i in range(len(lanes))]
        tick()
        fold = lambda m: m[0:C, :] + m[C:2 * C, :]
        w2 = [r_0[j][:, pr * LANES:(pr + 1) * LANES] + fold(xo[i][:, 0:LANES]) for i, (j, pr) in enumerate(lanes)]
        y0 = [fold(m[:, LANES:2 * LANES]) for m in xo]
        bh = [stack(b_h[j], pr) for j, pr in lanes]
        m_st = [_dot_tn(w1[i], bh[i]) for i in range(len(lanes))]
        tick()
        n_st = [_dot_tn(u0v[i], jnp.concatenate([bh[i], stack(k_h[j], pr)], axis=0))
                for i, (j, pr) in enumerate(lanes)]
        tick()
        return rows, e_last, w2, y0, m_st, n_st

    st = [st_scr[pr] for pr in pairs]

    def chain_steps(part):
        rows, e_last, w2, y0, m_st, n_st = part

        def step(j):
            for pr in pairs:
                i = j * len(pairs) + pr
                cols = slice(pr * LANES, (pr + 1) * LANES)
                y_scr[rows[j], cols] = _dot_nt(w2[i], st[pr]) + y0[i]
                st[pr] = st[pr] * e_last[j][:, cols] + _dot(st[pr], m_st[i]) + n_st[i]

        return [functools.partial(step, j) for j in range(len(rows))]

    n_chunks = tb // C
    first, second = list(range(n_chunks // 2)), list(range(n_chunks // 2, n_chunks))
    queue = chain_steps(state_free_part(first, lambda: None))

    def tick():
        if queue:
            queue.pop(0)()

    second_part = state_free_part(second, tick)
    for step in queue + chain_steps(second_part):
        step()
    for pr in pairs:
        st_scr[pr] = st[pr]

    y = y_scr[...]
    mean = seg_sum(y, 2) * (1.0 / RWKV_N)
    d = y - mean
    var = seg_sum(d * d, 1) * (1.0 / RWKV_N)
    y = d * lax.rsqrt(var + RWKV_LN_EPS) * lng_ref[...] + lnb_ref[...] + bonus_scr[...]
    o_ref[...] = (y * _silu(g_scr[...])).astype(o_ref.dtype)


def _rwkv(h, w, vecs, w2p, a2p, seg, layer, batch, seq, interpret):
    tb = min(TOKEN_TILE, seq)
    nt = seq // tb
    kern = functools.partial(_rwkv_kernel, tb=tb)
    mu, w0, a0, kk, ka, rk, lng, lnb = vecs
    ins = [mu, w0, w2p, a0, a2p, kk, ka, rk, lng, lnb]
    wide = lambda: pltpu.VMEM((tb, WIDTH), F32)
    n_rows = _IN_OFFS[14] - _IN_OFFS[12]
    return pl.pallas_call(
        kern,
        grid=(batch, nt),
        in_specs=[pl.BlockSpec((tb, D_MODEL), lambda b, i: (b * nt + i, 0)),
                  _w_rows_spec(w, layer, _IN_OFFS[12], n_rows)]
        + [_layer_spec(a, layer) for a in ins] + [_full_spec(seg.shape)],
        out_specs=pl.BlockSpec((tb, WIDTH), lambda b, i: (b * nt + i, 0)),
        out_shape=jax.ShapeDtypeStruct((batch * seq, WIDTH), MXU_DTYPE),
        scratch_shapes=[pltpu.VMEM((tb + CARRY_ROWS, 3 * WIDTH + 2 * RWKV_RANK), F32)]
        + [wide() for _ in range(9)]
        + [pltpu.VMEM((RWKV_HEADS // 2, LANES, LANES), F32), pltpu.VMEM((n_rows, D_MODEL), MXU_DTYPE)],
        compiler_params=_params(("arbitrary", "arbitrary")),
        interpret=interpret,
        name="rwkv7",
    )(h, w, *ins, seg)


def _memkv_kernel(mem_ref, g_ref, w_ref, k_ref, v_ref):
    x = mem_ref[0]
    y = x * lax.rsqrt(jnp.mean(x * x, axis=-1, keepdims=True) + NORM_EPS) * g_ref[...]
    kv = _dot(y, w_ref[...])
    head = _iota((1, MEM_WIDTH), 1) >> 6
    for hd in range(MEM_HEADS):
        k_ref[0, hd] = jnp.where(head == hd, kv[:, 0:MEM_WIDTH], 0.0).astype(k_ref.dtype)
        v_ref[0, hd] = jnp.where(head == hd, kv[:, MEM_WIDTH:2 * MEM_WIDTH], 0.0).astype(v_ref.dtype)


def _memkv(mem, g, w, layer, interpret):
    b, m, d = mem.shape
    out = jax.ShapeDtypeStruct((b, MEM_HEADS, m, MEM_WIDTH), MXU_DTYPE)
    return pl.pallas_call(
        _memkv_kernel,
        grid=(b,),
        in_specs=[pl.BlockSpec((1, m, d), lambda i: (i, 0, 0)), _layer_spec(g, layer), _layer_spec(w, layer)],
        out_specs=[pl.BlockSpec((1, MEM_HEADS, m, MEM_WIDTH), lambda i: (i, 0, 0, 0))] * 2,
        out_shape=[out, out],
        compiler_params=_params(("parallel",)),
        interpret=interpret,
        name="mem_kv",
    )(mem, g, w)


def _merge_kernel(x_ref, h_ref, oret_ref, ogla_ref, ossd_ref, orwkv_ref, km_ref, vm_ref,
                  wq_ref, wg_ref, uret_ref, ugla_ref, ussd_ref, urwkv_ref, umem_ref, wout_ref, gn_ref,
                  *refs):
    out_refs, wqb_scr = refs[:-1], refs[-1]

    @pl.when(_first_step())
    def _():
        wqb_scr[...] = wq_ref[0].astype(wqb_scr.dtype)

    h = h_ref[...]
    q = _proj(h, wqb_scr[...]) * MEM_HEAD_DIM ** -0.5
    scores = [_dot_nt(q, km_ref[0, hd]) for hd in range(MEM_HEADS)]
    branches = ((oret_ref, uret_ref), (ogla_ref, ugla_ref), (ossd_ref, ussd_ref), (orwkv_ref, urwkv_ref))
    merged = None
    for i in range(N_BRANCHES):
        gate = jax.nn.sigmoid(_proj(h, wg_ref[i * D_MODEL:(i + 1) * D_MODEL, :]))
        if i < 4:
            o_ref, u_ref = branches[i]
            up = jnp.dot(o_ref[...], u_ref[...], preferred_element_type=F32)
        else:
            up = _dot(o_mem, umem_ref[...])
        merged = gate * up if merged is None else merged + gate * up
        if i == 0:
            o_mem = jnp.zeros(q.shape, F32)
            for hd in range(MEM_HEADS):
                s = jnp.exp(scores[hd] - jnp.max(scores[hd], axis=-1, keepdims=True))
                prob = s / jnp.sum(s, axis=-1, keepdims=True)
                o_mem = o_mem + _dot(prob, vm_ref[0, hd])
    x = x_ref[...] + _dot(merged, wout_ref[...])
    if len(out_refs) == 2:
        out_refs[0][...] = x
    y = x * lax.rsqrt(jnp.mean(x * x, axis=-1, keepdims=True) + NORM_EPS) * gn_ref[...]
    out_refs[-1][...] = y.astype(out_refs[-1].dtype)


def _merge(x2d, h, o_ret, o_gla, o_ssd, o_rwkv, km, vm, wq, wg, ups, wout, g_next, layer, last,
           batch, seq, interpret):
    tm = min(TOKEN_TILE, seq)
    nt = seq // tm
    row = lambda w: pl.BlockSpec((tm, w), lambda b, i: (b * nt + i, 0))
    kvspec = pl.BlockSpec((1,) + km.shape[1:], lambda b, i: (b, 0, 0, 0))
    weights = [wg, *ups, wout, g_next]
    return pl.pallas_call(
        _merge_kernel,
        grid=(batch, nt),
        in_specs=[row(D_MODEL), row(D_MODEL), row(WIDTH), row(WIDTH), row(WIDTH), row(WIDTH), kvspec, kvspec,
                  _w_rows_spec(wq, layer, _IN_OFFS[14], MEM_WIDTH)]
        + [_layer_spec(w, layer, pipeline_mode=pl.Buffered(1)) for w in weights],
        out_specs=[row(D_MODEL)] if last else [row(D_MODEL), row(D_MODEL)],
        out_shape=[jax.ShapeDtypeStruct(x2d.shape, F32)] if last else
        [jax.ShapeDtypeStruct(x2d.shape, F32), jax.ShapeDtypeStruct(x2d.shape, MXU_DTYPE)],
        scratch_shapes=[pltpu.VMEM((MEM_WIDTH, D_MODEL), MXU_DTYPE)],
        compiler_params=_params(("arbitrary", "arbitrary")),
        interpret=interpret,
        name="merge",
    )(x2d, h, o_ret, o_gla, o_ssd, o_rwkv, km, vm, wq, *weights)


def _pad_last(a, width):
    return jnp.pad(a, [(0, 0)] * (a.ndim - 1) + [(0, width - a.shape[-1])])


def _rows3(v, width=None):
    v = v.reshape(v.shape[0], 1, -1).astype(F32)
    return v if width is None else _pad_last(v, width)


def _forward(x, mem, positions, norm_g, w_in, gla_gk_w2, gla_gk_b, gla_norm_g,
             ssd_conv_w, ssd_conv_b, ssd_dt_bias, ssd_a_log, ssd_d, ssd_norm_g,
             rwkv_mu, rwkv_w0, rwkv_w2, rwkv_a0, rwkv_a2, rwkv_k_k, rwkv_k_a, rwkv_r_k,
             rwkv_ln_g, rwkv_ln_b, mem_norm_g, w_mem_kv,
             w_up_ret, w_up_gla, w_up_ssd, w_up_rwkv, w_up_mem, w_out, final_norm_g, interpret=False):
    batch, seq, d = x.shape
    assert d == D_MODEL and seq % (RET_CHUNK * RET_GROUP) == 0 and seq % (GLA_CHUNK * GLA_GROUP) == 0
    depth = w_in.shape[0]
    cdt = MXU_DTYPE
    o = _IN_OFFS

    half = np.arange(RET_DK // 2)
    ret_perm = np.concatenate([hd * RET_DK + 2 * half + par for par in (0, 1) for hd in range(RET_HEADS)])
    inv = 1.0 / (ROPE_BASE ** jnp.linspace(0.0, 1.0, RET_DK // 2, dtype=F32))
    inv_row = jnp.tile(inv, RET_HEADS).reshape(1, LANES)
    ret_tile, ret_rows = RET_CHUNK * RET_GROUP, RET_CHUNK * RET_GROUP // RET_HEADS
    ret_pos = jnp.repeat(positions.reshape(batch, seq // ret_tile, RET_HEADS, ret_rows).transpose(0, 1, 3, 2),
                         RET_DK // 2, axis=3).reshape(batch * seq // RET_HEADS, LANES)
    head_of_lane = np.arange(WIDTH) // SSD_P
    ssd_expand = jnp.asarray(np.arange(LANES)[:, None] == head_of_lane[None, :], F32)
    rwkv_seg = jnp.asarray(head_of_lane[:256, None] == head_of_lane[None, :256], F32)

    w_t = jnp.swapaxes(w_in, 1, 2)
    w_g = w_t[:, o[15]:o[16], :].astype(cdt)
    ret_perm_rows = jnp.asarray(ret_perm[:, None] == np.arange(RET_HEADS * RET_DK)[None, :], cdt)
    gla_w2p = jnp.pad(gla_gk_w2, ((0, 0), (0, LANES - GLA_RANK), (0, 0))).astype(cdt)
    zeros_rank = jnp.zeros((depth, RWKV_RANK, WIDTH), F32)
    rwkv_w2p = jnp.concatenate([rwkv_w2, zeros_rank], axis=1).astype(cdt)
    rwkv_a2p = jnp.concatenate([zeros_rank, rwkv_a2], axis=1).astype(cdt)
    rwkv_vecs = [_rows3(v) for v in (rwkv_mu, rwkv_w0, rwkv_a0, rwkv_k_k, rwkv_k_a, rwkv_r_k,
                                     rwkv_ln_g, rwkv_ln_b)]
    ssd_small = [ssd_conv_w.astype(F32), _rows3(ssd_conv_b), _rows3(ssd_dt_bias, LANES),
                 _rows3(-jnp.exp(ssd_a_log.astype(F32)), LANES), _rows3(jnp.repeat(ssd_d, SSD_P, axis=1)),
                 _rows3(ssd_norm_g)]
    gla_b, gla_ng = _rows3(gla_gk_b), _rows3(gla_norm_g)
    mem_g, w_kv = _rows3(mem_norm_g), w_mem_kv.astype(cdt)
    ups = [w.astype(cdt) for w in (w_up_ret, w_up_gla, w_up_ssd, w_up_rwkv, w_up_mem)]
    w_o = w_out.astype(cdt)
    g_next = _rows3(jnp.concatenate([norm_g[1:], final_norm_g[None]], axis=0))

    x2d = x.reshape(batch * seq, d)
    h = _rmsnorm(x2d, norm_g[0], cdt, interpret)
    for l in range(depth):
        o_ret = _retention(h, ret_pos, inv_row, w_t, ret_perm_rows, l, batch, seq, interpret)
        o_gla = _gla(h, w_t, gla_w2p, gla_b, gla_ng, l, batch, seq, interpret)
        o_ssd = _ssd(h, w_t, *ssd_small, ssd_expand, l, batch, seq, interpret)
        o_rwkv = _rwkv(h, w_t, rwkv_vecs, rwkv_w2p, rwkv_a2p, rwkv_seg, l, batch, seq, interpret)
        km, vm = _memkv(mem, mem_g, w_kv, l, interpret)
        last = l == depth - 1
        outs = _merge(x2d, h, o_ret, o_gla, o_ssd, o_rwkv, km, vm, w_t, w_g, ups, w_o, g_next, l, last,
                      batch, seq, interpret)
        x2d, h = (None, outs[0]) if last else outs
    return h.reshape(batch, seq, d)


def kernel(x, mem, positions, norm_g, w_in, gla_gk_w2, gla_gk_b, gla_norm_g, ssd_conv_w, ssd_conv_b, ssd_dt_bias, ssd_a_log, ssd_d, ssd_norm_g, rwkv_mu, rwkv_w0, rwkv_w2, rwkv_a0, rwkv_a2, rwkv_k_k, rwkv_k_a, rwkv_r_k, rwkv_ln_g, rwkv_ln_b, mem_norm_g, w_mem_kv, w_up_ret, w_up_gla, w_up_ssd, w_up_rwkv, w_up_mem, w_out, final_norm_g):
    return _forward(x, mem, positions, norm_g, w_in, gla_gk_w2, gla_gk_b, gla_norm_g,
                    ssd_conv_w, ssd_conv_b, ssd_dt_bias, ssd_a_log, ssd_d, ssd_norm_g,
                    rwkv_mu, rwkv_w0, rwkv_w2, rwkv_a0, rwkv_a2, rwkv_k_k, rwkv_k_a, rwkv_r_k,
                    rwkv_ln_g, rwkv_ln_b, mem_norm_g, w_mem_kv,
                    w_up_ret, w_up_gla, w_up_ssd, w_up_rwkv, w_up_mem, w_out, final_norm_g)
```

```python
import functools
import math

import jax
import jax.numpy as jnp
import numpy as np
from jax import lax
from jax.experimental import pallas as pl
from jax.experimental.pallas import tpu as pltpu

F32 = jnp.float32
BF16 = jnp.bfloat16
MXU_DTYPE = jnp.bfloat16

D_MODEL = 1024
WIDTH = 512
NORM_EPS = 1e-6
N_BRANCHES = 5

RET_HEADS, RET_DK, RET_DV, RET_CHUNK = 4, 64, 128, 128
ROPE_BASE = 10000.0
GLA_HEADS, GLA_DK, GLA_DV, GLA_RANK, GLA_NORMALIZER, GLA_CHUNK = 4, 64, 128, 16, 16.0, 64
SSD_HEADS, SSD_P, SSD_GROUPS, SSD_STATE, SSD_CONV, SSD_CHUNK = 8, 64, 2, 128, 4, 128
RWKV_HEADS, RWKV_N, RWKV_RANK, RWKV_CHUNK = 8, 64, 64, 64
RWKV_LN_EPS = 64e-5
RET_GROUP = 8
SSD_GROUP = 4
GLA_GROUP = 16
MEM_HEADS, MEM_HEAD_DIM, MEM_WIDTH = 4, 64, 256

LANES = 128
CARRY_ROWS = 8
TOKEN_TILE = 512
VMEM_LIMIT = 56 * 1024 * 1024

_IN_SIZES = (256, 256, 512, 512, 256, 256, 512, 16, 512, 1024, 8, 512, 1664, 512, 256, 5120)
_IN_OFFS = tuple(int(v) for v in np.cumsum((0,) + _IN_SIZES))


def _dot(a, b):
    return jnp.dot(a.astype(MXU_DTYPE), b.astype(MXU_DTYPE), preferred_element_type=F32)


def _dot_nt(a, b):
    return lax.dot_general(a.astype(MXU_DTYPE), b.astype(MXU_DTYPE), (((1,), (1,)), ((), ())),
                           preferred_element_type=F32)


def _dot_tn(a, b):
    return lax.dot_general(a.astype(MXU_DTYPE), b.astype(MXU_DTYPE), (((0,), (0,)), ((), ())),
                           preferred_element_type=F32)


def _proj(x, w_t):
    return lax.dot_general(x, w_t, (((1,), (1,)), ((), ())), preferred_element_type=F32)


def _split3(x):
    hi = x.astype(BF16)
    r1 = x - hi.astype(F32)
    mid = r1.astype(BF16)
    lo = (r1 - mid.astype(F32)).astype(BF16)
    return hi, mid, lo


def _sel_dot(sel, x):
    return sum(jnp.dot(sel, p, preferred_element_type=F32) for p in _split3(x))


def _dot_sel(x, sel, pieces=3):
    return sum(jnp.dot(p, sel, preferred_element_type=F32) for p in _split3(x)[:pieces])


def _iota(shape, dim):
    return lax.broadcasted_iota(jnp.int32, shape, dim)


def _tril(n, strict=False):
    r, c = _iota((n, n), 0), _iota((n, n), 1)
    return (r > c) if strict else (r >= c)


def _silu(x):
    return x * jax.nn.sigmoid(x)


def _softplus(x):
    return jnp.maximum(x, 0.0) + jnp.log(1.0 + jnp.exp(-jnp.abs(x)))


def _rows(c, n):
    return pl.ds(pl.multiple_of(c * n, n), n)


def _loop(trips, body):
    if trips == 1:
        body(0, 0)
    else:
        lax.fori_loop(0, trips, body, 0)


def _full_spec(shape):
    zeros = (0,) * len(shape)
    return pl.BlockSpec(shape, lambda *_: zeros)


def _layer_spec(arr, layer, **kwargs):
    tail = tuple(arr.shape[1:])
    index = (layer,) + (0,) * len(tail)
    return pl.BlockSpec((None,) + tail, lambda *_: index, **kwargs)


def _w_rows_spec(w_t, layer, row0, rows):
    return pl.BlockSpec((pl.Element(1), pl.Element(rows), pl.Element(w_t.shape[2])),
                        lambda *_: (layer, row0, 0), pipeline_mode=pl.Buffered(1))


def _first_step():
    return (pl.program_id(0) == 0) & (pl.program_id(1) == 0)


def _params(semantics):
    return pltpu.CompilerParams(dimension_semantics=semantics, vmem_limit_bytes=VMEM_LIMIT)


def _rmsnorm_kernel(x_ref, g_ref, o_ref):
    x = x_ref[...]
    y = x * lax.rsqrt(jnp.mean(x * x, axis=-1, keepdims=True) + NORM_EPS)
    o_ref[...] = (y * g_ref[...]).astype(o_ref.dtype)


def _rmsnorm(x2d, g, out_dtype, interpret):
    m, d = x2d.shape
    tm = min(1024, m)
    return pl.pallas_call(
        _rmsnorm_kernel,
        grid=(m // tm,),
        in_specs=[pl.BlockSpec((tm, d), lambda i: (i, 0)), _full_spec((1, d))],
        out_specs=pl.BlockSpec((tm, d), lambda i: (i, 0)),
        out_shape=jax.ShapeDtypeStruct((m, d), out_dtype),
        compiler_params=_params(("parallel",)),
        interpret=interpret,
        name="rmsnorm",
    )(x2d, g.reshape(1, d))


def _ret_kernel(h_ref, pos_ref, inv_ref, w_ref, perm_ref, o_ref, p_scr, s_scr, wb_scr, *, tb):
    C = RET_CHUNK

    @pl.when(_first_step())
    def _():
        for lo in (0, 256):
            rows = w_ref[0, lo:lo + 256, :].astype(wb_scr.dtype)
            wb_scr[lo:lo + 256, :] = jnp.dot(perm_ref[...], rows, preferred_element_type=F32).astype(wb_scr.dtype)
        wb_scr[512:1536, :] = w_ref[0, 512:1536, :].astype(wb_scr.dtype)

    @pl.when(pl.program_id(1) == 0)
    def _():
        s_scr[...] = jnp.zeros_like(s_scr)

    ang = pos_ref[...].astype(F32) * inv_ref[...]
    lane_group = _iota((1, LANES), 1) >> 5

    def unpack(t):
        blocks = []
        for j in range(RET_HEADS):
            own = jnp.where(lane_group == j, t, 0.0)
            blocks.append(own + pltpu.roll(own, 32, 1) + pltpu.roll(own, 64, 1) + pltpu.roll(own, 96, 1))
        return jnp.concatenate(blocks, axis=0)

    cos, sin = unpack(jnp.cos(ang)), unpack(jnp.sin(ang))
    p_scr[:, 512:1536] = _proj(h_ref[...], wb_scr[512:1536, :])
    p = _proj(h_ref[...], wb_scr[0:512, :])
    q1, q2 = p[:, 0:128], p[:, 128:256]
    k1, k2 = p[:, 256:384] * RET_DK ** -0.5, p[:, 384:512] * RET_DK ** -0.5
    p_scr[:, 0:128] = q1 * cos - q2 * sin
    p_scr[:, 128:256] = q2 * cos + q1 * sin
    p_scr[:, 256:384] = k1 * cos - k2 * sin
    p_scr[:, 384:512] = k2 * cos + k1 * sin

    def log_gamma(head):
        return jnp.log(1.0 - jnp.exp2(-5.0 - head.astype(F32)))

    qk_head = (_iota((1, 256), 1) >> 5) & 3
    lg_lane = log_gamma(qk_head)
    tau = _iota((C, 1), 0).astype(F32)
    dq = jnp.exp(lg_lane * (tau + 1.0))
    dk = jnp.exp(lg_lane * (C - 1.0 - tau))
    ds = jnp.exp(lg_lane * float(C))
    diff = (_iota((C, C), 0) - _iota((C, C), 1)).astype(F32)
    causal = _tril(C)
    bd_mask = (_iota((WIDTH, 256), 0) >> 7) == ((_iota((WIDTH, 256), 1) >> 5) & 3)

    heads = range(RET_HEADS)
    hcols = [slice(hd * RET_DV, (hd + 1) * RET_DV) for hd in heads]
    seg = [jnp.where(causal, jnp.exp(math.log(1.0 - 2.0 ** (-5.0 - hd)) * diff), 0.0) for hd in heads]

    def group(gi, carry):
        us = range(RET_GROUP)
        rows = [_rows(gi * RET_GROUP + u, C) for u in us]
        q = [p_scr[rw, 0:256] for rw in rows]
        k = [p_scr[rw, 256:512] for rw in rows]
        v = [p_scr[rw, 512:1024] for rw in rows]
        sc = [[_dot_nt(q[u], jnp.where(qk_head == hd, k[u], 0.0)) * seg[hd] for hd in heads] for u in us]
        y_intra = [[_dot(sc[u][hd], v[u][:, hcols[hd]]) for hd in heads] for u in us]
        kv = [jnp.where(bd_mask, _dot_tn(v[u], k[u] * dk), 0.0) for u in us]
        st = s_scr[...]
        y_inter = []
        for u in us:
            y_inter.append(_dot_nt(q[u] * dq, st))
            st = st * ds + kv[u]
        s_scr[...] = st
        for u in us:
            for hd in heads:
                y = y_intra[u][hd] + y_inter[u][:, hcols[hd]]
                y = y * lax.rsqrt(jnp.mean(y * y, axis=-1, keepdims=True) + NORM_EPS)
                g = p_scr[rows[u], 1024 + hd * RET_DV:1024 + (hd + 1) * RET_DV]
                o_ref[rows[u], hcols[hd]] = (y * _silu(g)).astype(o_ref.dtype)
        return carry

    _loop(tb // (C * RET_GROUP), group)


def _retention(h, ret_pos, inv_row, w, perm, layer, batch, seq, interpret):
    tb = RET_CHUNK * RET_GROUP
    nt = seq // tb
    kern = functools.partial(_ret_kernel, tb=tb)
    return pl.pallas_call(
        kern,
        grid=(batch, nt),
        in_specs=[pl.BlockSpec((tb, D_MODEL), lambda b, i: (b * nt + i, 0)),
                  pl.BlockSpec((tb // RET_HEADS, LANES), lambda b, i: (b * nt + i, 0)),
                  _full_spec((1, LANES)),
                  _w_rows_spec(w, layer, _IN_OFFS[0], 1536), _full_spec(perm.shape)],
        out_specs=pl.BlockSpec((tb, WIDTH), lambda b, i: (b * nt + i, 0)),
        out_shape=jax.ShapeDtypeStruct((batch * seq, WIDTH), MXU_DTYPE),
        scratch_shapes=[pltpu.VMEM((tb, 1536), F32), pltpu.VMEM((WIDTH, 256), F32),
                        pltpu.VMEM((1536, D_MODEL), MXU_DTYPE)],
        compiler_params=_params(("arbitrary", "arbitrary")),
        interpret=interpret,
        name="retention",
    )(h, ret_pos, inv_row, w, perm)


def _gla_kernel(h_ref, w_ref, w2_ref, gb_ref, ng_ref, o_ref, p_scr, lg_scr, s_scr, wb_scr, *, tb):
    C = GLA_CHUNK

    @pl.when(_first_step())
    def _():
        cast = lambda v: v.astype(wb_scr.dtype)
        wb_scr[0:1024, :] = cast(w_ref[0, 0:1024, :])
        wb_scr[1024:1536, :] = cast(w_ref[0, 1024 + GLA_RANK:1536 + GLA_RANK, :])
        wb_scr[1536:1664, :] = cast(jnp.concatenate(
            [w_ref[0, 1024:1024 + GLA_RANK, :], jnp.zeros((LANES - GLA_RANK, D_MODEL), F32)], axis=0))

    @pl.when(pl.program_id(1) == 0)
    def _():
        s_scr[...] = jnp.zeros_like(s_scr)

    pre = _dot(_proj(h_ref[...], wb_scr[1536:1664, :]), w2_ref[...]) + gb_ref[...]
    lg_scr[...] = -_softplus(-pre) / GLA_NORMALIZER
    tri = _tril(C)
    assert tb == C * GLA_GROUP
    tri_sel = tri.astype(BF16)
    cum_all = [_sel_dot(tri_sel, lg_scr[u * C:(u + 1) * C, :]) for u in range(GLA_GROUP)]
    p_scr[:, 0:512] = _proj(h_ref[...], wb_scr[0:512, :])
    p_scr[:, 512:1536] = _proj(h_ref[...], wb_scr[512:1536, :])

    k_head = _iota((1, 256), 1) >> 6
    pairs = range(GLA_HEADS // 2)
    bd_mask = (_iota((256, LANES), 0) >> 7) == (_iota((256, LANES), 1) >> 6)
    ng = ng_ref[...]

    heads = range(GLA_HEADS)
    hcols = [slice(hd * GLA_DV, (hd + 1) * GLA_DV) for hd in heads]

    def group(gi, carry):
        us = range(GLA_GROUP)
        rows = [_rows(gi * GLA_GROUP + u, C) for u in us]
        q = [p_scr[rw, 0:256] * GLA_DK ** -0.5 for rw in rows]
        k = [p_scr[rw, 256:512] for rw in rows]
        v = [p_scr[rw, 512:1024] for rw in rows]
        cum = cum_all
        ref = [c[C // 2:C // 2 + 1, :] for c in cum]
        last = [c[C - 1:C, :] for c in cum]
        q_in = [q[u] * jnp.exp(cum[u] - ref[u]) for u in us]
        k_in = [k[u] * jnp.exp(ref[u] - cum[u]) for u in us]
        q_dec = [q[u] * jnp.exp(cum[u]) for u in us]
        k_st = [k[u] * jnp.exp(last[u] - cum[u]) for u in us]
        sc = [[jnp.where(tri, _dot_nt(q_in[u], jnp.where(k_head == hd, k_in[u], 0.0)), 0.0) for hd in heads]
              for u in us]
        y_intra = [[_dot(sc[u][hd], v[u][:, hcols[hd]]) for hd in heads] for u in us]
        vcols = [slice(p * 256, (p + 1) * 256) for p in pairs]
        kcols = [slice(p * LANES, (p + 1) * LANES) for p in pairs]
        kv = [[jnp.where(bd_mask, _dot_tn(v[u][:, vcols[p]], k_st[u][:, kcols[p]]), 0.0) for p in pairs]
              for u in us]
        st = [s_scr[p] for p in pairs]
        y_inter = []
        for u in us:
            y_inter.append(jnp.concatenate([_dot_nt(q_dec[u][:, kcols[p]], st[p]) for p in pairs], axis=1))
            st = [st[p] * jnp.exp(last[u][:, kcols[p]]) + kv[u][p] for p in pairs]
        for p in pairs:
            s_scr[p] = st[p]
        for u in us:
            for hd in heads:
                y = y_intra[u][hd] + y_inter[u][:, hcols[hd]]
                y = y * lax.rsqrt(jnp.mean(y * y, axis=-1, keepdims=True) + NORM_EPS) * ng
                g = p_scr[rows[u], 1024 + hd * GLA_DV:1024 + (hd + 1) * GLA_DV]
                o_ref[rows[u], hcols[hd]] = (y * _silu(g)).astype(o_ref.dtype)
        return carry

    _loop(tb // (C * GLA_GROUP), group)


def _gla(h, w, w2p, gb, ng, layer, batch, seq, interpret):
    tb = GLA_CHUNK * GLA_GROUP
    nt = seq // tb
    kern = functools.partial(_gla_kernel, tb=tb)
    return pl.pallas_call(
        kern,
        grid=(batch, nt),
        in_specs=[pl.BlockSpec((tb, D_MODEL), lambda b, i: (b * nt + i, 0)),
                  _w_rows_spec(w, layer, _IN_OFFS[4], 1536 + GLA_RANK), _layer_spec(w2p, layer),
                  _layer_spec(gb, layer), _layer_spec(ng, layer)],
        out_specs=pl.BlockSpec((tb, WIDTH), lambda b, i: (b * nt + i, 0)),
        out_shape=jax.ShapeDtypeStruct((batch * seq, WIDTH), MXU_DTYPE),
        scratch_shapes=[pltpu.VMEM((tb, 1536), F32), pltpu.VMEM((tb, 256), F32),
                        pltpu.VMEM((GLA_HEADS // 2, 256, LANES), F32), pltpu.VMEM((1664, D_MODEL), MXU_DTYPE)],
        compiler_params=_params(("arbitrary", "arbitrary")),
        interpret=interpret,
        name="gla",
    )(h, w, w2p, gb, ng)


def _ssd_kernel(h_ref, w_ref, cw_ref, cb_ref, dtb_ref, a_ref, dskip_ref, ng_ref, exp_ref, o_ref,
                raw_scr, xc_scr, z_scr, dt_scr, s_scr, wb_scr, *, tb):
    C = SSD_CHUNK
    NCH = 1024

    @pl.when(_first_step())
    def _():
        cast = lambda v: v.astype(wb_scr.dtype)
        wb_scr[0:NCH, :] = cast(w_ref[0, 0:NCH, :])
        wb_scr[NCH:NCH + WIDTH, :] = cast(w_ref[0, NCH + SSD_HEADS:NCH + SSD_HEADS + WIDTH, :])
        wb_scr[1536:1664, :] = cast(jnp.concatenate(
            [w_ref[0, NCH:NCH + SSD_HEADS, :], jnp.zeros((LANES - SSD_HEADS, D_MODEL), F32)], axis=0))

    @pl.when(pl.program_id(1) == 0)
    def _():
        s_scr[...] = jnp.zeros_like(s_scr)
        raw_scr[0:CARRY_ROWS, :] = jnp.zeros((CARRY_ROWS, NCH), F32)

    dt_scr[...] = _softplus(_proj(h_ref[...], wb_scr[1536:1664, :]) + dtb_ref[...])
    first_tap = CARRY_ROWS - (SSD_CONV - 1)
    for lo in range(0, NCH, 256):
        cols = slice(lo, lo + 256)
        raw_scr[CARRY_ROWS:CARRY_ROWS + tb, cols] = _proj(h_ref[...], wb_scr[cols, :])
        conv = cb_ref[:, cols] + sum(raw_scr[first_tap + j:first_tap + j + tb, cols] * cw_ref[j:j + 1, cols]
                                     for j in range(SSD_CONV))
        xc_scr[:, cols] = _silu(conv)
        raw_scr[0:CARRY_ROWS, cols] = raw_scr[tb:tb + CARRY_ROWS, cols]
    z_scr[...] = _proj(h_ref[...], wb_scr[NCH:NCH + WIDTH, :])

    tri = _tril(C)
    lane_lo = _iota((1, LANES), 1) < SSD_P
    tri_sel = tri.astype(BF16)
    expand = exp_ref[...].astype(BF16)
    a_row = a_ref[...]

    groups = range(SSD_GROUPS)
    gcols = [slice(g * 256, (g + 1) * 256) for g in groups]
    ncols = [slice(g * SSD_STATE, (g + 1) * SSD_STATE) for g in groups]

    def group(gi, carry):
        us = range(SSD_GROUP)
        rows = [_rows(gi * SSD_GROUP + u, C) for u in us]
        xs = [xc_scr[rw, 0:512] for rw in rows]
        bm = [xc_scr[rw, 512:768] for rw in rows]
        cm = [xc_scr[rw, 768:1024] for rw in rows]
        dt = [dt_scr[rw, :] for rw in rows]
        cum = [_sel_dot(tri_sel, dt[u] * a_row) for u in us]
        cum_t = [c.T for c in cum]
        dt_e = [_dot_sel(d, expand) for d in dt]
        cum_e = [_dot_sel(c, expand) for c in cum]
        last_e = [c[C - 1:C, :] for c in cum_e]
        xdt = [xs[u] * dt_e[u] for u in us]
        v_st = [xdt[u] * jnp.exp(last_e[u] - cum_e[u]) for u in us]
        e_cum = [jnp.exp(c) for c in cum_e]
        scores = [[_dot_nt(cm[u][:, ncols[g]], bm[u][:, ncols[g]]) for g in groups] for u in us]
        y_intra = []
        for u in us:
            parts = []
            for pr in range(SSD_HEADS // 2):
                sc = scores[u][pr // 2]
                segs = []
                for hd in (2 * pr, 2 * pr + 1):
                    d = jnp.minimum(cum[u][:, hd:hd + 1] - cum_t[u][hd:hd + 1, :], 0.0)
                    segs.append(sc * jnp.where(tri, jnp.exp(d), 0.0))
                xp = xdt[u][:, pr * LANES:(pr + 1) * LANES]
                rhs = jnp.concatenate([jnp.where(lane_lo, xp, 0.0), jnp.where(lane_lo, 0.0, xp)], axis=0)
                parts.append(_dot(jnp.concatenate(segs, axis=1), rhs))
            y_intra.append(jnp.concatenate(parts, axis=1))
        kv = [[_dot_tn(bm[u][:, ncols[g]], v_st[u][:, gcols[g]]) for g in groups] for u in us]
        st = [s_scr[g] for g in groups]
        y_inter = []
        for u in us:
            y_inter.append(jnp.concatenate(
                [_dot(cm[u][:, ncols[g]], st[g]) * e_cum[u][:, gcols[g]] for g in groups], axis=1))
            st = [st[g] * jnp.exp(last_e[u][:, gcols[g]]) + kv[u][g] for g in groups]
        for g in groups:
            s_scr[g] = st[g]
        for u in us:
            y = y_intra[u] + y_inter[u] + dskip_ref[...] * xs[u]
            y = y * _silu(z_scr[rows[u], :])
            for g in groups:
                yg = y[:, gcols[g]]
                yg = yg * lax.rsqrt(jnp.mean(yg * yg, axis=-1, keepdims=True) + NORM_EPS)
                o_ref[rows[u], gcols[g]] = (yg * ng_ref[:, gcols[g]]).astype(o_ref.dtype)
        return carry

    _loop(tb // (C * SSD_GROUP), group)


def _ssd(h, w, cw, cb, dtb, a_row, dskip, ng, expand, layer, batch, seq, interpret):
    tb = SSD_CHUNK * SSD_GROUP
    nt = seq // tb
    kern = functools.partial(_ssd_kernel, tb=tb)
    small = [cw, cb, dtb, a_row, dskip, ng]
    return pl.pallas_call(
        kern,
        grid=(batch, nt),
        in_specs=[pl.BlockSpec((tb, D_MODEL), lambda b, i: (b * nt + i, 0)),
                  _w_rows_spec(w, layer, _IN_OFFS[9], 1536 + SSD_HEADS)]
        + [_layer_spec(a, layer) for a in small] + [_full_spec(expand.shape)],
        out_specs=pl.BlockSpec((tb, WIDTH), lambda b, i: (b * nt + i, 0)),
        out_shape=jax.ShapeDtypeStruct((batch * seq, WIDTH), MXU_DTYPE),
        scratch_shapes=[pltpu.VMEM((tb + CARRY_ROWS, 1024), F32), pltpu.VMEM((tb, 1024), F32),
                        pltpu.VMEM((tb, WIDTH), F32), pltpu.VMEM((tb, LANES), F32),
                        pltpu.VMEM((SSD_GROUPS, SSD_STATE, 256), F32), pltpu.VMEM((1664, D_MODEL), MXU_DTYPE)],
        compiler_params=_params(("arbitrary", "arbitrary")),
        interpret=interpret,
        name="ssd",
    )(h, w, *small, expand)


def _rwkv_kernel(h_ref, w_ref, mu_ref, w0_ref, w2_ref, a0_ref, a2_ref, kk_ref, ka_ref, rk_ref,
                 lng_ref, lnb_ref, seg_ref, o_ref,
                 u_scr, g_scr, r_scr, lw_scr, k_scr, v_scr, a_scr, b_scr, y_scr, bonus_scr, st_scr, wb_scr,
                 *, tb):
    C = RWKV_CHUNK
    NU = 3 * WIDTH + 2 * RWKV_RANK

    @pl.when(_first_step())
    def _():
        wb_scr[...] = w_ref[0].astype(wb_scr.dtype)

    @pl.when(pl.program_id(1) == 0)
    def _():
        st_scr[...] = jnp.zeros_like(st_scr)
        u_scr[0:CARRY_ROWS, :] = jnp.zeros((CARRY_ROWS, NU), F32)

    seg = seg_ref[...].astype(BF16)

    def seg_sum(x, pieces):
        return jnp.concatenate([_dot_sel(x[:, j * 256:(j + 1) * 256], seg, pieces) for j in range(2)], axis=1)

    def shifted(lo, hi):
        u = _proj(h_ref[...], wb_scr[lo:hi, :])
        u_scr[CARRY_ROWS:CARRY_ROWS + tb, lo:hi] = u
        u_prev = u_scr[CARRY_ROWS - 1:CARRY_ROWS - 1 + tb, lo:hi]
        u_scr[0:CARRY_ROWS, lo:hi] = u_scr[tb:tb + CARRY_ROWS, lo:hi]
        return u + (u_prev - u) * mu_ref[:, lo:hi]

    wa = shifted(1536, 1664)
    w_pre = w0_ref[...] + _dot(jnp.tanh(wa), w2_ref[...])
    a_gate = jax.nn.sigmoid(a0_ref[...] + _dot(wa, a2_ref[...]))
    lw_scr[...] = -jnp.exp(-_softplus(-w_pre) - 0.5)
    k = shifted(512, 1024)
    kk = k * kk_ref[...]
    k = k * (1.0 + (a_gate - 1.0) * ka_ref[...])
    kk = kk / jnp.maximum(jnp.sqrt(seg_sum(kk * kk, 1)), 1e-12)
    k_scr[...] = k
    a_scr[...] = -kk
    b_scr[...] = kk * a_gate
    r = shifted(0, 512)
    r_scr[...] = r
    v = shifted(1024, 1536)
    v_scr[...] = v
    bonus_scr[...] = seg_sum(r * k * rk_ref[...], 1) * v
    g_scr[...] = _proj(h_ref[...], wb_scr[NU:NU + WIDTH, :])

    tri_sel = _tril(C).astype(BF16)
    r2, c2 = _iota((2 * C, 2 * C), 0), _iota((2 * C, 2 * C), 1)
    same = (r2 >> 6) == (c2 >> 6)
    low_s = same & ((r2 & 63) > (c2 & 63))
    low_i = same & ((r2 & 63) >= (c2 & 63))
    eye = (r2 == c2).astype(F32)
    lane_lo = _iota((1, LANES), 1) < RWKV_N

    pairs = range(RWKV_HEADS // 2)

    def stack_pair(xp):
        return jnp.concatenate([jnp.where(lane_lo, xp, 0.0), jnp.where(lane_lo, 0.0, xp)], axis=0)

    def stack(x, pr):
        return stack_pair(x[:, pr * LANES:(pr + 1) * LANES])

    def state_free_part(chunks, tick):
        lanes = [(j, pr) for j in range(len(chunks)) for pr in pairs]
        rows, a_0, r_0, v_c, b_h, k_h, e_last, lhs, rhs_t = [], [], [], [], [], [], [], {}, {}
        for j, u in enumerate(chunks):
            rw = _rows(u, C)
            lw = lw_scr[rw, :]
            cum = _sel_dot(tri_sel, lw)
            cum_p = cum - lw
            ref = cum[C // 2:C // 2 + 1, :]
            last = cum[C - 1:C, :]
            e_fwd = jnp.exp(cum - ref)
            e_bwd = jnp.exp(ref - cum)
            e_end = jnp.exp(last - cum)
            r_c, k_c, a_c, b_c = r_scr[rw, :], k_scr[rw, :], a_scr[rw, :], b_scr[rw, :]
            r_t = r_c * e_fwd
            a_t = a_c * jnp.exp(cum_p - ref)
            b_t = b_c * e_bwd
            k_t = k_c * e_bwd
            for pr in pairs:
                lhs[j, pr] = jnp.concatenate([stack(a_t, pr), stack(r_t, pr)], axis=0)
                rhs_t[j, pr] = jnp.concatenate([stack(b_t, pr), stack(k_t, pr)], axis=0)
            rows.append(rw)
            a_0.append(a_c * jnp.exp(cum_p))
            r_0.append(r_c * jnp.exp(cum))
            v_c.append(v_scr[rw, :])
            b_h.append(b_c * e_end)
            k_h.append(k_c * e_end)
            e_last.append(jnp.exp(last))
        tick()
        big = [_dot_nt(lhs[ln], rhs_t[ln]) for ln in lanes]
        tick()
        a_ab = [jnp.where(low_s, m[0:128, 0:128], 0.0) for m in big]
        a_ak = [jnp.where(low_s, m[0:128, 128:256], 0.0) for m in big]
        a_rb = [jnp.where(low_i, m[128:256, 0:128], 0.0) for m in big]
        a_rk = [jnp.where(low_i, m[128:256, 128:256], 0.0) for m in big]
        inv = [eye + m for m in a_ab]
        pw = [_dot(m, m) for m in a_ab]
        tick()
        for _ in range(4):
            prod = [_dot(p, jnp.concatenate([p, t], axis=1)) for p, t in zip(pw, inv)]
            pw = [m[:, 0:128] for m in prod]
            inv = [t + m[:, 128:256] for t, m in zip(inv, prod)]
            tick()
        inv = [t + _dot(p, t) for p, t in zip(pw, inv)]
        tick()
        vs = [stack(v_c[j], pr) for j, pr in lanes]
        akv = [_dot(m, x) for m, x in zip(a_ak, vs)]
        tick()
        wu = [_dot(inv[i], jnp.concatenate([stack(a_0[j], pr), akv[i]], axis=1)) for i, (j, pr) in enumerate(lanes)]
        tick()
        w1 = [m[:, 0:LANES] for m in wu]
        u0v = [jnp.concatenate([m[:, LANES:2 * LANES], x], axis=0) for m, x in zip(wu, vs)]
        zero = jnp.zeros((2 * C, LANES), F32)
        xo = [_dot(jnp.concatenate([a_rb[i], a_rk[i]], axis=1),
                   jnp.concatenate([jnp.concatenate([w1[i], zero], axis=0), u0v[i]], axis=1))
              for i in range(len(lanes))]
        tick()
        fold = lambda m: m[0:C, :] + m[C:2 * C, :]
        w2 = [r_0[j][:, pr * LANES:(pr + 1) * LANES] + fold(xo[i][:, 0:LANES]) for i, (j, pr) in enumerate(lanes)]
        y0 = [fold(m[:, LANES:2 * LANES]) for m in xo]
        bh = [stack(b_h[j], pr) for j, pr in lanes]
        m_st = [_dot_tn(w1[i], bh[i]) for i in range(len(lanes))]
        tick()
        n_st = [_dot_tn(u0v[i], jnp.concatenate([bh[i], stack(k_h[j], pr)], axis=0))
                for i, (j, pr) in enumerate(lanes)]
        tick()
        return rows, e_last, w2, y0, m_st, n_st

    st = [st_scr[pr] for pr in pairs]

    def chain_steps(part):
        rows, e_last, w2, y0, m_st, n_st = part

        def step(j):
            for pr in pairs:
                i = j * len(pairs) + pr
                cols = slice(pr * LANES, (pr + 1) * LANES)
                y_scr[rows[j], cols] = _dot_nt(w2[i], st[pr]) + y0[i]
                st[pr] = st[pr] * e_last[j][:, cols] + _dot(st[pr], m_st[i]) + n_st[i]

        return [functools.partial(step, j) for j in range(len(rows))]

    n_chunks = tb // C
    first, second = list(range(n_chunks // 2)), list(range(n_chunks // 2, n_chunks))
    queue = chain_steps(state_free_part(first, lambda: None))

    def tick():
        if queue:
            queue.pop(0)()

    second_part = state_free_part(second, tick)
    for step in queue + chain_steps(second_part):
        step()
    for pr in pairs:
        st_scr[pr] = st[pr]

    y = y_scr[...]
    mean = seg_sum(y, 2) * (1.0 / RWKV_N)
    d = y - mean
    var = seg_sum(d * d, 1) * (1.0 / RWKV_N)
    y = d * lax.rsqrt(var + RWKV_LN_EPS) * lng_ref[...] + lnb_ref[...] + bonus_scr[...]
    o_ref[...] = (y * _silu(g_scr[...])).astype(o_ref.dtype)


def _rwkv(h, w, vecs, w2p, a2p, seg, layer, batch, seq, interpret):
    tb = min(TOKEN_TILE, seq)
    nt = seq // tb
    kern = functools.partial(_rwkv_kernel, tb=tb)
    mu, w0, a0, kk, ka, rk, lng, lnb = vecs
    ins = [mu, w0, w2p, a0, a2p, kk, ka, rk, lng, lnb]
    wide = lambda: pltpu.VMEM((tb, WIDTH), F32)
    n_rows = _IN_OFFS[14] - _IN_OFFS[12]
    return pl.pallas_call(
        kern,
        grid=(batch, nt),
        in_specs=[pl.BlockSpec((tb, D_MODEL), lambda b, i: (b * nt + i, 0)),
                  _w_rows_spec(w, layer, _IN_OFFS[12], n_rows)]
        + [_layer_spec(a, layer) for a in ins] + [_full_spec(seg.shape)],
        out_specs=pl.BlockSpec((tb, WIDTH), lambda b, i: (b * nt + i, 0)),
        out_shape=jax.ShapeDtypeStruct((batch * seq, WIDTH), MXU_DTYPE),
        scratch_shapes=[pltpu.VMEM((tb + CARRY_ROWS, 3 * WIDTH + 2 * RWKV_RANK), F32)]
        + [wide() for _ in range(9)]
        + [pltpu.VMEM((RWKV_HEADS // 2, LANES, LANES), F32), pltpu.VMEM((n_rows, D_MODEL), MXU_DTYPE)],
        compiler_params=_params(("arbitrary", "arbitrary")),
        interpret=interpret,
        name="rwkv7",
    )(h, w, *ins, seg)


def _memkv_kernel(mem_ref, g_ref, w_ref, k_ref, v_ref):
    x = mem_ref[0]
    y = x * lax.rsqrt(jnp.mean(x * x, axis=-1, keepdims=True) + NORM_EPS) * g_ref[...]
    kv = _dot(y, w_ref[...])
    head = _iota((1, MEM_WIDTH), 1) >> 6
    for hd in range(MEM_HEADS):
        k_ref[0, hd] = jnp.where(head == hd, kv[:, 0:MEM_WIDTH], 0.0).astype(k_ref.dtype)
        v_ref[0, hd] = jnp.where(head == hd, kv[:, MEM_WIDTH:2 * MEM_WIDTH], 0.0).astype(v_ref.dtype)


def _memkv(mem, g, w, layer, interpret):
    b, m, d = mem.shape
    out = jax.ShapeDtypeStruct((b, MEM_HEADS, m, MEM_WIDTH), MXU_DTYPE)
    return pl.pallas_call(
        _memkv_kernel,
        grid=(b,),
        in_specs=[pl.BlockSpec((1, m, d), lambda i: (i, 0, 0)), _layer_spec(g, layer), _layer_spec(w, layer)],
        out_specs=[pl.BlockSpec((1, MEM_HEADS, m, MEM_WIDTH), lambda i: (i, 0, 0, 0))] * 2,
        out_shape=[out, out],
        compiler_params=_params(("parallel",)),
        interpret=interpret,
        name="mem_kv",
    )(mem, g, w)


def _merge_kernel(x_ref, h_ref, oret_ref, ogla_ref, ossd_ref, orwkv_ref, km_ref, vm_ref,
                  wq_ref, wg_ref, uret_ref, ugla_ref, ussd_ref, urwkv_ref, umem_ref, wout_ref, gn_ref,
                  *refs):
    out_refs, wqb_scr = refs[:-1], refs[-1]

    @pl.when(_first_step())
    def _():
        wqb_scr[...] = wq_ref[0].astype(wqb_scr.dtype)

    h = h_ref[...]
    q = _proj(h, wqb_scr[...]) * MEM_HEAD_DIM ** -0.5
    scores = [_dot_nt(q, km_ref[0, hd]) for hd in range(MEM_HEADS)]
    branches = ((oret_ref, uret_ref), (ogla_ref, ugla_ref), (ossd_ref, ussd_ref), (orwkv_ref, urwkv_ref))
    merged = None
    for i in range(N_BRANCHES):
        gate = jax.nn.sigmoid(_proj(h, wg_ref[i * D_MODEL:(i + 1) * D_MODEL, :]))
        if i < 4:
            o_ref, u_ref = branches[i]
            up = jnp.dot(o_ref[...], u_ref[...], preferred_element_type=F32)
        else:
            up = _dot(o_mem, umem_ref[...])
        merged = gate * up if merged is None else merged + gate * up
        if i == 0:
            o_mem = jnp.zeros(q.shape, F32)
            for hd in range(MEM_HEADS):
                s = jnp.exp(scores[hd] - jnp.max(scores[hd], axis=-1, keepdims=True))
                prob = s / jnp.sum(s, axis=-1, keepdims=True)
                o_mem = o_mem + _dot(prob, vm_ref[0, hd])
    x = x_ref[...] + _dot(merged, wout_ref[...])
    if len(out_refs) == 2:
        out_refs[0][...] = x
    y = x * lax.rsqrt(jnp.mean(x * x, axis=-1, keepdims=True) + NORM_EPS) * gn_ref[...]
    out_refs[-1][...] = y.astype(out_refs[-1].dtype)


def _merge(x2d, h, o_ret, o_gla, o_ssd, o_rwkv, km, vm, wq, wg, ups, wout, g_next, layer, last,
           batch, seq, interpret):
    tm = min(TOKEN_TILE, seq)
    nt = seq // tm
    row = lambda w: pl.BlockSpec((tm, w), lambda b, i: (b * nt + i, 0))
    kvspec = pl.BlockSpec((1,) + km.shape[1:], lambda b, i: (b, 0, 0, 0))
    weights = [wg, *ups, wout, g_next]
    return pl.pallas_call(
        _merge_kernel,
        grid=(batch, nt),
        in_specs=[row(D_MODEL), row(D_MODEL), row(WIDTH), row(WIDTH), row(WIDTH), row(WIDTH), kvspec, kvspec,
                  _w_rows_spec(wq, layer, _IN_OFFS[14], MEM_WIDTH)]
        + [_layer_spec(w, layer, pipeline_mode=pl.Buffered(1)) for w in weights],
        out_specs=[row(D_MODEL)] if last else [row(D_MODEL), row(D_MODEL)],
        out_shape=[jax.ShapeDtypeStruct(x2d.shape, F32)] if last else
        [jax.ShapeDtypeStruct(x2d.shape, F32), jax.ShapeDtypeStruct(x2d.shape, MXU_DTYPE)],
        scratch_shapes=[pltpu.VMEM((MEM_WIDTH, D_MODEL), MXU_DTYPE)],
        compiler_params=_params(("arbitrary", "arbitrary")),
        interpret=interpret,
        name="merge",
    )(x2d, h, o_ret, o_gla, o_ssd, o_rwkv, km, vm, wq, *weights)


def _pad_last(a, width):
    return jnp.pad(a, [(0, 0)] * (a.ndim - 1) + [(0, width - a.shape[-1])])


def _rows3(v, width=None):
    v = v.reshape(v.shape[0], 1, -1).astype(F32)
    return v if width is None else _pad_last(v, width)


def _forward(x, mem, positions, norm_g, w_in, gla_gk_w2, gla_gk_b, gla_norm_g,
             ssd_conv_w, ssd_conv_b, ssd_dt_bias, ssd_a_log, ssd_d, ssd_norm_g,
             rwkv_mu, rwkv_w0, rwkv_w2, rwkv_a0, rwkv_a2, rwkv_k_k, rwkv_k_a, rwkv_r_k,
             rwkv_ln_g, rwkv_ln_b, mem_norm_g, w_mem_kv,
             w_up_ret, w_up_gla, w_up_ssd, w_up_rwkv, w_up_mem, w_out, final_norm_g, interpret=False):
    batch, seq, d = x.shape
    assert d == D_MODEL and seq % (RET_CHUNK * RET_GROUP) == 0 and seq % (GLA_CHUNK * GLA_GROUP) == 0
    depth = w_in.shape[0]
    cdt = MXU_DTYPE
    o = _IN_OFFS

    half = np.arange(RET_DK // 2)
    ret_perm = np.concatenate([hd * RET_DK + 2 * half + par for par in (0, 1) for hd in range(RET_HEADS)])
    inv = 1.0 / (ROPE_BASE ** jnp.linspace(0.0, 1.0, RET_DK // 2, dtype=F32))
    inv_row = jnp.tile(inv, RET_HEADS).reshape(1, LANES)
    ret_tile, ret_rows = RET_CHUNK * RET_GROUP, RET_CHUNK * RET_GROUP // RET_HEADS
    ret_pos = jnp.repeat(positions.reshape(batch, seq // ret_tile, RET_HEADS, ret_rows).transpose(0, 1, 3, 2),
                         RET_DK // 2, axis=3).reshape(batch * seq // RET_HEADS, LANES)
    head_of_lane = np.arange(WIDTH) // SSD_P
    ssd_expand = jnp.asarray(np.arange(LANES)[:, None] == head_of_lane[None, :], F32)
    rwkv_seg = jnp.asarray(head_of_lane[:256, None] == head_of_lane[None, :256], F32)

    w_t = jnp.swapaxes(w_in, 1, 2)
    w_g = w_t[:, o[15]:o[16], :].astype(cdt)
    ret_perm_rows = jnp.asarray(ret_perm[:, None] == np.arange(RET_HEADS * RET_DK)[None, :], cdt)
    gla_w2p = jnp.pad(gla_gk_w2, ((0, 0), (0, LANES - GLA_RANK), (0, 0))).astype(cdt)
    zeros_rank = jnp.zeros((depth, RWKV_RANK, WIDTH), F32)
    rwkv_w2p = jnp.concatenate([rwkv_w2, zeros_rank], axis=1).astype(cdt)
    rwkv_a2p = jnp.concatenate([zeros_rank, rwkv_a2], axis=1).astype(cdt)
    rwkv_vecs = [_rows3(v) for v in (rwkv_mu, rwkv_w0, rwkv_a0, rwkv_k_k, rwkv_k_a, rwkv_r_k,
                                     rwkv_ln_g, rwkv_ln_b)]
    ssd_small = [ssd_conv_w.astype(F32), _rows3(ssd_conv_b), _rows3(ssd_dt_bias, LANES),
                 _rows3(-jnp.exp(ssd_a_log.astype(F32)), LANES), _rows3(jnp.repeat(ssd_d, SSD_P, axis=1)),
                 _rows3(ssd_norm_g)]
    gla_b, gla_ng = _rows3(gla_gk_b), _rows3(gla_norm_g)
    mem_g, w_kv = _rows3(mem_norm_g), w_mem_kv.astype(cdt)
    ups = [w.astype(cdt) for w in (w_up_ret, w_up_gla, w_up_ssd, w_up_rwkv, w_up_mem)]
    w_o = w_out.astype(cdt)
    g_next = _rows3(jnp.concatenate([norm_g[1:], final_norm_g[None]], axis=0))

    x2d = x.reshape(batch * seq, d)
    h = _rmsnorm(x2d, norm_g[0], cdt, interpret)
    for l in range(depth):
        o_ret = _retention(h, ret_pos, inv_row, w_t, ret_perm_rows, l, batch, seq, interpret)
        o_gla = _gla(h, w_t, gla_w2p, gla_b, gla_ng, l, batch, seq, interpret)
        o_ssd = _ssd(h, w_t, *ssd_small, ssd_expand, l, batch, seq, interpret)
        o_rwkv = _rwkv(h, w_t, rwkv_vecs, rwkv_w2p, rwkv_a2p, rwkv_seg, l, batch, seq, interpret)
        km, vm = _memkv(mem, mem_g, w_kv, l, interpret)
        last = l == depth - 1
        outs = _merge(x2d, h, o_ret, o_gla, o_ssd, o_rwkv, km, vm, w_t, w_g, ups, w_o, g_next, l, last,
                      batch, seq, interpret)
        x2d, h = (None, outs[0]) if last else outs
    return h.reshape(batch, seq, d)


def kernel(x, mem, positions, norm_g, w_in, gla_gk_w2, gla_gk_b, gla_norm_g, ssd_conv_w, ssd_conv_b, ssd_dt_bias, ssd_a_log, ssd_d, ssd_norm_g, rwkv_mu, rwkv_w0, rwkv_w2, rwkv_a0, rwkv_a2, rwkv_k_k, rwkv_k_a, rwkv_r_k, rwkv_ln_g, rwkv_ln_b, mem_norm_g, w_mem_kv, w_up_ret, w_up_gla, w_up_ssd, w_up_rwkv, w_up_mem, w_out, final_norm_g):
    return _forward(x, mem, positions, norm_g, w_in, gla_gk_w2, gla_gk_b, gla_norm_g,
                    ssd_conv_w, ssd_conv_b, ssd_dt_bias, ssd_a_log, ssd_d, ssd_norm_g,
                    rwkv_mu, rwkv_w0, rwkv_w2, rwkv_a0, rwkv_a2, rwkv_k_k, rwkv_k_a, rwkv_r_k,
                    rwkv_ln_g, rwkv_ln_b, mem_norm_g, w_mem_kv,
                    w_up_ret, w_up_gla, w_up_ssd, w_up_rwkv, w_up_mem, w_out, final_norm_g)
```

```python
import functools
import math

import jax
import jax.numpy as jnp
import numpy as np
from jax import lax
from jax.experimental import pallas as pl
from jax.experimental.pallas import tpu as pltpu

F32 = jnp.float32
BF16 = jnp.bfloat16
MXU_DTYPE = jnp.bfloat16

D_MODEL = 1024
WIDTH = 512
NORM_EPS = 1e-6
N_BRANCHES = 5

RET_HEADS, RET_DK, RET_DV, RET_CHUNK = 4, 64, 128, 128
ROPE_BASE = 10000.0
GLA_HEADS, GLA_DK, GLA_DV, GLA_RANK, GLA_NORMALIZER, GLA_CHUNK = 4, 64, 128, 16, 16.0, 64
SSD_HEADS, SSD_P, SSD_GROUPS, SSD_STATE, SSD_CONV, SSD_CHUNK = 8, 64, 2, 128, 4, 128
RWKV_HEADS, RWKV_N, RWKV_RANK, RWKV_CHUNK = 8, 64, 64, 64
RWKV_LN_EPS = 64e-5
RET_GROUP = 8
SSD_GROUP = 4
GLA_GROUP = 16
MEM_HEADS, MEM_HEAD_DIM, MEM_WIDTH = 4, 64, 256

LANES = 128
CARRY_ROWS = 8
TOKEN_TILE = 512
VMEM_LIMIT = 56 * 1024 * 1024

_IN_SIZES = (256, 256, 512, 512, 256, 256, 512, 16, 512, 1024, 8, 512, 1664, 512, 256, 5120)
_IN_OFFS = tuple(int(v) for v in np.cumsum((0,) + _IN_SIZES))


def _dot(a, b):
    return jnp.dot(a.astype(MXU_DTYPE), b.astype(MXU_DTYPE), preferred_element_type=F32)


def _dot_nt(a, b):
    return lax.dot_general(a.astype(MXU_DTYPE), b.astype(MXU_DTYPE), (((1,), (1,)), ((), ())),
                           preferred_element_type=F32)


def _dot_tn(a, b):
    return lax.dot_general(a.astype(MXU_DTYPE), b.astype(MXU_DTYPE), (((0,), (0,)), ((), ())),
                           preferred_element_type=F32)


def _proj(x, w_t):
    return lax.dot_general(x, w_t, (((1,), (1,)), ((), ())), preferred_element_type=F32)


def _split3(x):
    hi = x.astype(BF16)
    r1 = x - hi.astype(F32)
    mid = r1.astype(BF16)
    lo = (r1 - mid.astype(F32)).astype(BF16)
    return hi, mid, lo


def _sel_dot(sel, x):
    return sum(jnp.dot(sel, p, preferred_element_type=F32) for p in _split3(x))


def _dot_sel(x, sel, pieces=3):
    return sum(jnp.dot(p, sel, preferred_element_type=F32) for p in _split3(x)[:pieces])


def _iota(shape, dim):
    return lax.broadcasted_iota(jnp.int32, shape, dim)


def _tril(n, strict=False):
    r, c = _iota((n, n), 0), _iota((n, n), 1)
    return (r > c) if strict else (r >= c)


def _sigmoid(x):
    return 0.5 + 0.5 * jnp.tanh(0.5 * x)


def _silu(x):
    half = 0.5 * x
    return half + half * jnp.tanh(half)


def _softplus(x):
    return jnp.maximum(x, 0.0) + jnp.log(1.0 + jnp.exp(-jnp.abs(x)))


def _rows(c, n):
    return pl.ds(pl.multiple_of(c * n, n), n)


def _loop(trips, body):
    if trips == 1:
        body(0, 0)
    else:
        lax.fori_loop(0, trips, body, 0)


def _full_spec(shape):
    zeros = (0,) * len(shape)
    return pl.BlockSpec(shape, lambda *_: zeros)


def _layer_spec(arr, layer, **kwargs):
    tail = tuple(arr.shape[1:])
    index = (layer,) + (0,) * len(tail)
    return pl.BlockSpec((None,) + tail, lambda *_: index, **kwargs)


def _w_rows_spec(w_t, layer, row0, rows):
    return pl.BlockSpec((pl.Element(1), pl.Element(rows), pl.Element(w_t.shape[2])),
                        lambda *_: (layer, row0, 0), pipeline_mode=pl.Buffered(1))


def _first_step():
    return (pl.program_id(0) == 0) & (pl.program_id(1) == 0)


def _params(semantics):
    return pltpu.CompilerParams(dimension_semantics=semantics, vmem_limit_bytes=VMEM_LIMIT)


def _rmsnorm_kernel(x_ref, g_ref, o_ref):
    x = x_ref[...]
    y = x * lax.rsqrt(jnp.mean(x * x, axis=-1, keepdims=True) + NORM_EPS)
    o_ref[...] = (y * g_ref[...]).astype(o_ref.dtype)


def _rmsnorm(x2d, g, out_dtype, interpret):
    m, d = x2d.shape
    tm = min(1024, m)
    return pl.pallas_call(
        _rmsnorm_kernel,
        grid=(m // tm,),
        in_specs=[pl.BlockSpec((tm, d), lambda i: (i, 0)), _full_spec((1, d))],
        out_specs=pl.BlockSpec((tm, d), lambda i: (i, 0)),
        out_shape=jax.ShapeDtypeStruct((m, d), out_dtype),
        compiler_params=_params(("parallel",)),
        interpret=interpret,
        name="rmsnorm",
    )(x2d, g.reshape(1, d))


def _ret_kernel(h_ref, pos_ref, inv_ref, w_ref, perm_ref, o_ref, p_scr, s_scr, wb_scr, *, tb):
    C = RET_CHUNK

    @pl.when(_first_step())
    def _():
        for lo in (0, 256):
            rows = w_ref[0, lo:lo + 256, :].astype(wb_scr.dtype)
            wb_scr[lo:lo + 256, :] = jnp.dot(perm_ref[...], rows, preferred_element_type=F32).astype(wb_scr.dtype)
        wb_scr[512:1536, :] = w_ref[0, 512:1536, :].astype(wb_scr.dtype)

    @pl.when(pl.program_id(1) == 0)
    def _():
        s_scr[...] = jnp.zeros_like(s_scr)

    ang = pos_ref[...].astype(F32) * inv_ref[...]
    lane_group = _iota((1, LANES), 1) >> 5

    def unpack(t):
        blocks = []
        for j in range(RET_HEADS):
            own = jnp.where(lane_group == j, t, 0.0)
            blocks.append(own + pltpu.roll(own, 32, 1) + pltpu.roll(own, 64, 1) + pltpu.roll(own, 96, 1))
        return jnp.concatenate(blocks, axis=0)

    cos, sin = unpack(jnp.cos(ang)), unpack(jnp.sin(ang))
    p_scr[:, 512:1536] = _proj(h_ref[...], wb_scr[512:1536, :])
    p = _proj(h_ref[...], wb_scr[0:512, :])
    q1, q2 = p[:, 0:128], p[:, 128:256]
    k1, k2 = p[:, 256:384] * RET_DK ** -0.5, p[:, 384:512] * RET_DK ** -0.5
    p_scr[:, 0:128] = q1 * cos - q2 * sin
    p_scr[:, 128:256] = q2 * cos + q1 * sin
    p_scr[:, 256:384] = k1 * cos - k2 * sin
    p_scr[:, 384:512] = k2 * cos + k1 * sin

    def log_gamma(head):
        return jnp.log(1.0 - jnp.exp2(-5.0 - head.astype(F32)))

    qk_head = (_iota((1, 256), 1) >> 5) & 3
    lg_lane = log_gamma(qk_head)
    tau = _iota((C, 1), 0).astype(F32)
    dq = jnp.exp(lg_lane * (tau + 1.0))
    dk = jnp.exp(lg_lane * (C - 1.0 - tau))
    ds = jnp.exp(lg_lane * float(C))
    diff = (_iota((C, C), 0) - _iota((C, C), 1)).astype(F32)
    causal = _tril(C)
    bd_mask = (_iota((WIDTH, 256), 0) >> 7) == ((_iota((WIDTH, 256), 1) >> 5) & 3)

    heads = range(RET_HEADS)
    hcols = [slice(hd * RET_DV, (hd + 1) * RET_DV) for hd in heads]
    seg = [jnp.where(causal, jnp.exp(math.log(1.0 - 2.0 ** (-5.0 - hd)) * diff), 0.0) for hd in heads]

    def group(gi, carry):
        us = range(RET_GROUP)
        rows = [_rows(gi * RET_GROUP + u, C) for u in us]
        q = [p_scr[rw, 0:256] for rw in rows]
        k = [p_scr[rw, 256:512] for rw in rows]
        v = [p_scr[rw, 512:1024] for rw in rows]
        sc = [[_dot_nt(q[u], jnp.where(qk_head == hd, k[u], 0.0)) * seg[hd] for hd in heads] for u in us]
        y_intra = [[_dot(sc[u][hd], v[u][:, hcols[hd]]) for hd in heads] for u in us]
        kv = [jnp.where(bd_mask, _dot_tn(v[u], k[u] * dk), 0.0) for u in us]
        st = s_scr[...]
        y_inter = []
        for u in us:
            y_inter.append(_dot_nt(q[u] * dq, st))
            st = st * ds + kv[u]
        s_scr[...] = st
        for u in us:
            for hd in heads:
                y = y_intra[u][hd] + y_inter[u][:, hcols[hd]]
                y = y * lax.rsqrt(jnp.mean(y * y, axis=-1, keepdims=True) + NORM_EPS)
                g = p_scr[rows[u], 1024 + hd * RET_DV:1024 + (hd + 1) * RET_DV]
                o_ref[rows[u], hcols[hd]] = (y * _silu(g)).astype(o_ref.dtype)
        return carry

    _loop(tb // (C * RET_GROUP), group)


def _retention(h, ret_pos, inv_row, w, perm, layer, batch, seq, interpret):
    tb = RET_CHUNK * RET_GROUP
    nt = seq // tb
    kern = functools.partial(_ret_kernel, tb=tb)
    return pl.pallas_call(
        kern,
        grid=(batch, nt),
        in_specs=[pl.BlockSpec((tb, D_MODEL), lambda b, i: (b * nt + i, 0)),
                  pl.BlockSpec((tb // RET_HEADS, LANES), lambda b, i: (b * nt + i, 0)),
                  _full_spec((1, LANES)),
                  _w_rows_spec(w, layer, _IN_OFFS[0], 1536), _full_spec(perm.shape)],
        out_specs=pl.BlockSpec((tb, WIDTH), lambda b, i: (b * nt + i, 0)),
        out_shape=jax.ShapeDtypeStruct((batch * seq, WIDTH), MXU_DTYPE),
        scratch_shapes=[pltpu.VMEM((tb, 1536), F32), pltpu.VMEM((WIDTH, 256), F32),
                        pltpu.VMEM((1536, D_MODEL), MXU_DTYPE)],
        compiler_params=_params(("arbitrary", "arbitrary")),
        interpret=interpret,
        name="retention",
    )(h, ret_pos, inv_row, w, perm)


def _gla_kernel(h_ref, w_ref, w2_ref, gb_ref, ng_ref, o_ref, p_scr, lg_scr, s_scr, wb_scr, *, tb):
    C = GLA_CHUNK

    @pl.when(_first_step())
    def _():
        cast = lambda v: v.astype(wb_scr.dtype)
        wb_scr[0:1024, :] = cast(w_ref[0, 0:1024, :])
        wb_scr[1024:1536, :] = cast(w_ref[0, 1024 + GLA_RANK:1536 + GLA_RANK, :])
        wb_scr[1536:1664, :] = cast(jnp.concatenate(
            [w_ref[0, 1024:1024 + GLA_RANK, :], jnp.zeros((LANES - GLA_RANK, D_MODEL), F32)], axis=0))

    @pl.when(pl.program_id(1) == 0)
    def _():
        s_scr[...] = jnp.zeros_like(s_scr)

    pre = _dot(_proj(h_ref[...], wb_scr[1536:1664, :]), w2_ref[...]) + gb_ref[...]
    lg_scr[...] = -_softplus(-pre) / GLA_NORMALIZER
    tri = _tril(C)
    assert tb == C * GLA_GROUP
    tri_sel = tri.astype(BF16)
    cum_all = [_sel_dot(tri_sel, lg_scr[u * C:(u + 1) * C, :]) for u in range(GLA_GROUP)]
    p_scr[:, 0:512] = _proj(h_ref[...], wb_scr[0:512, :])
    p_scr[:, 512:1536] = _proj(h_ref[...], wb_scr[512:1536, :])

    k_head = _iota((1, 256), 1) >> 6
    pairs = range(GLA_HEADS // 2)
    bd_mask = (_iota((256, LANES), 0) >> 7) == (_iota((256, LANES), 1) >> 6)
    ng = ng_ref[...]

    heads = range(GLA_HEADS)
    hcols = [slice(hd * GLA_DV, (hd + 1) * GLA_DV) for hd in heads]

    def group(gi, carry):
        us = range(GLA_GROUP)
        rows = [_rows(gi * GLA_GROUP + u, C) for u in us]
        q = [p_scr[rw, 0:256] * GLA_DK ** -0.5 for rw in rows]
        k = [p_scr[rw, 256:512] for rw in rows]
        v = [p_scr[rw, 512:1024] for rw in rows]
        cum = cum_all
        ref = [c[C // 2:C // 2 + 1, :] for c in cum]
        last = [c[C - 1:C, :] for c in cum]
        q_in = [q[u] * jnp.exp(cum[u] - ref[u]) for u in us]
        k_in = [k[u] * jnp.exp(ref[u] - cum[u]) for u in us]
        q_dec = [q[u] * jnp.exp(cum[u]) for u in us]
        k_st = [k[u] * jnp.exp(last[u] - cum[u]) for u in us]
        sc = [[jnp.where(tri, _dot_nt(q_in[u], jnp.where(k_head == hd, k_in[u], 0.0)), 0.0) for hd in heads]
              for u in us]
        y_intra = [[_dot(sc[u][hd], v[u][:, hcols[hd]]) for hd in heads] for u in us]
        vcols = [slice(p * 256, (p + 1) * 256) for p in pairs]
        kcols = [slice(p * LANES, (p + 1) * LANES) for p in pairs]
        kv = [[jnp.where(bd_mask, _dot_tn(v[u][:, vcols[p]], k_st[u][:, kcols[p]]), 0.0) for p in pairs]
              for u in us]
        st = [s_scr[p] for p in pairs]
        y_inter = []
        for u in us:
            y_inter.append(jnp.concatenate([_dot_nt(q_dec[u][:, kcols[p]], st[p]) for p in pairs], axis=1))
            st = [st[p] * jnp.exp(last[u][:, kcols[p]]) + kv[u][p] for p in pairs]
        for p in pairs:
            s_scr[p] = st[p]
        for u in us:
            for hd in heads:
                y = y_intra[u][hd] + y_inter[u][:, hcols[hd]]
                y = y * lax.rsqrt(jnp.mean(y * y, axis=-1, keepdims=True) + NORM_EPS) * ng
                g = p_scr[rows[u], 1024 + hd * GLA_DV:1024 + (hd + 1) * GLA_DV]
                o_ref[rows[u], hcols[hd]] = (y * _silu(g)).astype(o_ref.dtype)
        return carry

    _loop(tb // (C * GLA_GROUP), group)


def _gla(h, w, w2p, gb, ng, layer, batch, seq, interpret):
    tb = GLA_CHUNK * GLA_GROUP
    nt = seq // tb
    kern = functools.partial(_gla_kernel, tb=tb)
    return pl.pallas_call(
        kern,
        grid=(batch, nt),
        in_specs=[pl.BlockSpec((tb, D_MODEL), lambda b, i: (b * nt + i, 0)),
                  _w_rows_spec(w, layer, _IN_OFFS[4], 1536 + GLA_RANK), _layer_spec(w2p, layer),
                  _layer_spec(gb, layer), _layer_spec(ng, layer)],
        out_specs=pl.BlockSpec((tb, WIDTH), lambda b, i: (b * nt + i, 0)),
        out_shape=jax.ShapeDtypeStruct((batch * seq, WIDTH), MXU_DTYPE),
        scratch_shapes=[pltpu.VMEM((tb, 1536), F32), pltpu.VMEM((tb, 256), F32),
                        pltpu.VMEM((GLA_HEADS // 2, 256, LANES), F32), pltpu.VMEM((1664, D_MODEL), MXU_DTYPE)],
        compiler_params=_params(("arbitrary", "arbitrary")),
        interpret=interpret,
        name="gla",
    )(h, w, w2p, gb, ng)


def _ssd_kernel(h_ref, w_ref, cw_ref, cb_ref, dtb_ref, a_ref, dskip_ref, ng_ref, exp_ref, o_ref,
                raw_scr, xc_scr, z_scr, dt_scr, s_scr, wb_scr, *, tb):
    C = SSD_CHUNK
    NCH = 1024

    @pl.when(_first_step())
    def _():
        cast = lambda v: v.astype(wb_scr.dtype)
        wb_scr[0:NCH, :] = cast(w_ref[0, 0:NCH, :])
        wb_scr[NCH:NCH + WIDTH, :] = cast(w_ref[0, NCH + SSD_HEADS:NCH + SSD_HEADS + WIDTH, :])
        wb_scr[1536:1664, :] = cast(jnp.concatenate(
            [w_ref[0, NCH:NCH + SSD_HEADS, :], jnp.zeros((LANES - SSD_HEADS, D_MODEL), F32)], axis=0))

    @pl.when(pl.program_id(1) == 0)
    def _():
        s_scr[...] = jnp.zeros_like(s_scr)
        raw_scr[0:CARRY_ROWS, :] = jnp.zeros((CARRY_ROWS, NCH), F32)

    dt_scr[...] = _softplus(_proj(h_ref[...], wb_scr[1536:1664, :]) + dtb_ref[...])
    first_tap = CARRY_ROWS - (SSD_CONV - 1)
    for lo in range(0, NCH, 256):
        cols = slice(lo, lo + 256)
        raw_scr[CARRY_ROWS:CARRY_ROWS + tb, cols] = _proj(h_ref[...], wb_scr[cols, :])
        conv = cb_ref[:, cols] + sum(raw_scr[first_tap + j:first_tap + j + tb, cols] * cw_ref[j:j + 1, cols]
                                     for j in range(SSD_CONV))
        xc_scr[:, cols] = _silu(conv)
        raw_scr[0:CARRY_ROWS, cols] = raw_scr[tb:tb + CARRY_ROWS, cols]
    z_scr[...] = _proj(h_ref[...], wb_scr[NCH:NCH + WIDTH, :])

    tri = _tril(C)
    lane_lo = _iota((1, LANES), 1) < SSD_P
    tri_sel = tri.astype(BF16)
    expand = exp_ref[...].astype(BF16)
    a_row = a_ref[...]

    groups = range(SSD_GROUPS)
    gcols = [slice(g * 256, (g + 1) * 256) for g in groups]
    ncols = [slice(g * SSD_STATE, (g + 1) * SSD_STATE) for g in groups]

    def group(gi, carry):
        us = range(SSD_GROUP)
        rows = [_rows(gi * SSD_GROUP + u, C) for u in us]
        xs = [xc_scr[rw, 0:512] for rw in rows]
        bm = [xc_scr[rw, 512:768] for rw in rows]
        cm = [xc_scr[rw, 768:1024] for rw in rows]
        dt = [dt_scr[rw, :] for rw in rows]
        cum = [_sel_dot(tri_sel, dt[u] * a_row) for u in us]
        cum_t = [c.T for c in cum]
        dt_e = [_dot_sel(d, expand) for d in dt]
        cum_e = [_dot_sel(c, expand) for c in cum]
        last_e = [c[C - 1:C, :] for c in cum_e]
        xdt = [xs[u] * dt_e[u] for u in us]
        v_st = [xdt[u] * jnp.exp(last_e[u] - cum_e[u]) for u in us]
        e_cum = [jnp.exp(c) for c in cum_e]
        scores = [[_dot_nt(cm[u][:, ncols[g]], bm[u][:, ncols[g]]) for g in groups] for u in us]
        y_intra = []
        for u in us:
            parts = []
            for pr in range(SSD_HEADS // 2):
                sc = scores[u][pr // 2]
                segs = []
                for hd in (2 * pr, 2 * pr + 1):
                    d = jnp.minimum(cum[u][:, hd:hd + 1] - cum_t[u][hd:hd + 1, :], 0.0)
                    segs.append(sc * jnp.where(tri, jnp.exp(d), 0.0))
                xp = xdt[u][:, pr * LANES:(pr + 1) * LANES]
                rhs = jnp.concatenate([jnp.where(lane_lo, xp, 0.0), jnp.where(lane_lo, 0.0, xp)], axis=0)
                parts.append(_dot(jnp.concatenate(segs, axis=1), rhs))
            y_intra.append(jnp.concatenate(parts, axis=1))
        kv = [[_dot_tn(bm[u][:, ncols[g]], v_st[u][:, gcols[g]]) for g in groups] for u in us]
        st = [s_scr[g] for g in groups]
        y_inter = []
        for u in us:
            y_inter.append(jnp.concatenate(
                [_dot(cm[u][:, ncols[g]], st[g]) * e_cum[u][:, gcols[g]] for g in groups], axis=1))
            st = [st[g] * jnp.exp(last_e[u][:, gcols[g]]) + kv[u][g] for g in groups]
        for g in groups:
            s_scr[g] = st[g]
        for u in us:
            y = y_intra[u] + y_inter[u] + dskip_ref[...] * xs[u]
            y = y * _silu(z_scr[rows[u], :])
            for g in groups:
                yg = y[:, gcols[g]]
                yg = yg * lax.rsqrt(jnp.mean(yg * yg, axis=-1, keepdims=True) + NORM_EPS)
                o_ref[rows[u], gcols[g]] = (yg * ng_ref[:, gcols[g]]).astype(o_ref.dtype)
        return carry

    _loop(tb // (C * SSD_GROUP), group)


def _ssd(h, w, cw, cb, dtb, a_row, dskip, ng, expand, layer, batch, seq, interpret):
    tb = SSD_CHUNK * SSD_GROUP
    nt = seq // tb
    kern = functools.partial(_ssd_kernel, tb=tb)
    small = [cw, cb, dtb, a_row, dskip, ng]
    return pl.pallas_call(
        kern,
        grid=(batch, nt),
        in_specs=[pl.BlockSpec((tb, D_MODEL), lambda b, i: (b * nt + i, 0)),
                  _w_rows_spec(w, layer, _IN_OFFS[9], 1536 + SSD_HEADS)]
        + [_layer_spec(a, layer) for a in small] + [_full_spec(expand.shape)],
        out_specs=pl.BlockSpec((tb, WIDTH), lambda b, i: (b * nt + i, 0)),
        out_shape=jax.ShapeDtypeStruct((batch * seq, WIDTH), MXU_DTYPE),
        scratch_shapes=[pltpu.VMEM((tb + CARRY_ROWS, 1024), F32), pltpu.VMEM((tb, 1024), F32),
                        pltpu.VMEM((tb, WIDTH), F32), pltpu.VMEM((tb, LANES), F32),
                        pltpu.VMEM((SSD_GROUPS, SSD_STATE, 256), F32), pltpu.VMEM((1664, D_MODEL), MXU_DTYPE)],
        compiler_params=_params(("arbitrary", "arbitrary")),
        interpret=interpret,
        name="ssd",
    )(h, w, *small, expand)


def _rwkv_kernel(h_ref, w_ref, mu_ref, w0_ref, w2_ref, a0_ref, a2_ref, kk_ref, ka_ref, rk_ref,
                 lng_ref, lnb_ref, seg_ref, o_ref,
                 u_scr, g_scr, r_scr, lw_scr, k_scr, v_scr, a_scr, b_scr, y_scr, bonus_scr, st_scr, wb_scr,
                 *, tb):
    C = RWKV_CHUNK
    NU = 3 * WIDTH + 2 * RWKV_RANK

    @pl.when(_first_step())
    def _():
        wb_scr[...] = w_ref[0].astype(wb_scr.dtype)

    @pl.when(pl.program_id(1) == 0)
    def _():
        st_scr[...] = jnp.zeros_like(st_scr)
        u_scr[0:CARRY_ROWS, :] = jnp.zeros((CARRY_ROWS, NU), F32)

    seg = seg_ref[...].astype(BF16)

    def seg_sum(x, pieces):
        return jnp.concatenate([_dot_sel(x[:, j * 256:(j + 1) * 256], seg, pieces) for j in range(2)], axis=1)

    def shifted(lo, hi):
        u = _proj(h_ref[...], wb_scr[lo:hi, :])
        u_scr[CARRY_ROWS:CARRY_ROWS + tb, lo:hi] = u
        u_prev = u_scr[CARRY_ROWS - 1:CARRY_ROWS - 1 + tb, lo:hi]
        u_scr[0:CARRY_ROWS, lo:hi] = u_scr[tb:tb + CARRY_ROWS, lo:hi]
        return u + (u_prev - u) * mu_ref[:, lo:hi]

    wa = shifted(1536, 1664)
    w_pre = w0_ref[...] + _dot(jnp.tanh(wa), w2_ref[...])
    a_gate = _sigmoid(a0_ref[...] + _dot(wa, a2_ref[...]))
    lw_scr[...] = -jnp.exp(-_softplus(-w_pre) - 0.5)
    k = shifted(512, 1024)
    kk = k * kk_ref[...]
    k = k * (1.0 + (a_gate - 1.0) * ka_ref[...])
    kk = kk / jnp.maximum(jnp.sqrt(seg_sum(kk * kk, 1)), 1e-12)
    k_scr[...] = k
    a_scr[...] = -kk
    b_scr[...] = kk * a_gate
    r = shifted(0, 512)
    r_scr[...] = r
    v = shifted(1024, 1536)
    v_scr[...] = v
    bonus_scr[...] = seg_sum(r * k * rk_ref[...], 1) * v
    g_scr[...] = _proj(h_ref[...], wb_scr[NU:NU + WIDTH, :])

    tri_sel = _tril(C).astype(BF16)
    r2, c2 = _iota((2 * C, 2 * C), 0), _iota((2 * C, 2 * C), 1)
    same = (r2 >> 6) == (c2 >> 6)
    low_s = same & ((r2 & 63) > (c2 & 63))
    low_i = same & ((r2 & 63) >= (c2 & 63))
    eye = (r2 == c2).astype(F32)
    lane_lo = _iota((1, LANES), 1) < RWKV_N

    pairs = range(RWKV_HEADS // 2)

    def stack_pair(xp):
        return jnp.concatenate([jnp.where(lane_lo, xp, 0.0), jnp.where(lane_lo, 0.0, xp)], axis=0)

    def stack(x, pr):
        return stack_pair(x[:, pr * LANES:(pr + 1) * LANES])

    def state_free_part(chunks, tick):
        lanes = [(j, pr) for j in range(len(chunks)) for pr in pairs]
        rows, a_0, r_0, v_c, b_h, k_h, e_last, lhs, rhs_t = [], [], [], [], [], [], [], {}, {}
        for j, u in enumerate(chunks):
            rw = _rows(u, C)
            lw = lw_scr[rw, :]
            cum = _sel_dot(tri_sel, lw)
            cum_p = cum - lw
            ref = cum[C // 2:C // 2 + 1, :]
            last = cum[C - 1:C, :]
            e_fwd = jnp.exp(cum - ref)
            e_bwd = jnp.exp(ref - cum)
            e_end = jnp.exp(last - cum)
            r_c, k_c, a_c, b_c = r_scr[rw, :], k_scr[rw, :], a_scr[rw, :], b_scr[rw, :]
            r_t = r_c * e_fwd
            a_t = a_c * jnp.exp(cum_p - ref)
            b_t = b_c * e_bwd
            k_t = k_c * e_bwd
            for pr in pairs:
                lhs[j, pr] = jnp.concatenate([stack(a_t, pr), stack(r_t, pr)], axis=0)
                rhs_t[j, pr] = jnp.concatenate([stack(b_t, pr), stack(k_t, pr)], axis=0)
            rows.append(rw)
            a_0.append(a_c * jnp.exp(cum_p))
            r_0.append(r_c * jnp.exp(cum))
            v_c.append(v_scr[rw, :])
            b_h.append(b_c * e_end)
            k_h.append(k_c * e_end)
            e_last.append(jnp.exp(last))
        tick()
        big = [_dot_nt(lhs[ln], rhs_t[ln]) for ln in lanes]
        tick()
        a_ab = [jnp.where(low_s, m[0:128, 0:128], 0.0) for m in big]
        a_ak = [jnp.where(low_s, m[0:128, 128:256], 0.0) for m in big]
        a_rb = [jnp.where(low_i, m[128:256, 0:128], 0.0) for m in big]
        a_rk = [jnp.where(low_i, m[128:256, 128:256], 0.0) for m in big]
        inv = [eye + m for m in a_ab]
        pw = [_dot(m, m) for m in a_ab]
        tick()
        for _ in range(4):
            prod = [_dot(p, jnp.concatenate([p, t], axis=1)) for p, t in zip(pw, inv)]
            pw = [m[:, 0:128] for m in prod]
            inv = [t + m[:, 128:256] for t, m in zip(inv, prod)]
            tick()
        inv = [t + _dot(p, t) for p, t in zip(pw, inv)]
        tick()
        vs = [stack(v_c[j], pr) for j, pr in lanes]
        akv = [_dot(m, x) for m, x in zip(a_ak, vs)]
        tick()
        wu = [_dot(inv[i], jnp.concatenate([stack(a_0[j], pr), akv[i]], axis=1)) for i, (j, pr) in enumerate(lanes)]
        tick()
        w1 = [m[:, 0:LANES] for m in wu]
        u0v = [jnp.concatenate([m[:, LANES:2 * LANES], x], axis=0) for m, x in zip(wu, vs)]
        zero = jnp.zeros((2 * C, LANES), F32)
        xo = [_dot(jnp.concatenate([a_rb[i], a_rk[i]], axis=1),
                   jnp.concatenate([jnp.concatenate([w1[i], zero], axis=0), u0v[i]], axis=1))
              for i in range(len(lanes))]
        tick()
        fold = lambda m: m[0:C, :] + m[C:2 * C, :]
        w2 = [r_0[j][:, pr * LANES:(pr + 1) * LANES] + fold(xo[i][:, 0:LANES]) for i, (j, pr) in enumerate(lanes)]
        y0 = [fold(m[:, LANES:2 * LANES]) for m in xo]
        bh = [stack(b_h[j], pr) for j, pr in lanes]
        m_st = [_dot_tn(w1[i], bh[i]) for i in range(len(lanes))]
        tick()
        n_st = [_dot_tn(u0v[i], jnp.concatenate([bh[i], stack(k_h[j], pr)], axis=0))
                for i, (j, pr) in enumerate(lanes)]
        tick()
        return rows, e_last, w2, y0, m_st, n_st

    st = [st_scr[pr] for pr in pairs]

    def chain_steps(part):
        rows, e_last, w2, y0, m_st, n_st = part

        def step(j):
            for pr in pairs:
                i = j * len(pairs) + pr
                cols = slice(pr * LANES, (pr + 1) * LANES)
                y_scr[rows[j], cols] = _dot_nt(w2[i], st[pr]) + y0[i]
                st[pr] = st[pr] * e_last[j][:, cols] + _dot(st[pr], m_st[i]) + n_st[i]

        return [functools.partial(step, j) for j in range(len(rows))]

    n_chunks = tb // C
    first, second = list(range(n_chunks // 2)), list(range(n_chunks // 2, n_chunks))
    queue = chain_steps(state_free_part(first, lambda: None))

    def tick():
        if queue:
            queue.pop(0)()

    second_part = state_free_part(second, tick)
    for step in queue + chain_steps(second_part):
        step()
    for pr in pairs:
        st_scr[pr] = st[pr]

    y = y_scr[...]
    mean = seg_sum(y, 2) * (1.0 / RWKV_N)
    d = y - mean
    var = seg_sum(d * d, 1) * (1.0 / RWKV_N)
    y = d * lax.rsqrt(var + RWKV_LN_EPS) * lng_ref[...] + lnb_ref[...] + bonus_scr[...]
    o_ref[...] = (y * _silu(g_scr[...])).astype(o_ref.dtype)


def _rwkv(h, w, vecs, w2p, a2p, seg, layer, batch, seq, interpret):
    tb = min(TOKEN_TILE, seq)
    nt = seq // tb
    kern = functools.partial(_rwkv_kernel, tb=tb)
    mu, w0, a0, kk, ka, rk, lng, lnb = vecs
    ins = [mu, w0, w2p, a0, a2p, kk, ka, rk, lng, lnb]
    wide = lambda: pltpu.VMEM((tb, WIDTH), F32)
    n_rows = _IN_OFFS[14] - _IN_OFFS[12]
    return pl.pallas_call(
        kern,
        grid=(batch, nt),
        in_specs=[pl.BlockSpec((tb, D_MODEL), lambda b, i: (b * nt + i, 0)),
                  _w_rows_spec(w, layer, _IN_OFFS[12], n_rows)]
        + [_layer_spec(a, layer) for a in ins] + [_full_spec(seg.shape)],
        out_specs=pl.BlockSpec((tb, WIDTH), lambda b, i: (b * nt + i, 0)),
        out_shape=jax.ShapeDtypeStruct((batch * seq, WIDTH), MXU_DTYPE),
        scratch_shapes=[pltpu.VMEM((tb + CARRY_ROWS, 3 * WIDTH + 2 * RWKV_RANK), F32)]
        + [wide() for _ in range(9)]
        + [pltpu.VMEM((RWKV_HEADS // 2, LANES, LANES), F32), pltpu.VMEM((n_rows, D_MODEL), MXU_DTYPE)],
        compiler_params=_params(("arbitrary", "arbitrary")),
        interpret=interpret,
        name="rwkv7",
    )(h, w, *ins, seg)


def _memkv_kernel(mem_ref, g_ref, w_ref, k_ref, v_ref):
    x = mem_ref[0]
    y = x * lax.rsqrt(jnp.mean(x * x, axis=-1, keepdims=True) + NORM_EPS) * g_ref[...]
    kv = _dot(y, w_ref[...])
    head = _iota((1, MEM_WIDTH), 1) >> 6
    for hd in range(MEM_HEADS):
        k_ref[0, hd] = jnp.where(head == hd, kv[:, 0:MEM_WIDTH], 0.0).astype(k_ref.dtype)
        v_ref[0, hd] = jnp.where(head == hd, kv[:, MEM_WIDTH:2 * MEM_WIDTH], 0.0).astype(v_ref.dtype)


def _memkv(mem, g, w, layer, interpret):
    b, m, d = mem.shape
    out = jax.ShapeDtypeStruct((b, MEM_HEADS, m, MEM_WIDTH), MXU_DTYPE)
    return pl.pallas_call(
        _memkv_kernel,
        grid=(b,),
        in_specs=[pl.BlockSpec((1, m, d), lambda i: (i, 0, 0)), _layer_spec(g, layer), _layer_spec(w, layer)],
        out_specs=[pl.BlockSpec((1, MEM_HEADS, m, MEM_WIDTH), lambda i: (i, 0, 0, 0))] * 2,
        out_shape=[out, out],
        compiler_params=_params(("parallel",)),
        interpret=interpret,
        name="mem_kv",
    )(mem, g, w)


def _merge_kernel(x_ref, h_ref, oret_ref, ogla_ref, ossd_ref, orwkv_ref, km_ref, vm_ref,
                  wq_ref, wg_ref, uret_ref, ugla_ref, ussd_ref, urwkv_ref, umem_ref, wout_ref, gn_ref,
                  *refs):
    out_refs, wqb_scr = refs[:-1], refs[-1]

    @pl.when(_first_step())
    def _():
        wqb_scr[...] = wq_ref[0].astype(wqb_scr.dtype)

    h = h_ref[...]
    q = _proj(h, wqb_scr[...]) * MEM_HEAD_DIM ** -0.5
    scores = [_dot_nt(q, km_ref[0, hd]) for hd in range(MEM_HEADS)]
    branches = ((oret_ref, uret_ref), (ogla_ref, ugla_ref), (ossd_ref, ussd_ref), (orwkv_ref, urwkv_ref))
    merged = None
    for i in range(N_BRANCHES):
        gate = _sigmoid(_proj(h, wg_ref[i * D_MODEL:(i + 1) * D_MODEL, :]))
        if i < 4:
            o_ref, u_ref = branches[i]
            up = jnp.dot(o_ref[...], u_ref[...], preferred_element_type=F32)
        else:
            up = _dot(o_mem, umem_ref[...])
        merged = gate * up if merged is None else merged + gate * up
        if i == 0:
            o_mem = jnp.zeros(q.shape, F32)
            for hd in range(MEM_HEADS):
                s = jnp.exp(scores[hd] - jnp.max(scores[hd], axis=-1, keepdims=True))
                prob = s / jnp.sum(s, axis=-1, keepdims=True)
                o_mem = o_mem + _dot(prob, vm_ref[0, hd])
    x = x_ref[...] + _dot(merged, wout_ref[...])
    if len(out_refs) == 2:
        out_refs[0][...] = x
    y = x * lax.rsqrt(jnp.mean(x * x, axis=-1, keepdims=True) + NORM_EPS) * gn_ref[...]
    out_refs[-1][...] = y.astype(out_refs[-1].dtype)


def _merge(x2d, h, o_ret, o_gla, o_ssd, o_rwkv, km, vm, wq, wg, ups, wout, g_next, layer, last,
           batch, seq, interpret):
    tm = min(TOKEN_TILE, seq)
    nt = seq // tm
    row = lambda w: pl.BlockSpec((tm, w), lambda b, i: (b * nt + i, 0))
    kvspec = pl.BlockSpec((1,) + km.shape[1:], lambda b, i: (b, 0, 0, 0))
    weights = [wg, *ups, wout, g_next]
    return pl.pallas_call(
        _merge_kernel,
        grid=(batch, nt),
        in_specs=[row(D_MODEL), row(D_MODEL), row(WIDTH), row(WIDTH), row(WIDTH), row(WIDTH), kvspec, kvspec,
                  _w_rows_spec(wq, layer, _IN_OFFS[14], MEM_WIDTH)]
        + [_layer_spec(w, layer, pipeline_mode=pl.Buffered(1)) for w in weights],
        out_specs=[row(D_MODEL)] if last else [row(D_MODEL), row(D_MODEL)],
        out_shape=[jax.ShapeDtypeStruct(x2d.shape, F32)] if last else
        [jax.ShapeDtypeStruct(x2d.shape, F32), jax.ShapeDtypeStruct(x2d.shape, MXU_DTYPE)],
        scratch_shapes=[pltpu.VMEM((MEM_WIDTH, D_MODEL), MXU_DTYPE)],
        compiler_params=_params(("arbitrary", "arbitrary")),
        interpret=interpret,
        name="merge",
    )(x2d, h, o_ret, o_gla, o_ssd, o_rwkv, km, vm, wq, *weights)


def _pad_last(a, width):
    return jnp.pad(a, [(0, 0)] * (a.ndim - 1) + [(0, width - a.shape[-1])])


def _rows3(v, width=None):
    v = v.reshape(v.shape[0], 1, -1).astype(F32)
    return v if width is None else _pad_last(v, width)


def _forward(x, mem, positions, norm_g, w_in, gla_gk_w2, gla_gk_b, gla_norm_g,
             ssd_conv_w, ssd_conv_b, ssd_dt_bias, ssd_a_log, ssd_d, ssd_norm_g,
             rwkv_mu, rwkv_w0, rwkv_w2, rwkv_a0, rwkv_a2, rwkv_k_k, rwkv_k_a, rwkv_r_k,
             rwkv_ln_g, rwkv_ln_b, mem_norm_g, w_mem_kv,
             w_up_ret, w_up_gla, w_up_ssd, w_up_rwkv, w_up_mem, w_out, final_norm_g, interpret=False):
    batch, seq, d = x.shape
    assert d == D_MODEL and seq % (RET_CHUNK * RET_GROUP) == 0 and seq % (GLA_CHUNK * GLA_GROUP) == 0
    depth = w_in.shape[0]
    cdt = MXU_DTYPE
    o = _IN_OFFS

    half = np.arange(RET_DK // 2)
    ret_perm = np.concatenate([hd * RET_DK + 2 * half + par for par in (0, 1) for hd in range(RET_HEADS)])
    inv = 1.0 / (ROPE_BASE ** jnp.linspace(0.0, 1.0, RET_DK // 2, dtype=F32))
    inv_row = jnp.tile(inv, RET_HEADS).reshape(1, LANES)
    ret_tile, ret_rows = RET_CHUNK * RET_GROUP, RET_CHUNK * RET_GROUP // RET_HEADS
    ret_pos = jnp.repeat(positions.reshape(batch, seq // ret_tile, RET_HEADS, ret_rows).transpose(0, 1, 3, 2),
                         RET_DK // 2, axis=3).reshape(batch * seq // RET_HEADS, LANES)
    head_of_lane = np.arange(WIDTH) // SSD_P
    ssd_expand = jnp.asarray(np.arange(LANES)[:, None] == head_of_lane[None, :], F32)
    rwkv_seg = jnp.asarray(head_of_lane[:256, None] == head_of_lane[None, :256], F32)

    w_t = jnp.swapaxes(w_in, 1, 2)
    w_g = w_t[:, o[15]:o[16], :].astype(cdt)
    ret_perm_rows = jnp.asarray(ret_perm[:, None] == np.arange(RET_HEADS * RET_DK)[None, :], cdt)
    gla_w2p = jnp.pad(gla_gk_w2, ((0, 0), (0, LANES - GLA_RANK), (0, 0))).astype(cdt)
    zeros_rank = jnp.zeros((depth, RWKV_RANK, WIDTH), F32)
    rwkv_w2p = jnp.concatenate([rwkv_w2, zeros_rank], axis=1).astype(cdt)
    rwkv_a2p = jnp.concatenate([zeros_rank, rwkv_a2], axis=1).astype(cdt)
    rwkv_vecs = [_rows3(v) for v in (rwkv_mu, rwkv_w0, rwkv_a0, rwkv_k_k, rwkv_k_a, rwkv_r_k,
                                     rwkv_ln_g, rwkv_ln_b)]
    ssd_small = [ssd_conv_w.astype(F32), _rows3(ssd_conv_b), _rows3(ssd_dt_bias, LANES),
                 _rows3(-jnp.exp(ssd_a_log.astype(F32)), LANES), _rows3(jnp.repeat(ssd_d, SSD_P, axis=1)),
                 _rows3(ssd_norm_g)]
    gla_b, gla_ng = _rows3(gla_gk_b), _rows3(gla_norm_g)
    mem_g, w_kv = _rows3(mem_norm_g), w_mem_kv.astype(cdt)
    ups = [w.astype(cdt) for w in (w_up_ret, w_up_gla, w_up_ssd, w_up_rwkv, w_up_mem)]
    w_o = w_out.astype(cdt)
    g_next = _rows3(jnp.concatenate([norm_g[1:], final_norm_g[None]], axis=0))

    x2d = x.reshape(batch * seq, d)
    h = _rmsnorm(x2d, norm_g[0], cdt, interpret)
    for l in range(depth):
        o_ret = _retention(h, ret_pos, inv_row, w_t, ret_perm_rows, l, batch, seq, interpret)
        o_gla = _gla(h, w_t, gla_w2p, gla_b, gla_ng, l, batch, seq, interpret)
        o_ssd = _ssd(h, w_t, *ssd_small, ssd_expand, l, batch, seq, interpret)
        o_rwkv = _rwkv(h, w_t, rwkv_vecs, rwkv_w2p, rwkv_a2p, rwkv_seg, l, batch, seq, interpret)
        km, vm = _memkv(mem, mem_g, w_kv, l, interpret)
        last = l == depth - 1
        outs = _merge(x2d, h, o_ret, o_gla, o_ssd, o_rwkv, km, vm, w_t, w_g, ups, w_o, g_next, l, last,
                      batch, seq, interpret)
        x2d, h = (None, outs[0]) if last else outs
    return h.reshape(batch, seq, d)


def kernel(x, mem, positions, norm_g, w_in, gla_gk_w2, gla_gk_b, gla_norm_g, ssd_conv_w, ssd_conv_b, ssd_dt_bias, ssd_a_log, ssd_d, ssd_norm_g, rwkv_mu, rwkv_w0, rwkv_w2, rwkv_a0, rwkv_a2, rwkv_k_k, rwkv_k_a, rwkv_r_k, rwkv_ln_g, rwkv_ln_b, mem_norm_g, w_mem_kv, w_up_ret, w_up_gla, w_up_ssd, w_up_rwkv, w_up_mem, w_out, final_norm_g):
    return _forward(x, mem, positions, norm_g, w_in, gla_gk_w2, gla_gk_b, gla_norm_g,
                    ssd_conv_w, ssd_conv_b, ssd_dt_bias, ssd_a_log, ssd_d, ssd_norm_g,
                    rwkv_mu, rwkv_w0, rwkv_w2, rwkv_a0, rwkv_a2, rwkv_k_k, rwkv_k_a, rwkv_r_k,
                    rwkv_ln_g, rwkv_ln_b, mem_norm_g, w_mem_kv,
                    w_up_ret, w_up_gla, w_up_ssd, w_up_rwkv, w_up_mem, w_out, final_norm_g)
```

```python
import functools
import math

import jax
import jax.numpy as jnp
import numpy as np
from jax import lax
from jax.experimental import pallas as pl
from jax.experimental.pallas import tpu as pltpu

F32 = jnp.float32
BF16 = jnp.bfloat16
MXU_DTYPE = jnp.bfloat16

D_MODEL = 1024
WIDTH = 512
NORM_EPS = 1e-6
N_BRANCHES = 5

RET_HEADS, RET_DK, RET_DV, RET_CHUNK = 4, 64, 128, 128
ROPE_BASE = 10000.0
GLA_HEADS, GLA_DK, GLA_DV, GLA_RANK, GLA_NORMALIZER, GLA_CHUNK = 4, 64, 128, 16, 16.0, 64
SSD_HEADS, SSD_P, SSD_GROUPS, SSD_STATE, SSD_CONV, SSD_CHUNK = 8, 64, 2, 128, 4, 128
RWKV_HEADS, RWKV_N, RWKV_RANK, RWKV_CHUNK = 8, 64, 64, 64
RWKV_LN_EPS = 64e-5
RET_GROUP = 8
SSD_GROUP = 4
GLA_GROUP = 16
MEM_HEADS, MEM_HEAD_DIM, MEM_WIDTH = 4, 64, 256

LANES = 128
CARRY_ROWS = 8
TOKEN_TILE = 512
VMEM_LIMIT = 56 * 1024 * 1024

_IN_SIZES = (256, 256, 512, 512, 256, 256, 512, 16, 512, 1024, 8, 512, 1664, 512, 256, 5120)
_IN_OFFS = tuple(int(v) for v in np.cumsum((0,) + _IN_SIZES))


def _dot(a, b):
    return jnp.dot(a.astype(MXU_DTYPE), b.astype(MXU_DTYPE), preferred_element_type=F32)


def _dot_nt(a, b):
    return lax.dot_general(a.astype(MXU_DTYPE), b.astype(MXU_DTYPE), (((1,), (1,)), ((), ())),
                           preferred_element_type=F32)


def _dot_tn(a, b):
    return lax.dot_general(a.astype(MXU_DTYPE), b.astype(MXU_DTYPE), (((0,), (0,)), ((), ())),
                           preferred_element_type=F32)


def _proj(x, w_t):
    return lax.dot_general(x, w_t, (((1,), (1,)), ((), ())), preferred_element_type=F32)


def _split3(x):
    hi = x.astype(BF16)
    r1 = x - hi.astype(F32)
    mid = r1.astype(BF16)
    lo = (r1 - mid.astype(F32)).astype(BF16)
    return hi, mid, lo


def _sel_dot(sel, x):
    return sum(jnp.dot(sel, p, preferred_element_type=F32) for p in _split3(x))


def _dot_sel(x, sel, pieces=3):
    return sum(jnp.dot(p, sel, preferred_element_type=F32) for p in _split3(x)[:pieces])


def _iota(shape, dim):
    return lax.broadcasted_iota(jnp.int32, shape, dim)


def _tril(n, strict=False):
    r, c = _iota((n, n), 0), _iota((n, n), 1)
    return (r > c) if strict else (r >= c)


def _sigmoid(x):
    return 0.5 + 0.5 * jnp.tanh(0.5 * x)


def _silu(x):
    half = 0.5 * x
    return half + half * jnp.tanh(half)


def _softplus(x):
    return jnp.maximum(x, 0.0) + jnp.log(1.0 + jnp.exp(-jnp.abs(x)))


def _rows(c, n):
    return pl.ds(pl.multiple_of(c * n, n), n)


def _loop(trips, body):
    if trips == 1:
        body(0, 0)
    else:
        lax.fori_loop(0, trips, body, 0)


def _full_spec(shape):
    zeros = (0,) * len(shape)
    return pl.BlockSpec(shape, lambda *_: zeros)


def _layer_spec(arr, layer, **kwargs):
    tail = tuple(arr.shape[1:])
    index = (layer,) + (0,) * len(tail)
    return pl.BlockSpec((None,) + tail, lambda *_: index, **kwargs)


def _w_rows_spec(w_t, layer, row0, rows):
    return pl.BlockSpec((pl.Element(1), pl.Element(rows), pl.Element(w_t.shape[2])),
                        lambda *_: (layer, row0, 0), pipeline_mode=pl.Buffered(1))


def _first_step():
    return (pl.program_id(0) == 0) & (pl.program_id(1) == 0)


def _params(semantics):
    return pltpu.CompilerParams(dimension_semantics=semantics, vmem_limit_bytes=VMEM_LIMIT)


def _rmsnorm_kernel(x_ref, g_ref, o_ref):
    x = x_ref[...]
    y = x * lax.rsqrt(jnp.mean(x * x, axis=-1, keepdims=True) + NORM_EPS)
    o_ref[...] = (y * g_ref[...]).astype(o_ref.dtype)


def _rmsnorm(x2d, g, out_dtype, interpret):
    m, d = x2d.shape
    tm = min(1024, m)
    return pl.pallas_call(
        _rmsnorm_kernel,
        grid=(m // tm,),
        in_specs=[pl.BlockSpec((tm, d), lambda i: (i, 0)), _full_spec((1, d))],
        out_specs=pl.BlockSpec((tm, d), lambda i: (i, 0)),
        out_shape=jax.ShapeDtypeStruct((m, d), out_dtype),
        compiler_params=_params(("parallel",)),
        interpret=interpret,
        name="rmsnorm",
    )(x2d, g.reshape(1, d))


def _ret_kernel(h_ref, pos_ref, inv_ref, w_ref, perm_ref, o_ref, p_scr, s_scr, wb_scr, *, tb):
    C = RET_CHUNK

    @pl.when(_first_step())
    def _():
        for lo in (0, 256):
            rows = w_ref[0, lo:lo + 256, :].astype(wb_scr.dtype)
            wb_scr[lo:lo + 256, :] = jnp.dot(perm_ref[...], rows, preferred_element_type=F32).astype(wb_scr.dtype)
        wb_scr[512:1536, :] = w_ref[0, 512:1536, :].astype(wb_scr.dtype)

    @pl.when(pl.program_id(1) == 0)
    def _():
        s_scr[...] = jnp.zeros_like(s_scr)

    ang = pos_ref[...].astype(F32) * inv_ref[...]
    lane_group = _iota((1, LANES), 1) >> 5

    def unpack(t):
        blocks = []
        for j in range(RET_HEADS):
            own = jnp.where(lane_group == j, t, 0.0)
            blocks.append(own + pltpu.roll(own, 32, 1) + pltpu.roll(own, 64, 1) + pltpu.roll(own, 96, 1))
        return jnp.concatenate(blocks, axis=0)

    cos, sin = unpack(jnp.cos(ang)), unpack(jnp.sin(ang))
    p_scr[:, 512:1536] = _proj(h_ref[...], wb_scr[512:1536, :])
    p = _proj(h_ref[...], wb_scr[0:512, :])
    q1, q2 = p[:, 0:128], p[:, 128:256]
    k1, k2 = p[:, 256:384] * RET_DK ** -0.5, p[:, 384:512] * RET_DK ** -0.5
    p_scr[:, 0:128] = q1 * cos - q2 * sin
    p_scr[:, 128:256] = q2 * cos + q1 * sin
    p_scr[:, 256:384] = k1 * cos - k2 * sin
    p_scr[:, 384:512] = k2 * cos + k1 * sin

    def log_gamma(head):
        return jnp.log(1.0 - jnp.exp2(-5.0 - head.astype(F32)))

    qk_head = (_iota((1, 256), 1) >> 5) & 3
    lg_lane = log_gamma(qk_head)
    tau = _iota((C, 1), 0).astype(F32)
    dq = jnp.exp(lg_lane * (tau + 1.0))
    dk = jnp.exp(lg_lane * (C - 1.0 - tau))
    ds = jnp.exp(lg_lane * float(C))
    diff = (_iota((C, C), 0) - _iota((C, C), 1)).astype(F32)
    causal = _tril(C)
    bd_mask = (_iota((WIDTH, 256), 0) >> 7) == ((_iota((WIDTH, 256), 1) >> 5) & 3)

    heads = range(RET_HEADS)
    hcols = [slice(hd * RET_DV, (hd + 1) * RET_DV) for hd in heads]
    seg = [jnp.where(causal, jnp.exp(math.log(1.0 - 2.0 ** (-5.0 - hd)) * diff), 0.0) for hd in heads]

    def group(gi, carry):
        us = range(RET_GROUP)
        rows = [_rows(gi * RET_GROUP + u, C) for u in us]
        q = [p_scr[rw, 0:256] for rw in rows]
        k = [p_scr[rw, 256:512] for rw in rows]
        v = [p_scr[rw, 512:1024] for rw in rows]
        sc = [[_dot_nt(q[u], jnp.where(qk_head == hd, k[u], 0.0)) * seg[hd] for hd in heads] for u in us]
        y_intra = [[_dot(sc[u][hd], v[u][:, hcols[hd]]) for hd in heads] for u in us]
        kv = [jnp.where(bd_mask, _dot_tn(v[u], k[u] * dk), 0.0) for u in us]
        st = s_scr[...]
        y_inter = []
        for u in us:
            y_inter.append(_dot_nt(q[u] * dq, st))
            st = st * ds + kv[u]
        s_scr[...] = st
        for u in us:
            for hd in heads:
                y = y_intra[u][hd] + y_inter[u][:, hcols[hd]]
                y = y * lax.rsqrt(jnp.mean(y * y, axis=-1, keepdims=True) + NORM_EPS)
                g = p_scr[rows[u], 1024 + hd * RET_DV:1024 + (hd + 1) * RET_DV]
                o_ref[rows[u], hcols[hd]] = (y * _silu(g)).astype(o_ref.dtype)
        return carry

    _loop(tb // (C * RET_GROUP), group)


def _retention(h, ret_pos, inv_row, w, perm, layer, batch, seq, interpret):
    tb = RET_CHUNK * RET_GROUP
    nt = seq // tb
    kern = functools.partial(_ret_kernel, tb=tb)
    return pl.pallas_call(
        kern,
        grid=(batch, nt),
        in_specs=[pl.BlockSpec((tb, D_MODEL), lambda b, i: (b * nt + i, 0)),
                  pl.BlockSpec((tb // RET_HEADS, LANES), lambda b, i: (b * nt + i, 0)),
                  _full_spec((1, LANES)),
                  _w_rows_spec(w, layer, _IN_OFFS[0], 1536), _full_spec(perm.shape)],
        out_specs=pl.BlockSpec((tb, WIDTH), lambda b, i: (b * nt + i, 0)),
        out_shape=jax.ShapeDtypeStruct((batch * seq, WIDTH), MXU_DTYPE),
        scratch_shapes=[pltpu.VMEM((tb, 1536), F32), pltpu.VMEM((WIDTH, 256), F32),
                        pltpu.VMEM((1536, D_MODEL), MXU_DTYPE)],
        compiler_params=_params(("arbitrary", "arbitrary")),
        interpret=interpret,
        name="retention",
    )(h, ret_pos, inv_row, w, perm)


def _gla_kernel(h_ref, w_ref, w2_ref, gb_ref, ng_ref, o_ref, p_scr, lg_scr, s_scr, wb_scr, *, tb):
    C = GLA_CHUNK

    @pl.when(_first_step())
    def _():
        cast = lambda v: v.astype(wb_scr.dtype)
        wb_scr[0:1024, :] = cast(w_ref[0, 0:1024, :])
        wb_scr[1024:1536, :] = cast(w_ref[0, 1024 + GLA_RANK:1536 + GLA_RANK, :])
        wb_scr[1536:1664, :] = cast(jnp.concatenate(
            [w_ref[0, 1024:1024 + GLA_RANK, :], jnp.zeros((LANES - GLA_RANK, D_MODEL), F32)], axis=0))

    @pl.when(pl.program_id(1) == 0)
    def _():
        s_scr[...] = jnp.zeros_like(s_scr)

    pre = _dot(_proj(h_ref[...], wb_scr[1536:1664, :]), w2_ref[...]) + gb_ref[...]
    lg_scr[...] = -_softplus(-pre) / GLA_NORMALIZER
    tri = _tril(C)
    assert tb == C * GLA_GROUP
    tri_sel = tri.astype(BF16)
    cum_all = [_sel_dot(tri_sel, lg_scr[u * C:(u + 1) * C, :]) for u in range(GLA_GROUP)]
    p_scr[:, 0:512] = _proj(h_ref[...], wb_scr[0:512, :])
    p_scr[:, 512:1536] = _proj(h_ref[...], wb_scr[512:1536, :])

    k_head = _iota((1, 256), 1) >> 6
    pairs = range(GLA_HEADS // 2)
    bd_mask = (_iota((256, LANES), 0) >> 7) == (_iota((256, LANES), 1) >> 6)
    ng = ng_ref[...]

    heads = range(GLA_HEADS)
    hcols = [slice(hd * GLA_DV, (hd + 1) * GLA_DV) for hd in heads]

    def group(gi, carry):
        us = range(GLA_GROUP)
        rows = [_rows(gi * GLA_GROUP + u, C) for u in us]
        q = [p_scr[rw, 0:256] * GLA_DK ** -0.5 for rw in rows]
        k = [p_scr[rw, 256:512] for rw in rows]
        v = [p_scr[rw, 512:1024] for rw in rows]
        cum = cum_all
        ref = [c[C // 2:C // 2 + 1, :] for c in cum]
        last = [c[C - 1:C, :] for c in cum]
        q_in = [q[u] * jnp.exp(cum[u] - ref[u]) for u in us]
        k_in = [k[u] * jnp.exp(ref[u] - cum[u]) for u in us]
        q_dec = [q[u] * jnp.exp(cum[u]) for u in us]
        k_st = [k[u] * jnp.exp(last[u] - cum[u]) for u in us]
        sc = [[jnp.where(tri, _dot_nt(q_in[u], jnp.where(k_head == hd, k_in[u], 0.0)), 0.0) for hd in heads]
              for u in us]
        y_intra = [[_dot(sc[u][hd], v[u][:, hcols[hd]]) for hd in heads] for u in us]
        vcols = [slice(p * 256, (p + 1) * 256) for p in pairs]
        kcols = [slice(p * LANES, (p + 1) * LANES) for p in pairs]
        kv = [[jnp.where(bd_mask, _dot_tn(v[u][:, vcols[p]], k_st[u][:, kcols[p]]), 0.0) for p in pairs]
              for u in us]
        st = [s_scr[p] for p in pairs]
        y_inter = []
        for u in us:
            y_inter.append(jnp.concatenate([_dot_nt(q_dec[u][:, kcols[p]], st[p]) for p in pairs], axis=1))
            st = [st[p] * jnp.exp(last[u][:, kcols[p]]) + kv[u][p] for p in pairs]
        for p in pairs:
            s_scr[p] = st[p]
        for u in us:
            for hd in heads:
                y = y_intra[u][hd] + y_inter[u][:, hcols[hd]]
                y = y * lax.rsqrt(jnp.mean(y * y, axis=-1, keepdims=True) + NORM_EPS) * ng
                g = p_scr[rows[u], 1024 + hd * GLA_DV:1024 + (hd + 1) * GLA_DV]
                o_ref[rows[u], hcols[hd]] = (y * _silu(g)).astype(o_ref.dtype)
        return carry

    _loop(tb // (C * GLA_GROUP), group)


def _gla(h, w, w2p, gb, ng, layer, batch, seq, interpret):
    tb = GLA_CHUNK * GLA_GROUP
    nt = seq // tb
    kern = functools.partial(_gla_kernel, tb=tb)
    return pl.pallas_call(
        kern,
        grid=(batch, nt),
        in_specs=[pl.BlockSpec((tb, D_MODEL), lambda b, i: (b * nt + i, 0)),
                  _w_rows_spec(w, layer, _IN_OFFS[4], 1536 + GLA_RANK), _layer_spec(w2p, layer),
                  _layer_spec(gb, layer), _layer_spec(ng, layer)],
        out_specs=pl.BlockSpec((tb, WIDTH), lambda b, i: (b * nt + i, 0)),
        out_shape=jax.ShapeDtypeStruct((batch * seq, WIDTH), MXU_DTYPE),
        scratch_shapes=[pltpu.VMEM((tb, 1536), F32), pltpu.VMEM((tb, 256), F32),
                        pltpu.VMEM((GLA_HEADS // 2, 256, LANES), F32), pltpu.VMEM((1664, D_MODEL), MXU_DTYPE)],
        compiler_params=_params(("arbitrary", "arbitrary")),
        interpret=interpret,
        name="gla",
    )(h, w, w2p, gb, ng)


def _ssd_kernel(h_ref, w_ref, cw_ref, cb_ref, dtb_ref, a_ref, dskip_ref, ng_ref, exp_ref, o_ref,
                raw_scr, xc_scr, z_scr, dt_scr, s_scr, wb_scr, *, tb):
    C = SSD_CHUNK
    NCH = 1024

    @pl.when(_first_step())
    def _():
        cast = lambda v: v.astype(wb_scr.dtype)
        wb_scr[0:NCH, :] = cast(w_ref[0, 0:NCH, :])
        wb_scr[NCH:NCH + WIDTH, :] = cast(w_ref[0, NCH + SSD_HEADS:NCH + SSD_HEADS + WIDTH, :])
        wb_scr[1536:1664, :] = cast(jnp.concatenate(
            [w_ref[0, NCH:NCH + SSD_HEADS, :], jnp.zeros((LANES - SSD_HEADS, D_MODEL), F32)], axis=0))

    @pl.when(pl.program_id(1) == 0)
    def _():
        s_scr[...] = jnp.zeros_like(s_scr)
        raw_scr[0:CARRY_ROWS, :] = jnp.zeros((CARRY_ROWS, NCH), F32)

    dt_scr[...] = _softplus(_proj(h_ref[...], wb_scr[1536:1664, :]) + dtb_ref[...])
    first_tap = CARRY_ROWS - (SSD_CONV - 1)
    for lo in range(0, NCH, 256):
        cols = slice(lo, lo + 256)
        raw_scr[CARRY_ROWS:CARRY_ROWS + tb, cols] = _proj(h_ref[...], wb_scr[cols, :])
        conv = cb_ref[:, cols] + sum(raw_scr[first_tap + j:first_tap + j + tb, cols] * cw_ref[j:j + 1, cols]
                                     for j in range(SSD_CONV))
        xc_scr[:, cols] = _silu(conv)
        raw_scr[0:CARRY_ROWS, cols] = raw_scr[tb:tb + CARRY_ROWS, cols]
    z_scr[...] = _proj(h_ref[...], wb_scr[NCH:NCH + WIDTH, :])

    tri = _tril(C)
    lane_lo = _iota((1, LANES), 1) < SSD_P
    tri_sel = tri.astype(BF16)
    expand = exp_ref[...].astype(BF16)
    a_row = a_ref[...]

    groups = range(SSD_GROUPS)
    gcols = [slice(g * 256, (g + 1) * 256) for g in groups]
    ncols = [slice(g * SSD_STATE, (g + 1) * SSD_STATE) for g in groups]

    def group(gi, carry):
        us = range(SSD_GROUP)
        rows = [_rows(gi * SSD_GROUP + u, C) for u in us]
        xs = [xc_scr[rw, 0:512] for rw in rows]
        bm = [xc_scr[rw, 512:768] for rw in rows]
        cm = [xc_scr[rw, 768:1024] for rw in rows]
        dt = [dt_scr[rw, :] for rw in rows]
        cum = [_sel_dot(tri_sel, dt[u] * a_row) for u in us]
        cum_t = [c.T for c in cum]
        dt_e = [_dot_sel(d, expand) for d in dt]
        cum_e = [_dot_sel(c, expand) for c in cum]
        last_e = [c[C - 1:C, :] for c in cum_e]
        xdt = [xs[u] * dt_e[u] for u in us]
        v_st = [xdt[u] * jnp.exp(last_e[u] - cum_e[u]) for u in us]
        e_cum = [jnp.exp(c) for c in cum_e]
        scores = [[_dot_nt(cm[u][:, ncols[g]], bm[u][:, ncols[g]]) for g in groups] for u in us]
        y_intra = []
        for u in us:
            parts = []
            for pr in range(SSD_HEADS // 2):
                sc = scores[u][pr // 2]
                segs = []
                for hd in (2 * pr, 2 * pr + 1):
                    d = jnp.minimum(cum[u][:, hd:hd + 1] - cum_t[u][hd:hd + 1, :], 0.0)
                    segs.append(sc * jnp.where(tri, jnp.exp(d), 0.0))
                xp = xdt[u][:, pr * LANES:(pr + 1) * LANES]
                rhs = jnp.concatenate([jnp.where(lane_lo, xp, 0.0), jnp.where(lane_lo, 0.0, xp)], axis=0)
                parts.append(_dot(jnp.concatenate(segs, axis=1), rhs))
            y_intra.append(jnp.concatenate(parts, axis=1))
        kv = [[_dot_tn(bm[u][:, ncols[g]], v_st[u][:, gcols[g]]) for g in groups] for u in us]
        st = [s_scr[g] for g in groups]
        y_inter = []
        for u in us:
            y_inter.append(jnp.concatenate(
                [_dot(cm[u][:, ncols[g]], st[g]) * e_cum[u][:, gcols[g]] for g in groups], axis=1))
            st = [st[g] * jnp.exp(last_e[u][:, gcols[g]]) + kv[u][g] for g in groups]
        for g in groups:
            s_scr[g] = st[g]
        for u in us:
            y = y_intra[u] + y_inter[u] + dskip_ref[...] * xs[u]
            y = y * _silu(z_scr[rows[u], :])
            for g in groups:
                yg = y[:, gcols[g]]
                yg = yg * lax.rsqrt(jnp.mean(yg * yg, axis=-1, keepdims=True) + NORM_EPS)
                o_ref[rows[u], gcols[g]] = (yg * ng_ref[:, gcols[g]]).astype(o_ref.dtype)
        return carry

    _loop(tb // (C * SSD_GROUP), group)


def _ssd(h, w, cw, cb, dtb, a_row, dskip, ng, expand, layer, batch, seq, interpret):
    tb = SSD_CHUNK * SSD_GROUP
    nt = seq // tb
    kern = functools.partial(_ssd_kernel, tb=tb)
    small = [cw, cb, dtb, a_row, dskip, ng]
    return pl.pallas_call(
        kern,
        grid=(batch, nt),
        in_specs=[pl.BlockSpec((tb, D_MODEL), lambda b, i: (b * nt + i, 0)),
                  _w_rows_spec(w, layer, _IN_OFFS[9], 1536 + SSD_HEADS)]
        + [_layer_spec(a, layer) for a in small] + [_full_spec(expand.shape)],
        out_specs=pl.BlockSpec((tb, WIDTH), lambda b, i: (b * nt + i, 0)),
        out_shape=jax.ShapeDtypeStruct((batch * seq, WIDTH), MXU_DTYPE),
        scratch_shapes=[pltpu.VMEM((tb + CARRY_ROWS, 1024), F32), pltpu.VMEM((tb, 1024), F32),
                        pltpu.VMEM((tb, WIDTH), F32), pltpu.VMEM((tb, LANES), F32),
                        pltpu.VMEM((SSD_GROUPS, SSD_STATE, 256), F32), pltpu.VMEM((1664, D_MODEL), MXU_DTYPE)],
        compiler_params=_params(("arbitrary", "arbitrary")),
        interpret=interpret,
        name="ssd",
    )(h, w, *small, expand)


def _rwkv_kernel(h_ref, w_ref, mu_ref, w0_ref, w2_ref, a0_ref, a2_ref, kk_ref, ka_ref, rk_ref,
                 lng_ref, lnb_ref, seg_ref, o_ref,
                 u_scr, g_scr, r_scr, lw_scr, k_scr, v_scr, a_scr, b_scr, y_scr, bonus_scr, st_scr, wb_scr,
                 *, tb):
    C = RWKV_CHUNK
    NU = 3 * WIDTH + 2 * RWKV_RANK

    @pl.when(_first_step())
    def _():
        wb_scr[...] = w_ref[0].astype(wb_scr.dtype)

    @pl.when(pl.program_id(1) == 0)
    def _():
        st_scr[...] = jnp.zeros_like(st_scr)
        u_scr[0:CARRY_ROWS, :] = jnp.zeros((CARRY_ROWS, NU), F32)

    seg = seg_ref[...].astype(BF16)

    def seg_sum(x, pieces):
        return jnp.concatenate([_dot_sel(x[:, j * 256:(j + 1) * 256], seg, pieces) for j in range(2)], axis=1)

    def shifted(lo, hi):
        u = _proj(h_ref[...], wb_scr[lo:hi, :])
        u_scr[CARRY_ROWS:CARRY_ROWS + tb, lo:hi] = u
        u_prev = u_scr[CARRY_ROWS - 1:CARRY_ROWS - 1 + tb, lo:hi]
        u_scr[0:CARRY_ROWS, lo:hi] = u_scr[tb:tb + CARRY_ROWS, lo:hi]
        return u + (u_prev - u) * mu_ref[:, lo:hi]

    wa = shifted(1536, 1664)
    w_pre = w0_ref[...] + _dot(jnp.tanh(wa), w2_ref[...])
    a_gate = _sigmoid(a0_ref[...] + _dot(wa, a2_ref[...]))
    lw_scr[...] = _sigmoid(w_pre) * (-math.exp(-0.5))
    k = shifted(512, 1024)
    kk = k * kk_ref[...]
    k = k * (1.0 + (a_gate - 1.0) * ka_ref[...])
    kk = kk / jnp.maximum(jnp.sqrt(seg_sum(kk * kk, 1)), 1e-12)
    k_scr[...] = k
    a_scr[...] = -kk
    b_scr[...] = kk * a_gate
    r = shifted(0, 512)
    r_scr[...] = r
    v = shifted(1024, 1536)
    v_scr[...] = v
    bonus_scr[...] = seg_sum(r * k * rk_ref[...], 1) * v
    g_scr[...] = _proj(h_ref[...], wb_scr[NU:NU + WIDTH, :])

    tri_sel = _tril(C).astype(BF16)
    r2, c2 = _iota((2 * C, 2 * C), 0), _iota((2 * C, 2 * C), 1)
    same = (r2 >> 6) == (c2 >> 6)
    low_s = same & ((r2 & 63) > (c2 & 63))
    low_i = same & ((r2 & 63) >= (c2 & 63))
    eye = (r2 == c2).astype(F32)
    lane_lo = _iota((1, LANES), 1) < RWKV_N

    pairs = range(RWKV_HEADS // 2)

    def stack_pair(xp):
        return jnp.concatenate([jnp.where(lane_lo, xp, 0.0), jnp.where(lane_lo, 0.0, xp)], axis=0)

    def stack(x, pr):
        return stack_pair(x[:, pr * LANES:(pr + 1) * LANES])

    def state_free_part(chunks, tick):
        lanes = [(j, pr) for j in range(len(chunks)) for pr in pairs]
        rows, a_0, r_0, v_c, b_h, k_h, e_last, lhs, rhs_t = [], [], [], [], [], [], [], {}, {}
        for j, u in enumerate(chunks):
            rw = _rows(u, C)
            lw = lw_scr[rw, :]
            cum = _sel_dot(tri_sel, lw)
            cum_p = cum - lw
            ref = cum[C // 2:C // 2 + 1, :]
            last = cum[C - 1:C, :]
            e_fwd = jnp.exp(cum - ref)
            e_bwd = jnp.exp(ref - cum)
            e_end = jnp.exp(last - cum)
            r_c, k_c, a_c, b_c = r_scr[rw, :], k_scr[rw, :], a_scr[rw, :], b_scr[rw, :]
            r_t = r_c * e_fwd
            a_t = a_c * jnp.exp(cum_p - ref)
            b_t = b_c * e_bwd
            k_t = k_c * e_bwd
            for pr in pairs:
                lhs[j, pr] = jnp.concatenate([stack(a_t, pr), stack(r_t, pr)], axis=0)
                rhs_t[j, pr] = jnp.concatenate([stack(b_t, pr), stack(k_t, pr)], axis=0)
            rows.append(rw)
            a_0.append(a_c * jnp.exp(cum_p))
            r_0.append(r_c * jnp.exp(cum))
            v_c.append(v_scr[rw, :])
            b_h.append(b_c * e_end)
            k_h.append(k_c * e_end)
            e_last.append(jnp.exp(last))
        tick()
        big = [_dot_nt(lhs[ln], rhs_t[ln]) for ln in lanes]
        tick()
        a_ab = [jnp.where(low_s, m[0:128, 0:128], 0.0) for m in big]
        a_ak = [jnp.where(low_s, m[0:128, 128:256], 0.0) for m in big]
        a_rb = [jnp.where(low_i, m[128:256, 0:128], 0.0) for m in big]
        a_rk = [jnp.where(low_i, m[128:256, 128:256], 0.0) for m in big]
        inv = [eye + m for m in a_ab]
        pw = [_dot(m, m) for m in a_ab]
        tick()
        for _ in range(4):
            prod = [_dot(p, jnp.concatenate([p, t], axis=1)) for p, t in zip(pw, inv)]
            pw = [m[:, 0:128] for m in prod]
            inv = [t + m[:, 128:256] for t, m in zip(inv, prod)]
            tick()
        inv = [t + _dot(p, t) for p, t in zip(pw, inv)]
        tick()
        vs = [stack(v_c[j], pr) for j, pr in lanes]
        akv = [_dot(m, x) for m, x in zip(a_ak, vs)]
        tick()
        wu = [_dot(inv[i], jnp.concatenate([stack(a_0[j], pr), akv[i]], axis=1)) for i, (j, pr) in enumerate(lanes)]
        tick()
        w1 = [m[:, 0:LANES] for m in wu]
        u0v = [jnp.concatenate([m[:, LANES:2 * LANES], x], axis=0) for m, x in zip(wu, vs)]
        zero = jnp.zeros((2 * C, LANES), F32)
        xo = [_dot(jnp.concatenate([a_rb[i], a_rk[i]], axis=1),
                   jnp.concatenate([jnp.concatenate([w1[i], zero], axis=0), u0v[i]], axis=1))
              for i in range(len(lanes))]
        tick()
        fold = lambda m: m[0:C, :] + m[C:2 * C, :]
        w2 = [r_0[j][:, pr * LANES:(pr + 1) * LANES] + fold(xo[i][:, 0:LANES]) for i, (j, pr) in enumerate(lanes)]
        y0 = [fold(m[:, LANES:2 * LANES]) for m in xo]
        bh = [stack(b_h[j], pr) for j, pr in lanes]
        m_st = [_dot_tn(w1[i], bh[i]) for i in range(len(lanes))]
        tick()
        n_st = [_dot_tn(u0v[i], jnp.concatenate([bh[i], stack(k_h[j], pr)], axis=0))
                for i, (j, pr) in enumerate(lanes)]
        tick()
        return rows, e_last, w2, y0, m_st, n_st

    st = [st_scr[pr] for pr in pairs]

    def chain_steps(part):
        rows, e_last, w2, y0, m_st, n_st = part

        def step(j):
            for pr in pairs:
                i = j * len(pairs) + pr
                cols = slice(pr * LANES, (pr + 1) * LANES)
                y_scr[rows[j], cols] = _dot_nt(w2[i], st[pr]) + y0[i]
                st[pr] = st[pr] * e_last[j][:, cols] + _dot(st[pr], m_st[i]) + n_st[i]

        return [functools.partial(step, j) for j in range(len(rows))]

    n_chunks = tb // C
    first, second = list(range(n_chunks // 2)), list(range(n_chunks // 2, n_chunks))
    queue = chain_steps(state_free_part(first, lambda: None))

    def tick():
        if queue:
            queue.pop(0)()

    second_part = state_free_part(second, tick)
    for step in queue + chain_steps(second_part):
        step()
    for pr in pairs:
        st_scr[pr] = st[pr]

    y = y_scr[...]
    mean = seg_sum(y, 2) * (1.0 / RWKV_N)
    d = y - mean
    var = seg_sum(d * d, 1) * (1.0 / RWKV_N)
    y = d * lax.rsqrt(var + RWKV_LN_EPS) * lng_ref[...] + lnb_ref[...] + bonus_scr[...]
    o_ref[...] = (y * _silu(g_scr[...])).astype(o_ref.dtype)


def _rwkv(h, w, vecs, w2p, a2p, seg, layer, batch, seq, interpret):
    tb = min(TOKEN_TILE, seq)
    nt = seq // tb
    kern = functools.partial(_rwkv_kernel, tb=tb)
    mu, w0, a0, kk, ka, rk, lng, lnb = vecs
    ins = [mu, w0, w2p, a0, a2p, kk, ka, rk, lng, lnb]
    wide = lambda: pltpu.VMEM((tb, WIDTH), F32)
    n_rows = _IN_OFFS[14] - _IN_OFFS[12]
    return pl.pallas_call(
        kern,
        grid=(batch, nt),
        in_specs=[pl.BlockSpec((tb, D_MODEL), lambda b, i: (b * nt + i, 0)),
                  _w_rows_spec(w, layer, _IN_OFFS[12], n_rows)]
        + [_layer_spec(a, layer) for a in ins] + [_full_spec(seg.shape)],
        out_specs=pl.BlockSpec((tb, WIDTH), lambda b, i: (b * nt + i, 0)),
        out_shape=jax.ShapeDtypeStruct((batch * seq, WIDTH), MXU_DTYPE),
        scratch_shapes=[pltpu.VMEM((tb + CARRY_ROWS, 3 * WIDTH + 2 * RWKV_RANK), F32)]
        + [wide() for _ in range(9)]
        + [pltpu.VMEM((RWKV_HEADS // 2, LANES, LANES), F32), pltpu.VMEM((n_rows, D_MODEL), MXU_DTYPE)],
        compiler_params=_params(("arbitrary", "arbitrary")),
        interpret=interpret,
        name="rwkv7",
    )(h, w, *ins, seg)


def _memkv_kernel(mem_ref, g_ref, w_ref, k_ref, v_ref):
    x = mem_ref[0]
    y = x * lax.rsqrt(jnp.mean(x * x, axis=-1, keepdims=True) + NORM_EPS) * g_ref[...]
    kv = _dot(y, w_ref[...])
    head = _iota((1, MEM_WIDTH), 1) >> 6
    for hd in range(MEM_HEADS):
        k_ref[0, hd] = jnp.where(head == hd, kv[:, 0:MEM_WIDTH], 0.0).astype(k_ref.dtype)
        v_ref[0, hd] = jnp.where(head == hd, kv[:, MEM_WIDTH:2 * MEM_WIDTH], 0.0).astype(v_ref.dtype)


def _memkv(mem, g, w, layer, interpret):
    b, m, d = mem.shape
    out = jax.ShapeDtypeStruct((b, MEM_HEADS, m, MEM_WIDTH), MXU_DTYPE)
    return pl.pallas_call(
        _memkv_kernel,
        grid=(b,),
        in_specs=[pl.BlockSpec((1, m, d), lambda i: (i, 0, 0)), _layer_spec(g, layer), _layer_spec(w, layer)],
        out_specs=[pl.BlockSpec((1, MEM_HEADS, m, MEM_WIDTH), lambda i: (i, 0, 0, 0))] * 2,
        out_shape=[out, out],
        compiler_params=_params(("parallel",)),
        interpret=interpret,
        name="mem_kv",
    )(mem, g, w)


def _merge_kernel(x_ref, h_ref, oret_ref, ogla_ref, ossd_ref, orwkv_ref, km_ref, vm_ref,
                  wq_ref, wg_ref, uret_ref, ugla_ref, ussd_ref, urwkv_ref, umem_ref, wout_ref, gn_ref,
                  *refs):
    out_refs, wqb_scr = refs[:-1], refs[-1]

    @pl.when(_first_step())
    def _():
        wqb_scr[...] = wq_ref[0].astype(wqb_scr.dtype)

    h = h_ref[...]
    q = _proj(h, wqb_scr[...]) * MEM_HEAD_DIM ** -0.5
    scores = [_dot_nt(q, km_ref[0, hd]) for hd in range(MEM_HEADS)]
    branches = ((oret_ref, uret_ref), (ogla_ref, ugla_ref), (ossd_ref, ussd_ref), (orwkv_ref, urwkv_ref))
    merged = None
    for i in range(N_BRANCHES):
        gate = _sigmoid(_proj(h, wg_ref[i * D_MODEL:(i + 1) * D_MODEL, :]))
        if i < 4:
            o_ref, u_ref = branches[i]
            up = jnp.dot(o_ref[...], u_ref[...], preferred_element_type=F32)
        else:
            up = _dot(o_mem, umem_ref[...])
        merged = gate * up if merged is None else merged + gate * up
        if i == 0:
            o_mem = jnp.zeros(q.shape, F32)
            for hd in range(MEM_HEADS):
                s = jnp.exp(scores[hd] - jnp.max(scores[hd], axis=-1, keepdims=True))
                prob = s / jnp.sum(s, axis=-1, keepdims=True)
                o_mem = o_mem + _dot(prob, vm_ref[0, hd])
    x = x_ref[...] + _dot(merged, wout_ref[...])
    if len(out_refs) == 2:
        out_refs[0][...] = x
    y = x * lax.rsqrt(jnp.mean(x * x, axis=-1, keepdims=True) + NORM_EPS) * gn_ref[...]
    out_refs[-1][...] = y.astype(out_refs[-1].dtype)


def _merge(x2d, h, o_ret, o_gla, o_ssd, o_rwkv, km, vm, wq, wg, ups, wout, g_next, layer, last,
           batch, seq, interpret):
    tm = min(TOKEN_TILE, seq)
    nt = seq // tm
    row = lambda w: pl.BlockSpec((tm, w), lambda b, i: (b * nt + i, 0))
    kvspec = pl.BlockSpec((1,) + km.shape[1:], lambda b, i: (b, 0, 0, 0))
    weights = [wg, *ups, wout, g_next]
    return pl.pallas_call(
        _merge_kernel,
        grid=(batch, nt),
        in_specs=[row(D_MODEL), row(D_MODEL), row(WIDTH), row(WIDTH), row(WIDTH), row(WIDTH), kvspec, kvspec,
                  _w_rows_spec(wq, layer, _IN_OFFS[14], MEM_WIDTH)]
        + [_layer_spec(w, layer, pipeline_mode=pl.Buffered(1)) for w in weights],
        out_specs=[row(D_MODEL)] if last else [row(D_MODEL), row(D_MODEL)],
        out_shape=[jax.ShapeDtypeStruct(x2d.shape, F32)] if last else
        [jax.ShapeDtypeStruct(x2d.shape, F32), jax.ShapeDtypeStruct(x2d.shape, MXU_DTYPE)],
        scratch_shapes=[pltpu.VMEM((MEM_WIDTH, D_MODEL), MXU_DTYPE)],
        compiler_params=_params(("arbitrary", "arbitrary")),
        interpret=interpret,
        name="merge",
    )(x2d, h, o_ret, o_gla, o_ssd, o_rwkv, km, vm, wq, *weights)


def _pad_last(a, width):
    return jnp.pad(a, [(0, 0)] * (a.ndim - 1) + [(0, width - a.shape[-1])])


def _rows3(v, width=None):
    v = v.reshape(v.shape[0], 1, -1).astype(F32)
    return v if width is None else _pad_last(v, width)


def _forward(x, mem, positions, norm_g, w_in, gla_gk_w2, gla_gk_b, gla_norm_g,
             ssd_conv_w, ssd_conv_b, ssd_dt_bias, ssd_a_log, ssd_d, ssd_norm_g,
             rwkv_mu, rwkv_w0, rwkv_w2, rwkv_a0, rwkv_a2, rwkv_k_k, rwkv_k_a, rwkv_r_k,
             rwkv_ln_g, rwkv_ln_b, mem_norm_g, w_mem_kv,
             w_up_ret, w_up_gla, w_up_ssd, w_up_rwkv, w_up_mem, w_out, final_norm_g, interpret=False):
    batch, seq, d = x.shape
    assert d == D_MODEL and seq % (RET_CHUNK * RET_GROUP) == 0 and seq % (GLA_CHUNK * GLA_GROUP) == 0
    depth = w_in.shape[0]
    cdt = MXU_DTYPE
    o = _IN_OFFS

    half = np.arange(RET_DK // 2)
    ret_perm = np.concatenate([hd * RET_DK + 2 * half + par for par in (0, 1) for hd in range(RET_HEADS)])
    inv = 1.0 / (ROPE_BASE ** jnp.linspace(0.0, 1.0, RET_DK // 2, dtype=F32))
    inv_row = jnp.tile(inv, RET_HEADS).reshape(1, LANES)
    ret_tile, ret_rows = RET_CHUNK * RET_GROUP, RET_CHUNK * RET_GROUP // RET_HEADS
    ret_pos = jnp.repeat(positions.reshape(batch, seq // ret_tile, RET_HEADS, ret_rows).transpose(0, 1, 3, 2),
                         RET_DK // 2, axis=3).reshape(batch * seq // RET_HEADS, LANES)
    head_of_lane = np.arange(WIDTH) // SSD_P
    ssd_expand = jnp.asarray(np.arange(LANES)[:, None] == head_of_lane[None, :], F32)
    rwkv_seg = jnp.asarray(head_of_lane[:256, None] == head_of_lane[None, :256], F32)

    w_t = jnp.swapaxes(w_in, 1, 2)
    w_g = w_t[:, o[15]:o[16], :].astype(cdt)
    ret_perm_rows = jnp.asarray(ret_perm[:, None] == np.arange(RET_HEADS * RET_DK)[None, :], cdt)
    gla_w2p = jnp.pad(gla_gk_w2, ((0, 0), (0, LANES - GLA_RANK), (0, 0))).astype(cdt)
    zeros_rank = jnp.zeros((depth, RWKV_RANK, WIDTH), F32)
    rwkv_w2p = jnp.concatenate([rwkv_w2, zeros_rank], axis=1).astype(cdt)
    rwkv_a2p = jnp.concatenate([zeros_rank, rwkv_a2], axis=1).astype(cdt)
    rwkv_vecs = [_rows3(v) for v in (rwkv_mu, rwkv_w0, rwkv_a0, rwkv_k_k, rwkv_k_a, rwkv_r_k,
                                     rwkv_ln_g, rwkv_ln_b)]
    ssd_small = [ssd_conv_w.astype(F32), _rows3(ssd_conv_b), _rows3(ssd_dt_bias, LANES),
                 _rows3(-jnp.exp(ssd_a_log.astype(F32)), LANES), _rows3(jnp.repeat(ssd_d, SSD_P, axis=1)),
                 _rows3(ssd_norm_g)]
    gla_b, gla_ng = _rows3(gla_gk_b), _rows3(gla_norm_g)
    mem_g, w_kv = _rows3(mem_norm_g), w_mem_kv.astype(cdt)
    ups = [w.astype(cdt) for w in (w_up_ret, w_up_gla, w_up_ssd, w_up_rwkv, w_up_mem)]
    w_o = w_out.astype(cdt)
    g_next = _rows3(jnp.concatenate([norm_g[1:], final_norm_g[None]], axis=0))

    x2d = x.reshape(batch * seq, d)
    h = _rmsnorm(x2d, norm_g[0], cdt, interpret)
    for l in range(depth):
        o_ret = _retention(h, ret_pos, inv_row, w_t, ret_perm_rows, l, batch, seq, interpret)
        o_gla = _gla(h, w_t, gla_w2p, gla_b, gla_ng, l, batch, seq, interpret)
        o_ssd = _ssd(h, w_t, *ssd_small, ssd_expand, l, batch, seq, interpret)
        o_rwkv = _rwkv(h, w_t, rwkv_vecs, rwkv_w2p, rwkv_a2p, rwkv_seg, l, batch, seq, interpret)
        km, vm = _memkv(mem, mem_g, w_kv, l, interpret)
        last = l == depth - 1
        outs = _merge(x2d, h, o_ret, o_gla, o_ssd, o_rwkv, km, vm, w_t, w_g, ups, w_o, g_next, l, last,
                      batch, seq, interpret)
        x2d, h = (None, outs[0]) if last else outs
    return h.reshape(batch, seq, d)


def kernel(x, mem, positions, norm_g, w_in, gla_gk_w2, gla_gk_b, gla_norm_g, ssd_conv_w, ssd_conv_b, ssd_dt_bias, ssd_a_log, ssd_d, ssd_norm_g, rwkv_mu, rwkv_w0, rwkv_w2, rwkv_a0, rwkv_a2, rwkv_k_k, rwkv_k_a, rwkv_r_k, rwkv_ln_g, rwkv_ln_b, mem_norm_g, w_mem_kv, w_up_ret, w_up_gla, w_up_ssd, w_up_rwkv, w_up_mem, w_out, final_norm_g):
    return _forward(x, mem, positions, norm_g, w_in, gla_gk_w2, gla_gk_b, gla_norm_g,
                    ssd_conv_w, ssd_conv_b, ssd_dt_bias, ssd_a_log, ssd_d, ssd_norm_g,
                    rwkv_mu, rwkv_w0, rwkv_w2, rwkv_a0, rwkv_a2, rwkv_k_k, rwkv_k_a, rwkv_r_k,
                    rwkv_ln_g, rwkv_ln_b, mem_norm_g, w_mem_kv,
                    w_up_ret, w_up_gla, w_up_ssd, w_up_rwkv, w_up_mem, w_out, final_norm_g)
```
